```python
import jax, jax.numpy as jnp
from jax import lax
import numpy as np

D_MODEL = 1024
BATCH = 8
SEQ = 2048
DEPTH = 2

GRID_W = 64
CTX_LEN = 256
LN_EPS = 1e-5
D_CONV = 512
CONV_WIDTH = 31
N_Q_HEADS = 8
N_KV_HEADS = 2
HEAD_DIM = 64
WINDOW = 128
BLOCK = 128
ROPE_BASE = 10000.0
D_ATTN = N_Q_HEADS * HEAD_DIM
D_KV = N_KV_HEADS * HEAD_DIM
N_MLSTM_HEADS = 4
MLSTM_HEAD_DIM = 128
D_MLSTM = N_MLSTM_HEADS * MLSTM_HEAD_DIM
CHUNK = 64
N_GATE_COLS = 2 * 2 * N_MLSTM_HEADS
N_BRANCHES = 3
IN_SPLITS = (2 * D_CONV, D_ATTN, D_KV, D_KV, D_MLSTM, D_MLSTM, D_MLSTM, D_MLSTM, N_GATE_COLS, N_BRANCHES * D_MODEL)
D_IN = 2 * D_CONV + D_ATTN + 2 * D_KV + 4 * D_MLSTM + N_GATE_COLS + N_BRANCHES * D_MODEL
N_EXPERTS = 16
N_GROUPS = 4
EXPERTS_PER_GROUP = N_EXPERTS // N_GROUPS
TOP_K = 2
D_EXPERT = 512

kernel_name = "hybrid_conv_swa_mlstm_moe_dit"


def layer_norm(x, g, b):
    xf = x.astype(jnp.float32)
    mu = xf.mean(-1, keepdims=True)
    var = jnp.square(xf - mu).mean(-1, keepdims=True)
    y = (xf - mu) * lax.rsqrt(var + LN_EPS)
    return (y * g.astype(jnp.float32) + b.astype(jnp.float32)).astype(x.dtype)


def split_in(p):
    offs, acc = [], 0
    for s in IN_SPLITS[:-1]:
        acc += s
        offs.append(acc)
    return jnp.split(p, offs, axis=-1)


def conv_module(a_in, w, b, g, beta):
    val, gate = jnp.split(a_in, 2, axis=-1)
    u = val * jax.nn.sigmoid(gate)
    pad = CONV_WIDTH // 2
    y = lax.conv_general_dilated(u, w[:, None, :], window_strides=(1,), padding=[(pad, pad)],
                                 dimension_numbers=("NWC", "WIO", "NWC"), feature_group_count=D_CONV) + b
    return jax.nn.silu(layer_norm(y, g, b * 0 + beta) if False else layer_norm(y, g, beta))


def rope_half(u, pos):
    nf = u.shape[-1] // 2
    inv = ROPE_BASE ** (-jnp.arange(nf, dtype=jnp.float32) / nf)
    ang = pos.astype(jnp.float32)[:, None] * inv[None, :]
    cos = jnp.cos(ang)[None, :, None, :]
    sin = jnp.sin(ang)[None, :, None, :]
    u1 = u[..., :nf].astype(jnp.float32)
    u2 = u[..., nf:].astype(jnp.float32)
    return jnp.concatenate([u1 * cos - u2 * sin, u1 * sin + u2 * cos], axis=-1).astype(u.dtype)


def axial_rope(t, row_pos, col_pos):
    half = t.shape[-1] // 2
    return jnp.concatenate([rope_half(t[..., :half], row_pos), rope_half(t[..., half:], col_pos)], axis=-1)


def sink_softmax(scores, sink):
    m = sink
    for s in scores:
        m = jnp.maximum(m, s.max(-1, keepdims=True))
    ex = [jnp.exp(s - m) for s in scores]
    denom = jnp.exp(sink - m)
    for e in ex:
        denom = denom + e.sum(-1, keepdims=True)
    return [e / denom for e in ex]


def banded_attention(q, k, v, kc, vc, sink):
    B, S, H, Dh = q.shape
    G = H // N_KV_HEADS
    nb = S // BLOCK
    scale = Dh ** -0.5
    qb = q.reshape(B, nb, BLOCK, N_KV_HEADS, G, Dh)

    def band(t):
        tb = t.reshape(B, nb, BLOCK, N_KV_HEADS, Dh)
        tp = jnp.pad(tb, ((0, 0), (1, 1), (0, 0), (0, 0), (0, 0)))
        return jnp.concatenate([tp[:, :-2], tp[:, 1:-1], tp[:, 2:]], axis=2)

    kb, vb = band(k), band(v)
    s_loc = jnp.einsum("bnqhgd,bnkhd->bnhgqk", qb, kb).astype(jnp.float32) * scale
    blk = jnp.arange(nb)[:, None]
    qpos = blk * BLOCK + jnp.arange(BLOCK)[None, :]
    kpos = blk * BLOCK - BLOCK + jnp.arange(3 * BLOCK)[None, :]
    valid = ((jnp.abs(qpos[:, :, None] - kpos[:, None, :]) <= WINDOW)
             & (kpos[:, None, :] >= 0) & (kpos[:, None, :] < S))
    s_loc = jnp.where(valid[None, :, None, None], s_loc, -jnp.inf)
    s_ctx = jnp.einsum("bnqhgd,blhd->bnhgql", qb, kc).astype(jnp.float32) * scale
    sink_l = jnp.broadcast_to(sink.astype(jnp.float32).reshape(1, 1, N_KV_HEADS, G, 1, 1), s_loc.shape[:-1] + (1,))
    p_loc, p_ctx = sink_softmax([s_loc, s_ctx], sink_l)
    o = (jnp.einsum("bnhgqk,bnkhd->bnqhgd", p_loc.astype(v.dtype), vb)
         + jnp.einsum("bnhgql,blhd->bnqhgd", p_ctx.astype(vc.dtype), vc))
    return o.reshape(B, S, H * Dh)


def context_attention(qc, kc, vc, sink):
    B, L, H, Dh = qc.shape
    G = H // N_KV_HEADS
    qg = qc.reshape(B, L, N_KV_HEADS, G, Dh)
    s = jnp.einsum("bqhgd,bkhd->bhgqk", qg, kc).astype(jnp.float32) * (Dh ** -0.5)
    sink_l = jnp.broadcast_to(sink.astype(jnp.float32).reshape(1, N_KV_HEADS, G, 1, 1), s.shape[:-1] + (1,))
    (p,) = sink_softmax([s], sink_l)
    o = jnp.einsum("bhgqk,bkhd->bqhgd", p.astype(vc.dtype), vc)
    return o.reshape(B, L, H * Dh)


def mlstm_scan(q, k, v, ig, lf, state):
    B, T, H, dh = q.shape
    nc = T // CHUNK
    tril = jnp.tril(jnp.ones((CHUNK, CHUNK), dtype=bool))

    def to_chunks(t):
        return jnp.moveaxis(t.reshape((B, nc, CHUNK) + t.shape[2:]), 1, 0)

    def step(carry, inp):
        C, n, m = carry
        qc, kc, vc, igc, lfc = inp
        b = jnp.cumsum(lfc, axis=1)
        log_d = b[:, :, None, :] - b[:, None, :, :] + igc[:, None, :, :]
        log_d = jnp.where(tril[None, :, :, None], log_d, -jnp.inf)
        inter = b + m[:, None, :]
        m_out = jnp.maximum(inter, log_d.max(axis=2))
        w_intra = jnp.exp(log_d - m_out[:, :, None, :])
        w_inter = jnp.exp(inter - m_out)
        qk = jnp.einsum("bjhd,bshd->bjsh", qc, kc) * w_intra
        num = jnp.einsum("bjsh,bshe->bjhe", qk, vc) + w_inter[..., None] * jnp.einsum("bjhd,bhde->bjhe", qc, C)
        den = qk.sum(axis=2) + w_inter * jnp.einsum("bjhd,bhd->bjh", qc, n)
        h = num / jnp.maximum(jnp.abs(den), jnp.exp(-m_out))[..., None]
        b_last = b[:, -1, :]
        log_w = b_last[:, None, :] - b + igc
        m_new = jnp.maximum(b_last + m, log_w.max(axis=1))
        w_s = jnp.exp(log_w - m_new[:, None, :])
        decay = jnp.exp(b_last + m - m_new)
        C_new = decay[..., None, None] * C + jnp.einsum("bsh,bshd,bshe->bhde", w_s, kc, vc)
        n_new = decay[..., None] * n + jnp.einsum("bsh,bshd->bhd", w_s, kc)
        return (C_new, n_new, m_new), h

    xs = (to_chunks(q), to_chunks(k), to_chunks(v), to_chunks(ig), to_chunks(lf))
    state, h = lax.scan(step, state, xs)
    h = jnp.moveaxis(h, 0, 1).reshape(B, T, H, dh)
    return h, state


def mlstm_bidir(q, k, v, gates, init_fwd, init_bwd):
    ig_f, lf_f = gates[:, :, 0, 0], jax.nn.log_sigmoid(gates[:, :, 0, 1])
    ig_b, lf_b = gates[:, :, 1, 0], jax.nn.log_sigmoid(gates[:, :, 1, 1])
    h_f, st_f = mlstm_scan(q, k, v, ig_f, lf_f, init_fwd)
    fl = lambda t: jnp.flip(t, axis=1)
    h_b, st_b = mlstm_scan(fl(q), fl(k), fl(v), fl(ig_b), fl(lf_b), init_bwd)
    return h_f + fl(h_b), st_f, st_b


def head_norm(h, g):
    mu = h.mean(-1, keepdims=True)
    var = jnp.square(h - mu).mean(-1, keepdims=True)
    y = (h - mu) * lax.rsqrt(var + LN_EPS)
    return y.reshape(h.shape[:2] + (-1,)) * g.astype(jnp.float32)


def token_mixer(u, uc, row_pos, col_pos, w_in, conv_w, conv_b, conv_ln_g, conv_ln_b, w_a_out, attn_sink,
                w_b_out, mlstm_gate_b, mlstm_norm_g, w_c_out, w_out, need_ctx_out):
    B, S, _ = u.shape
    L = uc.shape[1]
    a_in, q, k, v, mq, mk, mv, mo, mg, bg = split_in(u @ w_in)
    a_in_c, q_c, k_c, v_c, mq_c, mk_c, mv_c, mo_c, mg_c, bg_c = split_in(uc @ w_in)

    qh = axial_rope(q.reshape(B, S, N_Q_HEADS, HEAD_DIM), row_pos, col_pos)
    kh = axial_rope(k.reshape(B, S, N_KV_HEADS, HEAD_DIM), row_pos, col_pos)
    vh = v.reshape(B, S, N_KV_HEADS, HEAD_DIM)
    kch = k_c.reshape(B, L, N_KV_HEADS, HEAD_DIM)
    vch = v_c.reshape(B, L, N_KV_HEADS, HEAD_DIM)
    y_b = banded_attention(qh, kh, vh, kch, vch, attn_sink)

    def heads(t, T):
        return t.reshape(B, T, N_MLSTM_HEADS, MLSTM_HEAD_DIM).astype(jnp.float32)
    k_scale = MLSTM_HEAD_DIM ** -0.5
    gb = mlstm_gate_b.astype(jnp.float32)
    g_lat = mg.astype(jnp.float32).reshape(B, S, 2, 2, N_MLSTM_HEADS) + gb
    g_ctx = mg_c.astype(jnp.float32).reshape(B, L, 2, 2, N_MLSTM_HEADS) + gb
    zero = (jnp.zeros((B, N_MLSTM_HEADS, MLSTM_HEAD_DIM, MLSTM_HEAD_DIM), jnp.float32),
            jnp.zeros((B, N_MLSTM_HEADS, MLSTM_HEAD_DIM), jnp.float32),
            jnp.zeros((B, N_MLSTM_HEADS), jnp.float32))
    h_ctx, st_f, st_b = mlstm_bidir(heads(mq_c, L), heads(mk_c, L) * k_scale, heads(mv_c, L), g_ctx, zero, zero)
    h_lat, _, _ = mlstm_bidir(heads(mq, S), heads(mk, S) * k_scale, heads(mv, S), g_lat, st_f, st_b)
    y_c = (jax.nn.sigmoid(mo.astype(jnp.float32)) * head_norm(h_lat, mlstm_norm_g)).astype(u.dtype)

    def merge(ya, yb, yc, gates):
        ga, gbr, gc = jnp.split(jax.nn.sigmoid(gates), N_BRANCHES, axis=-1)
        return (ga * (ya @ w_a_out) + gbr * (yb @ w_b_out) + gc * (yc @ w_c_out)) @ w_out

    y_a = conv_module(a_in, conv_w, conv_b, conv_ln_g, conv_ln_b)
    y = merge(y_a, y_b, y_c, bg)
    if need_ctx_out:
        y_a_c = conv_module(a_in_c, conv_w, conv_b, conv_ln_g, conv_ln_b)
        y_b_c = context_attention(q_c.reshape(B, L, N_Q_HEADS, HEAD_DIM), kch, vch, attn_sink)
        y_c_c = (jax.nn.sigmoid(mo_c.astype(jnp.float32)) * head_norm(h_ctx, mlstm_norm_g)).astype(uc.dtype)
        y_ctx = merge(y_a_c, y_b_c, y_c_c, bg_c)
    else:
        y_ctx = None
    return y, y_ctx


def grouped_moe(t, w_router, b_router, w_gu, w_dn):
    N, D = t.shape
    scores = jax.nn.sigmoid((t @ w_router).astype(jnp.float32))
    sel = (scores + b_router.astype(jnp.float32)).reshape(N, N_GROUPS, EXPERTS_PER_GROUP)
    group_score = lax.top_k(sel, TOP_K)[0].sum(-1)
    g_idx = jnp.argmax(group_score, axis=-1)
    idx = jnp.broadcast_to(g_idx[:, None, None], (N, 1, EXPERTS_PER_GROUP))
    in_group = jnp.take_along_axis(sel, idx, axis=1)[:, 0]
    _, loc = lax.top_k(in_group, TOP_K)
    e_idx = g_idx[:, None] * EXPERTS_PER_GROUP + loc
    w = jnp.take_along_axis(scores, e_idx, axis=1)
    w = w / w.sum(-1, keepdims=True)
    gates = jnp.sum(jax.nn.one_hot(e_idx, N_EXPERTS, dtype=jnp.float32) * w[..., None], axis=1)
    out = jnp.zeros((N, D), jnp.float32)
    for e in range(N_EXPERTS):
        g_, u_ = jnp.split(t @ w_gu[e], 2, axis=-1)
        out = out + gates[:, e:e + 1] * ((jax.nn.silu(g_) * u_) @ w_dn[e]).astype(jnp.float32)
    return out.astype(t.dtype)


def setup_inputs(seed: int = 0) -> dict:
    key = jax.random.key(seed)
    ks = iter(jax.random.split(key, 40))
    f32 = jnp.float32

    def nrm(shape, scale):
        return jax.random.normal(next(ks), shape, f32) * scale

    D = D_MODEL
    beta = (8.0 * DEPTH) ** -0.25
    f_bias = jnp.linspace(3.0, 6.0, N_MLSTM_HEADS, dtype=f32)
    ig_b = nrm((DEPTH, 2, N_MLSTM_HEADS), 0.1)
    fg_b = f_bias + nrm((DEPTH, 2, N_MLSTM_HEADS), 0.1)
    return {
        "x": nrm((BATCH, SEQ, D), 1.0),
        "c": nrm((BATCH, D), 1.0),
        "ctx": nrm((BATCH, CTX_LEN, D), 1.0),
        "c_ctx": nrm((D,), 1.0),
        "ln_in_g": 1.0 + nrm((D,), 0.02),
        "ln_in_b": nrm((D,), 0.02),
        "w_router": nrm((D, N_EXPERTS), D ** -0.5),
        "b_router": nrm((N_EXPERTS,), 0.01),
        "w_mod": nrm((DEPTH, D, 6 * D), D ** -0.5),
        "b_mod": nrm((DEPTH, 6 * D), 0.02),
        "w_in": nrm((DEPTH, D, D_IN), D ** -0.5),
        "conv_w": nrm((DEPTH, CONV_WIDTH, D_CONV), CONV_WIDTH ** -0.5),
        "conv_b": nrm((DEPTH, D_CONV), 0.02),
        "conv_ln_g": 1.0 + nrm((DEPTH, D_CONV), 0.02),
        "conv_ln_b": nrm((DEPTH, D_CONV), 0.02),
        "w_a_out": nrm((DEPTH, D_CONV, D), D_CONV ** -0.5),
        "attn_sink": nrm((DEPTH, N_Q_HEADS), 0.5),
        "w_b_out": nrm((DEPTH, D_ATTN, D), D_ATTN ** -0.5),
        "mlstm_gate_b": jnp.stack([ig_b, fg_b], axis=2),
        "mlstm_norm_g": 1.0 + nrm((DEPTH, D_MLSTM), 0.02),
        "w_c_out": nrm((DEPTH, D_MLSTM, D), D_MLSTM ** -0.5),
        "w_out": nrm((DEPTH, D, D), beta * D ** -0.5),
        "ln1_g": 1.0 + nrm((DEPTH, D), 0.02),
        "ln1_b": nrm((DEPTH, D), 0.02),
        "moe_w_gu": nrm((DEPTH, N_EXPERTS, D, 2 * D_EXPERT), D ** -0.5),
        "moe_w_dn": nrm((DEPTH, N_EXPERTS, D_EXPERT, D), beta * D_EXPERT ** -0.5),
        "ln2_g": 1.0 + nrm((DEPTH, D), 0.02),
        "ln2_b": nrm((DEPTH, D), 0.02),
    }


def reference(x, c, ctx, c_ctx, ln_in_g, ln_in_b, w_router, b_router, w_mod, b_mod, w_in, conv_w, conv_b,
              conv_ln_g, conv_ln_b, w_a_out, attn_sink, w_b_out, mlstm_gate_b, mlstm_norm_g, w_c_out, w_out,
              ln1_g, ln1_b, moe_w_gu, moe_w_dn, ln2_g, ln2_b):
    B, S, D = x.shape
    L = ctx.shape[1]
    alpha = (2.0 * DEPTH) ** 0.25
    rows = S // GRID_W
    row_pos = jnp.repeat(jnp.arange(rows), GRID_W)
    col_pos = jnp.arange(rows * GRID_W) % GRID_W

    h = layer_norm(x, ln_in_g, ln_in_b)
    hc = layer_norm(ctx, ln_in_g, ln_in_b)
    silu_c = jax.nn.silu(c)
    silu_cc = jax.nn.silu(c_ctx)
    for l in range(DEPTH):
        need_ctx = l < DEPTH - 1
        sh1, sc1, g1, sh2, sc2, g2 = [t[:, None, :] for t in jnp.split(silu_c @ w_mod[l] + b_mod[l], 6, axis=-1)]
        csh1, csc1, cg1, csh2, csc2, cg2 = jnp.split(silu_cc @ w_mod[l] + b_mod[l], 6, axis=-1)

        u = h * (1 + sc1) + sh1
        uc = hc * (1 + csc1) + csh1
        y, y_ctx = token_mixer(u, uc, row_pos, col_pos, w_in[l], conv_w[l], conv_b[l], conv_ln_g[l], conv_ln_b[l],
                               w_a_out[l], attn_sink[l], w_b_out[l], mlstm_gate_b[l], mlstm_norm_g[l], w_c_out[l],
                               w_out[l], need_ctx)
        h = layer_norm(alpha * h + g1 * y, ln1_g[l], ln1_b[l])

        u2 = h * (1 + sc2) + sh2
        if need_ctx:
            hc = layer_norm(alpha * hc + cg1 * y_ctx, ln1_g[l], ln1_b[l])
            uc2 = hc * (1 + csc2) + csh2
            tokens = jnp.concatenate([u2.reshape(B * S, D), uc2.reshape(B * L, D)], axis=0)
            f = grouped_moe(tokens, w_router, b_router, moe_w_gu[l], moe_w_dn[l])
            f_lat = f[:B * S].reshape(B, S, D)
            f_ctx = f[B * S:].reshape(B, L, D)
            hc = layer_norm(alpha * hc + cg2 * f_ctx, ln2_g[l], ln2_b[l])
        else:
            f_lat = grouped_moe(u2.reshape(B * S, D), w_router, b_router, moe_w_gu[l], moe_w_dn[l]).reshape(B, S, D)
        h = layer_norm(alpha * h + g2 * f_lat, ln2_g[l], ln2_b[l])
    return h
```

```python
import functools

import numpy as np
import jax
import jax.numpy as jnp
from jax import lax
from jax.experimental import pallas as pl
from jax.experimental.pallas import tpu as pltpu

GRID_W = 64
LN_EPS = 1e-5
D_CONV = 512
CONV_WIDTH = 31
N_Q_HEADS = 8
N_KV_HEADS = 2
HEAD_DIM = 64
WINDOW = 128
BLOCK = 128
ROPE_BASE = 10000.0
D_ATTN = N_Q_HEADS * HEAD_DIM
D_KV = N_KV_HEADS * HEAD_DIM
N_MLSTM_HEADS = 4
MLSTM_HEAD_DIM = 128
D_MLSTM = N_MLSTM_HEADS * MLSTM_HEAD_DIM
N_GATE_COLS = 2 * 2 * N_MLSTM_HEADS
N_BRANCHES = 3
N_EXPERTS = 16
N_GROUPS = 4
EXPERTS_PER_GROUP = N_EXPERTS // N_GROUPS
D_EXPERT = 512
N_MOD = 6

LANES = 128
V7X_VMEM_LIMIT_BYTES = 56 * 1024 * 1024

MOD_ROWS = 16
MOD_COL_BLOCK = 512
TOKEN_TILE = 256
MOE_TILE = 512
MLSTM_CHUNK = 128
CONV_ROWS = 64
CONV_PAD = 16

F32 = jnp.float32
BF16 = jnp.bfloat16
HIGHEST = lax.Precision.HIGHEST
NEG_INF = float("-inf")

_C_A = 0
_C_Q = _C_A + 2 * D_CONV
_C_QS = _C_Q + D_ATTN
_C_K = _C_QS + D_ATTN
_C_KS = _C_K + D_KV
_C_V = _C_KS + D_KV
_C_MQ = _C_V + D_KV
_C_MV = _C_MQ + D_MLSTM
_C_MO = _C_MV + D_MLSTM
_C_BG = _C_MO + D_MLSTM
_C_END = _C_BG + N_BRANCHES * 1024


def _cparams(*sem):
    return pltpu.CompilerParams(dimension_semantics=sem, vmem_limit_bytes=V7X_VMEM_LIMIT_BYTES)


def _ln(x, g, b):
    mu = jnp.mean(x, axis=-1, keepdims=True)
    xc = x - mu
    var = jnp.mean(xc * xc, axis=-1, keepdims=True)
    return xc * lax.rsqrt(var + LN_EPS) * g + b


def _sigmoid(x):
    return 1.0 / (1.0 + jnp.exp(-x))


def _log_sigmoid(x):
    return jnp.minimum(x, 0.0) - jnp.log(1.0 + jnp.exp(-jnp.abs(x)))


def _dot_nt(a, b, precision=None):
    return lax.dot_general(a, b, (((1,), (1,)), ((), ())), preferred_element_type=F32, precision=precision)


def _mod_kernel(c_ref, w_ref, b_ref, o_ref):
    c = c_ref[...]
    s = c * _sigmoid(c)
    o_ref[...] = jnp.dot(s, w_ref[...], preferred_element_type=F32, precision=HIGHEST) + b_ref[...]


def _modulation(cc, w_mod, b_mod):
    depth, d, n = w_mod.shape
    return pl.pallas_call(
        _mod_kernel,
        grid=(depth, n // MOD_COL_BLOCK),
        in_specs=[
            pl.BlockSpec((MOD_ROWS, d), lambda l, j: (0, 0)),
            pl.BlockSpec((None, d, MOD_COL_BLOCK), lambda l, j: (l, 0, j)),
            pl.BlockSpec((None, 1, MOD_COL_BLOCK), lambda l, j: (l, 0, j)),
        ],
        out_specs=pl.BlockSpec((None, MOD_ROWS, MOD_COL_BLOCK), lambda l, j: (l, 0, j)),
        out_shape=jax.ShapeDtypeStruct((depth, MOD_ROWS, n), F32),
        compiler_params=_cparams("parallel", "parallel"),
        name="modulation",
    )(cc, w_mod, b_mod.reshape(depth, 1, n))


def _mod_spec(which, tile, seq_len, ctx_row):
    tiles_per_seq = seq_len // tile
    if ctx_row is None:
        return pl.BlockSpec((None, 1, 1024), lambda i, *_: ((i // tiles_per_seq) * N_MOD + which, 0, 0))
    return pl.BlockSpec((None, 1, 1024), lambda i, *_: (ctx_row * N_MOD + which, 0, 0))


def _in_kernel(pre_ln, x_ref, lg_ref, lb_ref, sc_ref, sh_ref, w_ref, wt_ref, cq_ref, sq_ref, ck_ref, sk_ref,
               a_ref, q_ref, k_ref, v_ref, mq_ref, mv_ref, mo_ref, bg_ref, kt_ref, gt_ref):
    x = x_ref[...]
    if pre_ln:
        x = _ln(x, lg_ref[...], lb_ref[...])
    u = (x * (1.0 + sc_ref[...]) + sh_ref[...]).astype(BF16)

    def seg(lo, hi):
        return jnp.dot(u, w_ref[:, lo:hi], preferred_element_type=F32)

    a_ref[...] = seg(_C_A, _C_Q).astype(BF16)
    q_ref[...] = (seg(_C_Q, _C_QS) * cq_ref[...] + seg(_C_QS, _C_K) * sq_ref[...]).astype(BF16)
    k_ref[...] = (seg(_C_K, _C_KS) * ck_ref[...] + seg(_C_KS, _C_V) * sk_ref[...]).astype(BF16)
    v_ref[...] = seg(_C_V, _C_MQ).astype(BF16)
    mq_ref[...] = seg(_C_MQ, _C_MV).astype(BF16)
    mv_ref[...] = seg(_C_MV, _C_MO).astype(BF16)
    mo_ref[...] = seg(_C_MO, _C_BG).astype(BF16)
    for j in range(N_BRANCHES):
        bg_ref[:, j * 1024:(j + 1) * 1024] = seg(_C_BG + j * 1024, _C_BG + (j + 1) * 1024).astype(BF16)
    n_chunks = u.shape[0] // MLSTM_CHUNK
    kt = _dot_nt(wt_ref[0:D_MLSTM, :], u)
    for h in range(N_MLSTM_HEADS):
        for c in range(n_chunks):
            kt_ref[h, c] = kt[h * MLSTM_HEAD_DIM:(h + 1) * MLSTM_HEAD_DIM,
                              c * MLSTM_CHUNK:(c + 1) * MLSTM_CHUNK].astype(BF16)
    gt = _dot_nt(wt_ref[D_MLSTM:D_MLSTM + N_GATE_COLS, :], u)
    for h in range(N_MLSTM_HEADS):
        for c in range(n_chunks):
            gt_ref[h, c] = gt[h * 4:(h + 1) * 4, c * MLSTM_CHUNK:(c + 1) * MLSTM_CHUNK]


def _in_proj(x, ln_g, ln_b, mod_l, w_main, w_t, rope, seq_len, ctx_row, pre_ln):
    n = x.shape[0]
    tm = TOKEN_TILE
    cq, sq, ck, sk = rope
    tps = seq_len // tm
    nch = n // MLSTM_CHUNK
    row = lambda i: (i, 0)
    pos = lambda i: (i % tps, 0)
    const = lambda i: (0, 0)
    out_shape = [
        jax.ShapeDtypeStruct((n, 2 * D_CONV), BF16),
        jax.ShapeDtypeStruct((n, D_ATTN), BF16),
        jax.ShapeDtypeStruct((n, D_KV), BF16),
        jax.ShapeDtypeStruct((n, D_KV), BF16),
        jax.ShapeDtypeStruct((n, D_MLSTM), BF16),
        jax.ShapeDtypeStruct((n, D_MLSTM), BF16),
        jax.ShapeDtypeStruct((n, D_MLSTM), BF16),
        jax.ShapeDtypeStruct((n, N_BRANCHES * 1024), BF16),
        jax.ShapeDtypeStruct((N_MLSTM_HEADS, nch, MLSTM_HEAD_DIM, MLSTM_CHUNK), BF16),
        jax.ShapeDtypeStruct((N_MLSTM_HEADS, nch, 4, MLSTM_CHUNK), F32),
    ]
    cpt = tm // MLSTM_CHUNK
    out_specs = [
        pl.BlockSpec((tm, 2 * D_CONV), row),
        pl.BlockSpec((tm, D_ATTN), row),
        pl.BlockSpec((tm, D_KV), row),
        pl.BlockSpec((tm, D_KV), row),
        pl.BlockSpec((tm, D_MLSTM), row),
        pl.BlockSpec((tm, D_MLSTM), row),
        pl.BlockSpec((tm, D_MLSTM), row),
        pl.BlockSpec((tm, N_BRANCHES * 1024), row),
        pl.BlockSpec((N_MLSTM_HEADS, cpt, MLSTM_HEAD_DIM, MLSTM_CHUNK), lambda i: (0, i, 0, 0)),
        pl.BlockSpec((N_MLSTM_HEADS, cpt, 4, MLSTM_CHUNK), lambda i: (0, i, 0, 0)),
    ]
    in_specs = [
        pl.BlockSpec((tm, 1024), row),
        pl.BlockSpec((1, 1024), const),
        pl.BlockSpec((1, 1024), const),
        _mod_spec(1, tm, seq_len, ctx_row),
        _mod_spec(0, tm, seq_len, ctx_row),
        pl.BlockSpec(w_main.shape, const, pipeline_mode=pl.Buffered(1)),
        pl.BlockSpec(w_t.shape, const, pipeline_mode=pl.Buffered(1)),
        pl.BlockSpec((tm, D_ATTN), pos),
        pl.BlockSpec((tm, D_ATTN), pos),
        pl.BlockSpec((tm, D_KV), pos),
        pl.BlockSpec((tm, D_KV), pos),
    ]
    return pl.pallas_call(
        functools.partial(_in_kernel, pre_ln),
        grid=(n // tm,),
        in_specs=in_specs,
        out_specs=out_specs,
        out_shape=out_shape,
        compiler_params=_cparams("parallel"),
        name="in_proj",
    )(x, ln_g, ln_b, mod_l, mod_l, w_main, w_t, cq, sq, ck, sk)


def _conv_kernel(a_ref, w_ref, cb_ref, g_ref, b_ref, o_ref, upad_ref):
    t = a_ref.shape[0]
    zeros = jnp.zeros((CONV_PAD, D_CONV), F32)
    upad_ref[0:CONV_PAD, :] = zeros
    upad_ref[CONV_PAD + t:2 * CONV_PAD + t, :] = zeros
    val = a_ref[:, 0:D_CONV].astype(F32)
    gate = a_ref[:, D_CONV:2 * D_CONV].astype(F32)
    upad_ref[CONV_PAD:CONV_PAD + t, :] = val * _sigmoid(gate)
    half = CONV_WIDTH // 2

    def body(c, carry):
        r0 = pl.multiple_of(c * CONV_ROWS, CONV_ROWS)
        win = upad_ref[pl.ds(r0, CONV_ROWS + 2 * CONV_PAD), :]
        acc = jnp.zeros((CONV_ROWS, D_CONV), F32) + cb_ref[...]
        for k in range(CONV_WIDTH):
            off = CONV_PAD - half + k
            acc = acc + win[off:off + CONV_ROWS, :] * w_ref[k:k + 1, :]
        y = _ln(acc, g_ref[...], b_ref[...])
        o_ref[pl.ds(r0, CONV_ROWS), :] = (y * _sigmoid(y)).astype(BF16)
        return carry

    lax.fori_loop(0, t // CONV_ROWS, body, 0)


def _conv_branch(a_in, conv_w, conv_b, ln_g, ln_b, seq_len):
    n = a_in.shape[0]
    const = lambda b: (0, 0)
    return pl.pallas_call(
        _conv_kernel,
        grid=(n // seq_len,),
        in_specs=[
            pl.BlockSpec((seq_len, 2 * D_CONV), lambda b: (b, 0)),
            pl.BlockSpec((CONV_WIDTH, D_CONV), const),
            pl.BlockSpec((1, D_CONV), const),
            pl.BlockSpec((1, D_CONV), const),
            pl.BlockSpec((1, D_CONV), const),
        ],
        out_specs=pl.BlockSpec((seq_len, D_CONV), lambda b: (b, 0)),
        out_shape=jax.ShapeDtypeStruct((n, D_CONV), BF16),
        scratch_shapes=[pltpu.VMEM((seq_len + 2 * CONV_PAD, D_CONV), F32)],
        compiler_params=_cparams("parallel"),
        name="conv_branch",
    )(a_in, conv_w, conv_b, ln_g, ln_b)


def _attn_heads(q, keys, vals, masks, sink_ref, o_ref):
    rows = q.shape[0]
    group = N_Q_HEADS // N_KV_HEADS
    for hk in range(N_KV_HEADS):
        lo = hk * HEAD_DIM
        qs = jnp.concatenate([q[:, (hk * group + g) * HEAD_DIM:(hk * group + g + 1) * HEAD_DIM]
                              for g in range(group)], axis=0)
        sink = jnp.concatenate([jnp.full((rows, 1), sink_ref[hk * group + g], F32) for g in range(group)], axis=0)
        scores = []
        for kk, mask in zip(keys, masks):
            s = _dot_nt(qs, kk[:, lo:lo + HEAD_DIM])
            if mask is not None:
                s = jnp.where(mask, s, NEG_INF)
            scores.append(s)
        m = sink
        for s in scores:
            m = jnp.maximum(m, jnp.max(s, axis=-1, keepdims=True))
        denom = jnp.exp(sink - m)
        acc = jnp.zeros((rows * group, HEAD_DIM), F32)
        for s, vv in zip(scores, vals):
            e = jnp.exp(s - m)
            denom = denom + jnp.sum(e, axis=-1, keepdims=True)
            acc = acc + jnp.dot(e.astype(BF16), vv[:, lo:lo + HEAD_DIM], preferred_element_type=F32)
        o = acc / denom
        for g in range(group):
            col = (hk * group + g) * HEAD_DIM
            o_ref[:, col:col + HEAD_DIM] = o[g * rows:(g + 1) * rows, :].astype(BF16)


def _attn_lat_kernel(sink_ref, q_ref, kp_ref, k0_ref, kn_ref, vp_ref, v0_ref, vn_ref, kc_ref, vc_ref, o_ref):
    n = pl.program_id(1)
    nb = pl.num_programs(1)
    stacked = (N_Q_HEADS // N_KV_HEADS) * BLOCK
    qi = lax.broadcasted_iota(jnp.int32, (stacked, BLOCK), 0) % BLOCK
    ki = lax.broadcasted_iota(jnp.int32, (stacked, BLOCK), 1)
    mask_prev = ki >= qi + jnp.where(n > 0, 0, BLOCK)
    mask_next = ki <= qi - jnp.where(n < nb - 1, 0, BLOCK)
    _attn_heads(q_ref[...],
                [kp_ref[...], k0_ref[...], kn_ref[...], kc_ref[...]],
                [vp_ref[...], v0_ref[...], vn_ref[...], vc_ref[...]],
                [mask_prev, None, mask_next, None], sink_ref, o_ref)


def _attn_ctx_kernel(sink_ref, q_ref, kc_ref, vc_ref, o_ref):
    _attn_heads(q_ref[...], [kc_ref[...]], [vc_ref[...]], [None], sink_ref, o_ref)


def _attn_latent(q, k, v, kc, vc, sink, seq_len, ctx_len):
    n = q.shape[0]
    nb = seq_len // BLOCK
    batch = n // seq_len
    cur = lambda b, j: (b * nb + j, 0)
    prev = lambda b, j: (b * nb + jnp.maximum(j - 1, 0), 0)
    nxt = lambda b, j: (b * nb + jnp.minimum(j + 1, nb - 1), 0)
    ctx = lambda b, j: (b, 0)
    kv = lambda im: pl.BlockSpec((BLOCK, D_KV), im)
    return pl.pallas_call(
        _attn_lat_kernel,
        grid=(batch, nb),
        in_specs=[
            pl.BlockSpec(memory_space=pltpu.SMEM),
            pl.BlockSpec((BLOCK, D_ATTN), cur),
            kv(prev), kv(cur), kv(nxt), kv(prev), kv(cur), kv(nxt),
            pl.BlockSpec((ctx_len, D_KV), ctx),
            pl.BlockSpec((ctx_len, D_KV), ctx),
        ],
        out_specs=pl.BlockSpec((BLOCK, D_ATTN), cur),
        out_shape=jax.ShapeDtypeStruct((n, D_ATTN), BF16),
        compiler_params=_cparams("parallel", "parallel"),
        name="attn_latent",
    )(sink, q, k, k, k, v, v, v, kc, vc)


def _attn_context(qc, kc, vc, sink, ctx_len):
    n = qc.shape[0]
    blk = lambda b: (b, 0)
    return pl.pallas_call(
        _attn_ctx_kernel,
        grid=(n // ctx_len,),
        in_specs=[
            pl.BlockSpec(memory_space=pltpu.SMEM),
            pl.BlockSpec((ctx_len, D_ATTN), blk),
            pl.BlockSpec((ctx_len, D_KV), blk),
            pl.BlockSpec((ctx_len, D_KV), blk),
        ],
        out_specs=pl.BlockSpec((ctx_len, D_ATTN), blk),
        out_shape=jax.ShapeDtypeStruct((n, D_ATTN), BF16),
        compiler_params=_cparams("parallel"),
        name="attn_context",
    )(sink, qc, kc, vc)


def _mlstm_kernel(ctx_out, nch_c, nch_l,
                  qc_ref, ktc_ref, vc_ref, moc_ref, gc_ref,
                  ql_ref, ktl_ref, vl_ref, mol_ref, gl_ref,
                  gb_ref, ng_ref, *rest):
    if ctx_out:
        yl_ref, yc_ref, af_ref, lff_ref, ab_ref, lfb_ref, cf_ref, cb_ref, hf_ref, hb_ref = rest
    else:
        yl_ref, af_ref, lff_ref, ab_ref, lfb_ref, cf_ref, cb_ref, hf_ref, hb_ref = rest
        yc_ref = None
    lc = MLSTM_CHUNK
    dh = MLSTM_HEAD_DIM
    k_scale = MLSTM_HEAD_DIM ** -0.5
    ti = lax.broadcasted_iota(jnp.int32, (lc, lc), 0)
    si = lax.broadcasted_iota(jnp.int32, (lc, lc), 1)
    lower = si <= ti
    upper = si >= ti
    pre_mat = jnp.where(upper, 1.0, 0.0)
    suf_mat = jnp.where(lower, 1.0, 0.0)
    gb = gb_ref[...]
    n_rows = nch_c + nch_l
    pad_rows = -n_rows % 8

    def gate_rows(kind):
        rows = [gc_ref[c, kind:kind + 1, :] for c in range(nch_c)] + [gl_ref[c, kind:kind + 1, :] for c in range(nch_l)]
        rows = jnp.concatenate(rows, axis=0) + gb[kind:kind + 1, :]
        if pad_rows:
            rows = jnp.concatenate([rows, jnp.zeros((pad_rows, lc), F32)], axis=0)
        return rows

    lf_f = _log_sigmoid(gate_rows(1))
    lf_b = _log_sigmoid(gate_rows(3))
    a_f = gate_rows(0) - jnp.dot(lf_f, pre_mat, preferred_element_type=F32, precision=HIGHEST)
    a_b = gate_rows(2) - jnp.dot(lf_b, suf_mat, preferred_element_type=F32, precision=HIGHEST)
    for c in range(n_rows):
        af_ref[c] = a_f[c:c + 1, :]
        lff_ref[c] = lf_f[c:c + 1, :]
        ab_ref[c] = a_b[c:c + 1, :]
        lfb_ref[c] = lf_b[c:c + 1, :]

    cf_ref[...] = jnp.zeros_like(cf_ref)
    cb_ref[...] = jnp.zeros_like(cb_ref)
    ones_col = jnp.where(lax.broadcasted_iota(jnp.int32, (lc, dh), 1) == 0, 1.0, 0.0).astype(BF16)

    def chunk(q, kt, v, a_row, lf_row, c_ref, m, mask):
        a_mat = jnp.where(mask, jnp.broadcast_to(a_row, (lc, lc)), NEG_INF)
        cm = jnp.max(a_mat, axis=1, keepdims=True)
        b_col = jnp.sum(jnp.where(mask, jnp.broadcast_to(lf_row, (lc, lc)), 0.0), axis=1, keepdims=True)
        mx = jnp.maximum(m, cm)
        mx_last = jnp.maximum(m, jnp.max(a_row, axis=1, keepdims=True))
        w = jnp.exp(a_mat - mx)
        s = jnp.dot(q, kt, preferred_element_type=F32) * k_scale
        p = (s * w).astype(BF16)
        w_s = jnp.exp(a_row - mx_last) * k_scale
        ktw = (kt.astype(F32) * w_s).astype(BF16)
        v_aug = jnp.concatenate([v, ones_col], axis=1)
        both = jnp.dot(jnp.concatenate([p, ktw], axis=0), v_aug, preferred_element_type=F32)
        c_old = c_ref[...]
        inter = jnp.dot(q, c_old.astype(BF16), preferred_element_type=F32)
        tot = both[0:lc, :] + jnp.exp(m - mx) * inter
        den = tot[:, dh:dh + 1]
        h = tot[:, 0:dh] / jnp.maximum(jnp.abs(den), jnp.exp(-(b_col + mx)))
        c_ref[...] = jnp.exp(m - mx_last) * c_old + both[lc:lc + dh, :]
        m_new = jnp.sum(lf_row, axis=1, keepdims=True) + mx_last
        return h, m_new

    m_f = jnp.zeros((1, 1), F32)
    m_b = jnp.zeros((1, 1), F32)
    for c in range(nch_c):
        cb_i = nch_c - 1 - c
        h_f, m_f = chunk(qc_ref[c * lc:(c + 1) * lc, :], ktc_ref[c], vc_ref[c * lc:(c + 1) * lc, :],
                         af_ref[c], lff_ref[c], cf_ref, m_f, lower)
        h_b, m_b = chunk(qc_ref[cb_i * lc:(cb_i + 1) * lc, :], ktc_ref[cb_i], vc_ref[cb_i * lc:(cb_i + 1) * lc, :],
                         ab_ref[cb_i], lfb_ref[cb_i], cb_ref, m_b, upper)
        if ctx_out:
            hf_ref[c * lc:(c + 1) * lc, :] = h_f
            hb_ref[cb_i * lc:(cb_i + 1) * lc, :] = h_b

    def finish(h, mo, o_ref):
        mu = jnp.mean(h, axis=-1, keepdims=True)
        hc = h - mu
        var = jnp.mean(hc * hc, axis=-1, keepdims=True)
        y = hc * lax.rsqrt(var + LN_EPS) * ng_ref[...]
        o_ref[...] = (_sigmoid(mo.astype(F32)) * y).astype(BF16)

    if ctx_out:
        n_c = nch_c * lc
        finish(hf_ref[0:n_c, :] + hb_ref[0:n_c, :], moc_ref[...], yc_ref)

    def body(c, carry):
        m_f, m_b = carry
        cb_i = nch_l - 1 - c
        rf = pl.multiple_of(c * lc, lc)
        rb = pl.multiple_of(cb_i * lc, lc)
        h_f, m_f = chunk(ql_ref[pl.ds(rf, lc), :], ktl_ref[c], vl_ref[pl.ds(rf, lc), :],
                         af_ref[nch_c + c], lff_ref[nch_c + c], cf_ref, m_f, lower)
        h_b, m_b = chunk(ql_ref[pl.ds(rb, lc), :], ktl_ref[cb_i], vl_ref[pl.ds(rb, lc), :],
                         ab_ref[nch_c + cb_i], lfb_ref[nch_c + cb_i], cb_ref, m_b, upper)
        hf_ref[pl.ds(rf, lc), :] = h_f
        hb_ref[pl.ds(rb, lc), :] = h_b
        return m_f, m_b

    lax.fori_loop(0, nch_l, body, (m_f, m_b))
    finish(hf_ref[...] + hb_ref[...], mol_ref[...], yl_ref)


def _mlstm_branch(ctx_p, lat_p, gate_b, norm_g, seq_len, ctx_len, ctx_out):
    mq_c, kt_c, mv_c, mo_c, g_c = ctx_p
    mq_l, kt_l, mv_l, mo_l, g_l = lat_p
    n_l, n_c = mq_l.shape[0], mq_c.shape[0]
    batch = n_l // seq_len
    lc, dh = MLSTM_CHUNK, MLSTM_HEAD_DIM
    nch_c, nch_l = ctx_len // lc, seq_len // lc

    def stream(t, nch):
        tok = pl.BlockSpec((t, dh), lambda b, h: (b, h))
        return [tok,
                pl.BlockSpec((None, nch, dh, lc), lambda b, h: (h, b, 0, 0)),
                tok, tok,
                pl.BlockSpec((None, nch, 4, lc), lambda b, h: (h, b, 0, 0))]

    in_specs = stream(ctx_len, nch_c) + stream(seq_len, nch_l) + [
        pl.BlockSpec((None, 4, 1), lambda b, h: (h, 0, 0)),
        pl.BlockSpec((1, dh), lambda b, h: (0, h)),
    ]
    out_specs = [pl.BlockSpec((seq_len, dh), lambda b, h: (b, h))]
    out_shape = [jax.ShapeDtypeStruct((n_l, D_MLSTM), BF16)]
    if ctx_out:
        out_specs.append(pl.BlockSpec((ctx_len, dh), lambda b, h: (b, h)))
        out_shape.append(jax.ShapeDtypeStruct((n_c, D_MLSTM), BF16))
    row_scratch = pltpu.VMEM((nch_c + nch_l, 1, lc), F32)
    outs = pl.pallas_call(
        functools.partial(_mlstm_kernel, ctx_out, nch_c, nch_l),
        grid=(batch, N_MLSTM_HEADS),
        in_specs=in_specs,
        out_specs=out_specs,
        out_shape=out_shape,
        scratch_shapes=[row_scratch, row_scratch, row_scratch, row_scratch,
                        pltpu.VMEM((dh, 2 * dh), F32), pltpu.VMEM((dh, 2 * dh), F32),
                        pltpu.VMEM((seq_len, dh), F32), pltpu.VMEM((seq_len, dh), F32)],
        compiler_params=_cparams("parallel", "parallel"),
        name="mlstm_branch",
    )(mq_c, kt_c, mv_c, mo_c, g_c, mq_l, kt_l, mv_l, mo_l, g_l, gate_b, norm_g)
    return outs if ctx_out else (outs[0], None)


def _route(logits_t, br):
    sc = [_sigmoid(logits_t[e:e + 1, :]) for e in range(N_EXPERTS)]
    sel = [sc[e] + br[e:e + 1, :] for e in range(N_EXPERTS)]
    epg = EXPERTS_PER_GROUP
    group_score = []
    for g in range(N_GROUPS):
        v = sel[g * epg:(g + 1) * epg]
        best = None
        for i in range(epg):
            for j in range(i + 1, epg):
                pair = v[i] + v[j]
                best = pair if best is None else jnp.maximum(best, pair)
        group_score.append(best)
    g_idx = jnp.zeros_like(group_score[0], dtype=jnp.int32)
    best = group_score[0]
    for g in range(1, N_GROUPS):
        better = group_score[g] > best
        g_idx = jnp.where(better, g, g_idx)
        best = jnp.maximum(best, group_score[g])
    chosen = []
    for g in range(N_GROUPS):
        v = sel[g * epg:(g + 1) * epg]
        in_g = g_idx == g
        for i in range(epg):
            rank = jnp.zeros_like(g_idx)
            for j in range(epg):
                if j == i:
                    continue
                ahead = (v[j] >= v[i]) if j < i else (v[j] > v[i])
                rank = rank + jnp.where(ahead, 1, 0)
            chosen.append(in_g & (rank < 2))
    total = jnp.zeros_like(sc[0])
    for e in range(N_EXPERTS):
        total = total + jnp.where(chosen[e], sc[e], 0.0)
    return jnp.concatenate([jnp.where(chosen[e], sc[e] / total, 0.0) for e in range(N_EXPERTS)], axis=0)


def _merge_kernel(pre_ln, alpha, ya_ref, yb_ref, yc_ref, bg_ref, h_ref, g1_ref, sc2_ref, sh2_ref,
                  lig_ref, lib_ref, l1g_ref, l1b_ref, wa_ref, wb_ref, wc_ref, wo_ref, wr_ref, br_ref,
                  h1_ref, u2_ref, gates_ref):
    def branch(y_ref, w_ref, j):
        gate = _sigmoid(bg_ref[:, j * 1024:(j + 1) * 1024].astype(F32))
        return gate * jnp.dot(y_ref[...], w_ref[...], preferred_element_type=F32)

    mix = branch(ya_ref, wa_ref, 0) + branch(yb_ref, wb_ref, 1) + branch(yc_ref, wc_ref, 2)
    y = jnp.dot(mix.astype(BF16), wo_ref[...], preferred_element_type=F32)
    h = h_ref[...]
    if pre_ln:
        h = _ln(h, lig_ref[...], lib_ref[...])
    h1 = _ln(alpha * h + g1_ref[...] * y, l1g_ref[...], l1b_ref[...])
    h1_ref[...] = h1
    u2 = h1 * (1.0 + sc2_ref[...]) + sh2_ref[...]
    u2_ref[...] = u2.astype(BF16)
    logits_t = _dot_nt(wr_ref[...], u2, precision=HIGHEST)
    gates_ref[...] = _route(logits_t, br_ref[...])


def _merge(ya, yb, yc, bg, h, mod_l, ln_in, ln1, w_a, w_b, w_c, w_o, w_rt, b_r, seq_len, ctx_row, pre_ln, alpha):
    n = ya.shape[0]
    tm = TOKEN_TILE
    row = lambda i: (i, 0)
    const = lambda i: (0, 0)
    vec = pl.BlockSpec((1, 1024), const)
    wspec = lambda w: pl.BlockSpec(w.shape, const)
    return pl.pallas_call(
        functools.partial(_merge_kernel, pre_ln, alpha),
        grid=(n // tm,),
        in_specs=[
            pl.BlockSpec((tm, D_CONV), row),
            pl.BlockSpec((tm, D_ATTN), row),
            pl.BlockSpec((tm, D_MLSTM), row),
            pl.BlockSpec((tm, N_BRANCHES * 1024), row),
            pl.BlockSpec((tm, 1024), row),
            _mod_spec(2, tm, seq_len, ctx_row),
            _mod_spec(4, tm, seq_len, ctx_row),
            _mod_spec(3, tm, seq_len, ctx_row),
            vec, vec, vec, vec,
            wspec(w_a), wspec(w_b), wspec(w_c), wspec(w_o), wspec(w_rt), wspec(b_r),
        ],
        out_specs=[
            pl.BlockSpec((tm, 1024), row),
            pl.BlockSpec((tm, 1024), row),
            pl.BlockSpec((N_EXPERTS, tm), lambda i: (0, i)),
        ],
        out_shape=[
            jax.ShapeDtypeStruct((n, 1024), F32),
            jax.ShapeDtypeStruct((n, 1024), BF16),
            jax.ShapeDtypeStruct((N_EXPERTS, n), F32),
        ],
        compiler_params=_cparams("parallel"),
        name="merge",
    )(ya, yb, yc, bg, h, mod_l, mod_l, mod_l, ln_in[0], ln_in[1], ln1[0], ln1[1], w_a, w_b, w_c, w_o, w_rt, b_r)


def _moe_kernel(alpha, u2_ref, gates_ref, h1_ref, g2_ref, l2g_ref, l2b_ref, wgu_ref, wdn_ref, o_ref, acc_ref):
    e = pl.program_id(1)

    @pl.when(e == 0)
    def _():
        acc_ref[...] = jnp.zeros_like(acc_ref)

    gu = jnp.dot(u2_ref[...], wgu_ref[...], preferred_element_type=F32)
    g_ = gu[:, 0:D_EXPERT]
    act = (g_ * _sigmoid(g_) * gu[:, D_EXPERT:2 * D_EXPERT]).astype(BF16)
    gates = gates_ref[...]
    lane = lax.broadcasted_iota(jnp.int32, gates.shape, 1)
    gate_e = jnp.sum(jnp.where(lane == e, gates, 0.0), axis=1, keepdims=True)
    acc_ref[...] += gate_e * jnp.dot(act, wdn_ref[...], preferred_element_type=F32)

    @pl.when(e == N_EXPERTS - 1)
    def _():
        o_ref[...] = _ln(alpha * h1_ref[...] + g2_ref[...] * acc_ref[...], l2g_ref[...], l2b_ref[...])


def _moe(u2, gates, h1, mod_l, ln2, w_gu, w_dn, seq_len, ctx_row, alpha):
    n = u2.shape[0]
    tm = min(MOE_TILE, seq_len)
    row = lambda i, e: (i, 0)
    const = lambda i, e: (0, 0)
    return pl.pallas_call(
        functools.partial(_moe_kernel, alpha),
        grid=(n // tm, N_EXPERTS),
        in_specs=[
            pl.BlockSpec((tm, 1024), row),
            pl.BlockSpec((tm, N_EXPERTS), row),
            pl.BlockSpec((tm, 1024), row),
            _mod_spec(5, tm, seq_len, ctx_row),
            pl.BlockSpec((1, 1024), const),
            pl.BlockSpec((1, 1024), const),
            pl.BlockSpec((None, 1024, 2 * D_EXPERT), lambda i, e: (e, 0, 0)),
            pl.BlockSpec((None, D_EXPERT, 1024), lambda i, e: (e, 0, 0)),
        ],
        out_specs=pl.BlockSpec((tm, 1024), row),
        out_shape=jax.ShapeDtypeStruct((n, 1024), F32),
        scratch_shapes=[pltpu.VMEM((tm, 1024), F32)],
        compiler_params=_cparams("parallel", "arbitrary"),
        name="moe_experts",
    )(u2, gates, h1, mod_l, ln2[0], ln2[1], w_gu, w_dn)


def _rope_swap_index(n_heads):
    idx = np.arange(n_heads * HEAD_DIM)
    within = idx % (HEAD_DIM // 2)
    quarter = HEAD_DIM // 4
    return np.where(within < quarter, idx + quarter, idx - quarter)


def _rope_tables(seq_len, n_heads, scale):
    t = np.arange(seq_len)
    quarter = HEAD_DIM // 4
    inv = ROPE_BASE ** (-np.arange(quarter, dtype=np.float32) / quarter)
    d = np.arange(HEAD_DIM)
    pos = np.where((d // (HEAD_DIM // 2) == 0)[None, :], (t // GRID_W)[:, None], (t % GRID_W)[:, None])
    ang = jnp.asarray(pos.astype(np.float32)) * jnp.asarray(inv[d % quarter])[None, :]
    sign = np.where(d % (HEAD_DIM // 2) < quarter, -1.0, 1.0).astype(np.float32)
    cos = jnp.cos(ang) * scale
    sin = jnp.sin(ang) * (sign * scale)[None, :]
    return jnp.tile(cos, (1, n_heads)), jnp.tile(sin, (1, n_heads))


def _flat_tables(seq_len, n_heads, scale):
    return (jnp.full((seq_len, n_heads * HEAD_DIM), scale, F32), jnp.zeros((seq_len, n_heads * HEAD_DIM), F32))


def _prep_in_weights(w_in_l):
    splits = np.cumsum([2 * D_CONV, D_ATTN, D_KV, D_KV, D_MLSTM, D_MLSTM, D_MLSTM, D_MLSTM, N_GATE_COLS])
    a, q, k, v, mq, mk, mv, mo, mg, bg = jnp.split(w_in_l, splits, axis=1)
    w_main = jnp.concatenate([a, q, q[:, _rope_swap_index(N_Q_HEADS)], k, k[:, _rope_swap_index(N_KV_HEADS)],
                              v, mq, mv, mo, bg], axis=1).astype(BF16)
    order = np.array([d * 8 + kind * 4 + h for h in range(N_MLSTM_HEADS) for d in range(2) for kind in range(2)])
    w_t = jnp.concatenate([mk.T, mg[:, order].T], axis=0).astype(BF16)
    return w_main, w_t


def kernel(x, c, ctx, c_ctx, ln_in_g, ln_in_b, w_router, b_router, w_mod, b_mod, w_in, conv_w, conv_b, conv_ln_g,
           conv_ln_b, w_a_out, attn_sink, w_b_out, mlstm_gate_b, mlstm_norm_g, w_c_out, w_out, ln1_g, ln1_b,
           moe_w_gu, moe_w_dn, ln2_g, ln2_b):
    batch, seq_len, d = x.shape
    ctx_len = ctx.shape[1]
    depth = w_in.shape[0]
    alpha = (2.0 * depth) ** 0.25
    ctx_row = batch
    assert d == 1024 and batch < MOD_ROWS
    assert seq_len % TOKEN_TILE == 0 and ctx_len % TOKEN_TILE == 0 and seq_len % MOE_TILE == 0

    cc = jnp.zeros((MOD_ROWS, d), F32).at[0:batch].set(c).at[batch].set(c_ctx)
    mod = _modulation(cc, w_mod, b_mod).reshape(depth, MOD_ROWS * N_MOD, 1, d)

    attn_scale = HEAD_DIM ** -0.5
    rope_lat = _rope_tables(seq_len, N_Q_HEADS, attn_scale) + _rope_tables(seq_len, N_KV_HEADS, 1.0)
    rope_ctx = _flat_tables(ctx_len, N_Q_HEADS, attn_scale) + _flat_tables(ctx_len, N_KV_HEADS, 1.0)

    vec = lambda t: t.reshape(1, -1)
    ln_in = (vec(ln_in_g), vec(ln_in_b))
    w_rt = w_router.T
    b_r = b_router.reshape(N_EXPERTS, 1)

    h = x.reshape(batch * seq_len, d)
    hc = ctx.reshape(batch * ctx_len, d)
    for l in range(depth):
        need_ctx = l < depth - 1
        pre_ln = l == 0
        mod_l = mod[l]
        w_main, w_t = _prep_in_weights(w_in[l])
        lat = _in_proj(h, ln_in[0], ln_in[1], mod_l, w_main, w_t, rope_lat, seq_len, None, pre_ln)
        cx = _in_proj(hc, ln_in[0], ln_in[1], mod_l, w_main, w_t, rope_ctx, ctx_len, ctx_row, pre_ln)
        a_l, q_l, k_l, v_l, mq_l, mv_l, mo_l, bg_l, kt_l, gt_l = lat
        a_c, q_c, k_c, v_c, mq_c, mv_c, mo_c, bg_c, kt_c, gt_c = cx

        conv_args = (conv_w[l], vec(conv_b[l]), vec(conv_ln_g[l]), vec(conv_ln_b[l]))
        gate_b = jnp.transpose(mlstm_gate_b[l], (2, 0, 1)).reshape(N_MLSTM_HEADS, 4, 1)
        ya = _conv_branch(a_l, *conv_args, seq_len)
        yb = _attn_latent(q_l, k_l, v_l, k_c, v_c, attn_sink[l], seq_len, ctx_len)
        yc, yc_c = _mlstm_branch((mq_c, kt_c, mv_c, mo_c, gt_c), (mq_l, kt_l, mv_l, mo_l, gt_l),
                                 gate_b, vec(mlstm_norm_g[l]), seq_len, ctx_len, need_ctx)

        ln1 = (vec(ln1_g[l]), vec(ln1_b[l]))
        ln2 = (vec(ln2_g[l]), vec(ln2_b[l]))
        w_a, w_b, w_c, w_o = (w.astype(BF16) for w in (w_a_out[l], w_b_out[l], w_c_out[l], w_out[l]))
        w_gu = moe_w_gu[l].astype(BF16)
        w_dn = moe_w_dn[l].astype(BF16)

        h1, u2, gates_t = _merge(ya, yb, yc, bg_l, h, mod_l, ln_in, ln1, w_a, w_b, w_c, w_o, w_rt, b_r,
                                 seq_len, None, pre_ln, alpha)
        h = _moe(u2, gates_t.T, h1, mod_l, ln2, w_gu, w_dn, seq_len, None, alpha)
        if need_ctx:
            ya_c = _conv_branch(a_c, *conv_args, ctx_len)
            yb_c = _attn_context(q_c, k_c, v_c, attn_sink[l], ctx_len)
            h1c, u2c, gates_tc = _merge(ya_c, yb_c, yc_c, bg_c, hc, mod_l, ln_in, ln1, w_a, w_b, w_c, w_o, w_rt, b_r,
                                        ctx_len, ctx_row, pre_ln, alpha)
            hc = _moe(u2c, gates_tc.T, h1c, mod_l, ln2, w_gu, w_dn, ctx_len, ctx_row, alpha)
    return h.reshape(batch, seq_len, d)
```

```python
import functools

import numpy as np
import jax
import jax.numpy as jnp
from jax import lax
from jax.experimental import pallas as pl
from jax.experimental.pallas import tpu as pltpu

GRID_W = 64
LN_EPS = 1e-5
D_CONV = 512
CONV_WIDTH = 31
N_Q_HEADS = 8
N_KV_HEADS = 2
HEAD_DIM = 64
WINDOW = 128
BLOCK = 128
ROPE_BASE = 10000.0
D_ATTN = N_Q_HEADS * HEAD_DIM
D_KV = N_KV_HEADS * HEAD_DIM
N_MLSTM_HEADS = 4
MLSTM_HEAD_DIM = 128
D_MLSTM = N_MLSTM_HEADS * MLSTM_HEAD_DIM
N_GATE_COLS = 2 * 2 * N_MLSTM_HEADS
N_BRANCHES = 3
N_EXPERTS = 16
N_GROUPS = 4
EXPERTS_PER_GROUP = N_EXPERTS // N_GROUPS
D_EXPERT = 512
N_MOD = 6

LANES = 128
V7X_VMEM_LIMIT_BYTES = 56 * 1024 * 1024

MOD_ROWS = 16
MOD_COL_BLOCK = 512
TOKEN_TILE = 256
EXPERT_TILE = 256
MOVE_ROWS = 2048
MLSTM_CHUNK = 128
CONV_ROWS = 64
CONV_PAD = 16

_PAIRS = [(i, j) for i in range(EXPERTS_PER_GROUP) for j in range(i + 1, EXPERTS_PER_GROUP)]
N_CLASSES = N_GROUPS * len(_PAIRS)
N_CLASS_ROWS = 32
PAYLOAD_COLS = 1024 + LANES
AUX_CLS, AUX_W_LO, AUX_W_HI, AUX_RANK = 0, 1, 2, 3

F32 = jnp.float32
BF16 = jnp.bfloat16
HIGHEST = lax.Precision.HIGHEST
NEG_INF = float("-inf")

_C_A = 0
_C_Q = _C_A + 2 * D_CONV
_C_QS = _C_Q + D_ATTN
_C_K = _C_QS + D_ATTN
_C_KS = _C_K + D_KV
_C_V = _C_KS + D_KV
_C_MQ = _C_V + D_KV
_C_MV = _C_MQ + D_MLSTM
_C_MO = _C_MV + D_MLSTM
_C_BG = _C_MO + D_MLSTM
_C_END = _C_BG + N_BRANCHES * 1024


def _cparams(*sem):
    return pltpu.CompilerParams(dimension_semantics=sem, vmem_limit_bytes=V7X_VMEM_LIMIT_BYTES)


def _ln(x, g, b):
    mu = jnp.mean(x, axis=-1, keepdims=True)
    xc = x - mu
    var = jnp.mean(xc * xc, axis=-1, keepdims=True)
    return xc * lax.rsqrt(var + LN_EPS) * g + b


def _sigmoid(x):
    return 1.0 / (1.0 + jnp.exp(-x))


def _log_sigmoid(x):
    return jnp.minimum(x, 0.0) - jnp.log(1.0 + jnp.exp(-jnp.abs(x)))


def _dot_nt(a, b, precision=None):
    return lax.dot_general(a, b, (((1,), (1,)), ((), ())), preferred_element_type=F32, precision=precision)


def _mod_kernel(c_ref, w_ref, b_ref, o_ref):
    c = c_ref[...]
    s = c * _sigmoid(c)
    o_ref[...] = jnp.dot(s, w_ref[...], preferred_element_type=F32, precision=HIGHEST) + b_ref[...]


def _modulation(cc, w_mod, b_mod):
    depth, d, n = w_mod.shape
    return pl.pallas_call(
        _mod_kernel,
        grid=(depth, n // MOD_COL_BLOCK),
        in_specs=[
            pl.BlockSpec((MOD_ROWS, d), lambda l, j: (0, 0)),
            pl.BlockSpec((None, d, MOD_COL_BLOCK), lambda l, j: (l, 0, j)),
            pl.BlockSpec((None, 1, MOD_COL_BLOCK), lambda l, j: (l, 0, j)),
        ],
        out_specs=pl.BlockSpec((None, MOD_ROWS, MOD_COL_BLOCK), lambda l, j: (l, 0, j)),
        out_shape=jax.ShapeDtypeStruct((depth, MOD_ROWS, n), F32),
        compiler_params=_cparams("parallel", "parallel"),
        name="modulation",
    )(cc, w_mod, b_mod.reshape(depth, 1, n))


def _mod_spec(which, tile, seq_len, ctx_row):
    tiles_per_seq = seq_len // tile
    if ctx_row is None:
        return pl.BlockSpec((None, 1, 1024), lambda i, *_: ((i // tiles_per_seq) * N_MOD + which, 0, 0))
    return pl.BlockSpec((None, 1, 1024), lambda i, *_: (ctx_row * N_MOD + which, 0, 0))


def _in_kernel(pre_ln, x_ref, lg_ref, lb_ref, sc_ref, sh_ref, w_ref, wt_ref, cq_ref, sq_ref, ck_ref, sk_ref,
               a_ref, q_ref, k_ref, v_ref, mq_ref, mv_ref, mo_ref, bg_ref, kt_ref, gt_ref):
    x = x_ref[...]
    if pre_ln:
        x = _ln(x, lg_ref[...], lb_ref[...])
    u = (x * (1.0 + sc_ref[...]) + sh_ref[...]).astype(BF16)

    def seg(lo, hi):
        return jnp.dot(u, w_ref[:, lo:hi], preferred_element_type=F32)

    a_ref[...] = seg(_C_A, _C_Q).astype(BF16)
    q_ref[...] = (seg(_C_Q, _C_QS) * cq_ref[...] + seg(_C_QS, _C_K) * sq_ref[...]).astype(BF16)
    k_ref[...] = (seg(_C_K, _C_KS) * ck_ref[...] + seg(_C_KS, _C_V) * sk_ref[...]).astype(BF16)
    v_ref[...] = seg(_C_V, _C_MQ).astype(BF16)
    mq_ref[...] = seg(_C_MQ, _C_MV).astype(BF16)
    mv_ref[...] = seg(_C_MV, _C_MO).astype(BF16)
    mo_ref[...] = seg(_C_MO, _C_BG).astype(BF16)
    for j in range(N_BRANCHES):
        bg_ref[:, j * 1024:(j + 1) * 1024] = seg(_C_BG + j * 1024, _C_BG + (j + 1) * 1024).astype(BF16)
    n_chunks = u.shape[0] // MLSTM_CHUNK
    kt = _dot_nt(wt_ref[0:D_MLSTM, :], u)
    for h in range(N_MLSTM_HEADS):
        for c in range(n_chunks):
            kt_ref[h, c] = kt[h * MLSTM_HEAD_DIM:(h + 1) * MLSTM_HEAD_DIM,
                              c * MLSTM_CHUNK:(c + 1) * MLSTM_CHUNK].astype(BF16)
    gt = _dot_nt(wt_ref[D_MLSTM:D_MLSTM + N_GATE_COLS, :], u)
    for h in range(N_MLSTM_HEADS):
        for c in range(n_chunks):
            gt_ref[h, c] = gt[h * 4:(h + 1) * 4, c * MLSTM_CHUNK:(c + 1) * MLSTM_CHUNK]


def _in_proj(x, ln_g, ln_b, mod_l, w_main, w_t, rope, seq_len, ctx_row, pre_ln):
    n = x.shape[0]
    tm = TOKEN_TILE
    cq, sq, ck, sk = rope
    tps = seq_len // tm
    nch = n // MLSTM_CHUNK
    row = lambda i: (i, 0)
    pos = lambda i: (i % tps, 0)
    const = lambda i: (0, 0)
    out_shape = [
        jax.ShapeDtypeStruct((n, 2 * D_CONV), BF16),
        jax.ShapeDtypeStruct((n, D_ATTN), BF16),
        jax.ShapeDtypeStruct((n, D_KV), BF16),
        jax.ShapeDtypeStruct((n, D_KV), BF16),
        jax.ShapeDtypeStruct((n, D_MLSTM), BF16),
        jax.ShapeDtypeStruct((n, D_MLSTM), BF16),
        jax.ShapeDtypeStruct((n, D_MLSTM), BF16),
        jax.ShapeDtypeStruct((n, N_BRANCHES * 1024), BF16),
        jax.ShapeDtypeStruct((N_MLSTM_HEADS, nch, MLSTM_HEAD_DIM, MLSTM_CHUNK), BF16),
        jax.ShapeDtypeStruct((N_MLSTM_HEADS, nch, 4, MLSTM_CHUNK), F32),
    ]
    cpt = tm // MLSTM_CHUNK
    out_specs = [
        pl.BlockSpec((tm, 2 * D_CONV), row),
        pl.BlockSpec((tm, D_ATTN), row),
        pl.BlockSpec((tm, D_KV), row),
        pl.BlockSpec((tm, D_KV), row),
        pl.BlockSpec((tm, D_MLSTM), row),
        pl.BlockSpec((tm, D_MLSTM), row),
        pl.BlockSpec((tm, D_MLSTM), row),
        pl.BlockSpec((tm, N_BRANCHES * 1024), row),
        pl.BlockSpec((N_MLSTM_HEADS, cpt, MLSTM_HEAD_DIM, MLSTM_CHUNK), lambda i: (0, i, 0, 0)),
        pl.BlockSpec((N_MLSTM_HEADS, cpt, 4, MLSTM_CHUNK), lambda i: (0, i, 0, 0)),
    ]
    in_specs = [
        pl.BlockSpec((tm, 1024), row),
        pl.BlockSpec((1, 1024), const),
        pl.BlockSpec((1, 1024), const),
        _mod_spec(1, tm, seq_len, ctx_row),
        _mod_spec(0, tm, seq_len, ctx_row),
        pl.BlockSpec(w_main.shape, const, pipeline_mode=pl.Buffered(1)),
        pl.BlockSpec(w_t.shape, const, pipeline_mode=pl.Buffered(1)),
        pl.BlockSpec((tm, D_ATTN), pos),
        pl.BlockSpec((tm, D_ATTN), pos),
        pl.BlockSpec((tm, D_KV), pos),
        pl.BlockSpec((tm, D_KV), pos),
    ]
    return pl.pallas_call(
        functools.partial(_in_kernel, pre_ln),
        grid=(n // tm,),
        in_specs=in_specs,
        out_specs=out_specs,
        out_shape=out_shape,
        compiler_params=_cparams("parallel"),
        name="in_proj",
    )(x, ln_g, ln_b, mod_l, mod_l, w_main, w_t, cq, sq, ck, sk)


def _conv_kernel(a_ref, w_ref, cb_ref, g_ref, b_ref, o_ref, upad_ref):
    t = a_ref.shape[0]
    zeros = jnp.zeros((CONV_PAD, D_CONV), F32)
    upad_ref[0:CONV_PAD, :] = zeros
    upad_ref[CONV_PAD + t:2 * CONV_PAD + t, :] = zeros
    val = a_ref[:, 0:D_CONV].astype(F32)
    gate = a_ref[:, D_CONV:2 * D_CONV].astype(F32)
    upad_ref[CONV_PAD:CONV_PAD + t, :] = val * _sigmoid(gate)
    half = CONV_WIDTH // 2

    def body(c, carry):
        r0 = pl.multiple_of(c * CONV_ROWS, CONV_ROWS)
        win = upad_ref[pl.ds(r0, CONV_ROWS + 2 * CONV_PAD), :]
        acc = jnp.zeros((CONV_ROWS, D_CONV), F32) + cb_ref[...]
        for k in range(CONV_WIDTH):
            off = CONV_PAD - half + k
            acc = acc + win[off:off + CONV_ROWS, :] * w_ref[k:k + 1, :]
        y = _ln(acc, g_ref[...], b_ref[...])
        o_ref[pl.ds(r0, CONV_ROWS), :] = (y * _sigmoid(y)).astype(BF16)
        return carry

    lax.fori_loop(0, t // CONV_ROWS, body, 0)


def _conv_branch(a_in, conv_w, conv_b, ln_g, ln_b, seq_len):
    n = a_in.shape[0]
    const = lambda b: (0, 0)
    return pl.pallas_call(
        _conv_kernel,
        grid=(n // seq_len,),
        in_specs=[
            pl.BlockSpec((seq_len, 2 * D_CONV), lambda b: (b, 0)),
            pl.BlockSpec((CONV_WIDTH, D_CONV), const),
            pl.BlockSpec((1, D_CONV), const),
            pl.BlockSpec((1, D_CONV), const),
            pl.BlockSpec((1, D_CONV), const),
        ],
        out_specs=pl.BlockSpec((seq_len, D_CONV), lambda b: (b, 0)),
        out_shape=jax.ShapeDtypeStruct((n, D_CONV), BF16),
        scratch_shapes=[pltpu.VMEM((seq_len + 2 * CONV_PAD, D_CONV), F32)],
        compiler_params=_cparams("parallel"),
        name="conv_branch",
    )(a_in, conv_w, conv_b, ln_g, ln_b)


def _attn_heads(q, keys, vals, masks, sink_ref, o_ref):
    rows = q.shape[0]
    group = N_Q_HEADS // N_KV_HEADS
    for hk in range(N_KV_HEADS):
        lo = hk * HEAD_DIM
        qs = jnp.concatenate([q[:, (hk * group + g) * HEAD_DIM:(hk * group + g + 1) * HEAD_DIM]
                              for g in range(group)], axis=0)
        sink = jnp.concatenate([jnp.full((rows, 1), sink_ref[hk * group + g], F32) for g in range(group)], axis=0)
        scores = []
        for kk, mask in zip(keys, masks):
            s = _dot_nt(qs, kk[:, lo:lo + HEAD_DIM])
            if mask is not None:
                s = jnp.where(mask, s, NEG_INF)
            scores.append(s)
        m = sink
        for s in scores:
            m = jnp.maximum(m, jnp.max(s, axis=-1, keepdims=True))
        denom = jnp.exp(sink - m)
        acc = jnp.zeros((rows * group, HEAD_DIM), F32)
        for s, vv in zip(scores, vals):
            e = jnp.exp(s - m)
            denom = denom + jnp.sum(e, axis=-1, keepdims=True)
            acc = acc + jnp.dot(e.astype(BF16), vv[:, lo:lo + HEAD_DIM], preferred_element_type=F32)
        o = acc / denom
        for g in range(group):
            col = (hk * group + g) * HEAD_DIM
            o_ref[:, col:col + HEAD_DIM] = o[g * rows:(g + 1) * rows, :].astype(BF16)


def _attn_lat_kernel(sink_ref, q_ref, kp_ref, k0_ref, kn_ref, vp_ref, v0_ref, vn_ref, kc_ref, vc_ref, o_ref):
    n = pl.program_id(1)
    nb = pl.num_programs(1)
    stacked = (N_Q_HEADS // N_KV_HEADS) * BLOCK
    qi = lax.broadcasted_iota(jnp.int32, (stacked, BLOCK), 0) % BLOCK
    ki = lax.broadcasted_iota(jnp.int32, (stacked, BLOCK), 1)
    mask_prev = ki >= qi + jnp.where(n > 0, 0, BLOCK)
    mask_next = ki <= qi - jnp.where(n < nb - 1, 0, BLOCK)
    _attn_heads(q_ref[...],
                [kp_ref[...], k0_ref[...], kn_ref[...], kc_ref[...]],
                [vp_ref[...], v0_ref[...], vn_ref[...], vc_ref[...]],
                [mask_prev, None, mask_next, None], sink_ref, o_ref)


def _attn_ctx_kernel(sink_ref, q_ref, kc_ref, vc_ref, o_ref):
    _attn_heads(q_ref[...], [kc_ref[...]], [vc_ref[...]], [None], sink_ref, o_ref)


def _attn_latent(q, k, v, kc, vc, sink, seq_len, ctx_len):
    n = q.shape[0]
    nb = seq_len // BLOCK
    batch = n // seq_len
    cur = lambda b, j: (b * nb + j, 0)
    prev = lambda b, j: (b * nb + jnp.maximum(j - 1, 0), 0)
    nxt = lambda b, j: (b * nb + jnp.minimum(j + 1, nb - 1), 0)
    ctx = lambda b, j: (b, 0)
    kv = lambda im: pl.BlockSpec((BLOCK, D_KV), im)
    return pl.pallas_call(
        _attn_lat_kernel,
        grid=(batch, nb),
        in_specs=[
            pl.BlockSpec(memory_space=pltpu.SMEM),
            pl.BlockSpec((BLOCK, D_ATTN), cur),
            kv(prev), kv(cur), kv(nxt), kv(prev), kv(cur), kv(nxt),
            pl.BlockSpec((ctx_len, D_KV), ctx),
            pl.BlockSpec((ctx_len, D_KV), ctx),
        ],
        out_specs=pl.BlockSpec((BLOCK, D_ATTN), cur),
        out_shape=jax.ShapeDtypeStruct((n, D_ATTN), BF16),
        compiler_params=_cparams("parallel", "parallel"),
        name="attn_latent",
    )(sink, q, k, k, k, v, v, v, kc, vc)


def _attn_context(qc, kc, vc, sink, ctx_len):
    n = qc.shape[0]
    blk = lambda b: (b, 0)
    return pl.pallas_call(
        _attn_ctx_kernel,
        grid=(n // ctx_len,),
        in_specs=[
            pl.BlockSpec(memory_space=pltpu.SMEM),
            pl.BlockSpec((ctx_len, D_ATTN), blk),
            pl.BlockSpec((ctx_len, D_KV), blk),
            pl.BlockSpec((ctx_len, D_KV), blk),
        ],
        out_specs=pl.BlockSpec((ctx_len, D_ATTN), blk),
        out_shape=jax.ShapeDtypeStruct((n, D_ATTN), BF16),
        compiler_params=_cparams("parallel"),
        name="attn_context",
    )(sink, qc, kc, vc)


def _mlstm_kernel(ctx_out, nch_c, nch_l,
                  qc_ref, ktc_ref, vc_ref, moc_ref, gc_ref,
                  ql_ref, ktl_ref, vl_ref, mol_ref, gl_ref,
                  gb_ref, ng_ref, *rest):
    if ctx_out:
        yl_ref, yc_ref, af_ref, lff_ref, ab_ref, lfb_ref, cf_ref, cb_ref, hf_ref, hb_ref = rest
    else:
        yl_ref, af_ref, lff_ref, ab_ref, lfb_ref, cf_ref, cb_ref, hf_ref, hb_ref = rest
        yc_ref = None
    lc = MLSTM_CHUNK
    dh = MLSTM_HEAD_DIM
    k_scale = MLSTM_HEAD_DIM ** -0.5
    ti = lax.broadcasted_iota(jnp.int32, (lc, lc), 0)
    si = lax.broadcasted_iota(jnp.int32, (lc, lc), 1)
    lower = si <= ti
    upper = si >= ti
    pre_mat = jnp.where(upper, 1.0, 0.0)
    suf_mat = jnp.where(lower, 1.0, 0.0)
    gb = gb_ref[...]
    n_rows = nch_c + nch_l
    pad_rows = -n_rows % 8

    def gate_rows(kind):
        rows = [gc_ref[c, kind:kind + 1, :] for c in range(nch_c)] + [gl_ref[c, kind:kind + 1, :] for c in range(nch_l)]
        rows = jnp.concatenate(rows, axis=0) + gb[kind:kind + 1, :]
        if pad_rows:
            rows = jnp.concatenate([rows, jnp.zeros((pad_rows, lc), F32)], axis=0)
        return rows

    lf_f = _log_sigmoid(gate_rows(1))
    lf_b = _log_sigmoid(gate_rows(3))
    a_f = gate_rows(0) - jnp.dot(lf_f, pre_mat, preferred_element_type=F32, precision=HIGHEST)
    a_b = gate_rows(2) - jnp.dot(lf_b, suf_mat, preferred_element_type=F32, precision=HIGHEST)
    for c in range(n_rows):
        af_ref[c] = a_f[c:c + 1, :]
        lff_ref[c] = lf_f[c:c + 1, :]
        ab_ref[c] = a_b[c:c + 1, :]
        lfb_ref[c] = lf_b[c:c + 1, :]

    cf_ref[...] = jnp.zeros_like(cf_ref)
    cb_ref[...] = jnp.zeros_like(cb_ref)
    ones_col = jnp.where(lax.broadcasted_iota(jnp.int32, (lc, dh), 1) == 0, 1.0, 0.0).astype(BF16)

    def chunk(q, kt, v, a_row, lf_row, c_ref, m, mask):
        a_mat = jnp.where(mask, jnp.broadcast_to(a_row, (lc, lc)), NEG_INF)
        cm = jnp.max(a_mat, axis=1, keepdims=True)
        b_col = jnp.sum(jnp.where(mask, jnp.broadcast_to(lf_row, (lc, lc)), 0.0), axis=1, keepdims=True)
        mx = jnp.maximum(m, cm)
        mx_last = jnp.maximum(m, jnp.max(a_row, axis=1, keepdims=True))
        w = jnp.exp(a_mat - mx)
        s = jnp.dot(q, kt, preferred_element_type=F32) * k_scale
        p = (s * w).astype(BF16)
        w_s = jnp.exp(a_row - mx_last) * k_scale
        ktw = (kt.astype(F32) * w_s).astype(BF16)
        v_aug = jnp.concatenate([v, ones_col], axis=1)
        both = jnp.dot(jnp.concatenate([p, ktw], axis=0), v_aug, preferred_element_type=F32)
        c_old = c_ref[...]
        inter = jnp.dot(q, c_old.astype(BF16), preferred_element_type=F32)
        tot = both[0:lc, :] + jnp.exp(m - mx) * inter
        den = tot[:, dh:dh + 1]
        h = tot[:, 0:dh] / jnp.maximum(jnp.abs(den), jnp.exp(-(b_col + mx)))
        c_ref[...] = jnp.exp(m - mx_last) * c_old + both[lc:lc + dh, :]
        m_new = jnp.sum(lf_row, axis=1, keepdims=True) + mx_last
        return h, m_new

    m_f = jnp.zeros((1, 1), F32)
    m_b = jnp.zeros((1, 1), F32)
    for c in range(nch_c):
        cb_i = nch_c - 1 - c
        h_f, m_f = chunk(qc_ref[c * lc:(c + 1) * lc, :], ktc_ref[c], vc_ref[c * lc:(c + 1) * lc, :],
                         af_ref[c], lff_ref[c], cf_ref, m_f, lower)
        h_b, m_b = chunk(qc_ref[cb_i * lc:(cb_i + 1) * lc, :], ktc_ref[cb_i], vc_ref[cb_i * lc:(cb_i + 1) * lc, :],
                         ab_ref[cb_i], lfb_ref[cb_i], cb_ref, m_b, upper)
        if ctx_out:
            hf_ref[c * lc:(c + 1) * lc, :] = h_f
            hb_ref[cb_i * lc:(cb_i + 1) * lc, :] = h_b

    def finish(h, mo, o_ref):
        mu = jnp.mean(h, axis=-1, keepdims=True)
        hc = h - mu
        var = jnp.mean(hc * hc, axis=-1, keepdims=True)
        y = hc * lax.rsqrt(var + LN_EPS) * ng_ref[...]
        o_ref[...] = (_sigmoid(mo.astype(F32)) * y).astype(BF16)

    if ctx_out:
        n_c = nch_c * lc
        finish(hf_ref[0:n_c, :] + hb_ref[0:n_c, :], moc_ref[...], yc_ref)

    def body(c, carry):
        m_f, m_b = carry
        cb_i = nch_l - 1 - c
        rf = pl.multiple_of(c * lc, lc)
        rb = pl.multiple_of(cb_i * lc, lc)
        h_f, m_f = chunk(ql_ref[pl.ds(rf, lc), :], ktl_ref[c], vl_ref[pl.ds(rf, lc), :],
                         af_ref[nch_c + c], lff_ref[nch_c + c], cf_ref, m_f, lower)
        h_b, m_b = chunk(ql_ref[pl.ds(rb, lc), :], ktl_ref[cb_i], vl_ref[pl.ds(rb, lc), :],
                         ab_ref[nch_c + cb_i], lfb_ref[nch_c + cb_i], cb_ref, m_b, upper)
        hf_ref[pl.ds(rf, lc), :] = h_f
        hb_ref[pl.ds(rb, lc), :] = h_b
        return m_f, m_b

    lax.fori_loop(0, nch_l, body, (m_f, m_b))
    finish(hf_ref[...] + hb_ref[...], mol_ref[...], yl_ref)


def _mlstm_branch(ctx_p, lat_p, gate_b, norm_g, seq_len, ctx_len, ctx_out):
    mq_c, kt_c, mv_c, mo_c, g_c = ctx_p
    mq_l, kt_l, mv_l, mo_l, g_l = lat_p
    n_l, n_c = mq_l.shape[0], mq_c.shape[0]
    batch = n_l // seq_len
    lc, dh = MLSTM_CHUNK, MLSTM_HEAD_DIM
    nch_c, nch_l = ctx_len // lc, seq_len // lc

    def stream(t, nch):
        tok = pl.BlockSpec((t, dh), lambda b, h: (b, h))
        return [tok,
                pl.BlockSpec((None, nch, dh, lc), lambda b, h: (h, b, 0, 0)),
                tok, tok,
                pl.BlockSpec((None, nch, 4, lc), lambda b, h: (h, b, 0, 0))]

    in_specs = stream(ctx_len, nch_c) + stream(seq_len, nch_l) + [
        pl.BlockSpec((None, 4, 1), lambda b, h: (h, 0, 0)),
        pl.BlockSpec((1, dh), lambda b, h: (0, h)),
    ]
    out_specs = [pl.BlockSpec((seq_len, dh), lambda b, h: (b, h))]
    out_shape = [jax.ShapeDtypeStruct((n_l, D_MLSTM), BF16)]
    if ctx_out:
        out_specs.append(pl.BlockSpec((ctx_len, dh), lambda b, h: (b, h)))
        out_shape.append(jax.ShapeDtypeStruct((n_c, D_MLSTM), BF16))
    row_scratch = pltpu.VMEM((nch_c + nch_l, 1, lc), F32)
    outs = pl.pallas_call(
        functools.partial(_mlstm_kernel, ctx_out, nch_c, nch_l),
        grid=(batch, N_MLSTM_HEADS),
        in_specs=in_specs,
        out_specs=out_specs,
        out_shape=out_shape,
        scratch_shapes=[row_scratch, row_scratch, row_scratch, row_scratch,
                        pltpu.VMEM((dh, 2 * dh), F32), pltpu.VMEM((dh, 2 * dh), F32),
                        pltpu.VMEM((seq_len, dh), F32), pltpu.VMEM((seq_len, dh), F32)],
        compiler_params=_cparams("parallel", "parallel"),
        name="mlstm_branch",
    )(mq_c, kt_c, mv_c, mo_c, g_c, mq_l, kt_l, mv_l, mo_l, g_l, gate_b, norm_g)
    return outs if ctx_out else (outs[0], None)


def _route(logits_t, br):
    sc = [_sigmoid(logits_t[e:e + 1, :]) for e in range(N_EXPERTS)]
    sel = [sc[e] + br[e:e + 1, :] for e in range(N_EXPERTS)]
    epg = EXPERTS_PER_GROUP
    group_score = []
    for g in range(N_GROUPS):
        v = sel[g * epg:(g + 1) * epg]
        best = None
        for i in range(epg):
            for j in range(i + 1, epg):
                pair = v[i] + v[j]
                best = pair if best is None else jnp.maximum(best, pair)
        group_score.append(best)
    g_idx = jnp.zeros_like(group_score[0], dtype=jnp.int32)
    best = group_score[0]
    for g in range(1, N_GROUPS):
        better = group_score[g] > best
        g_idx = jnp.where(better, g, g_idx)
        best = jnp.maximum(best, group_score[g])
    chosen = []
    for g in range(N_GROUPS):
        v = sel[g * epg:(g + 1) * epg]
        in_g = g_idx == g
        for i in range(epg):
            rank = jnp.zeros_like(g_idx)
            for j in range(epg):
                if j == i:
                    continue
                ahead = (v[j] >= v[i]) if j < i else (v[j] > v[i])
                rank = rank + jnp.where(ahead, 1, 0)
            chosen.append(in_g & (rank < 2))
    cls = jnp.zeros_like(sc[0])
    w_lo = jnp.zeros_like(sc[0])
    w_hi = jnp.zeros_like(sc[0])
    for g in range(N_GROUPS):
        for pid, (i, j) in enumerate(_PAIRS):
            lo, hi = g * epg + i, g * epg + j
            is_pair = chosen[lo] & chosen[hi]
            cls = jnp.where(is_pair, float(g * len(_PAIRS) + pid), cls)
            w_lo = jnp.where(is_pair, sc[lo], w_lo)
            w_hi = jnp.where(is_pair, sc[hi], w_hi)
    total = w_lo + w_hi
    return cls, w_lo / total, w_hi / total


def _merge_kernel(pre_ln, alpha, ya_ref, yb_ref, yc_ref, bg_ref, h_ref, g1_ref, sc2_ref, sh2_ref,
                  lig_ref, lib_ref, l1g_ref, l1b_ref, wa_ref, wb_ref, wc_ref, wo_ref, wr_ref, br_ref,
                  h1_ref, pay_ref, route_ref, cnt_ref):
    tm = h_ref.shape[0]

    def branch(y_ref, w_ref, j):
        gate = _sigmoid(bg_ref[:, j * 1024:(j + 1) * 1024].astype(F32))
        return gate * jnp.dot(y_ref[...], w_ref[...], preferred_element_type=F32)

    mix = branch(ya_ref, wa_ref, 0) + branch(yb_ref, wb_ref, 1) + branch(yc_ref, wc_ref, 2)
    y = jnp.dot(mix.astype(BF16), wo_ref[...], preferred_element_type=F32)
    h = h_ref[...]
    if pre_ln:
        h = _ln(h, lig_ref[...], lib_ref[...])
    h1 = _ln(alpha * h + g1_ref[...] * y, l1g_ref[...], l1b_ref[...])
    h1_ref[...] = h1
    u2 = h1 * (1.0 + sc2_ref[...]) + sh2_ref[...]
    logits_t = _dot_nt(wr_ref[...], u2, precision=HIGHEST)
    cls, w_lo, w_hi = _route(logits_t, br_ref[...])

    @pl.when(pl.program_id(0) == 0)
    def _():
        cnt_ref[...] = jnp.zeros_like(cnt_ref)

    crow = lax.broadcasted_iota(jnp.int32, (N_CLASS_ROWS, tm), 0).astype(F32)
    onehot = jnp.where(crow == cls, 1.0, 0.0)
    earlier = lax.broadcasted_iota(jnp.int32, (tm, tm), 0) <= lax.broadcasted_iota(jnp.int32, (tm, tm), 1)
    incl = jnp.dot(onehot.astype(BF16), jnp.where(earlier, 1.0, 0.0).astype(BF16), preferred_element_type=F32)
    base = cnt_ref[...]
    rank = jnp.sum(onehot * (incl - 1.0 + base), axis=0, keepdims=True)
    cnt_ref[...] = base + incl[:, tm - 1:tm]

    rows = jnp.concatenate([cls, w_lo, w_hi, rank, jnp.zeros((4, tm), F32)], axis=0)
    route_ref[...] = rows
    aux = jnp.concatenate([rows, jnp.zeros((LANES - 8, tm), F32)], axis=0).T
    pay_ref[:, 0:1024] = u2
    pay_ref[:, 1024:1024 + LANES] = aux


def _merge(ya, yb, yc, bg, h, mod_l, ln_in, ln1, w_a, w_b, w_c, w_o, w_rt, b_r, seq_len, ctx_row, pre_ln, alpha):
    n = ya.shape[0]
    tm = TOKEN_TILE
    row = lambda i: (i, 0)
    const = lambda i: (0, 0)
    vec = pl.BlockSpec((1, 1024), const)
    wspec = lambda w: pl.BlockSpec(w.shape, const)
    return pl.pallas_call(
        functools.partial(_merge_kernel, pre_ln, alpha),
        grid=(n // tm,),
        in_specs=[
            pl.BlockSpec((tm, D_CONV), row),
            pl.BlockSpec((tm, D_ATTN), row),
            pl.BlockSpec((tm, D_MLSTM), row),
            pl.BlockSpec((tm, N_BRANCHES * 1024), row),
            pl.BlockSpec((tm, 1024), row),
            _mod_spec(2, tm, seq_len, ctx_row),
            _mod_spec(4, tm, seq_len, ctx_row),
            _mod_spec(3, tm, seq_len, ctx_row),
            vec, vec, vec, vec,
            wspec(w_a), wspec(w_b), wspec(w_c), wspec(w_o), wspec(w_rt), wspec(b_r),
        ],
        out_specs=[
            pl.BlockSpec((tm, 1024), row),
            pl.BlockSpec((tm, PAYLOAD_COLS), row),
            pl.BlockSpec((8, tm), lambda i: (0, i)),
        ],
        out_shape=[
            jax.ShapeDtypeStruct((n, 1024), F32),
            jax.ShapeDtypeStruct((n, PAYLOAD_COLS), F32),
            jax.ShapeDtypeStruct((8, n), F32),
        ],
        scratch_shapes=[pltpu.VMEM((N_CLASS_ROWS, 1), F32)],
        compiler_params=_cparams("arbitrary"),
        name="merge",
    )(ya, yb, yc, bg, h, mod_l, mod_l, mod_l, ln_in[0], ln_in[1], ln1[0], ln1[1], w_a, w_b, w_c, w_o, w_rt, b_r)


def _move_rows_kernel(scatter, pos_ref, src_ref, *rest):
    dst_ref, sem = rest[-2], rest[-1]
    rows = pos_ref.shape[-1]
    base = pl.program_id(0) * rows

    def copy(k):
        t = base + k
        p = pos_ref[0, k]
        if scatter:
            return pltpu.make_async_copy(src_ref.at[pl.ds(t, 1)], dst_ref.at[pl.ds(p, 1)], sem)
        return pltpu.make_async_copy(src_ref.at[pl.ds(p, 1)], dst_ref.at[pl.ds(t, 1)], sem)

    def start(k, carry):
        copy(k).start()
        return carry

    def wait(k, carry):
        copy(k).wait()
        return carry

    lax.fori_loop(0, rows, start, 0)
    lax.fori_loop(0, rows, wait, 0)


def _move_rows(pos, src, n_out, scatter):
    n = pos.shape[0]
    rows = min(MOVE_ROWS, n)
    cols = src.shape[1]
    any_spec = pl.BlockSpec(memory_space=pl.ANY)
    in_specs = [pl.BlockSpec((None, 1, rows), lambda i: (i, 0, 0), memory_space=pltpu.SMEM), any_spec]
    args = [pos.reshape(n // rows, 1, rows), src]
    aliases = {}
    if scatter:
        in_specs.append(any_spec)
        args.append(jnp.zeros((n_out, cols), src.dtype))
        aliases = {2: 0}
    return pl.pallas_call(
        functools.partial(_move_rows_kernel, scatter),
        grid=(n // rows,),
        in_specs=in_specs,
        out_specs=any_spec,
        out_shape=jax.ShapeDtypeStruct((n_out, cols), src.dtype),
        input_output_aliases=aliases,
        scratch_shapes=[pltpu.SemaphoreType.DMA],
        compiler_params=_cparams("arbitrary"),
        name="scatter_rows" if scatter else "gather_rows",
    )(*args)


def _experts_kernel(lo_ref, hi_ref, valid_ref, x_ref, wgl_ref, wgh_ref, wdl_ref, wdh_ref, o_ref):
    i = pl.program_id(0)

    @pl.when(valid_ref[i] != 0)
    def _():
        x = x_ref[:, 0:1024].astype(BF16)

        def expert(wg_ref, wd_ref):
            gu = jnp.dot(x, wg_ref[...], preferred_element_type=F32)
            g_ = gu[:, 0:D_EXPERT]
            act = (g_ * _sigmoid(g_) * gu[:, D_EXPERT:2 * D_EXPERT]).astype(BF16)
            return jnp.dot(act, wd_ref[...], preferred_element_type=F32)

        w_lo = x_ref[:, 1024 + AUX_W_LO:1024 + AUX_W_LO + 1]
        w_hi = x_ref[:, 1024 + AUX_W_HI:1024 + AUX_W_HI + 1]
        o_ref[...] = w_lo * expert(wgl_ref, wdl_ref) + w_hi * expert(wgh_ref, wdh_ref)

    @pl.when(valid_ref[i] == 0)
    def _():
        o_ref[...] = jnp.zeros_like(o_ref)


def _experts(lo, hi, valid, xs, w_gu, w_dn):
    tm = EXPERT_TILE
    n_tiles = xs.shape[0] // tm
    row = lambda i, lo, hi, valid: (i, 0)
    grid_spec = pltpu.PrefetchScalarGridSpec(
        num_scalar_prefetch=3,
        grid=(n_tiles,),
        in_specs=[
            pl.BlockSpec((tm, PAYLOAD_COLS), row),
            pl.BlockSpec((None, 1024, 2 * D_EXPERT), lambda i, lo, hi, valid: (lo[i], 0, 0)),
            pl.BlockSpec((None, 1024, 2 * D_EXPERT), lambda i, lo, hi, valid: (hi[i], 0, 0)),
            pl.BlockSpec((None, D_EXPERT, 1024), lambda i, lo, hi, valid: (lo[i], 0, 0)),
            pl.BlockSpec((None, D_EXPERT, 1024), lambda i, lo, hi, valid: (hi[i], 0, 0)),
        ],
        out_specs=pl.BlockSpec((tm, 1024), row),
    )
    return pl.pallas_call(
        _experts_kernel,
        grid_spec=grid_spec,
        out_shape=jax.ShapeDtypeStruct((n_tiles * tm, 1024), F32),
        compiler_params=_cparams("arbitrary"),
        name="moe_experts",
    )(lo, hi, valid, xs, w_gu, w_gu, w_dn, w_dn)


def _final_kernel(alpha, h1_ref, f_ref, g2_ref, lg_ref, lb_ref, o_ref):
    o_ref[...] = _ln(alpha * h1_ref[...] + g2_ref[...] * f_ref[...], lg_ref[...], lb_ref[...])


def _final_ln(h1, f, mod_l, ln2, seq_len, ctx_row, alpha):
    n = h1.shape[0]
    tm = TOKEN_TILE
    row = lambda i: (i, 0)
    const = lambda i: (0, 0)
    return pl.pallas_call(
        functools.partial(_final_kernel, alpha),
        grid=(n // tm,),
        in_specs=[
            pl.BlockSpec((tm, 1024), row),
            pl.BlockSpec((tm, 1024), row),
            _mod_spec(5, tm, seq_len, ctx_row),
            pl.BlockSpec((1, 1024), const),
            pl.BlockSpec((1, 1024), const),
        ],
        out_specs=pl.BlockSpec((tm, 1024), row),
        out_shape=jax.ShapeDtypeStruct((n, 1024), F32),
        compiler_params=_cparams("parallel"),
        name="final_ln",
    )(h1, f, mod_l, ln2[0], ln2[1])


def _sort_plan(route_t, n):
    tm = EXPERT_TILE
    n_tiles = n // tm + N_CLASSES
    cls = route_t[AUX_CLS].astype(jnp.int32)
    rank = route_t[AUX_RANK].astype(jnp.int32)
    onehot = cls[:, None] == jnp.arange(N_CLASSES, dtype=jnp.int32)[None, :]
    counts = jnp.sum(onehot, axis=0, dtype=jnp.int32)
    padded = (counts + tm - 1) // tm * tm
    ends = jnp.cumsum(padded)
    offs = ends - padded
    pos = jnp.sum(jnp.where(onehot, offs[None, :], 0), axis=1) + rank
    tile_ends = ends // tm
    j = jnp.arange(n_tiles, dtype=jnp.int32)
    n_used = tile_ends[-1]
    valid = j < n_used
    tile_cls = jnp.sum(j[:, None] >= tile_ends[None, :], axis=1)
    last_cls = jnp.sum((n_used - 1) >= tile_ends)
    tile_cls = jnp.where(valid, tile_cls, last_cls)
    group, pid = tile_cls // len(_PAIRS), tile_cls % len(_PAIRS)
    pair = jnp.asarray(np.array(_PAIRS, dtype=np.int32))
    lo = group * EXPERTS_PER_GROUP + pair[pid, 0]
    hi = group * EXPERTS_PER_GROUP + pair[pid, 1]
    return pos.astype(jnp.int32), lo.astype(jnp.int32), hi.astype(jnp.int32), valid.astype(jnp.int32), n_tiles * tm


def _moe(payload, route_t, h1, mod_l, ln2, w_gu, w_dn, seq_len, ctx_row, alpha):
    n = h1.shape[0]
    pos, lo, hi, valid, n_rows = _sort_plan(route_t, n)
    xs = _move_rows(pos, payload, n_rows, scatter=True)
    fs = _experts(lo, hi, valid, xs, w_gu, w_dn)
    f = _move_rows(pos, fs, n, scatter=False)
    return _final_ln(h1, f, mod_l, ln2, seq_len, ctx_row, alpha)


def _rope_swap_index(n_heads):
    idx = np.arange(n_heads * HEAD_DIM)
    within = idx % (HEAD_DIM // 2)
    quarter = HEAD_DIM // 4
    return np.where(within < quarter, idx + quarter, idx - quarter)


def _rope_tables(seq_len, n_heads, scale):
    t = np.arange(seq_len)
    quarter = HEAD_DIM // 4
    inv = ROPE_BASE ** (-np.arange(quarter, dtype=np.float32) / quarter)
    d = np.arange(HEAD_DIM)
    pos = np.where((d // (HEAD_DIM // 2) == 0)[None, :], (t // GRID_W)[:, None], (t % GRID_W)[:, None])
    ang = jnp.asarray(pos.astype(np.float32)) * jnp.asarray(inv[d % quarter])[None, :]
    sign = np.where(d % (HEAD_DIM // 2) < quarter, -1.0, 1.0).astype(np.float32)
    cos = jnp.cos(ang) * scale
    sin = jnp.sin(ang) * (sign * scale)[None, :]
    return jnp.tile(cos, (1, n_heads)), jnp.tile(sin, (1, n_heads))


def _flat_tables(seq_len, n_heads, scale):
    return (jnp.full((seq_len, n_heads * HEAD_DIM), scale, F32), jnp.zeros((seq_len, n_heads * HEAD_DIM), F32))


def _prep_in_weights(w_in_l):
    splits = np.cumsum([2 * D_CONV, D_ATTN, D_KV, D_KV, D_MLSTM, D_MLSTM, D_MLSTM, D_MLSTM, N_GATE_COLS])
    a, q, k, v, mq, mk, mv, mo, mg, bg = jnp.split(w_in_l, splits, axis=1)
    w_main = jnp.concatenate([a, q, q[:, _rope_swap_index(N_Q_HEADS)], k, k[:, _rope_swap_index(N_KV_HEADS)],
                              v, mq, mv, mo, bg], axis=1).astype(BF16)
    order = np.array([d * 8 + kind * 4 + h for h in range(N_MLSTM_HEADS) for d in range(2) for kind in range(2)])
    w_t = jnp.concatenate([mk.T, mg[:, order].T], axis=0).astype(BF16)
    return w_main, w_t


def kernel(x, c, ctx, c_ctx, ln_in_g, ln_in_b, w_router, b_router, w_mod, b_mod, w_in, conv_w, conv_b, conv_ln_g,
           conv_ln_b, w_a_out, attn_sink, w_b_out, mlstm_gate_b, mlstm_norm_g, w_c_out, w_out, ln1_g, ln1_b,
           moe_w_gu, moe_w_dn, ln2_g, ln2_b):
    batch, seq_len, d = x.shape
    ctx_len = ctx.shape[1]
    depth = w_in.shape[0]
    alpha = (2.0 * depth) ** 0.25
    ctx_row = batch
    assert d == 1024 and batch < MOD_ROWS
    assert seq_len % TOKEN_TILE == 0 and ctx_len % TOKEN_TILE == 0
    assert all(n % min(MOVE_ROWS, n) == 0 and n % EXPERT_TILE == 0 for n in (batch * seq_len, batch * ctx_len))

    cc = jnp.zeros((MOD_ROWS, d), F32).at[0:batch].set(c).at[batch].set(c_ctx)
    mod = _modulation(cc, w_mod, b_mod).reshape(depth, MOD_ROWS * N_MOD, 1, d)

    attn_scale = HEAD_DIM ** -0.5
    rope_lat = _rope_tables(seq_len, N_Q_HEADS, attn_scale) + _rope_tables(seq_len, N_KV_HEADS, 1.0)
    rope_ctx = _flat_tables(ctx_len, N_Q_HEADS, attn_scale) + _flat_tables(ctx_len, N_KV_HEADS, 1.0)

    vec = lambda t: t.reshape(1, -1)
    ln_in = (vec(ln_in_g), vec(ln_in_b))
    w_rt = w_router.T
    b_r = b_router.reshape(N_EXPERTS, 1)

    h = x.reshape(batch * seq_len, d)
    hc = ctx.reshape(batch * ctx_len, d)
    for l in range(depth):
        need_ctx = l < depth - 1
        pre_ln = l == 0
        mod_l = mod[l]
        w_main, w_t = _prep_in_weights(w_in[l])
        lat = _in_proj(h, ln_in[0], ln_in[1], mod_l, w_main, w_t, rope_lat, seq_len, None, pre_ln)
        cx = _in_proj(hc, ln_in[0], ln_in[1], mod_l, w_main, w_t, rope_ctx, ctx_len, ctx_row, pre_ln)
        a_l, q_l, k_l, v_l, mq_l, mv_l, mo_l, bg_l, kt_l, gt_l = lat
        a_c, q_c, k_c, v_c, mq_c, mv_c, mo_c, bg_c, kt_c, gt_c = cx

        conv_args = (conv_w[l], vec(conv_b[l]), vec(conv_ln_g[l]), vec(conv_ln_b[l]))
        gate_b = jnp.transpose(mlstm_gate_b[l], (2, 0, 1)).reshape(N_MLSTM_HEADS, 4, 1)
        ya = _conv_branch(a_l, *conv_args, seq_len)
        yb = _attn_latent(q_l, k_l, v_l, k_c, v_c, attn_sink[l], seq_len, ctx_len)
        yc, yc_c = _mlstm_branch((mq_c, kt_c, mv_c, mo_c, gt_c), (mq_l, kt_l, mv_l, mo_l, gt_l),
                                 gate_b, vec(mlstm_norm_g[l]), seq_len, ctx_len, need_ctx)

        ln1 = (vec(ln1_g[l]), vec(ln1_b[l]))
        ln2 = (vec(ln2_g[l]), vec(ln2_b[l]))
        w_a, w_b, w_c, w_o = (w.astype(BF16) for w in (w_a_out[l], w_b_out[l], w_c_out[l], w_out[l]))
        w_gu = moe_w_gu[l].astype(BF16)
        w_dn = moe_w_dn[l].astype(BF16)

        h1, pay, route_t = _merge(ya, yb, yc, bg_l, h, mod_l, ln_in, ln1, w_a, w_b, w_c, w_o, w_rt, b_r,
                                  seq_len, None, pre_ln, alpha)
        h = _moe(pay, route_t, h1, mod_l, ln2, w_gu, w_dn, seq_len, None, alpha)
        if need_ctx:
            ya_c = _conv_branch(a_c, *conv_args, ctx_len)
            yb_c = _attn_context(q_c, k_c, v_c, attn_sink[l], ctx_len)
            h1c, pay_c, route_tc = _merge(ya_c, yb_c, yc_c, bg_c, hc, mod_l, ln_in, ln1, w_a, w_b, w_c, w_o, w_rt,
                                          b_r, ctx_len, ctx_row, pre_ln, alpha)
            hc = _moe(pay_c, route_tc, h1c, mod_l, ln2, w_gu, w_dn, ctx_len, ctx_row, alpha)
    return h.reshape(batch, seq_len, d)
```

```python
import functools

import numpy as np
import jax
import jax.numpy as jnp
from jax import lax
from jax.experimental import pallas as pl
from jax.experimental.pallas import tpu as pltpu

GRID_W = 64
LN_EPS = 1e-5
D_CONV = 512
CONV_WIDTH = 31
N_Q_HEADS = 8
N_KV_HEADS = 2
HEAD_DIM = 64
WINDOW = 128
BLOCK = 128
ROPE_BASE = 10000.0
D_ATTN = N_Q_HEADS * HEAD_DIM
D_KV = N_KV_HEADS * HEAD_DIM
N_MLSTM_HEADS = 4
MLSTM_HEAD_DIM = 128
D_MLSTM = N_MLSTM_HEADS * MLSTM_HEAD_DIM
N_GATE_COLS = 2 * 2 * N_MLSTM_HEADS
N_BRANCHES = 3
N_EXPERTS = 16
N_GROUPS = 4
EXPERTS_PER_GROUP = N_EXPERTS // N_GROUPS
D_EXPERT = 512
N_MOD = 6

LANES = 128
V7X_VMEM_LIMIT_BYTES = 56 * 1024 * 1024

MOD_ROWS = 16
MOD_COL_BLOCK = 512
TOKEN_TILE = 256
EXPERT_TILE = 256
MOVE_ROWS = 2048
MLSTM_CHUNK = 128
MLSTM_HEADS_PER_STEP = 4
MERGE_TILE = 512
CONV_ROWS = 64
CONV_PAD = 16

_PAIRS = [(i, j) for i in range(EXPERTS_PER_GROUP) for j in range(i + 1, EXPERTS_PER_GROUP)]
N_CLASSES = N_GROUPS * len(_PAIRS)
N_CLASS_ROWS = 32
AUX_CLS, AUX_W_LO, AUX_W_HI, AUX_RANK = 0, 1, 2, 3
TOKEN_SUBROWS = 1024 // LANES

F32 = jnp.float32
BF16 = jnp.bfloat16
HIGHEST = lax.Precision.HIGHEST
NEG_INF = float("-inf")

_C_A = 0
_C_Q = _C_A + 2 * D_CONV
_C_QS = _C_Q + D_ATTN
_C_K = _C_QS + D_ATTN
_C_KS = _C_K + D_KV
_C_MQ = _C_KS + D_KV
_C_MV = _C_MQ + D_MLSTM
_C_MO = _C_MV + D_MLSTM
_C_BG = _C_MO + D_MLSTM
_C_END = _C_BG + N_BRANCHES * 1024
_R_KT = 0
_R_GT = _R_KT + D_MLSTM
_R_VT = _R_GT + N_GATE_COLS


def _cparams(*sem):
    return pltpu.CompilerParams(dimension_semantics=sem, vmem_limit_bytes=V7X_VMEM_LIMIT_BYTES)


def _ln(x, g, b):
    mu = jnp.mean(x, axis=-1, keepdims=True)
    xc = x - mu
    var = jnp.mean(xc * xc, axis=-1, keepdims=True)
    return xc * lax.rsqrt(var + LN_EPS) * g + b


def _sigmoid(x):
    return 0.5 * jnp.tanh(0.5 * x) + 0.5


def _log_sigmoid(x):
    return jnp.minimum(x, 0.0) - jnp.log(1.0 + jnp.exp(-jnp.abs(x)))


def _dot_nt(a, b, precision=None):
    return lax.dot_general(a, b, (((1,), (1,)), ((), ())), preferred_element_type=F32, precision=precision)


def _mod_kernel(c_ref, w_ref, b_ref, o_ref):
    c = c_ref[...]
    s = c * _sigmoid(c)
    o_ref[...] = jnp.dot(s, w_ref[...], preferred_element_type=F32, precision=HIGHEST) + b_ref[...]


def _modulation(cc, w_mod, b_mod):
    depth, d, n = w_mod.shape
    return pl.pallas_call(
        _mod_kernel,
        grid=(depth, n // MOD_COL_BLOCK),
        in_specs=[
            pl.BlockSpec((MOD_ROWS, d), lambda l, j: (0, 0)),
            pl.BlockSpec((None, d, MOD_COL_BLOCK), lambda l, j: (l, 0, j)),
            pl.BlockSpec((None, 1, MOD_COL_BLOCK), lambda l, j: (l, 0, j)),
        ],
        out_specs=pl.BlockSpec((None, MOD_ROWS, MOD_COL_BLOCK), lambda l, j: (l, 0, j)),
        out_shape=jax.ShapeDtypeStruct((depth, MOD_ROWS, n), F32),
        compiler_params=_cparams("parallel", "parallel"),
        name="modulation",
    )(cc, w_mod, b_mod.reshape(depth, 1, n))


def _mod_spec(which, tile, seq_len, ctx_row):
    tiles_per_seq = seq_len // tile
    if ctx_row is None:
        return pl.BlockSpec((None, 1, 1024), lambda i, *_: ((i // tiles_per_seq) * N_MOD + which, 0, 0))
    return pl.BlockSpec((None, 1, 1024), lambda i, *_: (ctx_row * N_MOD + which, 0, 0))


def _in_kernel(pre_ln, x_ref, lg_ref, lb_ref, sc_ref, sh_ref, w_ref, wt_ref, cq_ref, sq_ref, ck_ref, sk_ref,
               a_ref, q_ref, k_ref, v_ref, mq_ref, mv_ref, mo_ref, bg_ref, kt_ref, gt_ref):
    x = x_ref[...]
    if pre_ln:
        x = _ln(x, lg_ref[...], lb_ref[...])
    u = (x * (1.0 + sc_ref[...]) + sh_ref[...]).astype(BF16)

    def seg(lo, hi):
        return jnp.dot(u, w_ref[:, lo:hi], preferred_element_type=F32)

    a_ref[...] = seg(_C_A, _C_Q).astype(BF16)
    q_ref[...] = (seg(_C_Q, _C_QS) * cq_ref[...] + seg(_C_QS, _C_K) * sq_ref[...]).astype(BF16)
    k_ref[...] = (seg(_C_K, _C_KS) * ck_ref[...] + seg(_C_KS, _C_MQ) * sk_ref[...]).astype(BF16)
    v_ref[...] = _dot_nt(wt_ref[_R_VT:_R_VT + D_KV, :], u).astype(BF16)
    mq_ref[...] = seg(_C_MQ, _C_MV).astype(BF16)
    mv_ref[...] = seg(_C_MV, _C_MO).astype(BF16)
    mo_ref[...] = seg(_C_MO, _C_BG).astype(BF16)
    for j in range(N_BRANCHES):
        bg_ref[:, j * 1024:(j + 1) * 1024] = seg(_C_BG + j * 1024, _C_BG + (j + 1) * 1024).astype(BF16)
    n_chunks = u.shape[0] // MLSTM_CHUNK
    kt = _dot_nt(wt_ref[_R_KT:_R_KT + D_MLSTM, :], u)
    for h in range(N_MLSTM_HEADS):
        for c in range(n_chunks):
            kt_ref[h, c] = kt[h * MLSTM_HEAD_DIM:(h + 1) * MLSTM_HEAD_DIM,
                              c * MLSTM_CHUNK:(c + 1) * MLSTM_CHUNK].astype(BF16)
    gt = _dot_nt(wt_ref[_R_GT:_R_GT + N_GATE_COLS, :], u)
    for h in range(N_MLSTM_HEADS):
        for c in range(n_chunks):
            gt_ref[h, c] = gt[h * 4:(h + 1) * 4, c * MLSTM_CHUNK:(c + 1) * MLSTM_CHUNK]


def _in_proj(x, ln_g, ln_b, mod_l, w_main, w_t, rope, seq_len, ctx_row, pre_ln):
    n = x.shape[0]
    tm = TOKEN_TILE
    cq, sq, ck, sk = rope
    tps = seq_len // tm
    nch = n // MLSTM_CHUNK
    row = lambda i: (i, 0)
    pos = lambda i: (i % tps, 0)
    const = lambda i: (0, 0)
    out_shape = [
        jax.ShapeDtypeStruct((n, 2 * D_CONV), BF16),
        jax.ShapeDtypeStruct((n, D_ATTN), BF16),
        jax.ShapeDtypeStruct((n, D_KV), BF16),
        jax.ShapeDtypeStruct((D_KV, n), BF16),
        jax.ShapeDtypeStruct((n, D_MLSTM), BF16),
        jax.ShapeDtypeStruct((n, D_MLSTM), BF16),
        jax.ShapeDtypeStruct((n, D_MLSTM), BF16),
        jax.ShapeDtypeStruct((n, N_BRANCHES * 1024), BF16),
        jax.ShapeDtypeStruct((N_MLSTM_HEADS, nch, MLSTM_HEAD_DIM, MLSTM_CHUNK), BF16),
        jax.ShapeDtypeStruct((N_MLSTM_HEADS, nch, 4, MLSTM_CHUNK), F32),
    ]
    cpt = tm // MLSTM_CHUNK
    out_specs = [
        pl.BlockSpec((tm, 2 * D_CONV), row),
        pl.BlockSpec((tm, D_ATTN), row),
        pl.BlockSpec((tm, D_KV), row),
        pl.BlockSpec((D_KV, tm), lambda i: (0, i)),
        pl.BlockSpec((tm, D_MLSTM), row),
        pl.BlockSpec((tm, D_MLSTM), row),
        pl.BlockSpec((tm, D_MLSTM), row),
        pl.BlockSpec((tm, N_BRANCHES * 1024), row),
        pl.BlockSpec((N_MLSTM_HEADS, cpt, MLSTM_HEAD_DIM, MLSTM_CHUNK), lambda i: (0, i, 0, 0)),
        pl.BlockSpec((N_MLSTM_HEADS, cpt, 4, MLSTM_CHUNK), lambda i: (0, i, 0, 0)),
    ]
    in_specs = [
        pl.BlockSpec((tm, 1024), row),
        pl.BlockSpec((1, 1024), const),
        pl.BlockSpec((1, 1024), const),
        _mod_spec(1, tm, seq_len, ctx_row),
        _mod_spec(0, tm, seq_len, ctx_row),
        pl.BlockSpec(w_main.shape, const, pipeline_mode=pl.Buffered(1)),
        pl.BlockSpec(w_t.shape, const, pipeline_mode=pl.Buffered(1)),
        pl.BlockSpec((tm, D_ATTN), pos),
        pl.BlockSpec((tm, D_ATTN), pos),
        pl.BlockSpec((tm, D_KV), pos),
        pl.BlockSpec((tm, D_KV), pos),
    ]
    return pl.pallas_call(
        functools.partial(_in_kernel, pre_ln),
        grid=(n // tm,),
        in_specs=in_specs,
        out_specs=out_specs,
        out_shape=out_shape,
        compiler_params=_cparams("parallel"),
        name="in_proj",
    )(x, ln_g, ln_b, mod_l, mod_l, w_main, w_t, cq, sq, ck, sk)


def _conv_kernel(a_ref, w_ref, cb_ref, g_ref, b_ref, o_ref, upad_ref):
    t = a_ref.shape[0]
    zeros = jnp.zeros((CONV_PAD, D_CONV), F32)
    upad_ref[0:CONV_PAD, :] = zeros
    upad_ref[CONV_PAD + t:2 * CONV_PAD + t, :] = zeros
    val = a_ref[:, 0:D_CONV].astype(F32)
    gate = a_ref[:, D_CONV:2 * D_CONV].astype(F32)
    upad_ref[CONV_PAD:CONV_PAD + t, :] = val * _sigmoid(gate)
    half = CONV_WIDTH // 2

    def body(c, carry):
        r0 = pl.multiple_of(c * CONV_ROWS, CONV_ROWS)
        n_win = CONV_ROWS + 2 * CONV_PAD
        win = upad_ref[pl.ds(r0, n_win), :]
        acc = jnp.zeros((CONV_ROWS, D_CONV), F32) + cb_ref[...]
        for res in range(8):
            rolled = win if res == 0 else pltpu.roll(win, shift=n_win - res, axis=0)
            for k in range(CONV_WIDTH):
                off = CONV_PAD - half + k
                if off % 8 == res:
                    acc = acc + rolled[off - res:off - res + CONV_ROWS, :] * w_ref[k:k + 1, :]
        y = _ln(acc, g_ref[...], b_ref[...])
        o_ref[pl.ds(r0, CONV_ROWS), :] = (y * _sigmoid(y)).astype(BF16)
        return carry

    lax.fori_loop(0, t // CONV_ROWS, body, 0)


def _conv_branch(a_in, conv_w, conv_b, ln_g, ln_b, seq_len):
    n = a_in.shape[0]
    const = lambda b: (0, 0)
    return pl.pallas_call(
        _conv_kernel,
        grid=(n // seq_len,),
        in_specs=[
            pl.BlockSpec((seq_len, 2 * D_CONV), lambda b: (b, 0)),
            pl.BlockSpec((CONV_WIDTH, D_CONV), const),
            pl.BlockSpec((1, D_CONV), const),
            pl.BlockSpec((1, D_CONV), const),
            pl.BlockSpec((1, D_CONV), const),
        ],
        out_specs=pl.BlockSpec((seq_len, D_CONV), lambda b: (b, 0)),
        out_shape=jax.ShapeDtypeStruct((n, D_CONV), BF16),
        scratch_shapes=[pltpu.VMEM((seq_len + 2 * CONV_PAD, D_CONV), F32)],
        compiler_params=_cparams("parallel"),
        name="conv_branch",
    )(a_in, conv_w, conv_b, ln_g, ln_b)


def _attn_heads(q, keys, vals_t, masks, sink_ref, o_ref):
    rows = q.shape[0]
    group = N_Q_HEADS // N_KV_HEADS
    for hk in range(N_KV_HEADS):
        lo = hk * HEAD_DIM
        qs = jnp.concatenate([q[:, (hk * group + g) * HEAD_DIM:(hk * group + g + 1) * HEAD_DIM]
                              for g in range(group)], axis=0)
        sink = jnp.concatenate([jnp.full((1, rows), sink_ref[hk * group + g], F32) for g in range(group)], axis=1)
        scores = []
        m = sink
        for kk, mask in zip(keys, masks):
            s = _dot_nt(kk[:, lo:lo + HEAD_DIM], qs)
            if mask is not None:
                s = jnp.where(mask, s, NEG_INF)
            scores.append(s)
            m = jnp.maximum(m, jnp.max(s, axis=0, keepdims=True))
        acc = jnp.zeros((2 * HEAD_DIM, rows * group), F32)
        for s, vt in zip(scores, vals_t):
            n_k = s.shape[0]
            p = jnp.exp(s - m).astype(BF16)
            ones_rows = jnp.where(lax.broadcasted_iota(jnp.int32, (HEAD_DIM, n_k), 0) == 0, 1.0, 0.0).astype(BF16)
            v_aug = jnp.concatenate([vt[lo:lo + HEAD_DIM, :], ones_rows], axis=0)
            acc = acc + jnp.dot(v_aug, p, preferred_element_type=F32)
        denom = acc[HEAD_DIM:HEAD_DIM + 1, :] + jnp.exp(sink - m)
        o_t = acc * (1.0 / denom)
        for g in range(group):
            col = (hk * group + g) * HEAD_DIM
            o_ref[:, col:col + HEAD_DIM] = o_t[:, g * rows:(g + 1) * rows].T[:, 0:HEAD_DIM].astype(BF16)


def _attn_lat_kernel(sink_ref, q_ref, kp_ref, k0_ref, kn_ref, vp_ref, v0_ref, vn_ref, kc_ref, vc_ref, o_ref):
    n = pl.program_id(1)
    nb = pl.num_programs(1)
    stacked = (N_Q_HEADS // N_KV_HEADS) * BLOCK
    ki = lax.broadcasted_iota(jnp.int32, (BLOCK, stacked), 0)
    qi = lax.broadcasted_iota(jnp.int32, (BLOCK, stacked), 1) % BLOCK
    mask_prev = ki >= qi + jnp.where(n > 0, 0, BLOCK)
    mask_next = ki <= qi - jnp.where(n < nb - 1, 0, BLOCK)
    _attn_heads(q_ref[...],
                [kp_ref[...], k0_ref[...], kn_ref[...], kc_ref[...]],
                [vp_ref[...], v0_ref[...], vn_ref[...], vc_ref[...]],
                [mask_prev, None, mask_next, None], sink_ref, o_ref)


def _attn_ctx_kernel(sink_ref, q_ref, kc_ref, vc_ref, o_ref):
    _attn_heads(q_ref[...], [kc_ref[...]], [vc_ref[...]], [None], sink_ref, o_ref)


def _attn_latent(q, k, vt, kc, vct, sink, seq_len, ctx_len):
    n = q.shape[0]
    nb = seq_len // BLOCK
    batch = n // seq_len
    blk_prev = lambda b, j: b * nb + jnp.maximum(j - 1, 0)
    blk_next = lambda b, j: b * nb + jnp.minimum(j + 1, nb - 1)
    cur = lambda b, j: (b * nb + j, 0)
    kspec = lambda blk: pl.BlockSpec((BLOCK, D_KV), lambda b, j: (blk(b, j), 0))
    vspec = lambda blk: pl.BlockSpec((D_KV, BLOCK), lambda b, j: (0, blk(b, j)))
    blk_cur = lambda b, j: b * nb + j
    return pl.pallas_call(
        _attn_lat_kernel,
        grid=(batch, nb),
        in_specs=[
            pl.BlockSpec(memory_space=pltpu.SMEM),
            pl.BlockSpec((BLOCK, D_ATTN), cur),
            kspec(blk_prev), kspec(blk_cur), kspec(blk_next), vspec(blk_prev), vspec(blk_cur), vspec(blk_next),
            pl.BlockSpec((ctx_len, D_KV), lambda b, j: (b, 0)),
            pl.BlockSpec((D_KV, ctx_len), lambda b, j: (0, b)),
        ],
        out_specs=pl.BlockSpec((BLOCK, D_ATTN), cur),
        out_shape=jax.ShapeDtypeStruct((n, D_ATTN), BF16),
        compiler_params=_cparams("parallel", "parallel"),
        name="attn_latent",
    )(sink, q, k, k, k, vt, vt, vt, kc, vct)


def _attn_context(qc, kc, vct, sink, ctx_len):
    n = qc.shape[0]
    blk = lambda b: (b, 0)
    return pl.pallas_call(
        _attn_ctx_kernel,
        grid=(n // ctx_len,),
        in_specs=[
            pl.BlockSpec(memory_space=pltpu.SMEM),
            pl.BlockSpec((ctx_len, D_ATTN), blk),
            pl.BlockSpec((ctx_len, D_KV), blk),
            pl.BlockSpec((D_KV, ctx_len), lambda b: (0, b)),
        ],
        out_specs=pl.BlockSpec((ctx_len, D_ATTN), blk),
        out_shape=jax.ShapeDtypeStruct((n, D_ATTN), BF16),
        compiler_params=_cparams("parallel"),
        name="attn_context",
    )(sink, qc, kc, vct)


def _mlstm_kernel(ctx_out, nch_c, nch_l, hps,
                  qc_ref, ktc_ref, vc_ref, moc_ref, gc_ref,
                  ql_ref, ktl_ref, vl_ref, mol_ref, gl_ref,
                  gb_ref, ng_ref, *rest):
    if ctx_out:
        yl_ref, yc_ref, af_ref, lff_ref, ab_ref, lfb_ref, cf_ref, cb_ref, hf_ref, hb_ref = rest
    else:
        yl_ref, af_ref, lff_ref, ab_ref, lfb_ref, cf_ref, cb_ref, hf_ref, hb_ref = rest
        yc_ref = None
    lc = MLSTM_CHUNK
    dh = MLSTM_HEAD_DIM
    k_scale = MLSTM_HEAD_DIM ** -0.5
    ti = lax.broadcasted_iota(jnp.int32, (lc, lc), 0)
    si = lax.broadcasted_iota(jnp.int32, (lc, lc), 1)
    lower = si <= ti
    upper = si >= ti
    pre_mat = jnp.where(upper, 1.0, 0.0)
    suf_mat = jnp.where(lower, 1.0, 0.0)
    n_rows = nch_c + nch_l
    pad_rows = -n_rows % 8

    for hh in range(hps):
        gb = gb_ref[hh]

        def gate_rows(kind):
            rows = ([gc_ref[hh, c, kind:kind + 1, :] for c in range(nch_c)]
                    + [gl_ref[hh, c, kind:kind + 1, :] for c in range(nch_l)])
            rows = jnp.concatenate(rows, axis=0) + gb[kind:kind + 1, :]
            if pad_rows:
                rows = jnp.concatenate([rows, jnp.zeros((pad_rows, lc), F32)], axis=0)
            return rows

        lf_f = _log_sigmoid(gate_rows(1))
        lf_b = _log_sigmoid(gate_rows(3))
        a_f = gate_rows(0) - jnp.dot(lf_f, pre_mat, preferred_element_type=F32, precision=HIGHEST)
        a_b = gate_rows(2) - jnp.dot(lf_b, suf_mat, preferred_element_type=F32, precision=HIGHEST)
        for c in range(n_rows):
            af_ref[hh * n_rows + c] = a_f[c:c + 1, :]
            lff_ref[hh * n_rows + c] = lf_f[c:c + 1, :]
            ab_ref[hh * n_rows + c] = a_b[c:c + 1, :]
            lfb_ref[hh * n_rows + c] = lf_b[c:c + 1, :]

    cf_ref[...] = jnp.zeros_like(cf_ref)
    cb_ref[...] = jnp.zeros_like(cb_ref)
    ones_col = jnp.where(lax.broadcasted_iota(jnp.int32, (lc, dh), 1) == 0, 1.0, 0.0).astype(BF16)

    def chunk(q, kt, v, a_row, lf_row, c_ref, hh, m, mask):
        a_mat = jnp.where(mask, jnp.broadcast_to(a_row, (lc, lc)), NEG_INF)
        cm = jnp.max(a_mat, axis=1, keepdims=True)
        b_col = jnp.sum(jnp.where(mask, jnp.broadcast_to(lf_row, (lc, lc)), 0.0), axis=1, keepdims=True)
        mx = jnp.maximum(m, cm)
        mx_last = jnp.maximum(m, jnp.max(a_row, axis=1, keepdims=True))
        w = jnp.exp(a_mat - mx)
        s = jnp.dot(q, kt, preferred_element_type=F32) * k_scale
        p = (s * w).astype(BF16)
        w_s = jnp.exp(a_row - mx_last) * k_scale
        ktw = (kt.astype(F32) * w_s).astype(BF16)
        v_aug = jnp.concatenate([v, ones_col], axis=1)
        both = jnp.dot(jnp.concatenate([p, ktw], axis=0), v_aug, preferred_element_type=F32)
        c_old = c_ref[hh]
        inter = jnp.dot(q, c_old.astype(BF16), preferred_element_type=F32)
        tot = both[0:lc, :] + jnp.exp(m - mx) * inter
        den = tot[:, dh:dh + 1]
        h = tot[:, 0:dh] / jnp.maximum(jnp.abs(den), jnp.exp(-(b_col + mx)))
        c_ref[hh] = jnp.exp(m - mx_last) * c_old + both[lc:lc + dh, :]
        m_new = jnp.sum(lf_row, axis=1, keepdims=True) + mx_last
        return h, m_new

    def step(q_ref, kt_ref, v_ref, row0, c_f, c_b, rf, rb, ms):
        out = []
        for hh in range(hps):
            cols = slice(hh * dh, (hh + 1) * dh)
            base = hh * n_rows + row0
            h_f, m_f = chunk(q_ref[pl.ds(rf, lc), cols], kt_ref[hh, c_f], v_ref[pl.ds(rf, lc), cols],
                             af_ref[base + c_f], lff_ref[base + c_f], cf_ref, hh, ms[2 * hh], lower)
            h_b, m_b = chunk(q_ref[pl.ds(rb, lc), cols], kt_ref[hh, c_b], v_ref[pl.ds(rb, lc), cols],
                             ab_ref[base + c_b], lfb_ref[base + c_b], cb_ref, hh, ms[2 * hh + 1], upper)
            hf_ref[pl.ds(rf, lc), cols] = h_f
            hb_ref[pl.ds(rb, lc), cols] = h_b
            out += [m_f, m_b]
        return tuple(out)

    def finish(n_tok, mo_ref, o_ref):
        for hh in range(hps):
            cols = slice(hh * dh, (hh + 1) * dh)
            h = hf_ref[0:n_tok, cols] + hb_ref[0:n_tok, cols]
            mu = jnp.mean(h, axis=-1, keepdims=True)
            hc = h - mu
            var = jnp.mean(hc * hc, axis=-1, keepdims=True)
            y = hc * lax.rsqrt(var + LN_EPS) * ng_ref[:, cols]
            o_ref[:, cols] = (_sigmoid(mo_ref[:, cols].astype(F32)) * y).astype(BF16)

    ms = tuple(jnp.zeros((1, 1), F32) for _ in range(2 * hps))
    for c in range(nch_c):
        c_b = nch_c - 1 - c
        ms = step(qc_ref, ktc_ref, vc_ref, 0, c, c_b, c * lc, c_b * lc, ms)
    if ctx_out:
        finish(nch_c * lc, moc_ref, yc_ref)

    def body(c, ms):
        c_b = nch_l - 1 - c
        return step(ql_ref, ktl_ref, vl_ref, nch_c, c, c_b,
                    pl.multiple_of(c * lc, lc), pl.multiple_of(c_b * lc, lc), ms)

    lax.fori_loop(0, nch_l, body, ms)
    finish(nch_l * lc, mol_ref, yl_ref)


def _mlstm_branch(ctx_p, lat_p, gate_b, norm_g, seq_len, ctx_len, ctx_out):
    mq_c, kt_c, mv_c, mo_c, g_c = ctx_p
    mq_l, kt_l, mv_l, mo_l, g_l = lat_p
    n_l, n_c = mq_l.shape[0], mq_c.shape[0]
    batch = n_l // seq_len
    lc, dh, hps = MLSTM_CHUNK, MLSTM_HEAD_DIM, MLSTM_HEADS_PER_STEP
    nch_c, nch_l = ctx_len // lc, seq_len // lc

    def stream(t, nch):
        tok = pl.BlockSpec((t, hps * dh), lambda b, h: (b, h))
        return [tok,
                pl.BlockSpec((hps, nch, dh, lc), lambda b, h: (h, b, 0, 0)),
                tok, tok,
                pl.BlockSpec((hps, nch, 4, lc), lambda b, h: (h, b, 0, 0))]

    in_specs = stream(ctx_len, nch_c) + stream(seq_len, nch_l) + [
        pl.BlockSpec((hps, 4, 1), lambda b, h: (h, 0, 0)),
        pl.BlockSpec((1, hps * dh), lambda b, h: (0, h)),
    ]
    out_specs = [pl.BlockSpec((seq_len, hps * dh), lambda b, h: (b, h))]
    out_shape = [jax.ShapeDtypeStruct((n_l, D_MLSTM), BF16)]
    if ctx_out:
        out_specs.append(pl.BlockSpec((ctx_len, hps * dh), lambda b, h: (b, h)))
        out_shape.append(jax.ShapeDtypeStruct((n_c, D_MLSTM), BF16))
    row_scratch = pltpu.VMEM((hps * (nch_c + nch_l), 1, lc), F32)
    outs = pl.pallas_call(
        functools.partial(_mlstm_kernel, ctx_out, nch_c, nch_l, hps),
        grid=(batch, N_MLSTM_HEADS // hps),
        in_specs=in_specs,
        out_specs=out_specs,
        out_shape=out_shape,
        scratch_shapes=[row_scratch, row_scratch, row_scratch, row_scratch,
                        pltpu.VMEM((hps, dh, 2 * dh), F32), pltpu.VMEM((hps, dh, 2 * dh), F32),
                        pltpu.VMEM((seq_len, hps * dh), F32), pltpu.VMEM((seq_len, hps * dh), F32)],
        compiler_params=_cparams("parallel", "parallel"),
        name="mlstm_branch",
    )(mq_c, kt_c, mv_c, mo_c, g_c, mq_l, kt_l, mv_l, mo_l, g_l, gate_b, norm_g)
    return outs if ctx_out else (outs[0], None)


def _route(logits_t, br):
    sc = [_sigmoid(logits_t[e:e + 1, :]) for e in range(N_EXPERTS)]
    sel = [sc[e] + br[e:e + 1, :] for e in range(N_EXPERTS)]
    epg = EXPERTS_PER_GROUP
    group_score = []
    for g in range(N_GROUPS):
        v = sel[g * epg:(g + 1) * epg]
        best = None
        for i in range(epg):
            for j in range(i + 1, epg):
                pair = v[i] + v[j]
                best = pair if best is None else jnp.maximum(best, pair)
        group_score.append(best)
    g_idx = jnp.zeros_like(group_score[0], dtype=jnp.int32)
    best = group_score[0]
    for g in range(1, N_GROUPS):
        better = group_score[g] > best
        g_idx = jnp.where(better, g, g_idx)
        best = jnp.maximum(best, group_score[g])
    chosen = []
    for g in range(N_GROUPS):
        v = sel[g * epg:(g + 1) * epg]
        in_g = g_idx == g
        for i in range(epg):
            rank = jnp.zeros_like(g_idx)
            for j in range(epg):
                if j == i:
                    continue
                ahead = (v[j] >= v[i]) if j < i else (v[j] > v[i])
                rank = rank + jnp.where(ahead, 1, 0)
            chosen.append(in_g & (rank < 2))
    cls = jnp.zeros_like(sc[0])
    w_lo = jnp.zeros_like(sc[0])
    w_hi = jnp.zeros_like(sc[0])
    for g in range(N_GROUPS):
        for pid, (i, j) in enumerate(_PAIRS):
            lo, hi = g * epg + i, g * epg + j
            is_pair = chosen[lo] & chosen[hi]
            cls = jnp.where(is_pair, float(g * len(_PAIRS) + pid), cls)
            w_lo = jnp.where(is_pair, sc[lo], w_lo)
            w_hi = jnp.where(is_pair, sc[hi], w_hi)
    total = w_lo + w_hi
    return cls, w_lo / total, w_hi / total


def _merge_kernel(pre_ln, alpha, ya_ref, yb_ref, yc_ref, bg_ref, h_ref, g1_ref, sc2_ref, sh2_ref,
                  lig_ref, lib_ref, l1g_ref, l1b_ref, wa_ref, wb_ref, wc_ref, wo_ref, wr_ref, br_ref,
                  h1_ref, u2t_ref, route_ref, cnt_ref):
    tm = h_ref.shape[0]

    def branch(y_ref, w_ref, j):
        gate = _sigmoid(bg_ref[:, j * 1024:(j + 1) * 1024].astype(F32))
        return gate * jnp.dot(y_ref[...], w_ref[...], preferred_element_type=F32)

    mix = branch(ya_ref, wa_ref, 0) + branch(yb_ref, wb_ref, 1) + branch(yc_ref, wc_ref, 2)
    y = jnp.dot(mix.astype(BF16), wo_ref[...], preferred_element_type=F32)
    h = h_ref[...]
    if pre_ln:
        h = _ln(h, lig_ref[...], lib_ref[...])
    h1 = _ln(alpha * h + g1_ref[...] * y, l1g_ref[...], l1b_ref[...])
    h1_ref[...] = h1
    u2 = h1 * (1.0 + sc2_ref[...]) + sh2_ref[...]
    logits_t = _dot_nt(wr_ref[...], u2, precision=HIGHEST)
    cls, w_lo, w_hi = _route(logits_t, br_ref[...])

    @pl.when(pl.program_id(0) == 0)
    def _():
        cnt_ref[...] = jnp.zeros_like(cnt_ref)

    crow = lax.broadcasted_iota(jnp.int32, (N_CLASS_ROWS, tm), 0).astype(F32)
    onehot = jnp.where(crow == cls, 1.0, 0.0)
    earlier = lax.broadcasted_iota(jnp.int32, (tm, tm), 0) <= lax.broadcasted_iota(jnp.int32, (tm, tm), 1)
    incl = jnp.dot(onehot.astype(BF16), jnp.where(earlier, 1.0, 0.0).astype(BF16), preferred_element_type=F32)
    base = cnt_ref[...]
    rank = jnp.sum(onehot * (incl - 1.0 + base), axis=0, keepdims=True)
    cnt_ref[...] = base + incl[:, tm - 1:tm]

    route_ref[...] = jnp.concatenate([cls, w_lo, w_hi, rank, jnp.zeros((4, tm), F32)], axis=0)
    for s in range(TOKEN_SUBROWS):
        u2t_ref[pl.ds(s, tm, stride=TOKEN_SUBROWS), :] = u2[:, s * LANES:(s + 1) * LANES]


def _merge(ya, yb, yc, bg, h, mod_l, ln_in, ln1, w_a, w_b, w_c, w_o, w_rt, b_r, seq_len, ctx_row, pre_ln, alpha):
    n = ya.shape[0]
    tm = MERGE_TILE
    row = lambda i: (i, 0)
    const = lambda i: (0, 0)
    vec = pl.BlockSpec((1, 1024), const)
    wspec = lambda w: pl.BlockSpec(w.shape, const)
    return pl.pallas_call(
        functools.partial(_merge_kernel, pre_ln, alpha),
        grid=(n // tm,),
        in_specs=[
            pl.BlockSpec((tm, D_CONV), row),
            pl.BlockSpec((tm, D_ATTN), row),
            pl.BlockSpec((tm, D_MLSTM), row),
            pl.BlockSpec((tm, N_BRANCHES * 1024), row),
            pl.BlockSpec((tm, 1024), row),
            _mod_spec(2, tm, seq_len, ctx_row),
            _mod_spec(4, tm, seq_len, ctx_row),
            _mod_spec(3, tm, seq_len, ctx_row),
            vec, vec, vec, vec,
            wspec(w_a), wspec(w_b), wspec(w_c), wspec(w_o), wspec(w_rt), wspec(b_r),
        ],
        out_specs=[
            pl.BlockSpec((tm, 1024), row),
            pl.BlockSpec((tm * TOKEN_SUBROWS, LANES), row),
            pl.BlockSpec((8, tm), lambda i: (0, i)),
        ],
        out_shape=[
            jax.ShapeDtypeStruct((n, 1024), F32),
            jax.ShapeDtypeStruct((n * TOKEN_SUBROWS, LANES), F32),
            jax.ShapeDtypeStruct((8, n), F32),
        ],
        scratch_shapes=[pltpu.VMEM((N_CLASS_ROWS, 1), F32)],
        compiler_params=_cparams("arbitrary"),
        name="merge",
    )(ya, yb, yc, bg, h, mod_l, mod_l, mod_l, ln_in[0], ln_in[1], ln1[0], ln1[1], w_a, w_b, w_c, w_o, w_rt, b_r)


def _move_rows_kernel(scatter, pos_ref, src_ref, *rest):
    dst_ref, sem = rest[-2], rest[-1]
    rows = pos_ref.shape[-1]
    base = pl.program_id(0) * rows
    sub = TOKEN_SUBROWS

    def copy(k):
        t = pl.multiple_of((base + k) * sub, sub)
        p = pl.multiple_of(pos_ref[0, k] * sub, sub)
        if scatter:
            return pltpu.make_async_copy(src_ref.at[pl.ds(t, sub)], dst_ref.at[pl.ds(p, sub)], sem)
        return pltpu.make_async_copy(src_ref.at[pl.ds(p, sub)], dst_ref.at[pl.ds(t, sub)], sem)

    def start(k, carry):
        copy(k).start()
        return carry

    def wait(k, carry):
        copy(k).wait()
        return carry

    lax.fori_loop(0, rows, start, 0)
    lax.fori_loop(0, rows, wait, 0)


def _move_rows(pos, src, n_out, scatter):
    n = pos.shape[0]
    rows = min(MOVE_ROWS, n)
    out_shape = (n_out * TOKEN_SUBROWS, LANES)
    any_spec = pl.BlockSpec(memory_space=pl.ANY)
    in_specs = [pl.BlockSpec((None, 1, rows), lambda i: (i, 0, 0), memory_space=pltpu.SMEM), any_spec]
    args = [pos.reshape(n // rows, 1, rows), src]
    aliases = {}
    if scatter:
        in_specs.append(any_spec)
        args.append(jnp.zeros(out_shape, src.dtype))
        aliases = {2: 0}
    return pl.pallas_call(
        functools.partial(_move_rows_kernel, scatter),
        grid=(n // rows,),
        in_specs=in_specs,
        out_specs=any_spec,
        out_shape=jax.ShapeDtypeStruct(out_shape, src.dtype),
        input_output_aliases=aliases,
        scratch_shapes=[pltpu.SemaphoreType.DMA],
        compiler_params=_cparams("arbitrary"),
        name="scatter_rows" if scatter else "gather_rows",
    )(*args)


def _untile_tokens(ref):
    tokens = ref.shape[0] // TOKEN_SUBROWS
    return jnp.concatenate([ref[pl.ds(s, tokens, stride=TOKEN_SUBROWS), :] for s in range(TOKEN_SUBROWS)], axis=1)


def _experts_kernel(lo_ref, hi_ref, valid_ref, x_ref, wr_ref, wgl_ref, wgh_ref, wdl_ref, wdh_ref, o_ref):
    i = pl.program_id(0)
    tm = x_ref.shape[0] // TOKEN_SUBROWS

    @pl.when(valid_ref[i] != 0)
    def _():
        x32 = _untile_tokens(x_ref)
        x = x32.astype(BF16)

        def affinity(e):
            logit = jnp.sum(x32 * wr_ref[pl.ds(e, 1), :], axis=1, keepdims=True)
            return _sigmoid(logit)

        def expert(wg_ref, wd_ref):
            gu = jnp.dot(x, wg_ref[...], preferred_element_type=F32)
            g_ = gu[:, 0:D_EXPERT]
            act = (g_ * _sigmoid(g_) * gu[:, D_EXPERT:2 * D_EXPERT]).astype(BF16)
            return jnp.dot(act, wd_ref[...], preferred_element_type=F32)

        s_lo, s_hi = affinity(lo_ref[i]), affinity(hi_ref[i])
        total = s_lo + s_hi
        out = (s_lo / total) * expert(wgl_ref, wdl_ref) + (s_hi / total) * expert(wgh_ref, wdh_ref)
        for s in range(TOKEN_SUBROWS):
            o_ref[pl.ds(s, tm, stride=TOKEN_SUBROWS), :] = out[:, s * LANES:(s + 1) * LANES]

    @pl.when(valid_ref[i] == 0)
    def _():
        o_ref[...] = jnp.zeros_like(o_ref)


def _experts(lo, hi, valid, xs, w_rt, w_gu, w_dn):
    tm = EXPERT_TILE
    n_tiles = xs.shape[0] // (tm * TOKEN_SUBROWS)
    row = lambda i, lo, hi, valid: (i, 0)
    grid_spec = pltpu.PrefetchScalarGridSpec(
        num_scalar_prefetch=3,
        grid=(n_tiles,),
        in_specs=[
            pl.BlockSpec((tm * TOKEN_SUBROWS, LANES), row),
            pl.BlockSpec(w_rt.shape, lambda i, lo, hi, valid: (0, 0)),
            pl.BlockSpec((None, 1024, 2 * D_EXPERT), lambda i, lo, hi, valid: (lo[i], 0, 0)),
            pl.BlockSpec((None, 1024, 2 * D_EXPERT), lambda i, lo, hi, valid: (hi[i], 0, 0)),
            pl.BlockSpec((None, D_EXPERT, 1024), lambda i, lo, hi, valid: (lo[i], 0, 0)),
            pl.BlockSpec((None, D_EXPERT, 1024), lambda i, lo, hi, valid: (hi[i], 0, 0)),
        ],
        out_specs=pl.BlockSpec((tm * TOKEN_SUBROWS, LANES), row),
    )
    return pl.pallas_call(
        _experts_kernel,
        grid_spec=grid_spec,
        out_shape=jax.ShapeDtypeStruct(xs.shape, F32),
        compiler_params=_cparams("arbitrary"),
        name="moe_experts",
    )(lo, hi, valid, xs, w_rt, w_gu, w_gu, w_dn, w_dn)


def _final_kernel(alpha, h1_ref, f_ref, g2_ref, lg_ref, lb_ref, o_ref):
    f = _untile_tokens(f_ref)
    o_ref[...] = _ln(alpha * h1_ref[...] + g2_ref[...] * f, lg_ref[...], lb_ref[...])


def _final_ln(h1, f, mod_l, ln2, seq_len, ctx_row, alpha):
    n = h1.shape[0]
    tm = TOKEN_TILE
    row = lambda i: (i, 0)
    const = lambda i: (0, 0)
    return pl.pallas_call(
        functools.partial(_final_kernel, alpha),
        grid=(n // tm,),
        in_specs=[
            pl.BlockSpec((tm, 1024), row),
            pl.BlockSpec((tm * TOKEN_SUBROWS, LANES), row),
            _mod_spec(5, tm, seq_len, ctx_row),
            pl.BlockSpec((1, 1024), const),
            pl.BlockSpec((1, 1024), const),
        ],
        out_specs=pl.BlockSpec((tm, 1024), row),
        out_shape=jax.ShapeDtypeStruct((n, 1024), F32),
        compiler_params=_cparams("parallel"),
        name="final_ln",
    )(h1, f, mod_l, ln2[0], ln2[1])


def _sort_plan(route_t, n):
    tm = EXPERT_TILE
    n_tiles = n // tm + N_CLASSES
    cls = route_t[AUX_CLS].astype(jnp.int32)
    rank = route_t[AUX_RANK].astype(jnp.int32)
    onehot = cls[:, None] == jnp.arange(N_CLASSES, dtype=jnp.int32)[None, :]
    counts = jnp.sum(onehot, axis=0, dtype=jnp.int32)
    padded = (counts + tm - 1) // tm * tm
    ends = jnp.cumsum(padded)
    offs = ends - padded
    pos = jnp.sum(jnp.where(onehot, offs[None, :], 0), axis=1) + rank
    tile_ends = ends // tm
    j = jnp.arange(n_tiles, dtype=jnp.int32)
    n_used = tile_ends[-1]
    valid = j < n_used
    tile_cls = jnp.sum(j[:, None] >= tile_ends[None, :], axis=1)
    last_cls = jnp.sum((n_used - 1) >= tile_ends)
    tile_cls = jnp.where(valid, tile_cls, last_cls)
    group, pid = tile_cls // len(_PAIRS), tile_cls % len(_PAIRS)
    pair = jnp.asarray(np.array(_PAIRS, dtype=np.int32))
    lo = group * EXPERTS_PER_GROUP + pair[pid, 0]
    hi = group * EXPERTS_PER_GROUP + pair[pid, 1]
    return pos.astype(jnp.int32), lo.astype(jnp.int32), hi.astype(jnp.int32), valid.astype(jnp.int32), n_tiles * tm


def _moe(u2t, route_t, h1, mod_l, ln2, w_rt, w_gu, w_dn, seq_len, ctx_row, alpha):
    n = h1.shape[0]
    pos, lo, hi, valid, n_rows = _sort_plan(route_t, n)
    xs = _move_rows(pos, u2t, n_rows, scatter=True)
    fs = _experts(lo, hi, valid, xs, w_rt, w_gu, w_dn)
    f = _move_rows(pos, fs, n, scatter=False)
    return _final_ln(h1, f, mod_l, ln2, seq_len, ctx_row, alpha)


def _rope_swap_index(n_heads):
    idx = np.arange(n_heads * HEAD_DIM)
    within = idx % (HEAD_DIM // 2)
    quarter = HEAD_DIM // 4
    return np.where(within < quarter, idx + quarter, idx - quarter)


def _rope_tables(seq_len, n_heads, scale):
    t = np.arange(seq_len)
    quarter = HEAD_DIM // 4
    inv = ROPE_BASE ** (-np.arange(quarter, dtype=np.float32) / quarter)
    d = np.arange(HEAD_DIM)
    pos = np.where((d // (HEAD_DIM // 2) == 0)[None, :], (t // GRID_W)[:, None], (t % GRID_W)[:, None])
    ang = jnp.asarray(pos.astype(np.float32)) * jnp.asarray(inv[d % quarter])[None, :]
    sign = np.where(d % (HEAD_DIM // 2) < quarter, -1.0, 1.0).astype(np.float32)
    cos = jnp.cos(ang) * scale
    sin = jnp.sin(ang) * (sign * scale)[None, :]
    return jnp.tile(cos, (1, n_heads)), jnp.tile(sin, (1, n_heads))


def _flat_tables(seq_len, n_heads, scale):
    return (jnp.full((seq_len, n_heads * HEAD_DIM), scale, F32), jnp.zeros((seq_len, n_heads * HEAD_DIM), F32))


def _prep_in_weights(w_in_l):
    splits = np.cumsum([2 * D_CONV, D_ATTN, D_KV, D_KV, D_MLSTM, D_MLSTM, D_MLSTM, D_MLSTM, N_GATE_COLS])
    a, q, k, v, mq, mk, mv, mo, mg, bg = jnp.split(w_in_l, splits, axis=1)
    w_main = jnp.concatenate([a, q, q[:, _rope_swap_index(N_Q_HEADS)], k, k[:, _rope_swap_index(N_KV_HEADS)],
                              mq, mv, mo, bg], axis=1).astype(BF16)
    order = np.array([d * 8 + kind * 4 + h for h in range(N_MLSTM_HEADS) for d in range(2) for kind in range(2)])
    w_t = jnp.concatenate([mk.T, mg[:, order].T, v.T], axis=0).astype(BF16)
    return w_main, w_t


def kernel(x, c, ctx, c_ctx, ln_in_g, ln_in_b, w_router, b_router, w_mod, b_mod, w_in, conv_w, conv_b, conv_ln_g,
           conv_ln_b, w_a_out, attn_sink, w_b_out, mlstm_gate_b, mlstm_norm_g, w_c_out, w_out, ln1_g, ln1_b,
           moe_w_gu, moe_w_dn, ln2_g, ln2_b):
    batch, seq_len, d = x.shape
    ctx_len = ctx.shape[1]
    depth = w_in.shape[0]
    alpha = (2.0 * depth) ** 0.25
    ctx_row = batch
    assert d == 1024 and batch < MOD_ROWS
    assert seq_len % MERGE_TILE == 0 and ctx_len % TOKEN_TILE == 0 and (batch * ctx_len) % MERGE_TILE == 0
    assert all(n % min(MOVE_ROWS, n) == 0 and n % EXPERT_TILE == 0 for n in (batch * seq_len, batch * ctx_len))

    cc = jnp.zeros((MOD_ROWS, d), F32).at[0:batch].set(c).at[batch].set(c_ctx)
    mod = _modulation(cc, w_mod, b_mod).reshape(depth, MOD_ROWS * N_MOD, 1, d)

    attn_scale = HEAD_DIM ** -0.5
    rope_lat = _rope_tables(seq_len, N_Q_HEADS, attn_scale) + _rope_tables(seq_len, N_KV_HEADS, 1.0)
    rope_ctx = _flat_tables(ctx_len, N_Q_HEADS, attn_scale) + _flat_tables(ctx_len, N_KV_HEADS, 1.0)

    vec = lambda t: t.reshape(1, -1)
    ln_in = (vec(ln_in_g), vec(ln_in_b))
    w_rt = w_router.T
    b_r = b_router.reshape(N_EXPERTS, 1)

    h = x.reshape(batch * seq_len, d)
    hc = ctx.reshape(batch * ctx_len, d)
    for l in range(depth):
        need_ctx = l < depth - 1
        pre_ln = l == 0
        mod_l = mod[l]
        w_main, w_t = _prep_in_weights(w_in[l])
        lat = _in_proj(h, ln_in[0], ln_in[1], mod_l, w_main, w_t, rope_lat, seq_len, None, pre_ln)
        cx = _in_proj(hc, ln_in[0], ln_in[1], mod_l, w_main, w_t, rope_ctx, ctx_len, ctx_row, pre_ln)
        a_l, q_l, k_l, v_l, mq_l, mv_l, mo_l, bg_l, kt_l, gt_l = lat
        a_c, q_c, k_c, v_c, mq_c, mv_c, mo_c, bg_c, kt_c, gt_c = cx

        conv_args = (conv_w[l], vec(conv_b[l]), vec(conv_ln_g[l]), vec(conv_ln_b[l]))
        gate_b = jnp.transpose(mlstm_gate_b[l], (2, 0, 1)).reshape(N_MLSTM_HEADS, 4, 1)
        ya = _conv_branch(a_l, *conv_args, seq_len)
        yb = _attn_latent(q_l, k_l, v_l, k_c, v_c, attn_sink[l], seq_len, ctx_len)
        yc, yc_c = _mlstm_branch((mq_c, kt_c, mv_c, mo_c, gt_c), (mq_l, kt_l, mv_l, mo_l, gt_l),
                                 gate_b, vec(mlstm_norm_g[l]), seq_len, ctx_len, need_ctx)

        ln1 = (vec(ln1_g[l]), vec(ln1_b[l]))
        ln2 = (vec(ln2_g[l]), vec(ln2_b[l]))
        w_a, w_b, w_c, w_o = (w.astype(BF16) for w in (w_a_out[l], w_b_out[l], w_c_out[l], w_out[l]))
        w_gu = moe_w_gu[l].astype(BF16)
        w_dn = moe_w_dn[l].astype(BF16)

        h1, pay, route_t = _merge(ya, yb, yc, bg_l, h, mod_l, ln_in, ln1, w_a, w_b, w_c, w_o, w_rt, b_r,
                                  seq_len, None, pre_ln, alpha)
        h = _moe(pay, route_t, h1, mod_l, ln2, w_rt, w_gu, w_dn, seq_len, None, alpha)
        if need_ctx:
            ya_c = _conv_branch(a_c, *conv_args, ctx_len)
            yb_c = _attn_context(q_c, k_c, v_c, attn_sink[l], ctx_len)
            h1c, pay_c, route_tc = _merge(ya_c, yb_c, yc_c, bg_c, hc, mod_l, ln_in, ln1, w_a, w_b, w_c, w_o, w_rt,
                                          b_r, ctx_len, ctx_row, pre_ln, alpha)
            hc = _moe(pay_c, route_tc, h1c, mod_l, ln2, w_rt, w_gu, w_dn, ctx_len, ctx_row, alpha)
    return h.reshape(batch, seq_len, d)
```

```python
import functools

import numpy as np
import jax
import jax.numpy as jnp
from jax import lax
from jax.experimental import pallas as pl
from jax.experimental.pallas import tpu as pltpu

GRID_W = 64
LN_EPS = 1e-5
D_CONV = 512
CONV_WIDTH = 31
N_Q_HEADS = 8
N_KV_HEADS = 2
HEAD_DIM = 64
WINDOW = 128
BLOCK = 128
ROPE_BASE = 10000.0
D_ATTN = N_Q_HEADS * HEAD_DIM
D_KV = N_KV_HEADS * HEAD_DIM
N_MLSTM_HEADS = 4
MLSTM_HEAD_DIM = 128
D_MLSTM = N_MLSTM_HEADS * MLSTM_HEAD_DIM
N_GATE_COLS = 2 * 2 * N_MLSTM_HEADS
N_BRANCHES = 3
N_EXPERTS = 16
N_GROUPS = 4
EXPERTS_PER_GROUP = N_EXPERTS // N_GROUPS
D_EXPERT = 512
N_MOD = 6

LANES = 128
V7X_VMEM_LIMIT_BYTES = 56 * 1024 * 1024

MOD_ROWS = 16
MOD_COL_BLOCK = 512
TOKEN_TILE = 256
EXPERT_TILE = 256
MOVE_ROWS = 2048
MLSTM_CHUNK = 128
MLSTM_HEADS_PER_STEP = 4
MERGE_TILE = 512
CONV_ROWS = 64
CONV_PAD = 16

_PAIRS = [(i, j) for i in range(EXPERTS_PER_GROUP) for j in range(i + 1, EXPERTS_PER_GROUP)]
N_CLASSES = N_GROUPS * len(_PAIRS)
N_CLASS_ROWS = 32
AUX_CLS, AUX_W_LO, AUX_W_HI, AUX_RANK = 0, 1, 2, 3
TOKEN_SUBROWS = 1024 // LANES

F32 = jnp.float32
BF16 = jnp.bfloat16
HIGHEST = lax.Precision.HIGHEST
NEG_INF = float("-inf")

_C_A = 0
_C_Q = _C_A + 2 * D_CONV
_C_QS = _C_Q + D_ATTN
_C_K = _C_QS + D_ATTN
_C_KS = _C_K + D_KV
_C_MQ = _C_KS + D_KV
_C_MV = _C_MQ + D_MLSTM
_C_MO = _C_MV + D_MLSTM
_C_BG = _C_MO + D_MLSTM
_C_END = _C_BG + N_BRANCHES * 1024
_R_KT = 0
_R_GT = _R_KT + D_MLSTM
_R_VT = _R_GT + N_GATE_COLS


def _cparams(*sem):
    return pltpu.CompilerParams(dimension_semantics=sem, vmem_limit_bytes=V7X_VMEM_LIMIT_BYTES)


def _ln(x, g, b):
    mu = jnp.mean(x, axis=-1, keepdims=True)
    xc = x - mu
    var = jnp.mean(xc * xc, axis=-1, keepdims=True)
    return xc * lax.rsqrt(var + LN_EPS) * g + b


def _sigmoid(x):
    return 0.5 * jnp.tanh(0.5 * x) + 0.5


def _log_sigmoid(x):
    return jnp.minimum(x, 0.0) - jnp.log(1.0 + jnp.exp(-jnp.abs(x)))


def _dot_nt(a, b, precision=None):
    return lax.dot_general(a, b, (((1,), (1,)), ((), ())), preferred_element_type=F32, precision=precision)


def _mod_kernel(c_ref, w_ref, b_ref, o_ref):
    c = c_ref[...]
    s = c * _sigmoid(c)
    o_ref[...] = jnp.dot(s, w_ref[...], preferred_element_type=F32, precision=HIGHEST) + b_ref[...]


def _modulation(cc, w_mod, b_mod):
    depth, d, n = w_mod.shape
    return pl.pallas_call(
        _mod_kernel,
        grid=(depth, n // MOD_COL_BLOCK),
        in_specs=[
            pl.BlockSpec((MOD_ROWS, d), lambda l, j: (0, 0)),
            pl.BlockSpec((None, d, MOD_COL_BLOCK), lambda l, j: (l, 0, j)),
            pl.BlockSpec((None, 1, MOD_COL_BLOCK), lambda l, j: (l, 0, j)),
        ],
        out_specs=pl.BlockSpec((None, MOD_ROWS, MOD_COL_BLOCK), lambda l, j: (l, 0, j)),
        out_shape=jax.ShapeDtypeStruct((depth, MOD_ROWS, n), F32),
        compiler_params=_cparams("parallel", "parallel"),
        name="modulation",
    )(cc, w_mod, b_mod.reshape(depth, 1, n))


def _mod_spec(which, tile, seq_len, ctx_row):
    tiles_per_seq = seq_len // tile
    if ctx_row is None:
        return pl.BlockSpec((None, 1, 1024), lambda i, *_: ((i // tiles_per_seq) * N_MOD + which, 0, 0))
    return pl.BlockSpec((None, 1, 1024), lambda i, *_: (ctx_row * N_MOD + which, 0, 0))


def _in_kernel(pre_ln, x_ref, lg_ref, lb_ref, sc_ref, sh_ref, w_ref, wt_ref, cq_ref, sq_ref, ck_ref, sk_ref,
               a_ref, q_ref, k_ref, v_ref, mq_ref, mv_ref, mo_ref, bg_ref, kt_ref, gt_ref):
    x = x_ref[...]
    if pre_ln:
        x = _ln(x, lg_ref[...], lb_ref[...])
    u = (x * (1.0 + sc_ref[...]) + sh_ref[...]).astype(BF16)

    def seg(lo, hi):
        return jnp.dot(u, w_ref[:, lo:hi], preferred_element_type=F32)

    a_ref[...] = seg(_C_A, _C_Q).astype(BF16)
    q_ref[...] = (seg(_C_Q, _C_QS) * cq_ref[...] + seg(_C_QS, _C_K) * sq_ref[...]).astype(BF16)
    k_ref[...] = (seg(_C_K, _C_KS) * ck_ref[...] + seg(_C_KS, _C_MQ) * sk_ref[...]).astype(BF16)
    v_ref[...] = _dot_nt(wt_ref[_R_VT:_R_VT + D_KV, :], u).astype(BF16)
    mq_ref[...] = seg(_C_MQ, _C_MV).astype(BF16)
    mv_ref[...] = seg(_C_MV, _C_MO).astype(BF16)
    mo_ref[...] = seg(_C_MO, _C_BG).astype(BF16)
    for j in range(N_BRANCHES):
        bg_ref[:, j * 1024:(j + 1) * 1024] = seg(_C_BG + j * 1024, _C_BG + (j + 1) * 1024).astype(BF16)
    n_chunks = u.shape[0] // MLSTM_CHUNK
    kt = _dot_nt(wt_ref[_R_KT:_R_KT + D_MLSTM, :], u)
    for h in range(N_MLSTM_HEADS):
        for c in range(n_chunks):
            kt_ref[h, c] = kt[h * MLSTM_HEAD_DIM:(h + 1) * MLSTM_HEAD_DIM,
                              c * MLSTM_CHUNK:(c + 1) * MLSTM_CHUNK].astype(BF16)
    gt = _dot_nt(wt_ref[_R_GT:_R_GT + N_GATE_COLS, :], u)
    for h in range(N_MLSTM_HEADS):
        for c in range(n_chunks):
            gt_ref[h, c] = gt[h * 4:(h + 1) * 4, c * MLSTM_CHUNK:(c + 1) * MLSTM_CHUNK]


def _in_proj(x, ln_g, ln_b, mod_l, w_main, w_t, rope, seq_len, ctx_row, pre_ln):
    n = x.shape[0]
    tm = TOKEN_TILE
    cq, sq, ck, sk = rope
    tps = seq_len // tm
    nch = n // MLSTM_CHUNK
    row = lambda i: (i, 0)
    pos = lambda i: (i % tps, 0)
    const = lambda i: (0, 0)
    out_shape = [
        jax.ShapeDtypeStruct((n, 2 * D_CONV), BF16),
        jax.ShapeDtypeStruct((n, D_ATTN), BF16),
        jax.ShapeDtypeStruct((n, D_KV), BF16),
        jax.ShapeDtypeStruct((D_KV, n), BF16),
        jax.ShapeDtypeStruct((n, D_MLSTM), BF16),
        jax.ShapeDtypeStruct((n, D_MLSTM), BF16),
        jax.ShapeDtypeStruct((n, D_MLSTM), BF16),
        jax.ShapeDtypeStruct((n, N_BRANCHES * 1024), BF16),
        jax.ShapeDtypeStruct((N_MLSTM_HEADS, nch, MLSTM_HEAD_DIM, MLSTM_CHUNK), BF16),
        jax.ShapeDtypeStruct((N_MLSTM_HEADS, nch, 4, MLSTM_CHUNK), F32),
    ]
    cpt = tm // MLSTM_CHUNK
    out_specs = [
        pl.BlockSpec((tm, 2 * D_CONV), row),
        pl.BlockSpec((tm, D_ATTN), row),
        pl.BlockSpec((tm, D_KV), row),
        pl.BlockSpec((D_KV, tm), lambda i: (0, i)),
        pl.BlockSpec((tm, D_MLSTM), row),
        pl.BlockSpec((tm, D_MLSTM), row),
        pl.BlockSpec((tm, D_MLSTM), row),
        pl.BlockSpec((tm, N_BRANCHES * 1024), row),
        pl.BlockSpec((N_MLSTM_HEADS, cpt, MLSTM_HEAD_DIM, MLSTM_CHUNK), lambda i: (0, i, 0, 0)),
        pl.BlockSpec((N_MLSTM_HEADS, cpt, 4, MLSTM_CHUNK), lambda i: (0, i, 0, 0)),
    ]
    in_specs = [
        pl.BlockSpec((tm, 1024), row),
        pl.BlockSpec((1, 1024), const),
        pl.BlockSpec((1, 1024), const),
        _mod_spec(1, tm, seq_len, ctx_row),
        _mod_spec(0, tm, seq_len, ctx_row),
        pl.BlockSpec(w_main.shape, const, pipeline_mode=pl.Buffered(1)),
        pl.BlockSpec(w_t.shape, const, pipeline_mode=pl.Buffered(1)),
        pl.BlockSpec((tm, D_ATTN), pos),
        pl.BlockSpec((tm, D_ATTN), pos),
        pl.BlockSpec((tm, D_KV), pos),
        pl.BlockSpec((tm, D_KV), pos),
    ]
    return pl.pallas_call(
        functools.partial(_in_kernel, pre_ln),
        grid=(n // tm,),
        in_specs=in_specs,
        out_specs=out_specs,
        out_shape=out_shape,
        compiler_params=_cparams("parallel"),
        name="in_proj",
    )(x, ln_g, ln_b, mod_l, mod_l, w_main, w_t, cq, sq, ck, sk)


def _conv_kernel(a_ref, w_ref, cb_ref, g_ref, b_ref, o_ref, upad_ref):
    t = a_ref.shape[0]
    zeros = jnp.zeros((CONV_PAD, D_CONV), F32)
    upad_ref[0:CONV_PAD, :] = zeros
    upad_ref[CONV_PAD + t:2 * CONV_PAD + t, :] = zeros
    val = a_ref[:, 0:D_CONV].astype(F32)
    gate = a_ref[:, D_CONV:2 * D_CONV].astype(F32)
    upad_ref[CONV_PAD:CONV_PAD + t, :] = val * _sigmoid(gate)
    half = CONV_WIDTH // 2

    def body(c, carry):
        r0 = pl.multiple_of(c * CONV_ROWS, CONV_ROWS)
        n_win = CONV_ROWS + 2 * CONV_PAD
        win = upad_ref[pl.ds(r0, n_win), :]
        acc = jnp.zeros((CONV_ROWS, D_CONV), F32) + cb_ref[...]
        for res in range(8):
            rolled = win if res == 0 else pltpu.roll(win, shift=n_win - res, axis=0)
            for k in range(CONV_WIDTH):
                off = CONV_PAD - half + k
                if off % 8 == res:
                    acc = acc + rolled[off - res:off - res + CONV_ROWS, :] * w_ref[k:k + 1, :]
        y = _ln(acc, g_ref[...], b_ref[...])
        o_ref[pl.ds(r0, CONV_ROWS), :] = (y * _sigmoid(y)).astype(BF16)
        return carry

    lax.fori_loop(0, t // CONV_ROWS, body, 0)


def _conv_branch(a_in, conv_w, conv_b, ln_g, ln_b, seq_len):
    n = a_in.shape[0]
    const = lambda b: (0, 0)
    return pl.pallas_call(
        _conv_kernel,
        grid=(n // seq_len,),
        in_specs=[
            pl.BlockSpec((seq_len, 2 * D_CONV), lambda b: (b, 0)),
            pl.BlockSpec((CONV_WIDTH, D_CONV), const),
            pl.BlockSpec((1, D_CONV), const),
            pl.BlockSpec((1, D_CONV), const),
            pl.BlockSpec((1, D_CONV), const),
        ],
        out_specs=pl.BlockSpec((seq_len, D_CONV), lambda b: (b, 0)),
        out_shape=jax.ShapeDtypeStruct((n, D_CONV), BF16),
        scratch_shapes=[pltpu.VMEM((seq_len + 2 * CONV_PAD, D_CONV), F32)],
        compiler_params=_cparams("parallel"),
        name="conv_branch",
    )(a_in, conv_w, conv_b, ln_g, ln_b)


def _attn_heads(q, keys, vals_t, masks, sink_ref, o_ref):
    rows = q.shape[0]
    group = N_Q_HEADS // N_KV_HEADS
    for hk in range(N_KV_HEADS):
        lo = hk * HEAD_DIM
        qs = jnp.concatenate([q[:, (hk * group + g) * HEAD_DIM:(hk * group + g + 1) * HEAD_DIM]
                              for g in range(group)], axis=0)
        sink = jnp.concatenate([jnp.full((1, rows), sink_ref[hk * group + g], F32) for g in range(group)], axis=1)
        scores = []
        m = sink
        for kk, mask in zip(keys, masks):
            s = _dot_nt(kk[:, lo:lo + HEAD_DIM], qs)
            if mask is not None:
                s = jnp.where(mask, s, NEG_INF)
            scores.append(s)
            m = jnp.maximum(m, jnp.max(s, axis=0, keepdims=True))
        acc = jnp.zeros((2 * HEAD_DIM, rows * group), F32)
        for s, vt in zip(scores, vals_t):
            n_k = s.shape[0]
            p = jnp.exp(s - m).astype(BF16)
            ones_rows = jnp.where(lax.broadcasted_iota(jnp.int32, (HEAD_DIM, n_k), 0) == 0, 1.0, 0.0).astype(BF16)
            v_aug = jnp.concatenate([vt[lo:lo + HEAD_DIM, :], ones_rows], axis=0)
            acc = acc + jnp.dot(v_aug, p, preferred_element_type=F32)
        denom = acc[HEAD_DIM:HEAD_DIM + 1, :] + jnp.exp(sink - m)
        o_t = acc * (1.0 / denom)
        for g in range(group):
            col = (hk * group + g) * HEAD_DIM
            o_ref[:, col:col + HEAD_DIM] = o_t[:, g * rows:(g + 1) * rows].T[:, 0:HEAD_DIM].astype(BF16)


def _attn_lat_kernel(sink_ref, q_ref, kp_ref, k0_ref, kn_ref, vp_ref, v0_ref, vn_ref, kc_ref, vc_ref, o_ref):
    n = pl.program_id(1)
    nb = pl.num_programs(1)
    stacked = (N_Q_HEADS // N_KV_HEADS) * BLOCK
    ki = lax.broadcasted_iota(jnp.int32, (BLOCK, stacked), 0)
    qi = lax.broadcasted_iota(jnp.int32, (BLOCK, stacked), 1) % BLOCK
    mask_prev = ki >= qi + jnp.where(n > 0, 0, BLOCK)
    mask_next = ki <= qi - jnp.where(n < nb - 1, 0, BLOCK)
    _attn_heads(q_ref[...],
                [kp_ref[...], k0_ref[...], kn_ref[...], kc_ref[...]],
                [vp_ref[...], v0_ref[...], vn_ref[...], vc_ref[...]],
                [mask_prev, None, mask_next, None], sink_ref, o_ref)


def _attn_ctx_kernel(sink_ref, q_ref, kc_ref, vc_ref, o_ref):
    _attn_heads(q_ref[...], [kc_ref[...]], [vc_ref[...]], [None], sink_ref, o_ref)


def _attn_latent(q, k, vt, kc, vct, sink, seq_len, ctx_len):
    n = q.shape[0]
    nb = seq_len // BLOCK
    batch = n // seq_len
    blk_prev = lambda b, j: b * nb + jnp.maximum(j - 1, 0)
    blk_next = lambda b, j: b * nb + jnp.minimum(j + 1, nb - 1)
    cur = lambda b, j: (b * nb + j, 0)
    kspec = lambda blk: pl.BlockSpec((BLOCK, D_KV), lambda b, j: (blk(b, j), 0))
    vspec = lambda blk: pl.BlockSpec((D_KV, BLOCK), lambda b, j: (0, blk(b, j)))
    blk_cur = lambda b, j: b * nb + j
    return pl.pallas_call(
        _attn_lat_kernel,
        grid=(batch, nb),
        in_specs=[
            pl.BlockSpec(memory_space=pltpu.SMEM),
            pl.BlockSpec((BLOCK, D_ATTN), cur),
            kspec(blk_prev), kspec(blk_cur), kspec(blk_next), vspec(blk_prev), vspec(blk_cur), vspec(blk_next),
            pl.BlockSpec((ctx_len, D_KV), lambda b, j: (b, 0)),
            pl.BlockSpec((D_KV, ctx_len), lambda b, j: (0, b)),
        ],
        out_specs=pl.BlockSpec((BLOCK, D_ATTN), cur),
        out_shape=jax.ShapeDtypeStruct((n, D_ATTN), BF16),
        compiler_params=_cparams("parallel", "parallel"),
        name="attn_latent",
    )(sink, q, k, k, k, vt, vt, vt, kc, vct)


def _attn_context(qc, kc, vct, sink, ctx_len):
    n = qc.shape[0]
    blk = lambda b: (b, 0)
    return pl.pallas_call(
        _attn_ctx_kernel,
        grid=(n // ctx_len,),
        in_specs=[
            pl.BlockSpec(memory_space=pltpu.SMEM),
            pl.BlockSpec((ctx_len, D_ATTN), blk),
            pl.BlockSpec((ctx_len, D_KV), blk),
            pl.BlockSpec((D_KV, ctx_len), lambda b: (0, b)),
        ],
        out_specs=pl.BlockSpec((ctx_len, D_ATTN), blk),
        out_shape=jax.ShapeDtypeStruct((n, D_ATTN), BF16),
        compiler_params=_cparams("parallel"),
        name="attn_context",
    )(sink, qc, kc, vct)


def _mlstm_kernel(ctx_out, nch_c, nch_l, hps,
                  qc_ref, ktc_ref, vc_ref, moc_ref, gc_ref,
                  ql_ref, ktl_ref, vl_ref, mol_ref, gl_ref,
                  gb_ref, ng_ref, *rest):
    if ctx_out:
        yl_ref, yc_ref, af_ref, lff_ref, ab_ref, lfb_ref, cf_ref, cb_ref, hf_ref, hb_ref = rest
    else:
        yl_ref, af_ref, lff_ref, ab_ref, lfb_ref, cf_ref, cb_ref, hf_ref, hb_ref = rest
        yc_ref = None
    lc = MLSTM_CHUNK
    dh = MLSTM_HEAD_DIM
    k_scale = MLSTM_HEAD_DIM ** -0.5
    ti = lax.broadcasted_iota(jnp.int32, (lc, lc), 0)
    si = lax.broadcasted_iota(jnp.int32, (lc, lc), 1)
    lower = si <= ti
    upper = si >= ti
    pre_mat = jnp.where(upper, 1.0, 0.0)
    suf_mat = jnp.where(lower, 1.0, 0.0)
    n_rows = nch_c + nch_l
    pad_rows = -n_rows % 8

    for hh in range(hps):
        gb = gb_ref[hh]

        def gate_rows(kind):
            rows = ([gc_ref[hh, c, kind:kind + 1, :] for c in range(nch_c)]
                    + [gl_ref[hh, c, kind:kind + 1, :] for c in range(nch_l)])
            rows = jnp.concatenate(rows, axis=0) + gb[kind:kind + 1, :]
            if pad_rows:
                rows = jnp.concatenate([rows, jnp.zeros((pad_rows, lc), F32)], axis=0)
            return rows

        lf_f = _log_sigmoid(gate_rows(1))
        lf_b = _log_sigmoid(gate_rows(3))
        a_f = gate_rows(0) - jnp.dot(lf_f, pre_mat, preferred_element_type=F32, precision=HIGHEST)
        a_b = gate_rows(2) - jnp.dot(lf_b, suf_mat, preferred_element_type=F32, precision=HIGHEST)
        for c in range(n_rows):
            af_ref[hh * n_rows + c] = a_f[c:c + 1, :]
            lff_ref[hh * n_rows + c] = lf_f[c:c + 1, :]
            ab_ref[hh * n_rows + c] = a_b[c:c + 1, :]
            lfb_ref[hh * n_rows + c] = lf_b[c:c + 1, :]

    cf_ref[...] = jnp.zeros_like(cf_ref)
    cb_ref[...] = jnp.zeros_like(cb_ref)
    ones_col = jnp.where(lax.broadcasted_iota(jnp.int32, (lc, dh), 1) == 0, 1.0, 0.0).astype(BF16)

    def chunk(q, kt, v, a_row, lf_row, c_ref, hh, m, mask):
        a_mat = jnp.where(mask, jnp.broadcast_to(a_row, (lc, lc)), NEG_INF)
        cm = jnp.max(a_mat, axis=1, keepdims=True)
        b_col = jnp.sum(jnp.where(mask, jnp.broadcast_to(lf_row, (lc, lc)), 0.0), axis=1, keepdims=True)
        mx = jnp.maximum(m, cm)
        mx_last = jnp.maximum(m, jnp.max(a_row, axis=1, keepdims=True))
        w = jnp.exp(a_mat - mx)
        s = jnp.dot(q, kt, preferred_element_type=F32) * k_scale
        p = (s * w).astype(BF16)
        w_s = jnp.exp(a_row - mx_last) * k_scale
        ktw = (kt.astype(F32) * w_s).astype(BF16)
        v_aug = jnp.concatenate([v, ones_col], axis=1)
        both = jnp.dot(jnp.concatenate([p, ktw], axis=0), v_aug, preferred_element_type=F32)
        c_old = c_ref[hh]
        inter = jnp.dot(q, c_old.astype(BF16), preferred_element_type=F32)
        tot = both[0:lc, :] + jnp.exp(m - mx) * inter
        den = tot[:, dh:dh + 1]
        h = tot[:, 0:dh] / jnp.maximum(jnp.abs(den), jnp.exp(-(b_col + mx)))
        c_ref[hh] = jnp.exp(m - mx_last) * c_old + both[lc:lc + dh, :]
        m_new = jnp.sum(lf_row, axis=1, keepdims=True) + mx_last
        return h, m_new

    def step(q_ref, kt_ref, v_ref, row0, c_f, c_b, rf, rb, ms):
        out = []
        for hh in range(hps):
            cols = slice(hh * dh, (hh + 1) * dh)
            base = hh * n_rows + row0
            h_f, m_f = chunk(q_ref[pl.ds(rf, lc), cols], kt_ref[hh, c_f], v_ref[pl.ds(rf, lc), cols],
                             af_ref[base + c_f], lff_ref[base + c_f], cf_ref, hh, ms[2 * hh], lower)
            h_b, m_b = chunk(q_ref[pl.ds(rb, lc), cols], kt_ref[hh, c_b], v_ref[pl.ds(rb, lc), cols],
                             ab_ref[base + c_b], lfb_ref[base + c_b], cb_ref, hh, ms[2 * hh + 1], upper)
            hf_ref[pl.ds(rf, lc), cols] = h_f
            hb_ref[pl.ds(rb, lc), cols] = h_b
            out += [m_f, m_b]
        return tuple(out)

    def finish(n_tok, mo_ref, o_ref):
        for hh in range(hps):
            cols = slice(hh * dh, (hh + 1) * dh)
            h = hf_ref[0:n_tok, cols] + hb_ref[0:n_tok, cols]
            mu = jnp.mean(h, axis=-1, keepdims=True)
            hc = h - mu
            var = jnp.mean(hc * hc, axis=-1, keepdims=True)
            y = hc * lax.rsqrt(var + LN_EPS) * ng_ref[:, cols]
            o_ref[:, cols] = (_sigmoid(mo_ref[:, cols].astype(F32)) * y).astype(BF16)

    ms = tuple(jnp.zeros((1, 1), F32) for _ in range(2 * hps))
    for c in range(nch_c):
        c_b = nch_c - 1 - c
        ms = step(qc_ref, ktc_ref, vc_ref, 0, c, c_b, c * lc, c_b * lc, ms)
    if ctx_out:
        finish(nch_c * lc, moc_ref, yc_ref)

    def body(c, ms):
        c_b = nch_l - 1 - c
        return step(ql_ref, ktl_ref, vl_ref, nch_c, c, c_b,
                    pl.multiple_of(c * lc, lc), pl.multiple_of(c_b * lc, lc), ms)

    lax.fori_loop(0, nch_l, body, ms)
    finish(nch_l * lc, mol_ref, yl_ref)


def _mlstm_branch(ctx_p, lat_p, gate_b, norm_g, seq_len, ctx_len, ctx_out):
    mq_c, kt_c, mv_c, mo_c, g_c = ctx_p
    mq_l, kt_l, mv_l, mo_l, g_l = lat_p
    n_l, n_c = mq_l.shape[0], mq_c.shape[0]
    batch = n_l // seq_len
    lc, dh, hps = MLSTM_CHUNK, MLSTM_HEAD_DIM, MLSTM_HEADS_PER_STEP
    nch_c, nch_l = ctx_len // lc, seq_len // lc

    def stream(t, nch):
        tok = pl.BlockSpec((t, hps * dh), lambda b, h: (b, h))
        return [tok,
                pl.BlockSpec((hps, nch, dh, lc), lambda b, h: (h, b, 0, 0)),
                tok, tok,
                pl.BlockSpec((hps, nch, 4, lc), lambda b, h: (h, b, 0, 0))]

    in_specs = stream(ctx_len, nch_c) + stream(seq_len, nch_l) + [
        pl.BlockSpec((hps, 4, 1), lambda b, h: (h, 0, 0)),
        pl.BlockSpec((1, hps * dh), lambda b, h: (0, h)),
    ]
    out_specs = [pl.BlockSpec((seq_len, hps * dh), lambda b, h: (b, h))]
    out_shape = [jax.ShapeDtypeStruct((n_l, D_MLSTM), BF16)]
    if ctx_out:
        out_specs.append(pl.BlockSpec((ctx_len, hps * dh), lambda b, h: (b, h)))
        out_shape.append(jax.ShapeDtypeStruct((n_c, D_MLSTM), BF16))
    row_scratch = pltpu.VMEM((hps * (nch_c + nch_l), 1, lc), F32)
    outs = pl.pallas_call(
        functools.partial(_mlstm_kernel, ctx_out, nch_c, nch_l, hps),
        grid=(batch, N_MLSTM_HEADS // hps),
        in_specs=in_specs,
        out_specs=out_specs,
        out_shape=out_shape,
        scratch_shapes=[row_scratch, row_scratch, row_scratch, row_scratch,
                        pltpu.VMEM((hps, dh, 2 * dh), F32), pltpu.VMEM((hps, dh, 2 * dh), F32),
                        pltpu.VMEM((seq_len, hps * dh), F32), pltpu.VMEM((seq_len, hps * dh), F32)],
        compiler_params=_cparams("parallel", "parallel"),
        name="mlstm_branch",
    )(mq_c, kt_c, mv_c, mo_c, g_c, mq_l, kt_l, mv_l, mo_l, g_l, gate_b, norm_g)
    return outs if ctx_out else (outs[0], None)


def _route(logits_t, br):
    sc = [_sigmoid(logits_t[e:e + 1, :]) for e in range(N_EXPERTS)]
    sel = [sc[e] + br[e:e + 1, :] for e in range(N_EXPERTS)]
    epg = EXPERTS_PER_GROUP
    group_score = []
    for g in range(N_GROUPS):
        v = sel[g * epg:(g + 1) * epg]
        best = None
        for i in range(epg):
            for j in range(i + 1, epg):
                pair = v[i] + v[j]
                best = pair if best is None else jnp.maximum(best, pair)
        group_score.append(best)
    g_idx = jnp.zeros_like(group_score[0], dtype=jnp.int32)
    best = group_score[0]
    for g in range(1, N_GROUPS):
        better = group_score[g] > best
        g_idx = jnp.where(better, g, g_idx)
        best = jnp.maximum(best, group_score[g])
    chosen = []
    for g in range(N_GROUPS):
        v = sel[g * epg:(g + 1) * epg]
        in_g = g_idx == g
        for i in range(epg):
            rank = jnp.zeros_like(g_idx)
            for j in range(epg):
                if j == i:
                    continue
                ahead = (v[j] >= v[i]) if j < i else (v[j] > v[i])
                rank = rank + jnp.where(ahead, 1, 0)
            chosen.append(in_g & (rank < 2))
    cls = jnp.zeros_like(sc[0])
    w_lo = jnp.zeros_like(sc[0])
    w_hi = jnp.zeros_like(sc[0])
    for g in range(N_GROUPS):
        for pid, (i, j) in enumerate(_PAIRS):
            lo, hi = g * epg + i, g * epg + j
            is_pair = chosen[lo] & chosen[hi]
            cls = jnp.where(is_pair, float(g * len(_PAIRS) + pid), cls)
            w_lo = jnp.where(is_pair, sc[lo], w_lo)
            w_hi = jnp.where(is_pair, sc[hi], w_hi)
    total = w_lo + w_hi
    return cls, w_lo / total, w_hi / total


def _merge_kernel(pre_ln, alpha, ya_ref, yb_ref, yc_ref, bg_ref, h_ref, g1_ref, sc2_ref, sh2_ref,
                  lig_ref, lib_ref, l1g_ref, l1b_ref, wa_ref, wb_ref, wc_ref, wo_ref, wr_ref, br_ref,
                  h1_ref, u2t_ref, route_ref, cnt_ref):
    tm = h_ref.shape[0]

    def branch(y_ref, w_ref, j):
        gate = _sigmoid(bg_ref[:, j * 1024:(j + 1) * 1024].astype(F32))
        return gate * jnp.dot(y_ref[...], w_ref[...], preferred_element_type=F32)

    mix = branch(ya_ref, wa_ref, 0) + branch(yb_ref, wb_ref, 1) + branch(yc_ref, wc_ref, 2)
    y = jnp.dot(mix.astype(BF16), wo_ref[...], preferred_element_type=F32)
    h = h_ref[...]
    if pre_ln:
        h = _ln(h, lig_ref[...], lib_ref[...])
    h1 = _ln(alpha * h + g1_ref[...] * y, l1g_ref[...], l1b_ref[...])
    h1_ref[...] = h1
    u2 = h1 * (1.0 + sc2_ref[...]) + sh2_ref[...]
    logits_t = _dot_nt(wr_ref[...], u2, precision=HIGHEST)
    cls, w_lo, w_hi = _route(logits_t, br_ref[...])

    @pl.when(pl.program_id(0) == 0)
    def _():
        cnt_ref[...] = jnp.zeros_like(cnt_ref)

    crow = lax.broadcasted_iota(jnp.int32, (N_CLASS_ROWS, tm), 0).astype(F32)
    onehot = jnp.where(crow == cls, 1.0, 0.0)
    earlier = lax.broadcasted_iota(jnp.int32, (tm, tm), 0) <= lax.broadcasted_iota(jnp.int32, (tm, tm), 1)
    incl = jnp.dot(onehot.astype(BF16), jnp.where(earlier, 1.0, 0.0).astype(BF16), preferred_element_type=F32)
    base = cnt_ref[...]
    rank = jnp.sum(onehot * (incl - 1.0 + base), axis=0, keepdims=True)
    cnt_ref[...] = base + incl[:, tm - 1:tm]

    route_ref[...] = jnp.concatenate([cls, w_lo, w_hi, rank, jnp.zeros((4, tm), F32)], axis=0)
    for s in range(TOKEN_SUBROWS):
        u2t_ref[pl.ds(s, tm, stride=TOKEN_SUBROWS), :] = u2[:, s * LANES:(s + 1) * LANES]


def _merge(ya, yb, yc, bg, h, mod_l, ln_in, ln1, w_a, w_b, w_c, w_o, w_rt, b_r, seq_len, ctx_row, pre_ln, alpha):
    n = ya.shape[0]
    tm = MERGE_TILE
    row = lambda i: (i, 0)
    const = lambda i: (0, 0)
    vec = pl.BlockSpec((1, 1024), const)
    wspec = lambda w: pl.BlockSpec(w.shape, const)
    return pl.pallas_call(
        functools.partial(_merge_kernel, pre_ln, alpha),
        grid=(n // tm,),
        in_specs=[
            pl.BlockSpec((tm, D_CONV), row),
            pl.BlockSpec((tm, D_ATTN), row),
            pl.BlockSpec((tm, D_MLSTM), row),
            pl.BlockSpec((tm, N_BRANCHES * 1024), row),
            pl.BlockSpec((tm, 1024), row),
            _mod_spec(2, tm, seq_len, ctx_row),
            _mod_spec(4, tm, seq_len, ctx_row),
            _mod_spec(3, tm, seq_len, ctx_row),
            vec, vec, vec, vec,
            wspec(w_a), wspec(w_b), wspec(w_c), wspec(w_o), wspec(w_rt), wspec(b_r),
        ],
        out_specs=[
            pl.BlockSpec((tm, 1024), row),
            pl.BlockSpec((tm * TOKEN_SUBROWS, LANES), row),
            pl.BlockSpec((8, tm), lambda i: (0, i)),
        ],
        out_shape=[
            jax.ShapeDtypeStruct((n, 1024), F32),
            jax.ShapeDtypeStruct((n * TOKEN_SUBROWS, LANES), F32),
            jax.ShapeDtypeStruct((8, n), F32),
        ],
        scratch_shapes=[pltpu.VMEM((N_CLASS_ROWS, 1), F32)],
        compiler_params=_cparams("arbitrary"),
        name="merge",
    )(ya, yb, yc, bg, h, mod_l, mod_l, mod_l, ln_in[0], ln_in[1], ln1[0], ln1[1], w_a, w_b, w_c, w_o, w_rt, b_r)


def _move_rows_kernel(scatter, pos_ref, src_ref, *rest):
    dst_ref, sem = rest[-2], rest[-1]
    rows = pos_ref.shape[-1]
    sub = TOKEN_SUBROWS

    def copy(k):
        t = pl.multiple_of(k * sub, sub)
        p = pl.multiple_of(pos_ref[0, k] * sub, sub)
        if scatter:
            return pltpu.make_async_copy(src_ref.at[pl.ds(t, sub)], dst_ref.at[pl.ds(p, sub)], sem)
        return pltpu.make_async_copy(src_ref.at[pl.ds(p, sub)], dst_ref.at[pl.ds(t, sub)], sem)

    def start(k, carry):
        copy(k).start()
        return carry

    def wait(k, carry):
        copy(k).wait()
        return carry

    lax.fori_loop(0, rows, start, 0)
    lax.fori_loop(0, rows, wait, 0)


def _move_rows(pos, src, n_out, scatter):
    n = pos.shape[0]
    rows = min(MOVE_ROWS, n)
    out_shape = (n_out * TOKEN_SUBROWS, LANES)
    any_spec = pl.BlockSpec(memory_space=pl.ANY)
    block_spec = pl.BlockSpec((rows * TOKEN_SUBROWS, LANES), lambda i: (i, 0))
    in_specs = [pl.BlockSpec((None, 1, rows), lambda i: (i, 0, 0), memory_space=pltpu.SMEM),
                block_spec if scatter else any_spec]
    args = [pos.reshape(n // rows, 1, rows), src]
    aliases = {}
    if scatter:
        in_specs.append(any_spec)
        args.append(jnp.zeros(out_shape, src.dtype))
        aliases = {2: 0}
    return pl.pallas_call(
        functools.partial(_move_rows_kernel, scatter),
        grid=(n // rows,),
        in_specs=in_specs,
        out_specs=any_spec if scatter else block_spec,
        out_shape=jax.ShapeDtypeStruct(out_shape, src.dtype),
        input_output_aliases=aliases,
        scratch_shapes=[pltpu.SemaphoreType.DMA],
        compiler_params=_cparams("arbitrary"),
        name="scatter_rows" if scatter else "gather_rows",
    )(*args)


def _untile_tokens(ref):
    tokens = ref.shape[0] // TOKEN_SUBROWS
    return jnp.concatenate([ref[pl.ds(s, tokens, stride=TOKEN_SUBROWS), :] for s in range(TOKEN_SUBROWS)], axis=1)


def _experts_kernel(lo_ref, hi_ref, valid_ref, x_ref, wr_ref, wgl_ref, wgh_ref, wdl_ref, wdh_ref, o_ref):
    i = pl.program_id(0)
    tm = x_ref.shape[0] // TOKEN_SUBROWS

    @pl.when(valid_ref[i] != 0)
    def _():
        x32 = _untile_tokens(x_ref)
        x = x32.astype(BF16)

        def affinity(e):
            logit = jnp.sum(x32 * wr_ref[pl.ds(e, 1), :], axis=1, keepdims=True)
            return _sigmoid(logit)

        def expert(wg_ref, wd_ref):
            gu = jnp.dot(x, wg_ref[...], preferred_element_type=F32)
            g_ = gu[:, 0:D_EXPERT]
            act = (g_ * _sigmoid(g_) * gu[:, D_EXPERT:2 * D_EXPERT]).astype(BF16)
            return jnp.dot(act, wd_ref[...], preferred_element_type=F32)

        s_lo, s_hi = affinity(lo_ref[i]), affinity(hi_ref[i])
        total = s_lo + s_hi
        out = (s_lo / total) * expert(wgl_ref, wdl_ref) + (s_hi / total) * expert(wgh_ref, wdh_ref)
        for s in range(TOKEN_SUBROWS):
            o_ref[pl.ds(s, tm, stride=TOKEN_SUBROWS), :] = out[:, s * LANES:(s + 1) * LANES]

    @pl.when(valid_ref[i] == 0)
    def _():
        o_ref[...] = jnp.zeros_like(o_ref)


def _experts(lo, hi, valid, xs, w_rt, w_gu, w_dn):
    tm = EXPERT_TILE
    n_tiles = xs.shape[0] // (tm * TOKEN_SUBROWS)
    row = lambda i, lo, hi, valid: (i, 0)
    grid_spec = pltpu.PrefetchScalarGridSpec(
        num_scalar_prefetch=3,
        grid=(n_tiles,),
        in_specs=[
            pl.BlockSpec((tm * TOKEN_SUBROWS, LANES), row),
            pl.BlockSpec(w_rt.shape, lambda i, lo, hi, valid: (0, 0)),
            pl.BlockSpec((None, 1024, 2 * D_EXPERT), lambda i, lo, hi, valid: (lo[i], 0, 0)),
            pl.BlockSpec((None, 1024, 2 * D_EXPERT), lambda i, lo, hi, valid: (hi[i], 0, 0)),
            pl.BlockSpec((None, D_EXPERT, 1024), lambda i, lo, hi, valid: (lo[i], 0, 0)),
            pl.BlockSpec((None, D_EXPERT, 1024), lambda i, lo, hi, valid: (hi[i], 0, 0)),
        ],
        out_specs=pl.BlockSpec((tm * TOKEN_SUBROWS, LANES), row),
    )
    return pl.pallas_call(
        _experts_kernel,
        grid_spec=grid_spec,
        out_shape=jax.ShapeDtypeStruct(xs.shape, F32),
        compiler_params=_cparams("arbitrary"),
        name="moe_experts",
    )(lo, hi, valid, xs, w_rt, w_gu, w_gu, w_dn, w_dn)


def _final_kernel(alpha, h1_ref, f_ref, g2_ref, lg_ref, lb_ref, o_ref):
    f = _untile_tokens(f_ref)
    o_ref[...] = _ln(alpha * h1_ref[...] + g2_ref[...] * f, lg_ref[...], lb_ref[...])


def _final_ln(h1, f, mod_l, ln2, seq_len, ctx_row, alpha):
    n = h1.shape[0]
    tm = TOKEN_TILE
    row = lambda i: (i, 0)
    const = lambda i: (0, 0)
    return pl.pallas_call(
        functools.partial(_final_kernel, alpha),
        grid=(n // tm,),
        in_specs=[
            pl.BlockSpec((tm, 1024), row),
            pl.BlockSpec((tm * TOKEN_SUBROWS, LANES), row),
            _mod_spec(5, tm, seq_len, ctx_row),
            pl.BlockSpec((1, 1024), const),
            pl.BlockSpec((1, 1024), const),
        ],
        out_specs=pl.BlockSpec((tm, 1024), row),
        out_shape=jax.ShapeDtypeStruct((n, 1024), F32),
        compiler_params=_cparams("parallel"),
        name="final_ln",
    )(h1, f, mod_l, ln2[0], ln2[1])


def _sort_plan(route_t, n):
    tm = EXPERT_TILE
    n_tiles = n // tm + N_CLASSES
    cls = route_t[AUX_CLS].astype(jnp.int32)
    rank = route_t[AUX_RANK].astype(jnp.int32)
    onehot = cls[:, None] == jnp.arange(N_CLASSES, dtype=jnp.int32)[None, :]
    counts = jnp.sum(onehot, axis=0, dtype=jnp.int32)
    padded = (counts + tm - 1) // tm * tm
    ends = jnp.cumsum(padded)
    offs = ends - padded
    pos = jnp.sum(jnp.where(onehot, offs[None, :], 0), axis=1) + rank
    tile_ends = ends // tm
    j = jnp.arange(n_tiles, dtype=jnp.int32)
    n_used = tile_ends[-1]
    valid = j < n_used
    tile_cls = jnp.sum(j[:, None] >= tile_ends[None, :], axis=1)
    last_cls = jnp.sum((n_used - 1) >= tile_ends)
    tile_cls = jnp.where(valid, tile_cls, last_cls)
    group, pid = tile_cls // len(_PAIRS), tile_cls % len(_PAIRS)
    pair = jnp.asarray(np.array(_PAIRS, dtype=np.int32))
    lo = group * EXPERTS_PER_GROUP + pair[pid, 0]
    hi = group * EXPERTS_PER_GROUP + pair[pid, 1]
    return pos.astype(jnp.int32), lo.astype(jnp.int32), hi.astype(jnp.int32), valid.astype(jnp.int32), n_tiles * tm


def _moe(u2t, route_t, h1, mod_l, ln2, w_rt, w_gu, w_dn, seq_len, ctx_row, alpha):
    n = h1.shape[0]
    pos, lo, hi, valid, n_rows = _sort_plan(route_t, n)
    xs = _move_rows(pos, u2t, n_rows, scatter=True)
    fs = _experts(lo, hi, valid, xs, w_rt, w_gu, w_dn)
    f = _move_rows(pos, fs, n, scatter=False)
    return _final_ln(h1, f, mod_l, ln2, seq_len, ctx_row, alpha)


def _rope_swap_index(n_heads):
    idx = np.arange(n_heads * HEAD_DIM)
    within = idx % (HEAD_DIM // 2)
    quarter = HEAD_DIM // 4
    return np.where(within < quarter, idx + quarter, idx - quarter)


def _rope_tables(seq_len, n_heads, scale):
    t = np.arange(seq_len)
    quarter = HEAD_DIM // 4
    inv = ROPE_BASE ** (-np.arange(quarter, dtype=np.float32) / quarter)
    d = np.arange(HEAD_DIM)
    pos = np.where((d // (HEAD_DIM // 2) == 0)[None, :], (t // GRID_W)[:, None], (t % GRID_W)[:, None])
    ang = jnp.asarray(pos.astype(np.float32)) * jnp.asarray(inv[d % quarter])[None, :]
    sign = np.where(d % (HEAD_DIM // 2) < quarter, -1.0, 1.0).astype(np.float32)
    cos = jnp.cos(ang) * scale
    sin = jnp.sin(ang) * (sign * scale)[None, :]
    return jnp.tile(cos, (1, n_heads)), jnp.tile(sin, (1, n_heads))


def _flat_tables(seq_len, n_heads, scale):
    return (jnp.full((seq_len, n_heads * HEAD_DIM), scale, F32), jnp.zeros((seq_len, n_heads * HEAD_DIM), F32))


def _prep_in_weights(w_in_l):
    splits = np.cumsum([2 * D_CONV, D_ATTN, D_KV, D_KV, D_MLSTM, D_MLSTM, D_MLSTM, D_MLSTM, N_GATE_COLS])
    a, q, k, v, mq, mk, mv, mo, mg, bg = jnp.split(w_in_l, splits, axis=1)
    w_main = jnp.concatenate([a, q, q[:, _rope_swap_index(N_Q_HEADS)], k, k[:, _rope_swap_index(N_KV_HEADS)],
                              mq, mv, mo, bg], axis=1).astype(BF16)
    order = np.array([d * 8 + kind * 4 + h for h in range(N_MLSTM_HEADS) for d in range(2) for kind in range(2)])
    w_t = jnp.concatenate([mk.T, mg[:, order].T, v.T], axis=0).astype(BF16)
    return w_main, w_t


def kernel(x, c, ctx, c_ctx, ln_in_g, ln_in_b, w_router, b_router, w_mod, b_mod, w_in, conv_w, conv_b, conv_ln_g,
           conv_ln_b, w_a_out, attn_sink, w_b_out, mlstm_gate_b, mlstm_norm_g, w_c_out, w_out, ln1_g, ln1_b,
           moe_w_gu, moe_w_dn, ln2_g, ln2_b):
    batch, seq_len, d = x.shape
    ctx_len = ctx.shape[1]
    depth = w_in.shape[0]
    alpha = (2.0 * depth) ** 0.25
    ctx_row = batch
    assert d == 1024 and batch < MOD_ROWS
    assert seq_len % MERGE_TILE == 0 and ctx_len % TOKEN_TILE == 0 and (batch * ctx_len) % MERGE_TILE == 0
    assert all(n % min(MOVE_ROWS, n) == 0 and n % EXPERT_TILE == 0 for n in (batch * seq_len, batch * ctx_len))

    cc = jnp.zeros((MOD_ROWS, d), F32).at[0:batch].set(c).at[batch].set(c_ctx)
    mod = _modulation(cc, w_mod, b_mod).reshape(depth, MOD_ROWS * N_MOD, 1, d)

    attn_scale = HEAD_DIM ** -0.5
    rope_lat = _rope_tables(seq_len, N_Q_HEADS, attn_scale) + _rope_tables(seq_len, N_KV_HEADS, 1.0)
    rope_ctx = _flat_tables(ctx_len, N_Q_HEADS, attn_scale) + _flat_tables(ctx_len, N_KV_HEADS, 1.0)

    vec = lambda t: t.reshape(1, -1)
    ln_in = (vec(ln_in_g), vec(ln_in_b))
    w_rt = w_router.T
    b_r = b_router.reshape(N_EXPERTS, 1)

    h = x.reshape(batch * seq_len, d)
    hc = ctx.reshape(batch * ctx_len, d)
    for l in range(depth):
        need_ctx = l < depth - 1
        pre_ln = l == 0
        mod_l = mod[l]
        w_main, w_t = _prep_in_weights(w_in[l])
        lat = _in_proj(h, ln_in[0], ln_in[1], mod_l, w_main, w_t, rope_lat, seq_len, None, pre_ln)
        cx = _in_proj(hc, ln_in[0], ln_in[1], mod_l, w_main, w_t, rope_ctx, ctx_len, ctx_row, pre_ln)
        a_l, q_l, k_l, v_l, mq_l, mv_l, mo_l, bg_l, kt_l, gt_l = lat
        a_c, q_c, k_c, v_c, mq_c, mv_c, mo_c, bg_c, kt_c, gt_c = cx

        conv_args = (conv_w[l], vec(conv_b[l]), vec(conv_ln_g[l]), vec(conv_ln_b[l]))
        gate_b = jnp.transpose(mlstm_gate_b[l], (2, 0, 1)).reshape(N_MLSTM_HEADS, 4, 1)
        ya = _conv_branch(a_l, *conv_args, seq_len)
        yb = _attn_latent(q_l, k_l, v_l, k_c, v_c, attn_sink[l], seq_len, ctx_len)
        yc, yc_c = _mlstm_branch((mq_c, kt_c, mv_c, mo_c, gt_c), (mq_l, kt_l, mv_l, mo_l, gt_l),
                                 gate_b, vec(mlstm_norm_g[l]), seq_len, ctx_len, need_ctx)

        ln1 = (vec(ln1_g[l]), vec(ln1_b[l]))
        ln2 = (vec(ln2_g[l]), vec(ln2_b[l]))
        w_a, w_b, w_c, w_o = (w.astype(BF16) for w in (w_a_out[l], w_b_out[l], w_c_out[l], w_out[l]))
        w_gu = moe_w_gu[l].astype(BF16)
        w_dn = moe_w_dn[l].astype(BF16)

        h1, pay, route_t = _merge(ya, yb, yc, bg_l, h, mod_l, ln_in, ln1, w_a, w_b, w_c, w_o, w_rt, b_r,
                                  seq_len, None, pre_ln, alpha)
        h = _moe(pay, route_t, h1, mod_l, ln2, w_rt, w_gu, w_dn, seq_len, None, alpha)
        if need_ctx:
            ya_c = _conv_branch(a_c, *conv_args, ctx_len)
            yb_c = _attn_context(q_c, k_c, v_c, attn_sink[l], ctx_len)
            h1c, pay_c, route_tc = _merge(ya_c, yb_c, yc_c, bg_c, hc, mod_l, ln_in, ln1, w_a, w_b, w_c, w_o, w_rt,
                                          b_r, ctx_len, ctx_row, pre_ln, alpha)
            hc = _moe(pay_c, route_tc, h1c, mod_l, ln2, w_rt, w_gu, w_dn, ctx_len, ctx_row, alpha)
    return h.reshape(batch, seq_len, d)
```

```python
import functools

import numpy as np
import jax
import jax.numpy as jnp
from jax import lax
from jax.experimental import pallas as pl
from jax.experimental.pallas import tpu as pltpu

GRID_W = 64
LN_EPS = 1e-5
D_CONV = 512
CONV_WIDTH = 31
N_Q_HEADS = 8
N_KV_HEADS = 2
HEAD_DIM = 64
WINDOW = 128
BLOCK = 128
ROPE_BASE = 10000.0
D_ATTN = N_Q_HEADS * HEAD_DIM
D_KV = N_KV_HEADS * HEAD_DIM
N_MLSTM_HEADS = 4
MLSTM_HEAD_DIM = 128
D_MLSTM = N_MLSTM_HEADS * MLSTM_HEAD_DIM
N_GATE_COLS = 2 * 2 * N_MLSTM_HEADS
N_BRANCHES = 3
N_EXPERTS = 16
N_GROUPS = 4
EXPERTS_PER_GROUP = N_EXPERTS // N_GROUPS
D_EXPERT = 512
N_MOD = 6

LANES = 128
V7X_VMEM_LIMIT_BYTES = 56 * 1024 * 1024

MOD_ROWS = 16
MOD_COL_BLOCK = 512
TOKEN_TILE = 256
EXPERT_TILE = 256
FINAL_TILE = 512
MLSTM_CHUNK = 128
MLSTM_HEADS_PER_STEP = 4
MERGE_TILE = 512
CONV_ROWS = 64
CONV_PAD = 16

_PAIRS = [(i, j) for i in range(EXPERTS_PER_GROUP) for j in range(i + 1, EXPERTS_PER_GROUP)]
N_CLASSES = N_GROUPS * len(_PAIRS)
N_CLASS_ROWS = 32
AUX_CLS, AUX_W_LO, AUX_W_HI, AUX_RANK = 0, 1, 2, 3
TOKEN_SUBROWS = 1024 // LANES

F32 = jnp.float32
BF16 = jnp.bfloat16
HIGHEST = lax.Precision.HIGHEST
NEG_INF = float("-inf")

_C_A = 0
_C_Q = _C_A + 2 * D_CONV
_C_QS = _C_Q + D_ATTN
_C_K = _C_QS + D_ATTN
_C_KS = _C_K + D_KV
_C_MQ = _C_KS + D_KV
_C_MV = _C_MQ + D_MLSTM
_C_MO = _C_MV + D_MLSTM
_C_BG = _C_MO + D_MLSTM
_C_END = _C_BG + N_BRANCHES * 1024
_R_KT = 0
_R_GT = _R_KT + D_MLSTM
_R_VT = _R_GT + N_GATE_COLS


def _cparams(*sem):
    return pltpu.CompilerParams(dimension_semantics=sem, vmem_limit_bytes=V7X_VMEM_LIMIT_BYTES)


def _ln(x, g, b):
    mu = jnp.mean(x, axis=-1, keepdims=True)
    xc = x - mu
    var = jnp.mean(xc * xc, axis=-1, keepdims=True)
    return xc * lax.rsqrt(var + LN_EPS) * g + b


def _sigmoid(x):
    return 0.5 * jnp.tanh(0.5 * x) + 0.5


def _log_sigmoid(x):
    return jnp.minimum(x, 0.0) - jnp.log(1.0 + jnp.exp(-jnp.abs(x)))


def _dot_nt(a, b, precision=None):
    return lax.dot_general(a, b, (((1,), (1,)), ((), ())), preferred_element_type=F32, precision=precision)


def _mod_kernel(c_ref, w_ref, b_ref, o_ref):
    c = c_ref[...]
    s = c * _sigmoid(c)
    o_ref[...] = jnp.dot(s, w_ref[...], preferred_element_type=F32, precision=HIGHEST) + b_ref[...]


def _modulation(cc, w_mod, b_mod):
    depth, d, n = w_mod.shape
    return pl.pallas_call(
        _mod_kernel,
        grid=(depth, n // MOD_COL_BLOCK),
        in_specs=[
            pl.BlockSpec((MOD_ROWS, d), lambda l, j: (0, 0)),
            pl.BlockSpec((None, d, MOD_COL_BLOCK), lambda l, j: (l, 0, j)),
            pl.BlockSpec((None, 1, MOD_COL_BLOCK), lambda l, j: (l, 0, j)),
        ],
        out_specs=pl.BlockSpec((None, MOD_ROWS, MOD_COL_BLOCK), lambda l, j: (l, 0, j)),
        out_shape=jax.ShapeDtypeStruct((depth, MOD_ROWS, n), F32),
        compiler_params=_cparams("parallel", "parallel"),
        name="modulation",
    )(cc, w_mod, b_mod.reshape(depth, 1, n))


def _mod_spec(which, tile, seq_len, ctx_row):
    tiles_per_seq = seq_len // tile
    if ctx_row is None:
        return pl.BlockSpec((None, 1, 1024), lambda i, *_: ((i // tiles_per_seq) * N_MOD + which, 0, 0))
    return pl.BlockSpec((None, 1, 1024), lambda i, *_: (ctx_row * N_MOD + which, 0, 0))


def _in_kernel(pre_ln, x_ref, lg_ref, lb_ref, sc_ref, sh_ref, w_ref, wt_ref, cq_ref, sq_ref, ck_ref, sk_ref,
               a_ref, q_ref, k_ref, v_ref, mq_ref, mv_ref, mo_ref, bg_ref, kt_ref, gt_ref):
    x = x_ref[...]
    if pre_ln:
        x = _ln(x, lg_ref[...], lb_ref[...])
    u = (x * (1.0 + sc_ref[...]) + sh_ref[...]).astype(BF16)

    def seg(lo, hi):
        return jnp.dot(u, w_ref[:, lo:hi], preferred_element_type=F32)

    a_ref[...] = seg(_C_A, _C_Q).astype(BF16)
    q_ref[...] = (seg(_C_Q, _C_QS) * cq_ref[...] + seg(_C_QS, _C_K) * sq_ref[...]).astype(BF16)
    k_ref[...] = (seg(_C_K, _C_KS) * ck_ref[...] + seg(_C_KS, _C_MQ) * sk_ref[...]).astype(BF16)
    v_ref[...] = _dot_nt(wt_ref[_R_VT:_R_VT + D_KV, :], u).astype(BF16)
    mq_ref[...] = seg(_C_MQ, _C_MV).astype(BF16)
    mv_ref[...] = seg(_C_MV, _C_MO).astype(BF16)
    mo_ref[...] = seg(_C_MO, _C_BG).astype(BF16)
    for j in range(N_BRANCHES):
        bg_ref[:, j * 1024:(j + 1) * 1024] = _sigmoid(seg(_C_BG + j * 1024, _C_BG + (j + 1) * 1024)).astype(BF16)
    n_chunks = u.shape[0] // MLSTM_CHUNK
    kt = _dot_nt(wt_ref[_R_KT:_R_KT + D_MLSTM, :], u)
    for h in range(N_MLSTM_HEADS):
        for c in range(n_chunks):
            kt_ref[h, c] = kt[h * MLSTM_HEAD_DIM:(h + 1) * MLSTM_HEAD_DIM,
                              c * MLSTM_CHUNK:(c + 1) * MLSTM_CHUNK].astype(BF16)
    gt = _dot_nt(wt_ref[_R_GT:_R_GT + N_GATE_COLS, :], u)
    for h in range(N_MLSTM_HEADS):
        for c in range(n_chunks):
            gt_ref[h, c] = gt[h * 4:(h + 1) * 4, c * MLSTM_CHUNK:(c + 1) * MLSTM_CHUNK]


def _in_proj(x, ln_g, ln_b, mod_l, w_main, w_t, rope, seq_len, ctx_row, pre_ln):
    n = x.shape[0]
    tm = TOKEN_TILE
    cq, sq, ck, sk = rope
    tps = seq_len // tm
    nch = n // MLSTM_CHUNK
    row = lambda i: (i, 0)
    pos = lambda i: (i % tps, 0)
    const = lambda i: (0, 0)
    out_shape = [
        jax.ShapeDtypeStruct((n, 2 * D_CONV), BF16),
        jax.ShapeDtypeStruct((n, D_ATTN), BF16),
        jax.ShapeDtypeStruct((n, D_KV), BF16),
        jax.ShapeDtypeStruct((D_KV, n), BF16),
        jax.ShapeDtypeStruct((n, D_MLSTM), BF16),
        jax.ShapeDtypeStruct((n, D_MLSTM), BF16),
        jax.ShapeDtypeStruct((n, D_MLSTM), BF16),
        jax.ShapeDtypeStruct((n, N_BRANCHES * 1024), BF16),
        jax.ShapeDtypeStruct((N_MLSTM_HEADS, nch, MLSTM_HEAD_DIM, MLSTM_CHUNK), BF16),
        jax.ShapeDtypeStruct((N_MLSTM_HEADS, nch, 4, MLSTM_CHUNK), F32),
    ]
    cpt = tm // MLSTM_CHUNK
    out_specs = [
        pl.BlockSpec((tm, 2 * D_CONV), row),
        pl.BlockSpec((tm, D_ATTN), row),
        pl.BlockSpec((tm, D_KV), row),
        pl.BlockSpec((D_KV, tm), lambda i: (0, i)),
        pl.BlockSpec((tm, D_MLSTM), row),
        pl.BlockSpec((tm, D_MLSTM), row),
        pl.BlockSpec((tm, D_MLSTM), row),
        pl.BlockSpec((tm, N_BRANCHES * 1024), row),
        pl.BlockSpec((N_MLSTM_HEADS, cpt, MLSTM_HEAD_DIM, MLSTM_CHUNK), lambda i: (0, i, 0, 0)),
        pl.BlockSpec((N_MLSTM_HEADS, cpt, 4, MLSTM_CHUNK), lambda i: (0, i, 0, 0)),
    ]
    in_specs = [
        pl.BlockSpec((tm, 1024), row),
        pl.BlockSpec((1, 1024), const),
        pl.BlockSpec((1, 1024), const),
        _mod_spec(1, tm, seq_len, ctx_row),
        _mod_spec(0, tm, seq_len, ctx_row),
        pl.BlockSpec(w_main.shape, const, pipeline_mode=pl.Buffered(1)),
        pl.BlockSpec(w_t.shape, const, pipeline_mode=pl.Buffered(1)),
        pl.BlockSpec((tm, D_ATTN), pos),
        pl.BlockSpec((tm, D_ATTN), pos),
        pl.BlockSpec((tm, D_KV), pos),
        pl.BlockSpec((tm, D_KV), pos),
    ]
    return pl.pallas_call(
        functools.partial(_in_kernel, pre_ln),
        grid=(n // tm,),
        in_specs=in_specs,
        out_specs=out_specs,
        out_shape=out_shape,
        compiler_params=_cparams("parallel"),
        name="in_proj",
    )(x, ln_g, ln_b, mod_l, mod_l, w_main, w_t, cq, sq, ck, sk)


def _conv_kernel(a_ref, w_ref, cb_ref, g_ref, b_ref, o_ref, upad_ref):
    t = a_ref.shape[0]
    zeros = jnp.zeros((CONV_PAD, D_CONV), F32)
    upad_ref[0:CONV_PAD, :] = zeros
    upad_ref[CONV_PAD + t:2 * CONV_PAD + t, :] = zeros
    val = a_ref[:, 0:D_CONV].astype(F32)
    gate = a_ref[:, D_CONV:2 * D_CONV].astype(F32)
    upad_ref[CONV_PAD:CONV_PAD + t, :] = val * _sigmoid(gate)
    half = CONV_WIDTH // 2

    def body(c, carry):
        r0 = pl.multiple_of(c * CONV_ROWS, CONV_ROWS)
        n_win = CONV_ROWS + 2 * CONV_PAD
        win = upad_ref[pl.ds(r0, n_win), :]
        acc = jnp.zeros((CONV_ROWS, D_CONV), F32) + cb_ref[...]
        for res in range(8):
            rolled = win if res == 0 else pltpu.roll(win, shift=n_win - res, axis=0)
            for k in range(CONV_WIDTH):
                off = CONV_PAD - half + k
                if off % 8 == res:
                    acc = acc + rolled[off - res:off - res + CONV_ROWS, :] * w_ref[k:k + 1, :]
        y = _ln(acc, g_ref[...], b_ref[...])
        o_ref[pl.ds(r0, CONV_ROWS), :] = (y * _sigmoid(y)).astype(BF16)
        return carry

    lax.fori_loop(0, t // CONV_ROWS, body, 0)


def _conv_branch(a_in, conv_w, conv_b, ln_g, ln_b, seq_len):
    n = a_in.shape[0]
    const = lambda b: (0, 0)
    return pl.pallas_call(
        _conv_kernel,
        grid=(n // seq_len,),
        in_specs=[
            pl.BlockSpec((seq_len, 2 * D_CONV), lambda b: (b, 0)),
            pl.BlockSpec((CONV_WIDTH, D_CONV), const),
            pl.BlockSpec((1, D_CONV), const),
            pl.BlockSpec((1, D_CONV), const),
            pl.BlockSpec((1, D_CONV), const),
        ],
        out_specs=pl.BlockSpec((seq_len, D_CONV), lambda b: (b, 0)),
        out_shape=jax.ShapeDtypeStruct((n, D_CONV), BF16),
        scratch_shapes=[pltpu.VMEM((seq_len + 2 * CONV_PAD, D_CONV), F32)],
        compiler_params=_cparams("parallel"),
        name="conv_branch",
    )(a_in, conv_w, conv_b, ln_g, ln_b)


def _attn_heads(q, keys, vals_t, masks, sink_ref, o_ref):
    rows = q.shape[0]
    group = N_Q_HEADS // N_KV_HEADS
    for hk in range(N_KV_HEADS):
        lo = hk * HEAD_DIM
        qs = jnp.concatenate([q[:, (hk * group + g) * HEAD_DIM:(hk * group + g + 1) * HEAD_DIM]
                              for g in range(group)], axis=0)
        sink = jnp.concatenate([jnp.full((1, rows), sink_ref[hk * group + g], F32) for g in range(group)], axis=1)
        scores = []
        m = sink
        for kk, mask in zip(keys, masks):
            s = _dot_nt(kk[:, lo:lo + HEAD_DIM], qs)
            if mask is not None:
                s = jnp.where(mask, s, NEG_INF)
            scores.append(s)
            m = jnp.maximum(m, jnp.max(s, axis=0, keepdims=True))
        acc = jnp.zeros((2 * HEAD_DIM, rows * group), F32)
        for s, vt in zip(scores, vals_t):
            n_k = s.shape[0]
            p = jnp.exp(s - m).astype(BF16)
            ones_rows = jnp.where(lax.broadcasted_iota(jnp.int32, (HEAD_DIM, n_k), 0) == 0, 1.0, 0.0).astype(BF16)
            v_aug = jnp.concatenate([vt[lo:lo + HEAD_DIM, :], ones_rows], axis=0)
            acc = acc + jnp.dot(v_aug, p, preferred_element_type=F32)
        denom = acc[HEAD_DIM:HEAD_DIM + 1, :] + jnp.exp(sink - m)
        o_t = acc * (1.0 / denom)
        for g in range(group):
            col = (hk * group + g) * HEAD_DIM
            o_ref[:, col:col + HEAD_DIM] = o_t[:, g * rows:(g + 1) * rows].T[:, 0:HEAD_DIM].astype(BF16)


def _attn_lat_kernel(sink_ref, q_ref, kp_ref, k0_ref, kn_ref, vp_ref, v0_ref, vn_ref, kc_ref, vc_ref, o_ref):
    n = pl.program_id(1)
    nb = pl.num_programs(1)
    stacked = (N_Q_HEADS // N_KV_HEADS) * BLOCK
    ki = lax.broadcasted_iota(jnp.int32, (BLOCK, stacked), 0)
    qi = lax.broadcasted_iota(jnp.int32, (BLOCK, stacked), 1) % BLOCK
    mask_prev = ki >= qi + jnp.where(n > 0, 0, BLOCK)
    mask_next = ki <= qi - jnp.where(n < nb - 1, 0, BLOCK)
    _attn_heads(q_ref[...],
                [kp_ref[...], k0_ref[...], kn_ref[...], kc_ref[...]],
                [vp_ref[...], v0_ref[...], vn_ref[...], vc_ref[...]],
                [mask_prev, None, mask_next, None], sink_ref, o_ref)


def _attn_ctx_kernel(sink_ref, q_ref, kc_ref, vc_ref, o_ref):
    _attn_heads(q_ref[...], [kc_ref[...]], [vc_ref[...]], [None], sink_ref, o_ref)


def _attn_latent(q, k, vt, kc, vct, sink, seq_len, ctx_len):
    n = q.shape[0]
    nb = seq_len // BLOCK
    batch = n // seq_len
    blk_prev = lambda b, j: b * nb + jnp.maximum(j - 1, 0)
    blk_next = lambda b, j: b * nb + jnp.minimum(j + 1, nb - 1)
    cur = lambda b, j: (b * nb + j, 0)
    kspec = lambda blk: pl.BlockSpec((BLOCK, D_KV), lambda b, j: (blk(b, j), 0))
    vspec = lambda blk: pl.BlockSpec((D_KV, BLOCK), lambda b, j: (0, blk(b, j)))
    blk_cur = lambda b, j: b * nb + j
    return pl.pallas_call(
        _attn_lat_kernel,
        grid=(batch, nb),
        in_specs=[
            pl.BlockSpec(memory_space=pltpu.SMEM),
            pl.BlockSpec((BLOCK, D_ATTN), cur),
            kspec(blk_prev), kspec(blk_cur), kspec(blk_next), vspec(blk_prev), vspec(blk_cur), vspec(blk_next),
            pl.BlockSpec((ctx_len, D_KV), lambda b, j: (b, 0)),
            pl.BlockSpec((D_KV, ctx_len), lambda b, j: (0, b)),
        ],
        out_specs=pl.BlockSpec((BLOCK, D_ATTN), cur),
        out_shape=jax.ShapeDtypeStruct((n, D_ATTN), BF16),
        compiler_params=_cparams("parallel", "parallel"),
        name="attn_latent",
    )(sink, q, k, k, k, vt, vt, vt, kc, vct)


def _attn_context(qc, kc, vct, sink, ctx_len):
    n = qc.shape[0]
    blk = lambda b: (b, 0)
    return pl.pallas_call(
        _attn_ctx_kernel,
        grid=(n // ctx_len,),
        in_specs=[
            pl.BlockSpec(memory_space=pltpu.SMEM),
            pl.BlockSpec((ctx_len, D_ATTN), blk),
            pl.BlockSpec((ctx_len, D_KV), blk),
            pl.BlockSpec((D_KV, ctx_len), lambda b: (0, b)),
        ],
        out_specs=pl.BlockSpec((ctx_len, D_ATTN), blk),
        out_shape=jax.ShapeDtypeStruct((n, D_ATTN), BF16),
        compiler_params=_cparams("parallel"),
        name="attn_context",
    )(sink, qc, kc, vct)


def _mlstm_kernel(ctx_out, nch_c, nch_l, hps,
                  qc_ref, ktc_ref, vc_ref, moc_ref, gc_ref,
                  ql_ref, ktl_ref, vl_ref, mol_ref, gl_ref,
                  gb_ref, ng_ref, *rest):
    if ctx_out:
        yl_ref, yc_ref, af_ref, lff_ref, ab_ref, lfb_ref, cf_ref, cb_ref, hf_ref, hb_ref = rest
    else:
        yl_ref, af_ref, lff_ref, ab_ref, lfb_ref, cf_ref, cb_ref, hf_ref, hb_ref = rest
        yc_ref = None
    lc = MLSTM_CHUNK
    dh = MLSTM_HEAD_DIM
    k_scale = MLSTM_HEAD_DIM ** -0.5
    ti = lax.broadcasted_iota(jnp.int32, (lc, lc), 0)
    si = lax.broadcasted_iota(jnp.int32, (lc, lc), 1)
    lower = si <= ti
    upper = si >= ti
    pre_mat = jnp.where(upper, 1.0, 0.0)
    suf_mat = jnp.where(lower, 1.0, 0.0)
    n_rows = nch_c + nch_l
    pad_rows = -n_rows % 8

    for hh in range(hps):
        gb = gb_ref[hh]

        def gate_rows(kind):
            rows = ([gc_ref[hh, c, kind:kind + 1, :] for c in range(nch_c)]
                    + [gl_ref[hh, c, kind:kind + 1, :] for c in range(nch_l)])
            rows = jnp.concatenate(rows, axis=0) + gb[kind:kind + 1, :]
            if pad_rows:
                rows = jnp.concatenate([rows, jnp.zeros((pad_rows, lc), F32)], axis=0)
            return rows

        lf_f = _log_sigmoid(gate_rows(1))
        lf_b = _log_sigmoid(gate_rows(3))
        a_f = gate_rows(0) - jnp.dot(lf_f, pre_mat, preferred_element_type=F32, precision=HIGHEST)
        a_b = gate_rows(2) - jnp.dot(lf_b, suf_mat, preferred_element_type=F32, precision=HIGHEST)
        for c in range(n_rows):
            af_ref[hh * n_rows + c] = a_f[c:c + 1, :]
            lff_ref[hh * n_rows + c] = lf_f[c:c + 1, :]
            ab_ref[hh * n_rows + c] = a_b[c:c + 1, :]
            lfb_ref[hh * n_rows + c] = lf_b[c:c + 1, :]

    cf_ref[...] = jnp.zeros_like(cf_ref)
    cb_ref[...] = jnp.zeros_like(cb_ref)
    ones_col = jnp.where(lax.broadcasted_iota(jnp.int32, (lc, dh), 1) == 0, 1.0, 0.0).astype(BF16)

    def chunk(q, kt, v, a_row, lf_row, c_ref, hh, m, mask):
        a_mat = jnp.where(mask, jnp.broadcast_to(a_row, (lc, lc)), NEG_INF)
        cm = jnp.max(a_mat, axis=1, keepdims=True)
        b_col = jnp.sum(jnp.where(mask, jnp.broadcast_to(lf_row, (lc, lc)), 0.0), axis=1, keepdims=True)
        mx = jnp.maximum(m, cm)
        mx_last = jnp.maximum(m, jnp.max(a_row, axis=1, keepdims=True))
        w = jnp.exp(a_mat - mx)
        s = jnp.dot(q, kt, preferred_element_type=F32) * k_scale
        p = (s * w).astype(BF16)
        w_s = jnp.exp(a_row - mx_last) * k_scale
        ktw = (kt.astype(F32) * w_s).astype(BF16)
        v_aug = jnp.concatenate([v, ones_col], axis=1)
        both = jnp.dot(jnp.concatenate([p, ktw], axis=0), v_aug, preferred_element_type=F32)
        c_old = c_ref[hh]
        inter = jnp.dot(q, c_old.astype(BF16), preferred_element_type=F32)
        tot = both[0:lc, :] + jnp.exp(m - mx) * inter
        den = tot[:, dh:dh + 1]
        h = tot[:, 0:dh] / jnp.maximum(jnp.abs(den), jnp.exp(-(b_col + mx)))
        c_ref[hh] = jnp.exp(m - mx_last) * c_old + both[lc:lc + dh, :]
        m_new = jnp.sum(lf_row, axis=1, keepdims=True) + mx_last
        return h, m_new

    def step(q_ref, kt_ref, v_ref, row0, c_f, c_b, rf, rb, ms):
        out = []
        for hh in range(hps):
            cols = slice(hh * dh, (hh + 1) * dh)
            base = hh * n_rows + row0
            h_f, m_f = chunk(q_ref[pl.ds(rf, lc), cols], kt_ref[hh, c_f], v_ref[pl.ds(rf, lc), cols],
                             af_ref[base + c_f], lff_ref[base + c_f], cf_ref, hh, ms[2 * hh], lower)
            h_b, m_b = chunk(q_ref[pl.ds(rb, lc), cols], kt_ref[hh, c_b], v_ref[pl.ds(rb, lc), cols],
                             ab_ref[base + c_b], lfb_ref[base + c_b], cb_ref, hh, ms[2 * hh + 1], upper)
            hf_ref[pl.ds(rf, lc), cols] = h_f
            hb_ref[pl.ds(rb, lc), cols] = h_b
            out += [m_f, m_b]
        return tuple(out)

    def finish(n_tok, mo_ref, o_ref):
        for hh in range(hps):
            cols = slice(hh * dh, (hh + 1) * dh)
            h = hf_ref[0:n_tok, cols] + hb_ref[0:n_tok, cols]
            mu = jnp.mean(h, axis=-1, keepdims=True)
            hc = h - mu
            var = jnp.mean(hc * hc, axis=-1, keepdims=True)
            y = hc * lax.rsqrt(var + LN_EPS) * ng_ref[:, cols]
            o_ref[:, cols] = (_sigmoid(mo_ref[:, cols].astype(F32)) * y).astype(BF16)

    ms = tuple(jnp.zeros((1, 1), F32) for _ in range(2 * hps))
    for c in range(nch_c):
        c_b = nch_c - 1 - c
        ms = step(qc_ref, ktc_ref, vc_ref, 0, c, c_b, c * lc, c_b * lc, ms)
    if ctx_out:
        finish(nch_c * lc, moc_ref, yc_ref)

    def body(c, ms):
        c_b = nch_l - 1 - c
        return step(ql_ref, ktl_ref, vl_ref, nch_c, c, c_b,
                    pl.multiple_of(c * lc, lc), pl.multiple_of(c_b * lc, lc), ms)

    lax.fori_loop(0, nch_l, body, ms)
    finish(nch_l * lc, mol_ref, yl_ref)


def _mlstm_branch(ctx_p, lat_p, gate_b, norm_g, seq_len, ctx_len, ctx_out):
    mq_c, kt_c, mv_c, mo_c, g_c = ctx_p
    mq_l, kt_l, mv_l, mo_l, g_l = lat_p
    n_l, n_c = mq_l.shape[0], mq_c.shape[0]
    batch = n_l // seq_len
    lc, dh, hps = MLSTM_CHUNK, MLSTM_HEAD_DIM, MLSTM_HEADS_PER_STEP
    nch_c, nch_l = ctx_len // lc, seq_len // lc

    def stream(t, nch):
        tok = pl.BlockSpec((t, hps * dh), lambda b, h: (b, h))
        return [tok,
                pl.BlockSpec((hps, nch, dh, lc), lambda b, h: (h, b, 0, 0)),
                tok, tok,
                pl.BlockSpec((hps, nch, 4, lc), lambda b, h: (h, b, 0, 0))]

    in_specs = stream(ctx_len, nch_c) + stream(seq_len, nch_l) + [
        pl.BlockSpec((hps, 4, 1), lambda b, h: (h, 0, 0)),
        pl.BlockSpec((1, hps * dh), lambda b, h: (0, h)),
    ]
    out_specs = [pl.BlockSpec((seq_len, hps * dh), lambda b, h: (b, h))]
    out_shape = [jax.ShapeDtypeStruct((n_l, D_MLSTM), BF16)]
    if ctx_out:
        out_specs.append(pl.BlockSpec((ctx_len, hps * dh), lambda b, h: (b, h)))
        out_shape.append(jax.ShapeDtypeStruct((n_c, D_MLSTM), BF16))
    row_scratch = pltpu.VMEM((hps * (nch_c + nch_l), 1, lc), F32)
    outs = pl.pallas_call(
        functools.partial(_mlstm_kernel, ctx_out, nch_c, nch_l, hps),
        grid=(batch, N_MLSTM_HEADS // hps),
        in_specs=in_specs,
        out_specs=out_specs,
        out_shape=out_shape,
        scratch_shapes=[row_scratch, row_scratch, row_scratch, row_scratch,
                        pltpu.VMEM((hps, dh, 2 * dh), F32), pltpu.VMEM((hps, dh, 2 * dh), F32),
                        pltpu.VMEM((seq_len, hps * dh), F32), pltpu.VMEM((seq_len, hps * dh), F32)],
        compiler_params=_cparams("parallel", "parallel"),
        name="mlstm_branch",
    )(mq_c, kt_c, mv_c, mo_c, g_c, mq_l, kt_l, mv_l, mo_l, g_l, gate_b, norm_g)
    return outs if ctx_out else (outs[0], None)


def _route(logits_t, br):
    sc = [_sigmoid(logits_t[e:e + 1, :]) for e in range(N_EXPERTS)]
    sel = [sc[e] + br[e:e + 1, :] for e in range(N_EXPERTS)]
    epg = EXPERTS_PER_GROUP
    group_score = []
    for g in range(N_GROUPS):
        v = sel[g * epg:(g + 1) * epg]
        best = None
        for i in range(epg):
            for j in range(i + 1, epg):
                pair = v[i] + v[j]
                best = pair if best is None else jnp.maximum(best, pair)
        group_score.append(best)
    g_idx = jnp.zeros_like(group_score[0], dtype=jnp.int32)
    best = group_score[0]
    for g in range(1, N_GROUPS):
        better = group_score[g] > best
        g_idx = jnp.where(better, g, g_idx)
        best = jnp.maximum(best, group_score[g])
    chosen = []
    for g in range(N_GROUPS):
        v = sel[g * epg:(g + 1) * epg]
        in_g = g_idx == g
        for i in range(epg):
            rank = jnp.zeros_like(g_idx)
            for j in range(epg):
                if j == i:
                    continue
                ahead = (v[j] >= v[i]) if j < i else (v[j] > v[i])
                rank = rank + jnp.where(ahead, 1, 0)
            chosen.append(in_g & (rank < 2))
    cls = jnp.zeros_like(sc[0])
    w_lo = jnp.zeros_like(sc[0])
    w_hi = jnp.zeros_like(sc[0])
    for g in range(N_GROUPS):
        for pid, (i, j) in enumerate(_PAIRS):
            lo, hi = g * epg + i, g * epg + j
            is_pair = chosen[lo] & chosen[hi]
            cls = jnp.where(is_pair, float(g * len(_PAIRS) + pid), cls)
            w_lo = jnp.where(is_pair, sc[lo], w_lo)
            w_hi = jnp.where(is_pair, sc[hi], w_hi)
    total = w_lo + w_hi
    return cls, w_lo / total, w_hi / total


def _merge_kernel(pre_ln, alpha, ya_ref, yb_ref, yc_ref, bg_ref, h_ref, g1_ref, sc2_ref, sh2_ref,
                  lig_ref, lib_ref, l1g_ref, l1b_ref, wa_ref, wb_ref, wc_ref, wo_ref, wr_ref, br_ref,
                  h1_ref, u2t_ref, route_ref, cnt_ref):
    tm = h_ref.shape[0]

    def branch(y_ref, w_ref, j):
        gate = bg_ref[:, j * 1024:(j + 1) * 1024].astype(F32)
        return gate * jnp.dot(y_ref[...], w_ref[...], preferred_element_type=F32)

    mix = branch(ya_ref, wa_ref, 0) + branch(yb_ref, wb_ref, 1) + branch(yc_ref, wc_ref, 2)
    y = jnp.dot(mix.astype(BF16), wo_ref[...], preferred_element_type=F32)
    h = h_ref[...]
    if pre_ln:
        h = _ln(h, lig_ref[...], lib_ref[...])
    h1 = _ln(alpha * h + g1_ref[...] * y, l1g_ref[...], l1b_ref[...])
    h1_ref[...] = h1
    u2 = h1 * (1.0 + sc2_ref[...]) + sh2_ref[...]
    logits_t = _dot_nt(wr_ref[...], u2, precision=HIGHEST)
    cls, w_lo, w_hi = _route(logits_t, br_ref[...])

    @pl.when(pl.program_id(0) == 0)
    def _():
        cnt_ref[...] = jnp.zeros_like(cnt_ref)

    crow = lax.broadcasted_iota(jnp.int32, (N_CLASS_ROWS, tm), 0).astype(F32)
    onehot = jnp.where(crow == cls, 1.0, 0.0)
    earlier = lax.broadcasted_iota(jnp.int32, (tm, tm), 0) <= lax.broadcasted_iota(jnp.int32, (tm, tm), 1)
    incl = jnp.dot(onehot.astype(BF16), jnp.where(earlier, 1.0, 0.0).astype(BF16), preferred_element_type=F32)
    base = cnt_ref[...]
    rank = jnp.sum(onehot * (incl - 1.0 + base), axis=0, keepdims=True)
    cnt_ref[...] = base + incl[:, tm - 1:tm]

    route_ref[...] = jnp.concatenate([cls, w_lo, w_hi, rank, jnp.zeros((4, tm), F32)], axis=0)
    for s in range(TOKEN_SUBROWS):
        u2t_ref[pl.ds(s, tm, stride=TOKEN_SUBROWS), :] = u2[:, s * LANES:(s + 1) * LANES]


def _merge(ya, yb, yc, bg, h, mod_l, ln_in, ln1, w_a, w_b, w_c, w_o, w_rt, b_r, seq_len, ctx_row, pre_ln, alpha):
    n = ya.shape[0]
    tm = MERGE_TILE
    row = lambda i: (i, 0)
    const = lambda i: (0, 0)
    vec = pl.BlockSpec((1, 1024), const)
    wspec = lambda w: pl.BlockSpec(w.shape, const)
    return pl.pallas_call(
        functools.partial(_merge_kernel, pre_ln, alpha),
        grid=(n // tm,),
        in_specs=[
            pl.BlockSpec((tm, D_CONV), row),
            pl.BlockSpec((tm, D_ATTN), row),
            pl.BlockSpec((tm, D_MLSTM), row),
            pl.BlockSpec((tm, N_BRANCHES * 1024), row),
            pl.BlockSpec((tm, 1024), row),
            _mod_spec(2, tm, seq_len, ctx_row),
            _mod_spec(4, tm, seq_len, ctx_row),
            _mod_spec(3, tm, seq_len, ctx_row),
            vec, vec, vec, vec,
            wspec(w_a), wspec(w_b), wspec(w_c), wspec(w_o), wspec(w_rt), wspec(b_r),
        ],
        out_specs=[
            pl.BlockSpec((tm, 1024), row),
            pl.BlockSpec((tm * TOKEN_SUBROWS, LANES), row),
            pl.BlockSpec((8, tm), lambda i: (0, i)),
        ],
        out_shape=[
            jax.ShapeDtypeStruct((n, 1024), F32),
            jax.ShapeDtypeStruct((n * TOKEN_SUBROWS, LANES), F32),
            jax.ShapeDtypeStruct((8, n), F32),
        ],
        scratch_shapes=[pltpu.VMEM((N_CLASS_ROWS, 1), F32)],
        compiler_params=_cparams("arbitrary"),
        name="merge",
    )(ya, yb, yc, bg, h, mod_l, mod_l, mod_l, ln_in[0], ln_in[1], ln1[0], ln1[1], w_a, w_b, w_c, w_o, w_rt, b_r)


class _TokenGather:
    def __init__(self, idx_ref, src_hbm, buf, sem, slot):
        self.idx_ref, self.src_hbm, self.buf, self.sem, self.slot = idx_ref, src_hbm, buf, sem, slot
        self.tokens = idx_ref.shape[-1]

    def _copy(self, k):
        sub = TOKEN_SUBROWS
        p = pl.multiple_of(self.idx_ref[0, k] * sub, sub)
        t = pl.multiple_of(k * sub, sub)
        return pltpu.make_async_copy(self.src_hbm.at[pl.ds(p, sub)], self.buf.at[self.slot, pl.ds(t, sub)],
                                     self.sem.at[self.slot])

    def start(self):
        def body(k, carry):
            self._copy(k).start()
            return carry
        lax.fori_loop(0, self.tokens, body, 0)

    def wait(self):
        def body(k, carry):
            self._copy(k).wait()
            return carry
        lax.fori_loop(0, self.tokens, body, 0)


def _gather_specs(idx, tile):
    n_tiles = idx.shape[0] // tile
    cur = pl.BlockSpec((None, 1, tile), lambda i, *_: (i, 0, 0), memory_space=pltpu.SMEM)
    nxt = pl.BlockSpec((None, 1, tile), lambda i, *_: (jnp.minimum(i + 1, n_tiles - 1), 0, 0),
                       memory_space=pltpu.SMEM)
    return idx.reshape(n_tiles, 1, tile), [cur, nxt]


def _untile_tokens(ref):
    tokens = ref.shape[0] // TOKEN_SUBROWS
    return jnp.concatenate([ref[pl.ds(s, tokens, stride=TOKEN_SUBROWS), :] for s in range(TOKEN_SUBROWS)], axis=1)


def _experts_kernel(lo_ref, hi_ref, valid_ref, src_ref, src_next_ref, x_hbm, wr_ref, wgl_ref, wgh_ref, wdl_ref,
                    wdh_ref, o_ref, buf, sem):
    i = pl.program_id(0)
    n_tiles = pl.num_programs(0)
    tm = src_ref.shape[-1]
    slot = i % 2
    nxt = jnp.minimum(i + 1, n_tiles - 1)

    @pl.when(jnp.logical_and(i == 0, valid_ref[0] != 0))
    def _():
        _TokenGather(src_ref, x_hbm, buf, sem, 0).start()

    @pl.when(jnp.logical_and(i + 1 < n_tiles, valid_ref[nxt] != 0))
    def _():
        _TokenGather(src_next_ref, x_hbm, buf, sem, 1 - slot).start()

    @pl.when(valid_ref[i] != 0)
    def _():
        _TokenGather(src_ref, x_hbm, buf, sem, slot).wait()
        x32 = _untile_tokens(buf.at[slot])
        x = x32.astype(BF16)

        def affinity(e):
            logit = jnp.sum(x32 * wr_ref[pl.ds(e, 1), :], axis=1, keepdims=True)
            return _sigmoid(logit)

        def expert(wg_ref, wd_ref):
            gu = jnp.dot(x, wg_ref[...], preferred_element_type=F32)
            g_ = gu[:, 0:D_EXPERT]
            act = (g_ * _sigmoid(g_) * gu[:, D_EXPERT:2 * D_EXPERT]).astype(BF16)
            return jnp.dot(act, wd_ref[...], preferred_element_type=F32)

        s_lo, s_hi = affinity(lo_ref[i]), affinity(hi_ref[i])
        total = s_lo + s_hi
        out = (s_lo / total) * expert(wgl_ref, wdl_ref) + (s_hi / total) * expert(wgh_ref, wdh_ref)
        for s in range(TOKEN_SUBROWS):
            o_ref[pl.ds(s, tm, stride=TOKEN_SUBROWS), :] = out[:, s * LANES:(s + 1) * LANES]

    @pl.when(valid_ref[i] == 0)
    def _():
        o_ref[...] = jnp.zeros_like(o_ref)


def _experts(lo, hi, valid, src, u2t, w_rt, w_gu, w_dn):
    tm = EXPERT_TILE
    n_tiles = src.shape[0] // tm
    row = lambda i, lo, hi, valid: (i, 0)
    src3, src_specs = _gather_specs(src, tm)
    grid_spec = pltpu.PrefetchScalarGridSpec(
        num_scalar_prefetch=3,
        grid=(n_tiles,),
        in_specs=src_specs + [
            pl.BlockSpec(memory_space=pl.ANY),
            pl.BlockSpec(w_rt.shape, lambda i, lo, hi, valid: (0, 0)),
            pl.BlockSpec((None, 1024, 2 * D_EXPERT), lambda i, lo, hi, valid: (lo[i], 0, 0)),
            pl.BlockSpec((None, 1024, 2 * D_EXPERT), lambda i, lo, hi, valid: (hi[i], 0, 0)),
            pl.BlockSpec((None, D_EXPERT, 1024), lambda i, lo, hi, valid: (lo[i], 0, 0)),
            pl.BlockSpec((None, D_EXPERT, 1024), lambda i, lo, hi, valid: (hi[i], 0, 0)),
        ],
        out_specs=pl.BlockSpec((tm * TOKEN_SUBROWS, LANES), row),
        scratch_shapes=[pltpu.VMEM((2, tm * TOKEN_SUBROWS, LANES), F32), pltpu.SemaphoreType.DMA((2,))],
    )
    return pl.pallas_call(
        _experts_kernel,
        grid_spec=grid_spec,
        out_shape=jax.ShapeDtypeStruct((n_tiles * tm * TOKEN_SUBROWS, LANES), F32),
        compiler_params=_cparams("arbitrary"),
        name="moe_experts",
    )(lo, hi, valid, src3, src3, u2t, w_rt, w_gu, w_gu, w_dn, w_dn)


def _final_kernel(alpha, pos_ref, pos_next_ref, f_hbm, h1_ref, g2_ref, lg_ref, lb_ref, o_ref, buf, sem):
    i = pl.program_id(0)
    n_tiles = pl.num_programs(0)
    slot = i % 2

    @pl.when(i == 0)
    def _():
        _TokenGather(pos_ref, f_hbm, buf, sem, 0).start()

    @pl.when(i + 1 < n_tiles)
    def _():
        _TokenGather(pos_next_ref, f_hbm, buf, sem, 1 - slot).start()

    _TokenGather(pos_ref, f_hbm, buf, sem, slot).wait()
    f = _untile_tokens(buf.at[slot])
    o_ref[...] = _ln(alpha * h1_ref[...] + g2_ref[...] * f, lg_ref[...], lb_ref[...])


def _final_ln(h1, pos, fs, mod_l, ln2, seq_len, ctx_row, alpha):
    n = h1.shape[0]
    tm = FINAL_TILE
    row = lambda i: (i, 0)
    const = lambda i: (0, 0)
    pos3, pos_specs = _gather_specs(pos, tm)
    return pl.pallas_call(
        functools.partial(_final_kernel, alpha),
        grid=(n // tm,),
        in_specs=pos_specs + [
            pl.BlockSpec(memory_space=pl.ANY),
            pl.BlockSpec((tm, 1024), row),
            _mod_spec(5, tm, seq_len, ctx_row),
            pl.BlockSpec((1, 1024), const),
            pl.BlockSpec((1, 1024), const),
        ],
        out_specs=pl.BlockSpec((tm, 1024), row),
        out_shape=jax.ShapeDtypeStruct((n, 1024), F32),
        scratch_shapes=[pltpu.VMEM((2, tm * TOKEN_SUBROWS, LANES), F32), pltpu.SemaphoreType.DMA((2,))],
        compiler_params=_cparams("arbitrary"),
        name="final_ln",
    )(pos3, pos3, fs, h1, mod_l, ln2[0], ln2[1])


def _sort_plan(route_t, n):
    tm = EXPERT_TILE
    n_tiles = n // tm + N_CLASSES
    cls = route_t[AUX_CLS].astype(jnp.int32)
    rank = route_t[AUX_RANK].astype(jnp.int32)
    onehot = cls[:, None] == jnp.arange(N_CLASSES, dtype=jnp.int32)[None, :]
    counts = jnp.sum(onehot, axis=0, dtype=jnp.int32)
    padded = (counts + tm - 1) // tm * tm
    ends = jnp.cumsum(padded)
    offs = ends - padded
    pos = jnp.sum(jnp.where(onehot, offs[None, :], 0), axis=1) + rank
    tile_ends = ends // tm
    j = jnp.arange(n_tiles, dtype=jnp.int32)
    n_used = tile_ends[-1]
    valid = j < n_used
    tile_cls = jnp.sum(j[:, None] >= tile_ends[None, :], axis=1)
    last_cls = jnp.sum((n_used - 1) >= tile_ends)
    tile_cls = jnp.where(valid, tile_cls, last_cls)
    group, pid = tile_cls // len(_PAIRS), tile_cls % len(_PAIRS)
    pair = jnp.asarray(np.array(_PAIRS, dtype=np.int32))
    lo = group * EXPERTS_PER_GROUP + pair[pid, 0]
    hi = group * EXPERTS_PER_GROUP + pair[pid, 1]
    order = jnp.argsort(cls, stable=True).astype(jnp.int32)
    starts = jnp.cumsum(counts) - counts
    row_cls = jnp.repeat(tile_cls, tm)
    within = jnp.arange(n_tiles * tm, dtype=jnp.int32) - offs[row_cls]
    real = jnp.repeat(valid, tm) & (within < counts[row_cls])
    src = jnp.where(real, order[jnp.clip(starts[row_cls] + within, 0, n - 1)], 0)
    return pos.astype(jnp.int32), src.astype(jnp.int32), lo.astype(jnp.int32), hi.astype(jnp.int32), valid.astype(jnp.int32)


def _moe(u2t, route_t, h1, mod_l, ln2, w_rt, w_gu, w_dn, seq_len, ctx_row, alpha):
    n = h1.shape[0]
    pos, src, lo, hi, valid = _sort_plan(route_t, n)
    fs = _experts(lo, hi, valid, src, u2t, w_rt, w_gu, w_dn)
    return _final_ln(h1, pos, fs, mod_l, ln2, seq_len, ctx_row, alpha)


def _rope_swap_index(n_heads):
    idx = np.arange(n_heads * HEAD_DIM)
    within = idx % (HEAD_DIM // 2)
    quarter = HEAD_DIM // 4
    return np.where(within < quarter, idx + quarter, idx - quarter)


def _rope_tables(seq_len, n_heads, scale):
    t = np.arange(seq_len)
    quarter = HEAD_DIM // 4
    inv = ROPE_BASE ** (-np.arange(quarter, dtype=np.float32) / quarter)
    d = np.arange(HEAD_DIM)
    pos = np.where((d // (HEAD_DIM // 2) == 0)[None, :], (t // GRID_W)[:, None], (t % GRID_W)[:, None])
    ang = jnp.asarray(pos.astype(np.float32)) * jnp.asarray(inv[d % quarter])[None, :]
    sign = np.where(d % (HEAD_DIM // 2) < quarter, -1.0, 1.0).astype(np.float32)
    cos = jnp.cos(ang) * scale
    sin = jnp.sin(ang) * (sign * scale)[None, :]
    return jnp.tile(cos, (1, n_heads)), jnp.tile(sin, (1, n_heads))


def _flat_tables(seq_len, n_heads, scale):
    return (jnp.full((seq_len, n_heads * HEAD_DIM), scale, F32), jnp.zeros((seq_len, n_heads * HEAD_DIM), F32))


def _prep_in_weights(w_in_l):
    splits = np.cumsum([2 * D_CONV, D_ATTN, D_KV, D_KV, D_MLSTM, D_MLSTM, D_MLSTM, D_MLSTM, N_GATE_COLS])
    a, q, k, v, mq, mk, mv, mo, mg, bg = jnp.split(w_in_l, splits, axis=1)
    w_main = jnp.concatenate([a, q, q[:, _rope_swap_index(N_Q_HEADS)], k, k[:, _rope_swap_index(N_KV_HEADS)],
                              mq, mv, mo, bg], axis=1).astype(BF16)
    order = np.array([d * 8 + kind * 4 + h for h in range(N_MLSTM_HEADS) for d in range(2) for kind in range(2)])
    w_t = jnp.concatenate([mk.T, mg[:, order].T, v.T], axis=0).astype(BF16)
    return w_main, w_t


def kernel(x, c, ctx, c_ctx, ln_in_g, ln_in_b, w_router, b_router, w_mod, b_mod, w_in, conv_w, conv_b, conv_ln_g,
           conv_ln_b, w_a_out, attn_sink, w_b_out, mlstm_gate_b, mlstm_norm_g, w_c_out, w_out, ln1_g, ln1_b,
           moe_w_gu, moe_w_dn, ln2_g, ln2_b):
    batch, seq_len, d = x.shape
    ctx_len = ctx.shape[1]
    depth = w_in.shape[0]
    alpha = (2.0 * depth) ** 0.25
    ctx_row = batch
    assert d == 1024 and batch < MOD_ROWS
    assert seq_len % MERGE_TILE == 0 and ctx_len % TOKEN_TILE == 0 and (batch * ctx_len) % MERGE_TILE == 0
    assert seq_len % FINAL_TILE == 0 and (batch * ctx_len) % FINAL_TILE == 0 and FINAL_TILE % EXPERT_TILE == 0

    cc = jnp.zeros((MOD_ROWS, d), F32).at[0:batch].set(c).at[batch].set(c_ctx)
    mod = _modulation(cc, w_mod, b_mod).reshape(depth, MOD_ROWS * N_MOD, 1, d)

    attn_scale = HEAD_DIM ** -0.5
    rope_lat = _rope_tables(seq_len, N_Q_HEADS, attn_scale) + _rope_tables(seq_len, N_KV_HEADS, 1.0)
    rope_ctx = _flat_tables(ctx_len, N_Q_HEADS, attn_scale) + _flat_tables(ctx_len, N_KV_HEADS, 1.0)

    vec = lambda t: t.reshape(1, -1)
    ln_in = (vec(ln_in_g), vec(ln_in_b))
    w_rt = w_router.T
    b_r = b_router.reshape(N_EXPERTS, 1)

    h = x.reshape(batch * seq_len, d)
    hc = ctx.reshape(batch * ctx_len, d)
    for l in range(depth):
        need_ctx = l < depth - 1
        pre_ln = l == 0
        mod_l = mod[l]
        w_main, w_t = _prep_in_weights(w_in[l])
        lat = _in_proj(h, ln_in[0], ln_in[1], mod_l, w_main, w_t, rope_lat, seq_len, None, pre_ln)
        cx = _in_proj(hc, ln_in[0], ln_in[1], mod_l, w_main, w_t, rope_ctx, ctx_len, ctx_row, pre_ln)
        a_l, q_l, k_l, v_l, mq_l, mv_l, mo_l, bg_l, kt_l, gt_l = lat
        a_c, q_c, k_c, v_c, mq_c, mv_c, mo_c, bg_c, kt_c, gt_c = cx

        conv_args = (conv_w[l], vec(conv_b[l]), vec(conv_ln_g[l]), vec(conv_ln_b[l]))
        gate_b = jnp.transpose(mlstm_gate_b[l], (2, 0, 1)).reshape(N_MLSTM_HEADS, 4, 1)
        ya = _conv_branch(a_l, *conv_args, seq_len)
        yb = _attn_latent(q_l, k_l, v_l, k_c, v_c, attn_sink[l], seq_len, ctx_len)
        yc, yc_c = _mlstm_branch((mq_c, kt_c, mv_c, mo_c, gt_c), (mq_l, kt_l, mv_l, mo_l, gt_l),
                                 gate_b, vec(mlstm_norm_g[l]), seq_len, ctx_len, need_ctx)

        ln1 = (vec(ln1_g[l]), vec(ln1_b[l]))
        ln2 = (vec(ln2_g[l]), vec(ln2_b[l]))
        w_a, w_b, w_c, w_o = (w.astype(BF16) for w in (w_a_out[l], w_b_out[l], w_c_out[l], w_out[l]))
        w_gu = moe_w_gu[l].astype(BF16)
        w_dn = moe_w_dn[l].astype(BF16)

        h1, pay, route_t = _merge(ya, yb, yc, bg_l, h, mod_l, ln_in, ln1, w_a, w_b, w_c, w_o, w_rt, b_r,
                                  seq_len, None, pre_ln, alpha)
        h = _moe(pay, route_t, h1, mod_l, ln2, w_rt, w_gu, w_dn, seq_len, None, alpha)
        if need_ctx:
            ya_c = _conv_branch(a_c, *conv_args, ctx_len)
            yb_c = _attn_context(q_c, k_c, v_c, attn_sink[l], ctx_len)
            h1c, pay_c, route_tc = _merge(ya_c, yb_c, yc_c, bg_c, hc, mod_l, ln_in, ln1, w_a, w_b, w_c, w_o, w_rt,
                                          b_r, ctx_len, ctx_row, pre_ln, alpha)
            hc = _moe(pay_c, route_tc, h1c, mod_l, ln2, w_rt, w_gu, w_dn, ctx_len, ctx_row, alpha)
    return h.reshape(batch, seq_len, d)
```

```python
import functools

import numpy as np
import jax
import jax.numpy as jnp
from jax import lax
from jax.experimental import pallas as pl
from jax.experimental.pallas import tpu as pltpu

GRID_W = 64
LN_EPS = 1e-5
D_CONV = 512
CONV_WIDTH = 31
N_Q_HEADS = 8
N_KV_HEADS = 2
HEAD_DIM = 64
WINDOW = 128
BLOCK = 128
ROPE_BASE = 10000.0
D_ATTN = N_Q_HEADS * HEAD_DIM
D_KV = N_KV_HEADS * HEAD_DIM
N_MLSTM_HEADS = 4
MLSTM_HEAD_DIM = 128
D_MLSTM = N_MLSTM_HEADS * MLSTM_HEAD_DIM
N_GATE_COLS = 2 * 2 * N_MLSTM_HEADS
N_BRANCHES = 3
N_EXPERTS = 16
N_GROUPS = 4
EXPERTS_PER_GROUP = N_EXPERTS // N_GROUPS
D_EXPERT = 512
N_MOD = 6

LANES = 128
V7X_VMEM_LIMIT_BYTES = 56 * 1024 * 1024

MOD_ROWS = 16
MOD_COL_BLOCK = 512
TOKEN_TILE = 256
EXPERT_TILE = 256
FINAL_TILE = 512
MLSTM_CHUNK = 128
MLSTM_HEADS_PER_STEP = 4
MERGE_TILE = 512
CONV_ROWS = 64
CONV_PAD = 16

_PAIRS = [(i, j) for i in range(EXPERTS_PER_GROUP) for j in range(i + 1, EXPERTS_PER_GROUP)]
N_CLASSES = N_GROUPS * len(_PAIRS)
N_CLASS_ROWS = 32
AUX_CLS, AUX_W_LO, AUX_W_HI, AUX_RANK = 0, 1, 2, 3
TOKEN_SUBROWS = 1024 // LANES

F32 = jnp.float32
BF16 = jnp.bfloat16
HIGHEST = lax.Precision.HIGHEST
NEG_INF = float("-inf")

_C_A = 0
_C_Q = _C_A + 2 * D_CONV
_C_QS = _C_Q + D_ATTN
_C_K = _C_QS + D_ATTN
_C_KS = _C_K + D_KV
_C_MQ = _C_KS + D_KV
_C_MV = _C_MQ + D_MLSTM
_C_MO = _C_MV + D_MLSTM
_C_BG = _C_MO + D_MLSTM
_C_END = _C_BG + N_BRANCHES * 1024
_R_KT = 0
_R_GT = _R_KT + D_MLSTM
_R_VT = _R_GT + N_GATE_COLS


def _cparams(*sem):
    return pltpu.CompilerParams(dimension_semantics=sem, vmem_limit_bytes=V7X_VMEM_LIMIT_BYTES)


def _ln(x, g, b):
    mu = jnp.mean(x, axis=-1, keepdims=True)
    xc = x - mu
    var = jnp.mean(xc * xc, axis=-1, keepdims=True)
    return xc * lax.rsqrt(var + LN_EPS) * g + b


def _sigmoid(x):
    return 0.5 * jnp.tanh(0.5 * x) + 0.5


def _log_sigmoid(x):
    return jnp.minimum(x, 0.0) - jnp.log(1.0 + jnp.exp(-jnp.abs(x)))


def _dot_nt(a, b, precision=None):
    return lax.dot_general(a, b, (((1,), (1,)), ((), ())), preferred_element_type=F32, precision=precision)


def _mod_kernel(c_ref, w_ref, b_ref, o_ref):
    c = c_ref[...]
    s = c * _sigmoid(c)
    o_ref[...] = jnp.dot(s, w_ref[...], preferred_element_type=F32, precision=HIGHEST) + b_ref[...]


def _modulation(cc, w_mod, b_mod):
    depth, d, n = w_mod.shape
    return pl.pallas_call(
        _mod_kernel,
        grid=(depth, n // MOD_COL_BLOCK),
        in_specs=[
            pl.BlockSpec((MOD_ROWS, d), lambda l, j: (0, 0)),
            pl.BlockSpec((None, d, MOD_COL_BLOCK), lambda l, j: (l, 0, j)),
            pl.BlockSpec((None, 1, MOD_COL_BLOCK), lambda l, j: (l, 0, j)),
        ],
        out_specs=pl.BlockSpec((None, MOD_ROWS, MOD_COL_BLOCK), lambda l, j: (l, 0, j)),
        out_shape=jax.ShapeDtypeStruct((depth, MOD_ROWS, n), F32),
        compiler_params=_cparams("parallel", "parallel"),
        name="modulation",
    )(cc, w_mod, b_mod.reshape(depth, 1, n))


def _mod_spec(which, tile, seq_len, ctx_row):
    tiles_per_seq = seq_len // tile
    if ctx_row is None:
        return pl.BlockSpec((None, 1, 1024), lambda i, *_: ((i // tiles_per_seq) * N_MOD + which, 0, 0))
    return pl.BlockSpec((None, 1, 1024), lambda i, *_: (ctx_row * N_MOD + which, 0, 0))


def _in_kernel(pre_ln, x_ref, lg_ref, lb_ref, sc_ref, sh_ref, w_ref, wt_ref, cq_ref, sq_ref, ck_ref, sk_ref,
               a_ref, q_ref, k_ref, v_ref, mq_ref, mv_ref, mo_ref, bg_ref, kt_ref, gt_ref):
    x = x_ref[...]
    if pre_ln:
        x = _ln(x, lg_ref[...], lb_ref[...])
    u = (x * (1.0 + sc_ref[...]) + sh_ref[...]).astype(BF16)

    def seg(lo, hi):
        return jnp.dot(u, w_ref[:, lo:hi], preferred_element_type=F32)

    a_ref[...] = seg(_C_A, _C_Q).astype(BF16)
    q_ref[...] = (seg(_C_Q, _C_QS) * cq_ref[...] + seg(_C_QS, _C_K) * sq_ref[...]).astype(BF16)
    k_ref[...] = (seg(_C_K, _C_KS) * ck_ref[...] + seg(_C_KS, _C_MQ) * sk_ref[...]).astype(BF16)
    v_ref[...] = _dot_nt(wt_ref[_R_VT:_R_VT + D_KV, :], u).astype(BF16)
    mq_ref[...] = seg(_C_MQ, _C_MV).astype(BF16)
    mv_ref[...] = seg(_C_MV, _C_MO).astype(BF16)
    mo_ref[...] = seg(_C_MO, _C_BG).astype(BF16)
    for j in range(N_BRANCHES):
        bg_ref[:, j * 1024:(j + 1) * 1024] = _sigmoid(seg(_C_BG + j * 1024, _C_BG + (j + 1) * 1024)).astype(BF16)
    n_chunks = u.shape[0] // MLSTM_CHUNK
    kt = _dot_nt(wt_ref[_R_KT:_R_KT + D_MLSTM, :], u)
    for h in range(N_MLSTM_HEADS):
        for c in range(n_chunks):
            kt_ref[h, c] = kt[h * MLSTM_HEAD_DIM:(h + 1) * MLSTM_HEAD_DIM,
                              c * MLSTM_CHUNK:(c + 1) * MLSTM_CHUNK].astype(BF16)
    gt = _dot_nt(wt_ref[_R_GT:_R_GT + N_GATE_COLS, :], u)
    for h in range(N_MLSTM_HEADS):
        for c in range(n_chunks):
            gt_ref[h, c] = gt[h * 4:(h + 1) * 4, c * MLSTM_CHUNK:(c + 1) * MLSTM_CHUNK]


def _in_proj(x, ln_g, ln_b, mod_l, w_main, w_t, rope, seq_len, ctx_row, pre_ln):
    n = x.shape[0]
    tm = TOKEN_TILE
    cq, sq, ck, sk = rope
    tps = seq_len // tm
    nch = n // MLSTM_CHUNK
    row = lambda i: (i, 0)
    pos = lambda i: (i % tps, 0)
    const = lambda i: (0, 0)
    out_shape = [
        jax.ShapeDtypeStruct((n, 2 * D_CONV), BF16),
        jax.ShapeDtypeStruct((n, D_ATTN), BF16),
        jax.ShapeDtypeStruct((n, D_KV), BF16),
        jax.ShapeDtypeStruct((D_KV, n), BF16),
        jax.ShapeDtypeStruct((n, D_MLSTM), BF16),
        jax.ShapeDtypeStruct((n, D_MLSTM), BF16),
        jax.ShapeDtypeStruct((n, D_MLSTM), BF16),
        jax.ShapeDtypeStruct((n, N_BRANCHES * 1024), BF16),
        jax.ShapeDtypeStruct((N_MLSTM_HEADS, nch, MLSTM_HEAD_DIM, MLSTM_CHUNK), BF16),
        jax.ShapeDtypeStruct((N_MLSTM_HEADS, nch, 4, MLSTM_CHUNK), F32),
    ]
    cpt = tm // MLSTM_CHUNK
    out_specs = [
        pl.BlockSpec((tm, 2 * D_CONV), row),
        pl.BlockSpec((tm, D_ATTN), row),
        pl.BlockSpec((tm, D_KV), row),
        pl.BlockSpec((D_KV, tm), lambda i: (0, i)),
        pl.BlockSpec((tm, D_MLSTM), row),
        pl.BlockSpec((tm, D_MLSTM), row),
        pl.BlockSpec((tm, D_MLSTM), row),
        pl.BlockSpec((tm, N_BRANCHES * 1024), row),
        pl.BlockSpec((N_MLSTM_HEADS, cpt, MLSTM_HEAD_DIM, MLSTM_CHUNK), lambda i: (0, i, 0, 0)),
        pl.BlockSpec((N_MLSTM_HEADS, cpt, 4, MLSTM_CHUNK), lambda i: (0, i, 0, 0)),
    ]
    in_specs = [
        pl.BlockSpec((tm, 1024), row),
        pl.BlockSpec((1, 1024), const),
        pl.BlockSpec((1, 1024), const),
        _mod_spec(1, tm, seq_len, ctx_row),
        _mod_spec(0, tm, seq_len, ctx_row),
        pl.BlockSpec(w_main.shape, const, pipeline_mode=pl.Buffered(1)),
        pl.BlockSpec(w_t.shape, const, pipeline_mode=pl.Buffered(1)),
        pl.BlockSpec((tm, D_ATTN), pos),
        pl.BlockSpec((tm, D_ATTN), pos),
        pl.BlockSpec((tm, D_KV), pos),
        pl.BlockSpec((tm, D_KV), pos),
    ]
    return pl.pallas_call(
        functools.partial(_in_kernel, pre_ln),
        grid=(n // tm,),
        in_specs=in_specs,
        out_specs=out_specs,
        out_shape=out_shape,
        compiler_params=_cparams("parallel"),
        name="in_proj",
    )(x, ln_g, ln_b, mod_l, mod_l, w_main, w_t, cq, sq, ck, sk)


def _conv_kernel(a_ref, w_ref, cb_ref, g_ref, b_ref, o_ref, upad_ref):
    t = a_ref.shape[0]
    zeros = jnp.zeros((CONV_PAD, D_CONV), F32)
    upad_ref[0:CONV_PAD, :] = zeros
    upad_ref[CONV_PAD + t:2 * CONV_PAD + t, :] = zeros
    val = a_ref[:, 0:D_CONV].astype(F32)
    gate = a_ref[:, D_CONV:2 * D_CONV].astype(F32)
    upad_ref[CONV_PAD:CONV_PAD + t, :] = val * _sigmoid(gate)
    half = CONV_WIDTH // 2

    def body(c, carry):
        r0 = pl.multiple_of(c * CONV_ROWS, CONV_ROWS)
        n_win = CONV_ROWS + 2 * CONV_PAD
        win = upad_ref[pl.ds(r0, n_win), :]
        acc = jnp.zeros((CONV_ROWS, D_CONV), F32) + cb_ref[...]
        for res in range(8):
            rolled = win if res == 0 else pltpu.roll(win, shift=n_win - res, axis=0)
            for k in range(CONV_WIDTH):
                off = CONV_PAD - half + k
                if off % 8 == res:
                    acc = acc + rolled[off - res:off - res + CONV_ROWS, :] * w_ref[k:k + 1, :]
        y = _ln(acc, g_ref[...], b_ref[...])
        o_ref[pl.ds(r0, CONV_ROWS), :] = (y * _sigmoid(y)).astype(BF16)
        return carry

    lax.fori_loop(0, t // CONV_ROWS, body, 0)


def _conv_branch(a_in, conv_w, conv_b, ln_g, ln_b, seq_len):
    n = a_in.shape[0]
    const = lambda b: (0, 0)
    return pl.pallas_call(
        _conv_kernel,
        grid=(n // seq_len,),
        in_specs=[
            pl.BlockSpec((seq_len, 2 * D_CONV), lambda b: (b, 0)),
            pl.BlockSpec((CONV_WIDTH, D_CONV), const),
            pl.BlockSpec((1, D_CONV), const),
            pl.BlockSpec((1, D_CONV), const),
            pl.BlockSpec((1, D_CONV), const),
        ],
        out_specs=pl.BlockSpec((seq_len, D_CONV), lambda b: (b, 0)),
        out_shape=jax.ShapeDtypeStruct((n, D_CONV), BF16),
        scratch_shapes=[pltpu.VMEM((seq_len + 2 * CONV_PAD, D_CONV), F32)],
        compiler_params=_cparams("parallel"),
        name="conv_branch",
    )(a_in, conv_w, conv_b, ln_g, ln_b)


def _attn_heads(q, keys, vals_t, masks, sink_ref, o_ref):
    rows = q.shape[0]
    group = N_Q_HEADS // N_KV_HEADS
    for hk in range(N_KV_HEADS):
        lo = hk * HEAD_DIM
        qs = jnp.concatenate([q[:, (hk * group + g) * HEAD_DIM:(hk * group + g + 1) * HEAD_DIM]
                              for g in range(group)], axis=0)
        sink = jnp.concatenate([jnp.full((1, rows), sink_ref[hk * group + g], F32) for g in range(group)], axis=1)
        scores = []
        m = sink
        for kk, mask in zip(keys, masks):
            s = _dot_nt(kk[:, lo:lo + HEAD_DIM], qs)
            if mask is not None:
                s = jnp.where(mask, s, NEG_INF)
            scores.append(s)
            m = jnp.maximum(m, jnp.max(s, axis=0, keepdims=True))
        acc = jnp.zeros((2 * HEAD_DIM, rows * group), F32)
        for s, vt in zip(scores, vals_t):
            n_k = s.shape[0]
            p = jnp.exp(s - m).astype(BF16)
            ones_rows = jnp.where(lax.broadcasted_iota(jnp.int32, (HEAD_DIM, n_k), 0) == 0, 1.0, 0.0).astype(BF16)
            v_aug = jnp.concatenate([vt[lo:lo + HEAD_DIM, :], ones_rows], axis=0)
            acc = acc + jnp.dot(v_aug, p, preferred_element_type=F32)
        denom = acc[HEAD_DIM:HEAD_DIM + 1, :] + jnp.exp(sink - m)
        o_t = acc * (1.0 / denom)
        for g in range(group):
            col = (hk * group + g) * HEAD_DIM
            o_ref[:, col:col + HEAD_DIM] = o_t[:, g * rows:(g + 1) * rows].T[:, 0:HEAD_DIM].astype(BF16)


def _attn_lat_kernel(sink_ref, q_ref, kp_ref, k0_ref, kn_ref, vp_ref, v0_ref, vn_ref, kc_ref, vc_ref, o_ref):
    n = pl.program_id(1)
    nb = pl.num_programs(1)
    stacked = (N_Q_HEADS // N_KV_HEADS) * BLOCK
    ki = lax.broadcasted_iota(jnp.int32, (BLOCK, stacked), 0)
    qi = lax.broadcasted_iota(jnp.int32, (BLOCK, stacked), 1) % BLOCK
    mask_prev = ki >= qi + jnp.where(n > 0, 0, BLOCK)
    mask_next = ki <= qi - jnp.where(n < nb - 1, 0, BLOCK)
    _attn_heads(q_ref[...],
                [kp_ref[...], k0_ref[...], kn_ref[...], kc_ref[...]],
                [vp_ref[...], v0_ref[...], vn_ref[...], vc_ref[...]],
                [mask_prev, None, mask_next, None], sink_ref, o_ref)


def _attn_ctx_kernel(sink_ref, q_ref, kc_ref, vc_ref, o_ref):
    _attn_heads(q_ref[...], [kc_ref[...]], [vc_ref[...]], [None], sink_ref, o_ref)


def _attn_latent(q, k, vt, kc, vct, sink, seq_len, ctx_len):
    n = q.shape[0]
    nb = seq_len // BLOCK
    batch = n // seq_len
    blk_prev = lambda b, j: b * nb + jnp.maximum(j - 1, 0)
    blk_next = lambda b, j: b * nb + jnp.minimum(j + 1, nb - 1)
    cur = lambda b, j: (b * nb + j, 0)
    kspec = lambda blk: pl.BlockSpec((BLOCK, D_KV), lambda b, j: (blk(b, j), 0))
    vspec = lambda blk: pl.BlockSpec((D_KV, BLOCK), lambda b, j: (0, blk(b, j)))
    blk_cur = lambda b, j: b * nb + j
    return pl.pallas_call(
        _attn_lat_kernel,
        grid=(batch, nb),
        in_specs=[
            pl.BlockSpec(memory_space=pltpu.SMEM),
            pl.BlockSpec((BLOCK, D_ATTN), cur),
            kspec(blk_prev), kspec(blk_cur), kspec(blk_next), vspec(blk_prev), vspec(blk_cur), vspec(blk_next),
            pl.BlockSpec((ctx_len, D_KV), lambda b, j: (b, 0)),
            pl.BlockSpec((D_KV, ctx_len), lambda b, j: (0, b)),
        ],
        out_specs=pl.BlockSpec((BLOCK, D_ATTN), cur),
        out_shape=jax.ShapeDtypeStruct((n, D_ATTN), BF16),
        compiler_params=_cparams("parallel", "parallel"),
        name="attn_latent",
    )(sink, q, k, k, k, vt, vt, vt, kc, vct)


def _attn_context(qc, kc, vct, sink, ctx_len):
    n = qc.shape[0]
    blk = lambda b: (b, 0)
    return pl.pallas_call(
        _attn_ctx_kernel,
        grid=(n // ctx_len,),
        in_specs=[
            pl.BlockSpec(memory_space=pltpu.SMEM),
            pl.BlockSpec((ctx_len, D_ATTN), blk),
            pl.BlockSpec((ctx_len, D_KV), blk),
            pl.BlockSpec((D_KV, ctx_len), lambda b: (0, b)),
        ],
        out_specs=pl.BlockSpec((ctx_len, D_ATTN), blk),
        out_shape=jax.ShapeDtypeStruct((n, D_ATTN), BF16),
        compiler_params=_cparams("parallel"),
        name="attn_context",
    )(sink, qc, kc, vct)


def _mlstm_kernel(ctx_out, nch_c, nch_l, hps,
                  qc_ref, ktc_ref, vc_ref, moc_ref, gc_ref,
                  ql_ref, ktl_ref, vl_ref, mol_ref, gl_ref,
                  gb_ref, ng_ref, *rest):
    if ctx_out:
        yl_ref, yc_ref, af_ref, lff_ref, ab_ref, lfb_ref, cf_ref, cb_ref, hf_ref, hb_ref = rest
    else:
        yl_ref, af_ref, lff_ref, ab_ref, lfb_ref, cf_ref, cb_ref, hf_ref, hb_ref = rest
        yc_ref = None
    lc = MLSTM_CHUNK
    dh = MLSTM_HEAD_DIM
    k_scale = MLSTM_HEAD_DIM ** -0.5
    ti = lax.broadcasted_iota(jnp.int32, (lc, lc), 0)
    si = lax.broadcasted_iota(jnp.int32, (lc, lc), 1)
    lower = si <= ti
    upper = si >= ti
    pre_mat = jnp.where(upper, 1.0, 0.0)
    suf_mat = jnp.where(lower, 1.0, 0.0)
    n_rows = nch_c + nch_l
    pad_rows = -n_rows % 8

    for hh in range(hps):
        gb = gb_ref[hh]

        def gate_rows(kind):
            rows = ([gc_ref[hh, c, kind:kind + 1, :] for c in range(nch_c)]
                    + [gl_ref[hh, c, kind:kind + 1, :] for c in range(nch_l)])
            rows = jnp.concatenate(rows, axis=0) + gb[kind:kind + 1, :]
            if pad_rows:
                rows = jnp.concatenate([rows, jnp.zeros((pad_rows, lc), F32)], axis=0)
            return rows

        lf_f = _log_sigmoid(gate_rows(1))
        lf_b = _log_sigmoid(gate_rows(3))
        a_f = gate_rows(0) - jnp.dot(lf_f, pre_mat, preferred_element_type=F32, precision=HIGHEST)
        a_b = gate_rows(2) - jnp.dot(lf_b, suf_mat, preferred_element_type=F32, precision=HIGHEST)
        for c in range(n_rows):
            af_ref[hh * n_rows + c] = a_f[c:c + 1, :]
            lff_ref[hh * n_rows + c] = lf_f[c:c + 1, :]
            ab_ref[hh * n_rows + c] = a_b[c:c + 1, :]
            lfb_ref[hh * n_rows + c] = lf_b[c:c + 1, :]

    cf_ref[...] = jnp.zeros_like(cf_ref)
    cb_ref[...] = jnp.zeros_like(cb_ref)
    ones_col = jnp.where(lax.broadcasted_iota(jnp.int32, (lc, dh), 1) == 0, 1.0, 0.0).astype(BF16)

    def chunk(q, kt, v, a_row, lf_row, c_ref, hh, m, mask):
        a_mat = jnp.where(mask, jnp.broadcast_to(a_row, (lc, lc)), NEG_INF)
        cm = jnp.max(a_mat, axis=1, keepdims=True)
        b_col = jnp.sum(jnp.where(mask, jnp.broadcast_to(lf_row, (lc, lc)), 0.0), axis=1, keepdims=True)
        mx = jnp.maximum(m, cm)
        mx_last = jnp.maximum(m, jnp.max(a_row, axis=1, keepdims=True))
        w = jnp.exp(a_mat - mx)
        s = jnp.dot(q, kt, preferred_element_type=F32) * k_scale
        p = (s * w).astype(BF16)
        w_s = jnp.exp(a_row - mx_last) * k_scale
        ktw = (kt.astype(F32) * w_s).astype(BF16)
        v_aug = jnp.concatenate([v, ones_col], axis=1)
        both = jnp.dot(jnp.concatenate([p, ktw], axis=0), v_aug, preferred_element_type=F32)
        c_old = c_ref[hh]
        inter = jnp.dot(q, c_old.astype(BF16), preferred_element_type=F32)
        tot = both[0:lc, :] + jnp.exp(m - mx) * inter
        den = tot[:, dh:dh + 1]
        h = tot[:, 0:dh] / jnp.maximum(jnp.abs(den), jnp.exp(-(b_col + mx)))
        c_ref[hh] = jnp.exp(m - mx_last) * c_old + both[lc:lc + dh, :]
        m_new = jnp.sum(lf_row, axis=1, keepdims=True) + mx_last
        return h, m_new

    def step(q_ref, kt_ref, v_ref, row0, c_f, c_b, rf, rb, ms):
        out = []
        for hh in range(hps):
            cols = slice(hh * dh, (hh + 1) * dh)
            base = hh * n_rows + row0
            h_f, m_f = chunk(q_ref[pl.ds(rf, lc), cols], kt_ref[hh, c_f], v_ref[pl.ds(rf, lc), cols],
                             af_ref[base + c_f], lff_ref[base + c_f], cf_ref, hh, ms[2 * hh], lower)
            h_b, m_b = chunk(q_ref[pl.ds(rb, lc), cols], kt_ref[hh, c_b], v_ref[pl.ds(rb, lc), cols],
                             ab_ref[base + c_b], lfb_ref[base + c_b], cb_ref, hh, ms[2 * hh + 1], upper)
            hf_ref[pl.ds(rf, lc), cols] = h_f
            hb_ref[pl.ds(rb, lc), cols] = h_b
            out += [m_f, m_b]
        return tuple(out)

    def finish(n_tok, mo_ref, o_ref):
        for hh in range(hps):
            cols = slice(hh * dh, (hh + 1) * dh)
            h = hf_ref[0:n_tok, cols] + hb_ref[0:n_tok, cols]
            mu = jnp.mean(h, axis=-1, keepdims=True)
            hc = h - mu
            var = jnp.mean(hc * hc, axis=-1, keepdims=True)
            y = hc * lax.rsqrt(var + LN_EPS) * ng_ref[:, cols]
            o_ref[:, cols] = (_sigmoid(mo_ref[:, cols].astype(F32)) * y).astype(BF16)

    ms = tuple(jnp.zeros((1, 1), F32) for _ in range(2 * hps))
    for c in range(nch_c):
        c_b = nch_c - 1 - c
        ms = step(qc_ref, ktc_ref, vc_ref, 0, c, c_b, c * lc, c_b * lc, ms)
    if ctx_out:
        finish(nch_c * lc, moc_ref, yc_ref)

    def body(c, ms):
        c_b = nch_l - 1 - c
        return step(ql_ref, ktl_ref, vl_ref, nch_c, c, c_b,
                    pl.multiple_of(c * lc, lc), pl.multiple_of(c_b * lc, lc), ms)

    lax.fori_loop(0, nch_l, body, ms)
    finish(nch_l * lc, mol_ref, yl_ref)


def _mlstm_branch(ctx_p, lat_p, gate_b, norm_g, seq_len, ctx_len, ctx_out):
    mq_c, kt_c, mv_c, mo_c, g_c = ctx_p
    mq_l, kt_l, mv_l, mo_l, g_l = lat_p
    n_l, n_c = mq_l.shape[0], mq_c.shape[0]
    batch = n_l // seq_len
    lc, dh, hps = MLSTM_CHUNK, MLSTM_HEAD_DIM, MLSTM_HEADS_PER_STEP
    nch_c, nch_l = ctx_len // lc, seq_len // lc

    def stream(t, nch):
        tok = pl.BlockSpec((t, hps * dh), lambda b, h: (b, h))
        return [tok,
                pl.BlockSpec((hps, nch, dh, lc), lambda b, h: (h, b, 0, 0)),
                tok, tok,
                pl.BlockSpec((hps, nch, 4, lc), lambda b, h: (h, b, 0, 0))]

    in_specs = stream(ctx_len, nch_c) + stream(seq_len, nch_l) + [
        pl.BlockSpec((hps, 4, 1), lambda b, h: (h, 0, 0)),
        pl.BlockSpec((1, hps * dh), lambda b, h: (0, h)),
    ]
    out_specs = [pl.BlockSpec((seq_len, hps * dh), lambda b, h: (b, h))]
    out_shape = [jax.ShapeDtypeStruct((n_l, D_MLSTM), BF16)]
    if ctx_out:
        out_specs.append(pl.BlockSpec((ctx_len, hps * dh), lambda b, h: (b, h)))
        out_shape.append(jax.ShapeDtypeStruct((n_c, D_MLSTM), BF16))
    row_scratch = pltpu.VMEM((hps * (nch_c + nch_l), 1, lc), F32)
    outs = pl.pallas_call(
        functools.partial(_mlstm_kernel, ctx_out, nch_c, nch_l, hps),
        grid=(batch, N_MLSTM_HEADS // hps),
        in_specs=in_specs,
        out_specs=out_specs,
        out_shape=out_shape,
        scratch_shapes=[row_scratch, row_scratch, row_scratch, row_scratch,
                        pltpu.VMEM((hps, dh, 2 * dh), F32), pltpu.VMEM((hps, dh, 2 * dh), F32),
                        pltpu.VMEM((seq_len, hps * dh), F32), pltpu.VMEM((seq_len, hps * dh), F32)],
        compiler_params=_cparams("parallel", "parallel"),
        name="mlstm_branch",
    )(mq_c, kt_c, mv_c, mo_c, g_c, mq_l, kt_l, mv_l, mo_l, g_l, gate_b, norm_g)
    return outs if ctx_out else (outs[0], None)


def _route(logits_t, br):
    sc = [_sigmoid(logits_t[e:e + 1, :]) for e in range(N_EXPERTS)]
    sel = [sc[e] + br[e:e + 1, :] for e in range(N_EXPERTS)]
    epg = EXPERTS_PER_GROUP
    group_score = []
    for g in range(N_GROUPS):
        v = sel[g * epg:(g + 1) * epg]
        best = None
        for i in range(epg):
            for j in range(i + 1, epg):
                pair = v[i] + v[j]
                best = pair if best is None else jnp.maximum(best, pair)
        group_score.append(best)
    g_idx = jnp.zeros_like(group_score[0], dtype=jnp.int32)
    best = group_score[0]
    for g in range(1, N_GROUPS):
        better = group_score[g] > best
        g_idx = jnp.where(better, g, g_idx)
        best = jnp.maximum(best, group_score[g])
    chosen = []
    for g in range(N_GROUPS):
        v = sel[g * epg:(g + 1) * epg]
        in_g = g_idx == g
        for i in range(epg):
            rank = jnp.zeros_like(g_idx)
            for j in range(epg):
                if j == i:
                    continue
                ahead = (v[j] >= v[i]) if j < i else (v[j] > v[i])
                rank = rank + jnp.where(ahead, 1, 0)
            chosen.append(in_g & (rank < 2))
    cls = jnp.zeros_like(sc[0])
    w_lo = jnp.zeros_like(sc[0])
    w_hi = jnp.zeros_like(sc[0])
    for g in range(N_GROUPS):
        for pid, (i, j) in enumerate(_PAIRS):
            lo, hi = g * epg + i, g * epg + j
            is_pair = chosen[lo] & chosen[hi]
            cls = jnp.where(is_pair, float(g * len(_PAIRS) + pid), cls)
            w_lo = jnp.where(is_pair, sc[lo], w_lo)
            w_hi = jnp.where(is_pair, sc[hi], w_hi)
    total = w_lo + w_hi
    return cls, w_lo / total, w_hi / total


def _merge_kernel(pre_ln, alpha, ya_ref, yb_ref, yc_ref, bg_ref, h_ref, g1_ref, sc2_ref, sh2_ref,
                  lig_ref, lib_ref, l1g_ref, l1b_ref, wa_ref, wb_ref, wc_ref, wo_ref, wr_ref, br_ref,
                  h1_ref, u2t_ref, route_ref, cnt_ref):
    tm = h_ref.shape[0]

    def branch(y_ref, w_ref, j):
        gate = bg_ref[:, j * 1024:(j + 1) * 1024].astype(F32)
        return gate * jnp.dot(y_ref[...], w_ref[...], preferred_element_type=F32)

    mix = branch(ya_ref, wa_ref, 0) + branch(yb_ref, wb_ref, 1) + branch(yc_ref, wc_ref, 2)
    y = jnp.dot(mix.astype(BF16), wo_ref[...], preferred_element_type=F32)
    h = h_ref[...]
    if pre_ln:
        h = _ln(h, lig_ref[...], lib_ref[...])
    h1 = _ln(alpha * h + g1_ref[...] * y, l1g_ref[...], l1b_ref[...])
    h1_ref[...] = h1
    u2 = h1 * (1.0 + sc2_ref[...]) + sh2_ref[...]
    logits_t = _dot_nt(wr_ref[...], u2, precision=HIGHEST)
    cls, w_lo, w_hi = _route(logits_t, br_ref[...])

    @pl.when(pl.program_id(0) == 0)
    def _():
        cnt_ref[...] = jnp.zeros_like(cnt_ref)

    crow = lax.broadcasted_iota(jnp.int32, (N_CLASS_ROWS, tm), 0).astype(F32)
    onehot = jnp.where(crow == cls, 1.0, 0.0)
    earlier = lax.broadcasted_iota(jnp.int32, (tm, tm), 0) <= lax.broadcasted_iota(jnp.int32, (tm, tm), 1)
    incl = jnp.dot(onehot.astype(BF16), jnp.where(earlier, 1.0, 0.0).astype(BF16), preferred_element_type=F32)
    base = cnt_ref[...]
    rank = jnp.sum(onehot * (incl - 1.0 + base), axis=0, keepdims=True)
    cnt_ref[...] = base + incl[:, tm - 1:tm]

    route_ref[...] = jnp.concatenate([cls, w_lo, w_hi, rank, jnp.zeros((4, tm), F32)], axis=0)
    for s in range(TOKEN_SUBROWS):
        u2t_ref[pl.ds(s, tm, stride=TOKEN_SUBROWS), :] = u2[:, s * LANES:(s + 1) * LANES]


def _merge(ya, yb, yc, bg, h, mod_l, ln_in, ln1, w_a, w_b, w_c, w_o, w_rt, b_r, seq_len, ctx_row, pre_ln, alpha):
    n = ya.shape[0]
    tm = MERGE_TILE
    row = lambda i: (i, 0)
    const = lambda i: (0, 0)
    vec = pl.BlockSpec((1, 1024), const)
    wspec = lambda w: pl.BlockSpec(w.shape, const)
    return pl.pallas_call(
        functools.partial(_merge_kernel, pre_ln, alpha),
        grid=(n // tm,),
        in_specs=[
            pl.BlockSpec((tm, D_CONV), row),
            pl.BlockSpec((tm, D_ATTN), row),
            pl.BlockSpec((tm, D_MLSTM), row),
            pl.BlockSpec((tm, N_BRANCHES * 1024), row),
            pl.BlockSpec((tm, 1024), row),
            _mod_spec(2, tm, seq_len, ctx_row),
            _mod_spec(4, tm, seq_len, ctx_row),
            _mod_spec(3, tm, seq_len, ctx_row),
            vec, vec, vec, vec,
            wspec(w_a), wspec(w_b), wspec(w_c), wspec(w_o), wspec(w_rt), wspec(b_r),
        ],
        out_specs=[
            pl.BlockSpec((tm, 1024), row),
            pl.BlockSpec((tm * TOKEN_SUBROWS, LANES), row),
            pl.BlockSpec((8, tm), lambda i: (0, i)),
        ],
        out_shape=[
            jax.ShapeDtypeStruct((n, 1024), F32),
            jax.ShapeDtypeStruct((n * TOKEN_SUBROWS, LANES), F32),
            jax.ShapeDtypeStruct((8, n), F32),
        ],
        scratch_shapes=[pltpu.VMEM((N_CLASS_ROWS, 1), F32)],
        compiler_params=_cparams("arbitrary"),
        name="merge",
    )(ya, yb, yc, bg, h, mod_l, mod_l, mod_l, ln_in[0], ln_in[1], ln1[0], ln1[1], w_a, w_b, w_c, w_o, w_rt, b_r)


class _TokenGather:
    def __init__(self, idx_ref, src_hbm, buf, sem, slot):
        self.idx_ref, self.src_hbm, self.buf, self.sem, self.slot = idx_ref, src_hbm, buf, sem, slot
        self.tokens = idx_ref.shape[-1]

    def _copy(self, k):
        sub = TOKEN_SUBROWS
        p = pl.multiple_of(self.idx_ref[0, k] * sub, sub)
        return pltpu.make_async_copy(self.src_hbm.at[pl.ds(p, sub)], self.buf.at[self.slot, pl.ds(k * sub, sub)],
                                     self.sem.at[self.slot])

    def start(self):
        for k in range(self.tokens):
            self._copy(k).start(priority=k % 2)

    def wait(self):
        for k in range(self.tokens):
            self._copy(k).wait()


def _gather_specs(idx, tile):
    n_tiles = idx.shape[0] // tile
    cur = pl.BlockSpec((None, 1, tile), lambda i, *_: (i, 0, 0), memory_space=pltpu.SMEM)
    nxt = pl.BlockSpec((None, 1, tile), lambda i, *_: (jnp.minimum(i + 1, n_tiles - 1), 0, 0),
                       memory_space=pltpu.SMEM)
    return idx.reshape(n_tiles, 1, tile), [cur, nxt]


def _untile_tokens(ref):
    tokens = ref.shape[0] // TOKEN_SUBROWS
    return jnp.concatenate([ref[pl.ds(s, tokens, stride=TOKEN_SUBROWS), :] for s in range(TOKEN_SUBROWS)], axis=1)


def _experts_kernel(lo_ref, hi_ref, valid_ref, src_ref, src_next_ref, x_hbm, wr_ref, wgl_ref, wgh_ref, wdl_ref,
                    wdh_ref, o_ref, buf, sem):
    i = pl.program_id(0)
    last = pl.num_programs(0) - 1
    tm = src_ref.shape[-1]
    slot = i % 2
    used = valid_ref[i] != 0
    prev_used = valid_ref[jnp.maximum(i - 1, 0)] != 0

    @pl.when(jnp.logical_and(i == 0, used))
    def _():
        _TokenGather(src_ref, x_hbm, buf, sem, 0).start()

    @pl.when(jnp.logical_and(jnp.logical_and(i > 0, prev_used), jnp.logical_not(used)))
    def _():
        _TokenGather(src_ref, x_hbm, buf, sem, slot).wait()

    @pl.when(used)
    def _():
        _TokenGather(src_ref, x_hbm, buf, sem, slot).wait()
        _TokenGather(src_next_ref, x_hbm, buf, sem, 1 - slot).start()
        x32 = _untile_tokens(buf.at[slot])
        x = x32.astype(BF16)

        def affinity(e):
            logit = jnp.sum(x32 * wr_ref[pl.ds(e, 1), :], axis=1, keepdims=True)
            return _sigmoid(logit)

        def expert(wg_ref, wd_ref):
            gu = jnp.dot(x, wg_ref[...], preferred_element_type=F32)
            g_ = gu[:, 0:D_EXPERT]
            act = (g_ * _sigmoid(g_) * gu[:, D_EXPERT:2 * D_EXPERT]).astype(BF16)
            return jnp.dot(act, wd_ref[...], preferred_element_type=F32)

        s_lo, s_hi = affinity(lo_ref[i]), affinity(hi_ref[i])
        total = s_lo + s_hi
        out = (s_lo / total) * expert(wgl_ref, wdl_ref) + (s_hi / total) * expert(wgh_ref, wdh_ref)
        for s in range(TOKEN_SUBROWS):
            o_ref[pl.ds(s, tm, stride=TOKEN_SUBROWS), :] = out[:, s * LANES:(s + 1) * LANES]

    @pl.when(jnp.logical_and(used, i == last))
    def _():
        _TokenGather(src_next_ref, x_hbm, buf, sem, 1 - slot).wait()

    @pl.when(jnp.logical_not(used))
    def _():
        o_ref[...] = jnp.zeros_like(o_ref)


def _experts(lo, hi, valid, src, u2t, w_rt, w_gu, w_dn):
    tm = EXPERT_TILE
    n_tiles = src.shape[0] // tm
    row = lambda i, lo, hi, valid: (i, 0)
    src3, src_specs = _gather_specs(src, tm)
    grid_spec = pltpu.PrefetchScalarGridSpec(
        num_scalar_prefetch=3,
        grid=(n_tiles,),
        in_specs=src_specs + [
            pl.BlockSpec(memory_space=pl.ANY),
            pl.BlockSpec(w_rt.shape, lambda i, lo, hi, valid: (0, 0)),
            pl.BlockSpec((None, 1024, 2 * D_EXPERT), lambda i, lo, hi, valid: (lo[i], 0, 0)),
            pl.BlockSpec((None, 1024, 2 * D_EXPERT), lambda i, lo, hi, valid: (hi[i], 0, 0)),
            pl.BlockSpec((None, D_EXPERT, 1024), lambda i, lo, hi, valid: (lo[i], 0, 0)),
            pl.BlockSpec((None, D_EXPERT, 1024), lambda i, lo, hi, valid: (hi[i], 0, 0)),
        ],
        out_specs=pl.BlockSpec((tm * TOKEN_SUBROWS, LANES), row),
        scratch_shapes=[pltpu.VMEM((2, tm * TOKEN_SUBROWS, LANES), F32), pltpu.SemaphoreType.DMA((2,))],
    )
    return pl.pallas_call(
        _experts_kernel,
        grid_spec=grid_spec,
        out_shape=jax.ShapeDtypeStruct((n_tiles * tm * TOKEN_SUBROWS, LANES), F32),
        compiler_params=_cparams("arbitrary"),
        name="moe_experts",
    )(lo, hi, valid, src3, src3, u2t, w_rt, w_gu, w_gu, w_dn, w_dn)


def _final_kernel(alpha, pos_ref, pos_next_ref, f_hbm, h1_ref, g2_ref, lg_ref, lb_ref, o_ref, buf, sem):
    i = pl.program_id(0)
    last = pl.num_programs(0) - 1
    slot = i % 2

    @pl.when(i == 0)
    def _():
        _TokenGather(pos_ref, f_hbm, buf, sem, 0).start()

    _TokenGather(pos_ref, f_hbm, buf, sem, slot).wait()
    _TokenGather(pos_next_ref, f_hbm, buf, sem, 1 - slot).start()
    f = _untile_tokens(buf.at[slot])
    o_ref[...] = _ln(alpha * h1_ref[...] + g2_ref[...] * f, lg_ref[...], lb_ref[...])

    @pl.when(i == last)
    def _():
        _TokenGather(pos_next_ref, f_hbm, buf, sem, 1 - slot).wait()


def _final_ln(h1, pos, fs, mod_l, ln2, seq_len, ctx_row, alpha):
    n = h1.shape[0]
    tm = FINAL_TILE
    row = lambda i: (i, 0)
    const = lambda i: (0, 0)
    pos3, pos_specs = _gather_specs(pos, tm)
    return pl.pallas_call(
        functools.partial(_final_kernel, alpha),
        grid=(n // tm,),
        in_specs=pos_specs + [
            pl.BlockSpec(memory_space=pl.ANY),
            pl.BlockSpec((tm, 1024), row),
            _mod_spec(5, tm, seq_len, ctx_row),
            pl.BlockSpec((1, 1024), const),
            pl.BlockSpec((1, 1024), const),
        ],
        out_specs=pl.BlockSpec((tm, 1024), row),
        out_shape=jax.ShapeDtypeStruct((n, 1024), F32),
        scratch_shapes=[pltpu.VMEM((2, tm * TOKEN_SUBROWS, LANES), F32), pltpu.SemaphoreType.DMA((2,))],
        compiler_params=_cparams("arbitrary"),
        name="final_ln",
    )(pos3, pos3, fs, h1, mod_l, ln2[0], ln2[1])


def _sort_plan(route_t, n):
    tm = EXPERT_TILE
    n_tiles = n // tm + N_CLASSES
    cls = route_t[AUX_CLS].astype(jnp.int32)
    rank = route_t[AUX_RANK].astype(jnp.int32)
    onehot = cls[:, None] == jnp.arange(N_CLASSES, dtype=jnp.int32)[None, :]
    counts = jnp.sum(onehot, axis=0, dtype=jnp.int32)
    padded = (counts + tm - 1) // tm * tm
    ends = jnp.cumsum(padded)
    offs = ends - padded
    pos = jnp.sum(jnp.where(onehot, offs[None, :], 0), axis=1) + rank
    tile_ends = ends // tm
    j = jnp.arange(n_tiles, dtype=jnp.int32)
    n_used = tile_ends[-1]
    valid = j < n_used
    tile_cls = jnp.sum(j[:, None] >= tile_ends[None, :], axis=1)
    last_cls = jnp.sum((n_used - 1) >= tile_ends)
    tile_cls = jnp.where(valid, tile_cls, last_cls)
    group, pid = tile_cls // len(_PAIRS), tile_cls % len(_PAIRS)
    pair = jnp.asarray(np.array(_PAIRS, dtype=np.int32))
    lo = group * EXPERTS_PER_GROUP + pair[pid, 0]
    hi = group * EXPERTS_PER_GROUP + pair[pid, 1]
    order = jnp.argsort(cls, stable=True).astype(jnp.int32)
    starts = jnp.cumsum(counts) - counts
    row_cls = jnp.repeat(tile_cls, tm)
    within = jnp.arange(n_tiles * tm, dtype=jnp.int32) - offs[row_cls]
    real = jnp.repeat(valid, tm) & (within < counts[row_cls])
    src = jnp.where(real, order[jnp.clip(starts[row_cls] + within, 0, n - 1)], 0)
    return pos.astype(jnp.int32), src.astype(jnp.int32), lo.astype(jnp.int32), hi.astype(jnp.int32), valid.astype(jnp.int32)


def _moe(u2t, route_t, h1, mod_l, ln2, w_rt, w_gu, w_dn, seq_len, ctx_row, alpha):
    n = h1.shape[0]
    pos, src, lo, hi, valid = _sort_plan(route_t, n)
    fs = _experts(lo, hi, valid, src, u2t, w_rt, w_gu, w_dn)
    return _final_ln(h1, pos, fs, mod_l, ln2, seq_len, ctx_row, alpha)


def _rope_swap_index(n_heads):
    idx = np.arange(n_heads * HEAD_DIM)
    within = idx % (HEAD_DIM // 2)
    quarter = HEAD_DIM // 4
    return np.where(within < quarter, idx + quarter, idx - quarter)


def _rope_tables(seq_len, n_heads, scale):
    t = np.arange(seq_len)
    quarter = HEAD_DIM // 4
    inv = ROPE_BASE ** (-np.arange(quarter, dtype=np.float32) / quarter)
    d = np.arange(HEAD_DIM)
    pos = np.where((d // (HEAD_DIM // 2) == 0)[None, :], (t // GRID_W)[:, None], (t % GRID_W)[:, None])
    ang = jnp.asarray(pos.astype(np.float32)) * jnp.asarray(inv[d % quarter])[None, :]
    sign = np.where(d % (HEAD_DIM // 2) < quarter, -1.0, 1.0).astype(np.float32)
    cos = jnp.cos(ang) * scale
    sin = jnp.sin(ang) * (sign * scale)[None, :]
    return jnp.tile(cos, (1, n_heads)), jnp.tile(sin, (1, n_heads))


def _flat_tables(seq_len, n_heads, scale):
    return (jnp.full((seq_len, n_heads * HEAD_DIM), scale, F32), jnp.zeros((seq_len, n_heads * HEAD_DIM), F32))


def _prep_in_weights(w_in_l):
    splits = np.cumsum([2 * D_CONV, D_ATTN, D_KV, D_KV, D_MLSTM, D_MLSTM, D_MLSTM, D_MLSTM, N_GATE_COLS])
    a, q, k, v, mq, mk, mv, mo, mg, bg = jnp.split(w_in_l, splits, axis=1)
    w_main = jnp.concatenate([a, q, q[:, _rope_swap_index(N_Q_HEADS)], k, k[:, _rope_swap_index(N_KV_HEADS)],
                              mq, mv, mo, bg], axis=1).astype(BF16)
    order = np.array([d * 8 + kind * 4 + h for h in range(N_MLSTM_HEADS) for d in range(2) for kind in range(2)])
    w_t = jnp.concatenate([mk.T, mg[:, order].T, v.T], axis=0).astype(BF16)
    return w_main, w_t


def kernel(x, c, ctx, c_ctx, ln_in_g, ln_in_b, w_router, b_router, w_mod, b_mod, w_in, conv_w, conv_b, conv_ln_g,
           conv_ln_b, w_a_out, attn_sink, w_b_out, mlstm_gate_b, mlstm_norm_g, w_c_out, w_out, ln1_g, ln1_b,
           moe_w_gu, moe_w_dn, ln2_g, ln2_b):
    batch, seq_len, d = x.shape
    ctx_len = ctx.shape[1]
    depth = w_in.shape[0]
    alpha = (2.0 * depth) ** 0.25
    ctx_row = batch
    assert d == 1024 and batch < MOD_ROWS
    assert seq_len % MERGE_TILE == 0 and ctx_len % TOKEN_TILE == 0 and (batch * ctx_len) % MERGE_TILE == 0
    assert seq_len % FINAL_TILE == 0 and (batch * ctx_len) % FINAL_TILE == 0 and FINAL_TILE % EXPERT_TILE == 0

    cc = jnp.zeros((MOD_ROWS, d), F32).at[0:batch].set(c).at[batch].set(c_ctx)
    mod = _modulation(cc, w_mod, b_mod).reshape(depth, MOD_ROWS * N_MOD, 1, d)

    attn_scale = HEAD_DIM ** -0.5
    rope_lat = _rope_tables(seq_len, N_Q_HEADS, attn_scale) + _rope_tables(seq_len, N_KV_HEADS, 1.0)
    rope_ctx = _flat_tables(ctx_len, N_Q_HEADS, attn_scale) + _flat_tables(ctx_len, N_KV_HEADS, 1.0)

    vec = lambda t: t.reshape(1, -1)
    ln_in = (vec(ln_in_g), vec(ln_in_b))
    w_rt = w_router.T
    b_r = b_router.reshape(N_EXPERTS, 1)

    h = x.reshape(batch * seq_len, d)
    hc = ctx.reshape(batch * ctx_len, d)
    for l in range(depth):
        need_ctx = l < depth - 1
        pre_ln = l == 0
        mod_l = mod[l]
        w_main, w_t = _prep_in_weights(w_in[l])
        lat = _in_proj(h, ln_in[0], ln_in[1], mod_l, w_main, w_t, rope_lat, seq_len, None, pre_ln)
        cx = _in_proj(hc, ln_in[0], ln_in[1], mod_l, w_main, w_t, rope_ctx, ctx_len, ctx_row, pre_ln)
        a_l, q_l, k_l, v_l, mq_l, mv_l, mo_l, bg_l, kt_l, gt_l = lat
        a_c, q_c, k_c, v_c, mq_c, mv_c, mo_c, bg_c, kt_c, gt_c = cx

        conv_args = (conv_w[l], vec(conv_b[l]), vec(conv_ln_g[l]), vec(conv_ln_b[l]))
        gate_b = jnp.transpose(mlstm_gate_b[l], (2, 0, 1)).reshape(N_MLSTM_HEADS, 4, 1)
        ya = _conv_branch(a_l, *conv_args, seq_len)
        yb = _attn_latent(q_l, k_l, v_l, k_c, v_c, attn_sink[l], seq_len, ctx_len)
        yc, yc_c = _mlstm_branch((mq_c, kt_c, mv_c, mo_c, gt_c), (mq_l, kt_l, mv_l, mo_l, gt_l),
                                 gate_b, vec(mlstm_norm_g[l]), seq_len, ctx_len, need_ctx)

        ln1 = (vec(ln1_g[l]), vec(ln1_b[l]))
        ln2 = (vec(ln2_g[l]), vec(ln2_b[l]))
        w_a, w_b, w_c, w_o = (w.astype(BF16) for w in (w_a_out[l], w_b_out[l], w_c_out[l], w_out[l]))
        w_gu = moe_w_gu[l].astype(BF16)
        w_dn = moe_w_dn[l].astype(BF16)

        h1, pay, route_t = _merge(ya, yb, yc, bg_l, h, mod_l, ln_in, ln1, w_a, w_b, w_c, w_o, w_rt, b_r,
                                  seq_len, None, pre_ln, alpha)
        h = _moe(pay, route_t, h1, mod_l, ln2, w_rt, w_gu, w_dn, seq_len, None, alpha)
        if need_ctx:
            ya_c = _conv_branch(a_c, *conv_args, ctx_len)
            yb_c = _attn_context(q_c, k_c, v_c, attn_sink[l], ctx_len)
            h1c, pay_c, route_tc = _merge(ya_c, yb_c, yc_c, bg_c, hc, mod_l, ln_in, ln1, w_a, w_b, w_c, w_o, w_rt,
                                          b_r, ctx_len, ctx_row, pre_ln, alpha)
            hc = _moe(pay_c, route_tc, h1c, mod_l, ln2, w_rt, w_gu, w_dn, ctx_len, ctx_row, alpha)
    return h.reshape(batch, seq_len, d)
```

```python
import functools

import numpy as np
import jax
import jax.numpy as jnp
from jax import lax
from jax.experimental import pallas as pl
from jax.experimental.pallas import tpu as pltpu

GRID_W = 64
LN_EPS = 1e-5
D_CONV = 512
CONV_WIDTH = 31
N_Q_HEADS = 8
N_KV_HEADS = 2
HEAD_DIM = 64
WINDOW = 128
BLOCK = 128
ROPE_BASE = 10000.0
D_ATTN = N_Q_HEADS * HEAD_DIM
D_KV = N_KV_HEADS * HEAD_DIM
N_MLSTM_HEADS = 4
MLSTM_HEAD_DIM = 128
D_MLSTM = N_MLSTM_HEADS * MLSTM_HEAD_DIM
N_GATE_COLS = 2 * 2 * N_MLSTM_HEADS
N_BRANCHES = 3
N_EXPERTS = 16
N_GROUPS = 4
EXPERTS_PER_GROUP = N_EXPERTS // N_GROUPS
D_EXPERT = 512
N_MOD = 6

LANES = 128
V7X_VMEM_LIMIT_BYTES = 56 * 1024 * 1024

MOD_ROWS = 16
MOD_COL_BLOCK = 512
TOKEN_TILE = 256
EXPERT_TILE = 256
EXPERT_GATHER_PRIORITIES = (1,)
FINAL_TILE = 512
MLSTM_CHUNK = 128
MLSTM_HEADS_PER_STEP = 4
MERGE_TILE = 512
CONV_ROWS = 64
CONV_PAD = 16

_PAIRS = [(i, j) for i in range(EXPERTS_PER_GROUP) for j in range(i + 1, EXPERTS_PER_GROUP)]
N_CLASSES = N_GROUPS * len(_PAIRS)
N_CLASS_ROWS = 32
AUX_CLS, AUX_W_LO, AUX_W_HI, AUX_RANK = 0, 1, 2, 3
TOKEN_SUBROWS = 1024 // LANES

F32 = jnp.float32
BF16 = jnp.bfloat16
HIGHEST = lax.Precision.HIGHEST
NEG_INF = float("-inf")

_C_A = 0
_C_Q = _C_A + 2 * D_CONV
_C_QS = _C_Q + D_ATTN
_C_K = _C_QS + D_ATTN
_C_KS = _C_K + D_KV
_C_MQ = _C_KS + D_KV
_C_MV = _C_MQ + D_MLSTM
_C_MO = _C_MV + D_MLSTM
_C_BG = _C_MO + D_MLSTM
_C_END = _C_BG + N_BRANCHES * 1024
_R_KT = 0
_R_GT = _R_KT + D_MLSTM
_R_VT = _R_GT + N_GATE_COLS


def _cparams(*sem):
    return pltpu.CompilerParams(dimension_semantics=sem, vmem_limit_bytes=V7X_VMEM_LIMIT_BYTES)


def _ln(x, g, b):
    mu = jnp.mean(x, axis=-1, keepdims=True)
    xc = x - mu
    var = jnp.mean(xc * xc, axis=-1, keepdims=True)
    return xc * lax.rsqrt(var + LN_EPS) * g + b


def _sigmoid(x):
    return 0.5 * jnp.tanh(0.5 * x) + 0.5


def _log_sigmoid(x):
    return jnp.minimum(x, 0.0) - jnp.log(1.0 + jnp.exp(-jnp.abs(x)))


def _dot_nt(a, b, precision=None):
    return lax.dot_general(a, b, (((1,), (1,)), ((), ())), preferred_element_type=F32, precision=precision)


def _mod_kernel(c_ref, w_ref, b_ref, o_ref):
    c = c_ref[...]
    s = c * _sigmoid(c)
    o_ref[...] = jnp.dot(s, w_ref[...], preferred_element_type=F32, precision=HIGHEST) + b_ref[...]


def _modulation(cc, w_mod, b_mod):
    depth, d, n = w_mod.shape
    return pl.pallas_call(
        _mod_kernel,
        grid=(depth, n // MOD_COL_BLOCK),
        in_specs=[
            pl.BlockSpec((MOD_ROWS, d), lambda l, j: (0, 0)),
            pl.BlockSpec((None, d, MOD_COL_BLOCK), lambda l, j: (l, 0, j)),
            pl.BlockSpec((None, 1, MOD_COL_BLOCK), lambda l, j: (l, 0, j)),
        ],
        out_specs=pl.BlockSpec((None, MOD_ROWS, MOD_COL_BLOCK), lambda l, j: (l, 0, j)),
        out_shape=jax.ShapeDtypeStruct((depth, MOD_ROWS, n), F32),
        compiler_params=_cparams("parallel", "parallel"),
        name="modulation",
    )(cc, w_mod, b_mod.reshape(depth, 1, n))


def _mod_spec(which, tile, seq_len, ctx_row):
    tiles_per_seq = seq_len // tile
    if ctx_row is None:
        return pl.BlockSpec((None, 1, 1024), lambda i, *_: ((i // tiles_per_seq) * N_MOD + which, 0, 0))
    return pl.BlockSpec((None, 1, 1024), lambda i, *_: (ctx_row * N_MOD + which, 0, 0))


def _in_kernel(pre_ln, x_ref, lg_ref, lb_ref, sc_ref, sh_ref, w_ref, wt_ref, cq_ref, sq_ref, ck_ref, sk_ref,
               a_ref, q_ref, k_ref, v_ref, mq_ref, mv_ref, mo_ref, bg_ref, kt_ref, gt_ref):
    x = x_ref[...]
    if pre_ln:
        x = _ln(x, lg_ref[...], lb_ref[...])
    u = (x * (1.0 + sc_ref[...]) + sh_ref[...]).astype(BF16)

    def seg(lo, hi):
        return jnp.dot(u, w_ref[:, lo:hi], preferred_element_type=F32)

    a_ref[...] = seg(_C_A, _C_Q).astype(BF16)
    q_ref[...] = (seg(_C_Q, _C_QS) * cq_ref[...] + seg(_C_QS, _C_K) * sq_ref[...]).astype(BF16)
    k_ref[...] = (seg(_C_K, _C_KS) * ck_ref[...] + seg(_C_KS, _C_MQ) * sk_ref[...]).astype(BF16)
    v_ref[...] = _dot_nt(wt_ref[_R_VT:_R_VT + D_KV, :], u).astype(BF16)
    mq_ref[...] = seg(_C_MQ, _C_MV).astype(BF16)
    mv_ref[...] = seg(_C_MV, _C_MO).astype(BF16)
    mo_ref[...] = seg(_C_MO, _C_BG).astype(BF16)
    for j in range(N_BRANCHES):
        bg_ref[:, j * 1024:(j + 1) * 1024] = _sigmoid(seg(_C_BG + j * 1024, _C_BG + (j + 1) * 1024)).astype(BF16)
    n_chunks = u.shape[0] // MLSTM_CHUNK
    kt = _dot_nt(wt_ref[_R_KT:_R_KT + D_MLSTM, :], u)
    for h in range(N_MLSTM_HEADS):
        for c in range(n_chunks):
            kt_ref[h, c] = kt[h * MLSTM_HEAD_DIM:(h + 1) * MLSTM_HEAD_DIM,
                              c * MLSTM_CHUNK:(c + 1) * MLSTM_CHUNK].astype(BF16)
    gt = _dot_nt(wt_ref[_R_GT:_R_GT + N_GATE_COLS, :], u)
    for h in range(N_MLSTM_HEADS):
        for c in range(n_chunks):
            gt_ref[h, c] = gt[h * 4:(h + 1) * 4, c * MLSTM_CHUNK:(c + 1) * MLSTM_CHUNK]


def _in_proj(x, ln_g, ln_b, mod_l, w_main, w_t, rope, seq_len, ctx_row, pre_ln):
    n = x.shape[0]
    tm = TOKEN_TILE
    cq, sq, ck, sk = rope
    tps = seq_len // tm
    nch = n // MLSTM_CHUNK
    row = lambda i: (i, 0)
    pos = lambda i: (i % tps, 0)
    const = lambda i: (0, 0)
    out_shape = [
        jax.ShapeDtypeStruct((n, 2 * D_CONV), BF16),
        jax.ShapeDtypeStruct((n, D_ATTN), BF16),
        jax.ShapeDtypeStruct((n, D_KV), BF16),
        jax.ShapeDtypeStruct((D_KV, n), BF16),
        jax.ShapeDtypeStruct((n, D_MLSTM), BF16),
        jax.ShapeDtypeStruct((n, D_MLSTM), BF16),
        jax.ShapeDtypeStruct((n, D_MLSTM), BF16),
        jax.ShapeDtypeStruct((n, N_BRANCHES * 1024), BF16),
        jax.ShapeDtypeStruct((N_MLSTM_HEADS, nch, MLSTM_HEAD_DIM, MLSTM_CHUNK), BF16),
        jax.ShapeDtypeStruct((N_MLSTM_HEADS, nch, 4, MLSTM_CHUNK), F32),
    ]
    cpt = tm // MLSTM_CHUNK
    out_specs = [
        pl.BlockSpec((tm, 2 * D_CONV), row),
        pl.BlockSpec((tm, D_ATTN), row),
        pl.BlockSpec((tm, D_KV), row),
        pl.BlockSpec((D_KV, tm), lambda i: (0, i)),
        pl.BlockSpec((tm, D_MLSTM), row),
        pl.BlockSpec((tm, D_MLSTM), row),
        pl.BlockSpec((tm, D_MLSTM), row),
        pl.BlockSpec((tm, N_BRANCHES * 1024), row),
        pl.BlockSpec((N_MLSTM_HEADS, cpt, MLSTM_HEAD_DIM, MLSTM_CHUNK), lambda i: (0, i, 0, 0)),
        pl.BlockSpec((N_MLSTM_HEADS, cpt, 4, MLSTM_CHUNK), lambda i: (0, i, 0, 0)),
    ]
    in_specs = [
        pl.BlockSpec((tm, 1024), row),
        pl.BlockSpec((1, 1024), const),
        pl.BlockSpec((1, 1024), const),
        _mod_spec(1, tm, seq_len, ctx_row),
        _mod_spec(0, tm, seq_len, ctx_row),
        pl.BlockSpec(w_main.shape, const, pipeline_mode=pl.Buffered(1)),
        pl.BlockSpec(w_t.shape, const, pipeline_mode=pl.Buffered(1)),
        pl.BlockSpec((tm, D_ATTN), pos),
        pl.BlockSpec((tm, D_ATTN), pos),
        pl.BlockSpec((tm, D_KV), pos),
        pl.BlockSpec((tm, D_KV), pos),
    ]
    return pl.pallas_call(
        functools.partial(_in_kernel, pre_ln),
        grid=(n // tm,),
        in_specs=in_specs,
        out_specs=out_specs,
        out_shape=out_shape,
        compiler_params=_cparams("parallel"),
        name="in_proj",
    )(x, ln_g, ln_b, mod_l, mod_l, w_main, w_t, cq, sq, ck, sk)


def _conv_kernel(a_ref, w_ref, cb_ref, g_ref, b_ref, o_ref, upad_ref):
    t = a_ref.shape[0]
    zeros = jnp.zeros((CONV_PAD, D_CONV), F32)
    upad_ref[0:CONV_PAD, :] = zeros
    upad_ref[CONV_PAD + t:2 * CONV_PAD + t, :] = zeros
    val = a_ref[:, 0:D_CONV].astype(F32)
    gate = a_ref[:, D_CONV:2 * D_CONV].astype(F32)
    upad_ref[CONV_PAD:CONV_PAD + t, :] = val * _sigmoid(gate)
    half = CONV_WIDTH // 2

    def body(c, carry):
        r0 = pl.multiple_of(c * CONV_ROWS, CONV_ROWS)
        n_win = CONV_ROWS + 2 * CONV_PAD
        win = upad_ref[pl.ds(r0, n_win), :]
        acc = jnp.zeros((CONV_ROWS, D_CONV), F32) + cb_ref[...]
        for res in range(8):
            rolled = win if res == 0 else pltpu.roll(win, shift=n_win - res, axis=0)
            for k in range(CONV_WIDTH):
                off = CONV_PAD - half + k
                if off % 8 == res:
                    acc = acc + rolled[off - res:off - res + CONV_ROWS, :] * w_ref[k:k + 1, :]
        y = _ln(acc, g_ref[...], b_ref[...])
        o_ref[pl.ds(r0, CONV_ROWS), :] = (y * _sigmoid(y)).astype(BF16)
        return carry

    lax.fori_loop(0, t // CONV_ROWS, body, 0)


def _conv_branch(a_in, conv_w, conv_b, ln_g, ln_b, seq_len):
    n = a_in.shape[0]
    const = lambda b: (0, 0)
    return pl.pallas_call(
        _conv_kernel,
        grid=(n // seq_len,),
        in_specs=[
            pl.BlockSpec((seq_len, 2 * D_CONV), lambda b: (b, 0)),
            pl.BlockSpec((CONV_WIDTH, D_CONV), const),
            pl.BlockSpec((1, D_CONV), const),
            pl.BlockSpec((1, D_CONV), const),
            pl.BlockSpec((1, D_CONV), const),
        ],
        out_specs=pl.BlockSpec((seq_len, D_CONV), lambda b: (b, 0)),
        out_shape=jax.ShapeDtypeStruct((n, D_CONV), BF16),
        scratch_shapes=[pltpu.VMEM((seq_len + 2 * CONV_PAD, D_CONV), F32)],
        compiler_params=_cparams("parallel"),
        name="conv_branch",
    )(a_in, conv_w, conv_b, ln_g, ln_b)


def _attn_heads(q, keys, vals_t, masks, sink_ref, o_ref):
    rows = q.shape[0]
    group = N_Q_HEADS // N_KV_HEADS
    for hk in range(N_KV_HEADS):
        lo = hk * HEAD_DIM
        qs = jnp.concatenate([q[:, (hk * group + g) * HEAD_DIM:(hk * group + g + 1) * HEAD_DIM]
                              for g in range(group)], axis=0)
        sink = jnp.concatenate([jnp.full((1, rows), sink_ref[hk * group + g], F32) for g in range(group)], axis=1)
        scores = []
        m = sink
        for kk, mask in zip(keys, masks):
            s = _dot_nt(kk[:, lo:lo + HEAD_DIM], qs)
            if mask is not None:
                s = jnp.where(mask, s, NEG_INF)
            scores.append(s)
            m = jnp.maximum(m, jnp.max(s, axis=0, keepdims=True))
        acc = jnp.zeros((2 * HEAD_DIM, rows * group), F32)
        for s, vt in zip(scores, vals_t):
            n_k = s.shape[0]
            p = jnp.exp(s - m).astype(BF16)
            ones_rows = jnp.where(lax.broadcasted_iota(jnp.int32, (HEAD_DIM, n_k), 0) == 0, 1.0, 0.0).astype(BF16)
            v_aug = jnp.concatenate([vt[lo:lo + HEAD_DIM, :], ones_rows], axis=0)
            acc = acc + jnp.dot(v_aug, p, preferred_element_type=F32)
        denom = acc[HEAD_DIM:HEAD_DIM + 1, :] + jnp.exp(sink - m)
        o_t = acc * (1.0 / denom)
        for g in range(group):
            col = (hk * group + g) * HEAD_DIM
            o_ref[:, col:col + HEAD_DIM] = o_t[:, g * rows:(g + 1) * rows].T[:, 0:HEAD_DIM].astype(BF16)


def _attn_lat_kernel(sink_ref, q_ref, kp_ref, k0_ref, kn_ref, vp_ref, v0_ref, vn_ref, kc_ref, vc_ref, o_ref):
    n = pl.program_id(1)
    nb = pl.num_programs(1)
    stacked = (N_Q_HEADS // N_KV_HEADS) * BLOCK
    ki = lax.broadcasted_iota(jnp.int32, (BLOCK, stacked), 0)
    qi = lax.broadcasted_iota(jnp.int32, (BLOCK, stacked), 1) % BLOCK
    mask_prev = ki >= qi + jnp.where(n > 0, 0, BLOCK)
    mask_next = ki <= qi - jnp.where(n < nb - 1, 0, BLOCK)
    _attn_heads(q_ref[...],
                [kp_ref[...], k0_ref[...], kn_ref[...], kc_ref[...]],
                [vp_ref[...], v0_ref[...], vn_ref[...], vc_ref[...]],
                [mask_prev, None, mask_next, None], sink_ref, o_ref)


def _attn_ctx_kernel(sink_ref, q_ref, kc_ref, vc_ref, o_ref):
    _attn_heads(q_ref[...], [kc_ref[...]], [vc_ref[...]], [None], sink_ref, o_ref)


def _attn_latent(q, k, vt, kc, vct, sink, seq_len, ctx_len):
    n = q.shape[0]
    nb = seq_len // BLOCK
    batch = n // seq_len
    blk_prev = lambda b, j: b * nb + jnp.maximum(j - 1, 0)
    blk_next = lambda b, j: b * nb + jnp.minimum(j + 1, nb - 1)
    cur = lambda b, j: (b * nb + j, 0)
    kspec = lambda blk: pl.BlockSpec((BLOCK, D_KV), lambda b, j: (blk(b, j), 0))
    vspec = lambda blk: pl.BlockSpec((D_KV, BLOCK), lambda b, j: (0, blk(b, j)))
    blk_cur = lambda b, j: b * nb + j
    return pl.pallas_call(
        _attn_lat_kernel,
        grid=(batch, nb),
        in_specs=[
            pl.BlockSpec(memory_space=pltpu.SMEM),
            pl.BlockSpec((BLOCK, D_ATTN), cur),
            kspec(blk_prev), kspec(blk_cur), kspec(blk_next), vspec(blk_prev), vspec(blk_cur), vspec(blk_next),
            pl.BlockSpec((ctx_len, D_KV), lambda b, j: (b, 0)),
            pl.BlockSpec((D_KV, ctx_len), lambda b, j: (0, b)),
        ],
        out_specs=pl.BlockSpec((BLOCK, D_ATTN), cur),
        out_shape=jax.ShapeDtypeStruct((n, D_ATTN), BF16),
        compiler_params=_cparams("parallel", "parallel"),
        name="attn_latent",
    )(sink, q, k, k, k, vt, vt, vt, kc, vct)


def _attn_context(qc, kc, vct, sink, ctx_len):
    n = qc.shape[0]
    blk = lambda b: (b, 0)
    return pl.pallas_call(
        _attn_ctx_kernel,
        grid=(n // ctx_len,),
        in_specs=[
            pl.BlockSpec(memory_space=pltpu.SMEM),
            pl.BlockSpec((ctx_len, D_ATTN), blk),
            pl.BlockSpec((ctx_len, D_KV), blk),
            pl.BlockSpec((D_KV, ctx_len), lambda b: (0, b)),
        ],
        out_specs=pl.BlockSpec((ctx_len, D_ATTN), blk),
        out_shape=jax.ShapeDtypeStruct((n, D_ATTN), BF16),
        compiler_params=_cparams("parallel"),
        name="attn_context",
    )(sink, qc, kc, vct)


def _mlstm_kernel(ctx_out, nch_c, nch_l, hps,
                  qc_ref, ktc_ref, vc_ref, moc_ref, gc_ref,
                  ql_ref, ktl_ref, vl_ref, mol_ref, gl_ref,
                  gb_ref, ng_ref, *rest):
    if ctx_out:
        yl_ref, yc_ref, af_ref, lff_ref, ab_ref, lfb_ref, cf_ref, cb_ref, hf_ref, hb_ref = rest
    else:
        yl_ref, af_ref, lff_ref, ab_ref, lfb_ref, cf_ref, cb_ref, hf_ref, hb_ref = rest
        yc_ref = None
    lc = MLSTM_CHUNK
    dh = MLSTM_HEAD_DIM
    k_scale = MLSTM_HEAD_DIM ** -0.5
    ti = lax.broadcasted_iota(jnp.int32, (lc, lc), 0)
    si = lax.broadcasted_iota(jnp.int32, (lc, lc), 1)
    lower = si <= ti
    upper = si >= ti
    pre_mat = jnp.where(upper, 1.0, 0.0)
    suf_mat = jnp.where(lower, 1.0, 0.0)
    n_rows = nch_c + nch_l
    pad_rows = -n_rows % 8

    for hh in range(hps):
        gb = gb_ref[hh]

        def gate_rows(kind):
            rows = ([gc_ref[hh, c, kind:kind + 1, :] for c in range(nch_c)]
                    + [gl_ref[hh, c, kind:kind + 1, :] for c in range(nch_l)])
            rows = jnp.concatenate(rows, axis=0) + gb[kind:kind + 1, :]
            if pad_rows:
                rows = jnp.concatenate([rows, jnp.zeros((pad_rows, lc), F32)], axis=0)
            return rows

        lf_f = _log_sigmoid(gate_rows(1))
        lf_b = _log_sigmoid(gate_rows(3))
        a_f = gate_rows(0) - jnp.dot(lf_f, pre_mat, preferred_element_type=F32, precision=HIGHEST)
        a_b = gate_rows(2) - jnp.dot(lf_b, suf_mat, preferred_element_type=F32, precision=HIGHEST)
        for c in range(n_rows):
            af_ref[hh * n_rows + c] = a_f[c:c + 1, :]
            lff_ref[hh * n_rows + c] = lf_f[c:c + 1, :]
            ab_ref[hh * n_rows + c] = a_b[c:c + 1, :]
            lfb_ref[hh * n_rows + c] = lf_b[c:c + 1, :]

    cf_ref[...] = jnp.zeros_like(cf_ref)
    cb_ref[...] = jnp.zeros_like(cb_ref)
    ones_col = jnp.where(lax.broadcasted_iota(jnp.int32, (lc, dh), 1) == 0, 1.0, 0.0).astype(BF16)

    def chunk(q, kt, v, a_row, lf_row, c_ref, hh, m, mask):
        a_mat = jnp.where(mask, jnp.broadcast_to(a_row, (lc, lc)), NEG_INF)
        cm = jnp.max(a_mat, axis=1, keepdims=True)
        b_col = jnp.sum(jnp.where(mask, jnp.broadcast_to(lf_row, (lc, lc)), 0.0), axis=1, keepdims=True)
        mx = jnp.maximum(m, cm)
        mx_last = jnp.maximum(m, jnp.max(a_row, axis=1, keepdims=True))
        w = jnp.exp(a_mat - mx)
        s = jnp.dot(q, kt, preferred_element_type=F32) * k_scale
        p = (s * w).astype(BF16)
        w_s = jnp.exp(a_row - mx_last) * k_scale
        ktw = (kt.astype(F32) * w_s).astype(BF16)
        v_aug = jnp.concatenate([v, ones_col], axis=1)
        both = jnp.dot(jnp.concatenate([p, ktw], axis=0), v_aug, preferred_element_type=F32)
        c_old = c_ref[hh]
        inter = jnp.dot(q, c_old.astype(BF16), preferred_element_type=F32)
        tot = both[0:lc, :] + jnp.exp(m - mx) * inter
        den = tot[:, dh:dh + 1]
        h = tot[:, 0:dh] / jnp.maximum(jnp.abs(den), jnp.exp(-(b_col + mx)))
        c_ref[hh] = jnp.exp(m - mx_last) * c_old + both[lc:lc + dh, :]
        m_new = jnp.sum(lf_row, axis=1, keepdims=True) + mx_last
        return h, m_new

    def step(q_ref, kt_ref, v_ref, row0, c_f, c_b, rf, rb, ms):
        out = []
        for hh in range(hps):
            cols = slice(hh * dh, (hh + 1) * dh)
            base = hh * n_rows + row0
            h_f, m_f = chunk(q_ref[pl.ds(rf, lc), cols], kt_ref[hh, c_f], v_ref[pl.ds(rf, lc), cols],
                             af_ref[base + c_f], lff_ref[base + c_f], cf_ref, hh, ms[2 * hh], lower)
            h_b, m_b = chunk(q_ref[pl.ds(rb, lc), cols], kt_ref[hh, c_b], v_ref[pl.ds(rb, lc), cols],
                             ab_ref[base + c_b], lfb_ref[base + c_b], cb_ref, hh, ms[2 * hh + 1], upper)
            hf_ref[pl.ds(rf, lc), cols] = h_f
            hb_ref[pl.ds(rb, lc), cols] = h_b
            out += [m_f, m_b]
        return tuple(out)

    def finish(n_tok, mo_ref, o_ref):
        for hh in range(hps):
            cols = slice(hh * dh, (hh + 1) * dh)
            h = hf_ref[0:n_tok, cols] + hb_ref[0:n_tok, cols]
            mu = jnp.mean(h, axis=-1, keepdims=True)
            hc = h - mu
            var = jnp.mean(hc * hc, axis=-1, keepdims=True)
            y = hc * lax.rsqrt(var + LN_EPS) * ng_ref[:, cols]
            o_ref[:, cols] = (_sigmoid(mo_ref[:, cols].astype(F32)) * y).astype(BF16)

    ms = tuple(jnp.zeros((1, 1), F32) for _ in range(2 * hps))
    for c in range(nch_c):
        c_b = nch_c - 1 - c
        ms = step(qc_ref, ktc_ref, vc_ref, 0, c, c_b, c * lc, c_b * lc, ms)
    if ctx_out:
        finish(nch_c * lc, moc_ref, yc_ref)

    def body(c, ms):
        c_b = nch_l - 1 - c
        return step(ql_ref, ktl_ref, vl_ref, nch_c, c, c_b,
                    pl.multiple_of(c * lc, lc), pl.multiple_of(c_b * lc, lc), ms)

    lax.fori_loop(0, nch_l, body, ms)
    finish(nch_l * lc, mol_ref, yl_ref)


def _mlstm_branch(ctx_p, lat_p, gate_b, norm_g, seq_len, ctx_len, ctx_out):
    mq_c, kt_c, mv_c, mo_c, g_c = ctx_p
    mq_l, kt_l, mv_l, mo_l, g_l = lat_p
    n_l, n_c = mq_l.shape[0], mq_c.shape[0]
    batch = n_l // seq_len
    lc, dh, hps = MLSTM_CHUNK, MLSTM_HEAD_DIM, MLSTM_HEADS_PER_STEP
    nch_c, nch_l = ctx_len // lc, seq_len // lc

    def stream(t, nch):
        tok = pl.BlockSpec((t, hps * dh), lambda b, h: (b, h))
        return [tok,
                pl.BlockSpec((hps, nch, dh, lc), lambda b, h: (h, b, 0, 0)),
                tok, tok,
                pl.BlockSpec((hps, nch, 4, lc), lambda b, h: (h, b, 0, 0))]

    in_specs = stream(ctx_len, nch_c) + stream(seq_len, nch_l) + [
        pl.BlockSpec((hps, 4, 1), lambda b, h: (h, 0, 0)),
        pl.BlockSpec((1, hps * dh), lambda b, h: (0, h)),
    ]
    out_specs = [pl.BlockSpec((seq_len, hps * dh), lambda b, h: (b, h))]
    out_shape = [jax.ShapeDtypeStruct((n_l, D_MLSTM), BF16)]
    if ctx_out:
        out_specs.append(pl.BlockSpec((ctx_len, hps * dh), lambda b, h: (b, h)))
        out_shape.append(jax.ShapeDtypeStruct((n_c, D_MLSTM), BF16))
    row_scratch = pltpu.VMEM((hps * (nch_c + nch_l), 1, lc), F32)
    outs = pl.pallas_call(
        functools.partial(_mlstm_kernel, ctx_out, nch_c, nch_l, hps),
        grid=(batch, N_MLSTM_HEADS // hps),
        in_specs=in_specs,
        out_specs=out_specs,
        out_shape=out_shape,
        scratch_shapes=[row_scratch, row_scratch, row_scratch, row_scratch,
                        pltpu.VMEM((hps, dh, 2 * dh), F32), pltpu.VMEM((hps, dh, 2 * dh), F32),
                        pltpu.VMEM((seq_len, hps * dh), F32), pltpu.VMEM((seq_len, hps * dh), F32)],
        compiler_params=_cparams("parallel", "parallel"),
        name="mlstm_branch",
    )(mq_c, kt_c, mv_c, mo_c, g_c, mq_l, kt_l, mv_l, mo_l, g_l, gate_b, norm_g)
    return outs if ctx_out else (outs[0], None)


def _route(logits_t, br):
    sc = [_sigmoid(logits_t[e:e + 1, :]) for e in range(N_EXPERTS)]
    sel = [sc[e] + br[e:e + 1, :] for e in range(N_EXPERTS)]
    epg = EXPERTS_PER_GROUP
    group_score = []
    for g in range(N_GROUPS):
        v = sel[g * epg:(g + 1) * epg]
        best = None
        for i in range(epg):
            for j in range(i + 1, epg):
                pair = v[i] + v[j]
                best = pair if best is None else jnp.maximum(best, pair)
        group_score.append(best)
    g_idx = jnp.zeros_like(group_score[0], dtype=jnp.int32)
    best = group_score[0]
    for g in range(1, N_GROUPS):
        better = group_score[g] > best
        g_idx = jnp.where(better, g, g_idx)
        best = jnp.maximum(best, group_score[g])
    chosen = []
    for g in range(N_GROUPS):
        v = sel[g * epg:(g + 1) * epg]
        in_g = g_idx == g
        for i in range(epg):
            rank = jnp.zeros_like(g_idx)
            for j in range(epg):
                if j == i:
                    continue
                ahead = (v[j] >= v[i]) if j < i else (v[j] > v[i])
                rank = rank + jnp.where(ahead, 1, 0)
            chosen.append(in_g & (rank < 2))
    cls = jnp.zeros_like(sc[0])
    w_lo = jnp.zeros_like(sc[0])
    w_hi = jnp.zeros_like(sc[0])
    for g in range(N_GROUPS):
        for pid, (i, j) in enumerate(_PAIRS):
            lo, hi = g * epg + i, g * epg + j
            is_pair = chosen[lo] & chosen[hi]
            cls = jnp.where(is_pair, float(g * len(_PAIRS) + pid), cls)
            w_lo = jnp.where(is_pair, sc[lo], w_lo)
            w_hi = jnp.where(is_pair, sc[hi], w_hi)
    total = w_lo + w_hi
    return cls, w_lo / total, w_hi / total


def _merge_kernel(pre_ln, alpha, ya_ref, yb_ref, yc_ref, bg_ref, h_ref, g1_ref, sc2_ref, sh2_ref,
                  lig_ref, lib_ref, l1g_ref, l1b_ref, wa_ref, wb_ref, wc_ref, wo_ref, wr_ref, br_ref,
                  h1_ref, u2t_ref, route_ref, cnt_ref):
    tm = h_ref.shape[0]

    def branch(y_ref, w_ref, j):
        gate = bg_ref[:, j * 1024:(j + 1) * 1024].astype(F32)
        return gate * jnp.dot(y_ref[...], w_ref[...], preferred_element_type=F32)

    mix = branch(ya_ref, wa_ref, 0) + branch(yb_ref, wb_ref, 1) + branch(yc_ref, wc_ref, 2)
    y = jnp.dot(mix.astype(BF16), wo_ref[...], preferred_element_type=F32)
    h = h_ref[...]
    if pre_ln:
        h = _ln(h, lig_ref[...], lib_ref[...])
    h1 = _ln(alpha * h + g1_ref[...] * y, l1g_ref[...], l1b_ref[...])
    h1_ref[...] = h1
    u2 = h1 * (1.0 + sc2_ref[...]) + sh2_ref[...]
    logits_t = _dot_nt(wr_ref[...], u2, precision=HIGHEST)
    cls, w_lo, w_hi = _route(logits_t, br_ref[...])

    @pl.when(pl.program_id(0) == 0)
    def _():
        cnt_ref[...] = jnp.zeros_like(cnt_ref)

    crow = lax.broadcasted_iota(jnp.int32, (N_CLASS_ROWS, tm), 0).astype(F32)
    onehot = jnp.where(crow == cls, 1.0, 0.0)
    earlier = lax.broadcasted_iota(jnp.int32, (tm, tm), 0) <= lax.broadcasted_iota(jnp.int32, (tm, tm), 1)
    incl = jnp.dot(onehot.astype(BF16), jnp.where(earlier, 1.0, 0.0).astype(BF16), preferred_element_type=F32)
    base = cnt_ref[...]
    rank = jnp.sum(onehot * (incl - 1.0 + base), axis=0, keepdims=True)
    cnt_ref[...] = base + incl[:, tm - 1:tm]

    route_ref[...] = jnp.concatenate([cls, w_lo, w_hi, rank, jnp.zeros((4, tm), F32)], axis=0)
    for s in range(TOKEN_SUBROWS):
        u2t_ref[pl.ds(s, tm, stride=TOKEN_SUBROWS), :] = u2[:, s * LANES:(s + 1) * LANES]


def _merge(ya, yb, yc, bg, h, mod_l, ln_in, ln1, w_a, w_b, w_c, w_o, w_rt, b_r, seq_len, ctx_row, pre_ln, alpha):
    n = ya.shape[0]
    tm = MERGE_TILE
    row = lambda i: (i, 0)
    const = lambda i: (0, 0)
    vec = pl.BlockSpec((1, 1024), const)
    wspec = lambda w: pl.BlockSpec(w.shape, const)
    return pl.pallas_call(
        functools.partial(_merge_kernel, pre_ln, alpha),
        grid=(n // tm,),
        in_specs=[
            pl.BlockSpec((tm, D_CONV), row),
            pl.BlockSpec((tm, D_ATTN), row),
            pl.BlockSpec((tm, D_MLSTM), row),
            pl.BlockSpec((tm, N_BRANCHES * 1024), row),
            pl.BlockSpec((tm, 1024), row),
            _mod_spec(2, tm, seq_len, ctx_row),
            _mod_spec(4, tm, seq_len, ctx_row),
            _mod_spec(3, tm, seq_len, ctx_row),
            vec, vec, vec, vec,
            wspec(w_a), wspec(w_b), wspec(w_c), wspec(w_o), wspec(w_rt), wspec(b_r),
        ],
        out_specs=[
            pl.BlockSpec((tm, 1024), row),
            pl.BlockSpec((tm * TOKEN_SUBROWS, LANES), row),
            pl.BlockSpec((8, tm), lambda i: (0, i)),
        ],
        out_shape=[
            jax.ShapeDtypeStruct((n, 1024), F32),
            jax.ShapeDtypeStruct((n * TOKEN_SUBROWS, LANES), F32),
            jax.ShapeDtypeStruct((8, n), F32),
        ],
        scratch_shapes=[pltpu.VMEM((N_CLASS_ROWS, 1), F32)],
        compiler_params=_cparams("arbitrary"),
        name="merge",
    )(ya, yb, yc, bg, h, mod_l, mod_l, mod_l, ln_in[0], ln_in[1], ln1[0], ln1[1], w_a, w_b, w_c, w_o, w_rt, b_r)


class _TokenGather:
    def __init__(self, idx_ref, src_hbm, buf, sem, slot, priorities=(0, 1)):
        self.idx_ref, self.src_hbm, self.buf, self.sem, self.slot = idx_ref, src_hbm, buf, sem, slot
        self.tokens = idx_ref.shape[-1]
        self.priorities = priorities

    def _copy(self, k):
        sub = TOKEN_SUBROWS
        p = pl.multiple_of(self.idx_ref[0, k] * sub, sub)
        return pltpu.make_async_copy(self.src_hbm.at[pl.ds(p, sub)], self.buf.at[self.slot, pl.ds(k * sub, sub)],
                                     self.sem.at[self.slot])

    def start(self):
        for k in range(self.tokens):
            self._copy(k).start(priority=self.priorities[k % len(self.priorities)])

    def wait(self):
        for k in range(self.tokens):
            self._copy(k).wait()


def _gather_specs(idx, tile):
    n_tiles = idx.shape[0] // tile
    cur = pl.BlockSpec((None, 1, tile), lambda i, *_: (i, 0, 0), memory_space=pltpu.SMEM)
    nxt = pl.BlockSpec((None, 1, tile), lambda i, *_: (jnp.minimum(i + 1, n_tiles - 1), 0, 0),
                       memory_space=pltpu.SMEM)
    return idx.reshape(n_tiles, 1, tile), [cur, nxt]


def _untile_tokens(ref):
    tokens = ref.shape[0] // TOKEN_SUBROWS
    return jnp.concatenate([ref[pl.ds(s, tokens, stride=TOKEN_SUBROWS), :] for s in range(TOKEN_SUBROWS)], axis=1)


def _experts_kernel(lo_ref, hi_ref, valid_ref, src_ref, src_next_ref, x_hbm, wr_ref, wgl_ref, wgh_ref, wdl_ref,
                    wdh_ref, o_ref, buf, sem):
    i = pl.program_id(0)
    last = pl.num_programs(0) - 1
    tm = src_ref.shape[-1]
    slot = i % 2
    used = valid_ref[i] != 0
    prev_used = valid_ref[jnp.maximum(i - 1, 0)] != 0

    @pl.when(jnp.logical_and(i == 0, used))
    def _():
        _TokenGather(src_ref, x_hbm, buf, sem, 0, EXPERT_GATHER_PRIORITIES).start()

    @pl.when(jnp.logical_and(jnp.logical_and(i > 0, prev_used), jnp.logical_not(used)))
    def _():
        _TokenGather(src_ref, x_hbm, buf, sem, slot).wait()

    @pl.when(used)
    def _():
        _TokenGather(src_ref, x_hbm, buf, sem, slot).wait()
        _TokenGather(src_next_ref, x_hbm, buf, sem, 1 - slot, EXPERT_GATHER_PRIORITIES).start()
        x32 = _untile_tokens(buf.at[slot])
        x = x32.astype(BF16)

        def affinity(e):
            logit = jnp.sum(x32 * wr_ref[pl.ds(e, 1), :], axis=1, keepdims=True)
            return _sigmoid(logit)

        def expert(wg_ref, wd_ref):
            gu = jnp.dot(x, wg_ref[...], preferred_element_type=F32)
            g_ = gu[:, 0:D_EXPERT]
            act = (g_ * _sigmoid(g_) * gu[:, D_EXPERT:2 * D_EXPERT]).astype(BF16)
            return jnp.dot(act, wd_ref[...], preferred_element_type=F32)

        s_lo, s_hi = affinity(lo_ref[i]), affinity(hi_ref[i])
        total = s_lo + s_hi
        out = (s_lo / total) * expert(wgl_ref, wdl_ref) + (s_hi / total) * expert(wgh_ref, wdh_ref)
        for s in range(TOKEN_SUBROWS):
            o_ref[pl.ds(s, tm, stride=TOKEN_SUBROWS), :] = out[:, s * LANES:(s + 1) * LANES]

    @pl.when(jnp.logical_and(used, i == last))
    def _():
        _TokenGather(src_next_ref, x_hbm, buf, sem, 1 - slot).wait()

    @pl.when(jnp.logical_not(used))
    def _():
        o_ref[...] = jnp.zeros_like(o_ref)


def _experts(lo, hi, valid, src, u2t, w_rt, w_gu, w_dn):
    tm = EXPERT_TILE
    n_tiles = src.shape[0] // tm
    row = lambda i, lo, hi, valid: (i, 0)
    src3, src_specs = _gather_specs(src, tm)
    grid_spec = pltpu.PrefetchScalarGridSpec(
        num_scalar_prefetch=3,
        grid=(n_tiles,),
        in_specs=src_specs + [
            pl.BlockSpec(memory_space=pl.ANY),
            pl.BlockSpec(w_rt.shape, lambda i, lo, hi, valid: (0, 0)),
            pl.BlockSpec((None, 1024, 2 * D_EXPERT), lambda i, lo, hi, valid: (lo[i], 0, 0)),
            pl.BlockSpec((None, 1024, 2 * D_EXPERT), lambda i, lo, hi, valid: (hi[i], 0, 0)),
            pl.BlockSpec((None, D_EXPERT, 1024), lambda i, lo, hi, valid: (lo[i], 0, 0)),
            pl.BlockSpec((None, D_EXPERT, 1024), lambda i, lo, hi, valid: (hi[i], 0, 0)),
        ],
        out_specs=pl.BlockSpec((tm * TOKEN_SUBROWS, LANES), row),
        scratch_shapes=[pltpu.VMEM((2, tm * TOKEN_SUBROWS, LANES), F32), pltpu.SemaphoreType.DMA((2,))],
    )
    return pl.pallas_call(
        _experts_kernel,
        grid_spec=grid_spec,
        out_shape=jax.ShapeDtypeStruct((n_tiles * tm * TOKEN_SUBROWS, LANES), F32),
        compiler_params=_cparams("arbitrary"),
        name="moe_experts",
    )(lo, hi, valid, src3, src3, u2t, w_rt, w_gu, w_gu, w_dn, w_dn)


def _final_kernel(alpha, pos_ref, pos_next_ref, f_hbm, h1_ref, g2_ref, lg_ref, lb_ref, o_ref, buf, sem):
    i = pl.program_id(0)
    last = pl.num_programs(0) - 1
    slot = i % 2

    @pl.when(i == 0)
    def _():
        _TokenGather(pos_ref, f_hbm, buf, sem, 0).start()

    _TokenGather(pos_ref, f_hbm, buf, sem, slot).wait()
    _TokenGather(pos_next_ref, f_hbm, buf, sem, 1 - slot).start()
    f = _untile_tokens(buf.at[slot])
    o_ref[...] = _ln(alpha * h1_ref[...] + g2_ref[...] * f, lg_ref[...], lb_ref[...])

    @pl.when(i == last)
    def _():
        _TokenGather(pos_next_ref, f_hbm, buf, sem, 1 - slot).wait()


def _final_ln(h1, pos, fs, mod_l, ln2, seq_len, ctx_row, alpha):
    n = h1.shape[0]
    tm = FINAL_TILE
    row = lambda i: (i, 0)
    const = lambda i: (0, 0)
    pos3, pos_specs = _gather_specs(pos, tm)
    return pl.pallas_call(
        functools.partial(_final_kernel, alpha),
        grid=(n // tm,),
        in_specs=pos_specs + [
            pl.BlockSpec(memory_space=pl.ANY),
            pl.BlockSpec((tm, 1024), row),
            _mod_spec(5, tm, seq_len, ctx_row),
            pl.BlockSpec((1, 1024), const),
            pl.BlockSpec((1, 1024), const),
        ],
        out_specs=pl.BlockSpec((tm, 1024), row),
        out_shape=jax.ShapeDtypeStruct((n, 1024), F32),
        scratch_shapes=[pltpu.VMEM((2, tm * TOKEN_SUBROWS, LANES), F32), pltpu.SemaphoreType.DMA((2,))],
        compiler_params=_cparams("arbitrary"),
        name="final_ln",
    )(pos3, pos3, fs, h1, mod_l, ln2[0], ln2[1])


def _sort_plan(route_t, n):
    tm = EXPERT_TILE
    n_tiles = n // tm + N_CLASSES
    cls = route_t[AUX_CLS].astype(jnp.int32)
    rank = route_t[AUX_RANK].astype(jnp.int32)
    onehot = cls[:, None] == jnp.arange(N_CLASSES, dtype=jnp.int32)[None, :]
    counts = jnp.sum(onehot, axis=0, dtype=jnp.int32)
    padded = (counts + tm - 1) // tm * tm
    ends = jnp.cumsum(padded)
    offs = ends - padded
    pos = jnp.sum(jnp.where(onehot, offs[None, :], 0), axis=1) + rank
    tile_ends = ends // tm
    j = jnp.arange(n_tiles, dtype=jnp.int32)
    n_used = tile_ends[-1]
    valid = j < n_used
    tile_cls = jnp.sum(j[:, None] >= tile_ends[None, :], axis=1)
    last_cls = jnp.sum((n_used - 1) >= tile_ends)
    tile_cls = jnp.where(valid, tile_cls, last_cls)
    group, pid = tile_cls // len(_PAIRS), tile_cls % len(_PAIRS)
    pair = jnp.asarray(np.array(_PAIRS, dtype=np.int32))
    lo = group * EXPERTS_PER_GROUP + pair[pid, 0]
    hi = group * EXPERTS_PER_GROUP + pair[pid, 1]
    order = jnp.argsort(cls, stable=True).astype(jnp.int32)
    starts = jnp.cumsum(counts) - counts
    within = (j * tm - offs[tile_cls])[:, None] + jnp.arange(tm, dtype=jnp.int32)[None, :]
    real = valid[:, None] & (within < counts[tile_cls][:, None])
    src = jnp.where(real, order[jnp.where(real, starts[tile_cls][:, None] + within, 0)], 0).reshape(-1)
    return pos.astype(jnp.int32), src.astype(jnp.int32), lo.astype(jnp.int32), hi.astype(jnp.int32), valid.astype(jnp.int32)


def _moe(u2t, route_t, h1, mod_l, ln2, w_rt, w_gu, w_dn, seq_len, ctx_row, alpha):
    n = h1.shape[0]
    pos, src, lo, hi, valid = _sort_plan(route_t, n)
    fs = _experts(lo, hi, valid, src, u2t, w_rt, w_gu, w_dn)
    return _final_ln(h1, pos, fs, mod_l, ln2, seq_len, ctx_row, alpha)


def _rope_swap_index(n_heads):
    idx = np.arange(n_heads * HEAD_DIM)
    within = idx % (HEAD_DIM // 2)
    quarter = HEAD_DIM // 4
    return np.where(within < quarter, idx + quarter, idx - quarter)


def _rope_tables(seq_len, n_heads, scale):
    t = np.arange(seq_len)
    quarter = HEAD_DIM // 4
    inv = ROPE_BASE ** (-np.arange(quarter, dtype=np.float32) / quarter)
    d = np.arange(HEAD_DIM)
    pos = np.where((d // (HEAD_DIM // 2) == 0)[None, :], (t // GRID_W)[:, None], (t % GRID_W)[:, None])
    ang = jnp.asarray(pos.astype(np.float32)) * jnp.asarray(inv[d % quarter])[None, :]
    sign = np.where(d % (HEAD_DIM // 2) < quarter, -1.0, 1.0).astype(np.float32)
    cos = jnp.cos(ang) * scale
    sin = jnp.sin(ang) * (sign * scale)[None, :]
    return jnp.tile(cos, (1, n_heads)), jnp.tile(sin, (1, n_heads))


def _flat_tables(seq_len, n_heads, scale):
    return (jnp.full((seq_len, n_heads * HEAD_DIM), scale, F32), jnp.zeros((seq_len, n_heads * HEAD_DIM), F32))


def _prep_in_weights(w_in_l):
    splits = np.cumsum([2 * D_CONV, D_ATTN, D_KV, D_KV, D_MLSTM, D_MLSTM, D_MLSTM, D_MLSTM, N_GATE_COLS])
    a, q, k, v, mq, mk, mv, mo, mg, bg = jnp.split(w_in_l, splits, axis=1)
    w_main = jnp.concatenate([a, q, q[:, _rope_swap_index(N_Q_HEADS)], k, k[:, _rope_swap_index(N_KV_HEADS)],
                              mq, mv, mo, bg], axis=1).astype(BF16)
    order = np.array([d * 8 + kind * 4 + h for h in range(N_MLSTM_HEADS) for d in range(2) for kind in range(2)])
    w_t = jnp.concatenate([mk.T, mg[:, order].T, v.T], axis=0).astype(BF16)
    return w_main, w_t


def kernel(x, c, ctx, c_ctx, ln_in_g, ln_in_b, w_router, b_router, w_mod, b_mod, w_in, conv_w, conv_b, conv_ln_g,
           conv_ln_b, w_a_out, attn_sink, w_b_out, mlstm_gate_b, mlstm_norm_g, w_c_out, w_out, ln1_g, ln1_b,
           moe_w_gu, moe_w_dn, ln2_g, ln2_b):
    batch, seq_len, d = x.shape
    ctx_len = ctx.shape[1]
    depth = w_in.shape[0]
    alpha = (2.0 * depth) ** 0.25
    ctx_row = batch
    assert d == 1024 and batch < MOD_ROWS
    assert seq_len % MERGE_TILE == 0 and ctx_len % TOKEN_TILE == 0 and (batch * ctx_len) % MERGE_TILE == 0
    assert seq_len % FINAL_TILE == 0 and (batch * ctx_len) % FINAL_TILE == 0 and FINAL_TILE % EXPERT_TILE == 0

    cc = jnp.zeros((MOD_ROWS, d), F32).at[0:batch].set(c).at[batch].set(c_ctx)
    mod = _modulation(cc, w_mod, b_mod).reshape(depth, MOD_ROWS * N_MOD, 1, d)

    attn_scale = HEAD_DIM ** -0.5
    rope_lat = _rope_tables(seq_len, N_Q_HEADS, attn_scale) + _rope_tables(seq_len, N_KV_HEADS, 1.0)
    rope_ctx = _flat_tables(ctx_len, N_Q_HEADS, attn_scale) + _flat_tables(ctx_len, N_KV_HEADS, 1.0)

    vec = lambda t: t.reshape(1, -1)
    ln_in = (vec(ln_in_g), vec(ln_in_b))
    w_rt = w_router.T
    b_r = b_router.reshape(N_EXPERTS, 1)

    h = x.reshape(batch * seq_len, d)
    hc = ctx.reshape(batch * ctx_len, d)
    for l in range(depth):
        need_ctx = l < depth - 1
        pre_ln = l == 0
        mod_l = mod[l]
        w_main, w_t = _prep_in_weights(w_in[l])
        lat = _in_proj(h, ln_in[0], ln_in[1], mod_l, w_main, w_t, rope_lat, seq_len, None, pre_ln)
        cx = _in_proj(hc, ln_in[0], ln_in[1], mod_l, w_main, w_t, rope_ctx, ctx_len, ctx_row, pre_ln)
        a_l, q_l, k_l, v_l, mq_l, mv_l, mo_l, bg_l, kt_l, gt_l = lat
        a_c, q_c, k_c, v_c, mq_c, mv_c, mo_c, bg_c, kt_c, gt_c = cx

        conv_args = (conv_w[l], vec(conv_b[l]), vec(conv_ln_g[l]), vec(conv_ln_b[l]))
        gate_b = jnp.transpose(mlstm_gate_b[l], (2, 0, 1)).reshape(N_MLSTM_HEADS, 4, 1)
        ya = _conv_branch(a_l, *conv_args, seq_len)
        yb = _attn_latent(q_l, k_l, v_l, k_c, v_c, attn_sink[l], seq_len, ctx_len)
        yc, yc_c = _mlstm_branch((mq_c, kt_c, mv_c, mo_c, gt_c), (mq_l, kt_l, mv_l, mo_l, gt_l),
                                 gate_b, vec(mlstm_norm_g[l]), seq_len, ctx_len, need_ctx)

        ln1 = (vec(ln1_g[l]), vec(ln1_b[l]))
        ln2 = (vec(ln2_g[l]), vec(ln2_b[l]))
        w_a, w_b, w_c, w_o = (w.astype(BF16) for w in (w_a_out[l], w_b_out[l], w_c_out[l], w_out[l]))
        w_gu = moe_w_gu[l].astype(BF16)
        w_dn = moe_w_dn[l].astype(BF16)

        h1, pay, route_t = _merge(ya, yb, yc, bg_l, h, mod_l, ln_in, ln1, w_a, w_b, w_c, w_o, w_rt, b_r,
                                  seq_len, None, pre_ln, alpha)
        h = _moe(pay, route_t, h1, mod_l, ln2, w_rt, w_gu, w_dn, seq_len, None, alpha)
        if need_ctx:
            ya_c = _conv_branch(a_c, *conv_args, ctx_len)
            yb_c = _attn_context(q_c, k_c, v_c, attn_sink[l], ctx_len)
            h1c, pay_c, route_tc = _merge(ya_c, yb_c, yc_c, bg_c, hc, mod_l, ln_in, ln1, w_a, w_b, w_c, w_o, w_rt,
                                          b_r, ctx_len, ctx_row, pre_ln, alpha)
            hc = _moe(pay_c, route_tc, h1c, mod_l, ln2, w_rt, w_gu, w_dn, ctx_len, ctx_row, alpha)
    return h.reshape(batch, seq_len, d)
```

```python
import functools

import numpy as np
import jax
import jax.numpy as jnp
from jax import lax
from jax.experimental import pallas as pl
from jax.experimental.pallas import tpu as pltpu

GRID_W = 64
LN_EPS = 1e-5
D_CONV = 512
CONV_WIDTH = 31
N_Q_HEADS = 8
N_KV_HEADS = 2
HEAD_DIM = 64
WINDOW = 128
BLOCK = 128
ROPE_BASE = 10000.0
D_ATTN = N_Q_HEADS * HEAD_DIM
D_KV = N_KV_HEADS * HEAD_DIM
N_MLSTM_HEADS = 4
MLSTM_HEAD_DIM = 128
D_MLSTM = N_MLSTM_HEADS * MLSTM_HEAD_DIM
N_GATE_COLS = 2 * 2 * N_MLSTM_HEADS
N_BRANCHES = 3
N_EXPERTS = 16
N_GROUPS = 4
EXPERTS_PER_GROUP = N_EXPERTS // N_GROUPS
D_EXPERT = 512
N_MOD = 6

LANES = 128
V7X_VMEM_LIMIT_BYTES = 56 * 1024 * 1024

MOD_ROWS = 16
MOD_COL_BLOCK = 512
TOKEN_TILE = 256
EXPERT_TILE = 256
EXPERT_GATHER_PRIORITIES = (1,)
FINAL_TILE = 512
MLSTM_CHUNK = 128
MLSTM_HEADS_PER_STEP = 4
MERGE_TILE = 512
CONV_ROWS = 64
CONV_PAD = 16

_PAIRS = [(i, j) for i in range(EXPERTS_PER_GROUP) for j in range(i + 1, EXPERTS_PER_GROUP)]
N_CLASSES = N_GROUPS * len(_PAIRS)
N_CLASS_ROWS = 32
AUX_CLS, AUX_W_LO, AUX_W_HI, AUX_RANK = 0, 1, 2, 3
TOKEN_SUBROWS = 1024 // LANES

F32 = jnp.float32
BF16 = jnp.bfloat16
HIGHEST = lax.Precision.HIGHEST
NEG_INF = float("-inf")

_C_A = 0
_C_Q = _C_A + 2 * D_CONV
_C_QS = _C_Q + D_ATTN
_C_K = _C_QS + D_ATTN
_C_KS = _C_K + D_KV
_C_MQ = _C_KS + D_KV
_C_MV = _C_MQ + D_MLSTM
_C_MO = _C_MV + D_MLSTM
_C_BG = _C_MO + D_MLSTM
_C_END = _C_BG + N_BRANCHES * 1024
_R_KT = 0
_R_GT = _R_KT + D_MLSTM
_R_VT = _R_GT + N_GATE_COLS


def _cparams(*sem):
    return pltpu.CompilerParams(dimension_semantics=sem, vmem_limit_bytes=V7X_VMEM_LIMIT_BYTES)


def _ln(x, g, b):
    mu = jnp.mean(x, axis=-1, keepdims=True)
    xc = x - mu
    var = jnp.mean(xc * xc, axis=-1, keepdims=True)
    return xc * lax.rsqrt(var + LN_EPS) * g + b


def _sigmoid(x):
    return 0.5 * jnp.tanh(0.5 * x) + 0.5


def _log_sigmoid(x):
    return jnp.minimum(x, 0.0) - jnp.log(1.0 + jnp.exp(-jnp.abs(x)))


def _dot_nt(a, b, precision=None):
    return lax.dot_general(a, b, (((1,), (1,)), ((), ())), preferred_element_type=F32, precision=precision)


def _mod_kernel(c_ref, w_ref, b_ref, o_ref):
    c = c_ref[...]
    s = c * _sigmoid(c)
    o_ref[...] = jnp.dot(s, w_ref[...], preferred_element_type=F32, precision=HIGHEST) + b_ref[...]


def _modulation(cc, w_mod, b_mod):
    depth, d, n = w_mod.shape
    return pl.pallas_call(
        _mod_kernel,
        grid=(depth, n // MOD_COL_BLOCK),
        in_specs=[
            pl.BlockSpec((MOD_ROWS, d), lambda l, j: (0, 0)),
            pl.BlockSpec((None, d, MOD_COL_BLOCK), lambda l, j: (l, 0, j)),
            pl.BlockSpec((None, 1, MOD_COL_BLOCK), lambda l, j: (l, 0, j)),
        ],
        out_specs=pl.BlockSpec((None, MOD_ROWS, MOD_COL_BLOCK), lambda l, j: (l, 0, j)),
        out_shape=jax.ShapeDtypeStruct((depth, MOD_ROWS, n), F32),
        compiler_params=_cparams("parallel", "parallel"),
        name="modulation",
    )(cc, w_mod, b_mod.reshape(depth, 1, n))


def _mod_spec(which, tile, seq_len, ctx_row):
    tiles_per_seq = seq_len // tile
    if ctx_row is None:
        return pl.BlockSpec((None, 1, 1024), lambda i, *_: ((i // tiles_per_seq) * N_MOD + which, 0, 0))
    return pl.BlockSpec((None, 1, 1024), lambda i, *_: (ctx_row * N_MOD + which, 0, 0))


def _in_kernel(pre_ln, x_ref, lg_ref, lb_ref, sc_ref, sh_ref, w_ref, wt_ref, cq_ref, sq_ref, ck_ref, sk_ref,
               a_ref, q_ref, k_ref, v_ref, mq_ref, mv_ref, mo_ref, bg_ref, kt_ref, gt_ref):
    x = x_ref[...]
    if pre_ln:
        x = _ln(x, lg_ref[...], lb_ref[...])
    u = (x * (1.0 + sc_ref[...]) + sh_ref[...]).astype(BF16)

    def seg(lo, hi):
        return jnp.dot(u, w_ref[:, lo:hi], preferred_element_type=F32)

    a_ref[...] = seg(_C_A, _C_Q).astype(BF16)
    q_ref[...] = (seg(_C_Q, _C_QS) * cq_ref[...] + seg(_C_QS, _C_K) * sq_ref[...]).astype(BF16)
    k_ref[...] = (seg(_C_K, _C_KS) * ck_ref[...] + seg(_C_KS, _C_MQ) * sk_ref[...]).astype(BF16)
    v_ref[...] = _dot_nt(wt_ref[_R_VT:_R_VT + D_KV, :], u).astype(BF16)
    mq_ref[...] = seg(_C_MQ, _C_MV).astype(BF16)
    mv_ref[...] = seg(_C_MV, _C_MO).astype(BF16)
    mo_ref[...] = seg(_C_MO, _C_BG).astype(BF16)
    for j in range(N_BRANCHES):
        bg_ref[:, j * 1024:(j + 1) * 1024] = _sigmoid(seg(_C_BG + j * 1024, _C_BG + (j + 1) * 1024)).astype(BF16)
    n_chunks = u.shape[0] // MLSTM_CHUNK
    kt = _dot_nt(wt_ref[_R_KT:_R_KT + D_MLSTM, :], u)
    for h in range(N_MLSTM_HEADS):
        for c in range(n_chunks):
            kt_ref[h, c] = kt[h * MLSTM_HEAD_DIM:(h + 1) * MLSTM_HEAD_DIM,
                              c * MLSTM_CHUNK:(c + 1) * MLSTM_CHUNK].astype(BF16)
    gt = _dot_nt(wt_ref[_R_GT:_R_GT + N_GATE_COLS, :], u)
    for h in range(N_MLSTM_HEADS):
        for c in range(n_chunks):
            gt_ref[h, c] = gt[h * 4:(h + 1) * 4, c * MLSTM_CHUNK:(c + 1) * MLSTM_CHUNK]


def _in_proj(x, ln_g, ln_b, mod_l, w_main, w_t, rope, seq_len, ctx_row, pre_ln):
    n = x.shape[0]
    tm = TOKEN_TILE
    cq, sq, ck, sk = rope
    tps = seq_len // tm
    nch = n // MLSTM_CHUNK
    row = lambda i: (i, 0)
    pos = lambda i: (i % tps, 0)
    const = lambda i: (0, 0)
    out_shape = [
        jax.ShapeDtypeStruct((n, 2 * D_CONV), BF16),
        jax.ShapeDtypeStruct((n, D_ATTN), BF16),
        jax.ShapeDtypeStruct((n, D_KV), BF16),
        jax.ShapeDtypeStruct((D_KV, n), BF16),
        jax.ShapeDtypeStruct((n, D_MLSTM), BF16),
        jax.ShapeDtypeStruct((n, D_MLSTM), BF16),
        jax.ShapeDtypeStruct((n, D_MLSTM), BF16),
        jax.ShapeDtypeStruct((n, N_BRANCHES * 1024), BF16),
        jax.ShapeDtypeStruct((N_MLSTM_HEADS, nch, MLSTM_HEAD_DIM, MLSTM_CHUNK), BF16),
        jax.ShapeDtypeStruct((N_MLSTM_HEADS, nch, 4, MLSTM_CHUNK), F32),
    ]
    cpt = tm // MLSTM_CHUNK
    out_specs = [
        pl.BlockSpec((tm, 2 * D_CONV), row),
        pl.BlockSpec((tm, D_ATTN), row),
        pl.BlockSpec((tm, D_KV), row),
        pl.BlockSpec((D_KV, tm), lambda i: (0, i)),
        pl.BlockSpec((tm, D_MLSTM), row),
        pl.BlockSpec((tm, D_MLSTM), row),
        pl.BlockSpec((tm, D_MLSTM), row),
        pl.BlockSpec((tm, N_BRANCHES * 1024), row),
        pl.BlockSpec((N_MLSTM_HEADS, cpt, MLSTM_HEAD_DIM, MLSTM_CHUNK), lambda i: (0, i, 0, 0)),
        pl.BlockSpec((N_MLSTM_HEADS, cpt, 4, MLSTM_CHUNK), lambda i: (0, i, 0, 0)),
    ]
    in_specs = [
        pl.BlockSpec((tm, 1024), row),
        pl.BlockSpec((1, 1024), const),
        pl.BlockSpec((1, 1024), const),
        _mod_spec(1, tm, seq_len, ctx_row),
        _mod_spec(0, tm, seq_len, ctx_row),
        pl.BlockSpec(w_main.shape, const, pipeline_mode=pl.Buffered(1)),
        pl.BlockSpec(w_t.shape, const, pipeline_mode=pl.Buffered(1)),
        pl.BlockSpec((tm, D_ATTN), pos),
        pl.BlockSpec((tm, D_ATTN), pos),
        pl.BlockSpec((tm, D_KV), pos),
        pl.BlockSpec((tm, D_KV), pos),
    ]
    return pl.pallas_call(
        functools.partial(_in_kernel, pre_ln),
        grid=(n // tm,),
        in_specs=in_specs,
        out_specs=out_specs,
        out_shape=out_shape,
        compiler_params=_cparams("parallel"),
        name="in_proj",
    )(x, ln_g, ln_b, mod_l, mod_l, w_main, w_t, cq, sq, ck, sk)


def _conv_kernel(a_ref, w_ref, cb_ref, g_ref, b_ref, o_ref, upad_ref):
    t = a_ref.shape[0]
    zeros = jnp.zeros((CONV_PAD, D_CONV), F32)
    upad_ref[0:CONV_PAD, :] = zeros
    upad_ref[CONV_PAD + t:2 * CONV_PAD + t, :] = zeros
    val = a_ref[:, 0:D_CONV].astype(F32)
    gate = a_ref[:, D_CONV:2 * D_CONV].astype(F32)
    upad_ref[CONV_PAD:CONV_PAD + t, :] = val * _sigmoid(gate)
    half = CONV_WIDTH // 2

    def body(c, carry):
        r0 = pl.multiple_of(c * CONV_ROWS, CONV_ROWS)
        n_win = CONV_ROWS + 2 * CONV_PAD
        win = upad_ref[pl.ds(r0, n_win), :]
        acc = jnp.zeros((CONV_ROWS, D_CONV), F32) + cb_ref[...]
        for res in range(8):
            rolled = win if res == 0 else pltpu.roll(win, shift=n_win - res, axis=0)
            for k in range(CONV_WIDTH):
                off = CONV_PAD - half + k
                if off % 8 == res:
                    acc = acc + rolled[off - res:off - res + CONV_ROWS, :] * w_ref[k:k + 1, :]
        y = _ln(acc, g_ref[...], b_ref[...])
        o_ref[pl.ds(r0, CONV_ROWS), :] = (y * _sigmoid(y)).astype(BF16)
        return carry

    lax.fori_loop(0, t // CONV_ROWS, body, 0)


def _conv_branch(a_in, conv_w, conv_b, ln_g, ln_b, seq_len):
    n = a_in.shape[0]
    const = lambda b: (0, 0)
    return pl.pallas_call(
        _conv_kernel,
        grid=(n // seq_len,),
        in_specs=[
            pl.BlockSpec((seq_len, 2 * D_CONV), lambda b: (b, 0)),
            pl.BlockSpec((CONV_WIDTH, D_CONV), const),
            pl.BlockSpec((1, D_CONV), const),
            pl.BlockSpec((1, D_CONV), const),
            pl.BlockSpec((1, D_CONV), const),
        ],
        out_specs=pl.BlockSpec((seq_len, D_CONV), lambda b: (b, 0)),
        out_shape=jax.ShapeDtypeStruct((n, D_CONV), BF16),
        scratch_shapes=[pltpu.VMEM((seq_len + 2 * CONV_PAD, D_CONV), F32)],
        compiler_params=_cparams("parallel"),
        name="conv_branch",
    )(a_in, conv_w, conv_b, ln_g, ln_b)


def _attn_heads(q, keys, vals_t, masks, sink_ref, o_ref):
    rows = q.shape[0]
    group = N_Q_HEADS // N_KV_HEADS
    for hk in range(N_KV_HEADS):
        lo = hk * HEAD_DIM
        qs = jnp.concatenate([q[:, (hk * group + g) * HEAD_DIM:(hk * group + g + 1) * HEAD_DIM]
                              for g in range(group)], axis=0)
        sink = jnp.concatenate([jnp.full((1, rows), sink_ref[hk * group + g], F32) for g in range(group)], axis=1)
        scores = []
        m = sink
        for kk, mask in zip(keys, masks):
            s = _dot_nt(kk[:, lo:lo + HEAD_DIM], qs)
            if mask is not None:
                s = jnp.where(mask, s, NEG_INF)
            scores.append(s)
            m = jnp.maximum(m, jnp.max(s, axis=0, keepdims=True))
        acc = jnp.zeros((2 * HEAD_DIM, rows * group), F32)
        for s, vt in zip(scores, vals_t):
            n_k = s.shape[0]
            p = jnp.exp(s - m).astype(BF16)
            ones_rows = jnp.where(lax.broadcasted_iota(jnp.int32, (HEAD_DIM, n_k), 0) == 0, 1.0, 0.0).astype(BF16)
            v_aug = jnp.concatenate([vt[lo:lo + HEAD_DIM, :], ones_rows], axis=0)
            acc = acc + jnp.dot(v_aug, p, preferred_element_type=F32)
        denom = acc[HEAD_DIM:HEAD_DIM + 1, :] + jnp.exp(sink - m)
        o_t = acc * (1.0 / denom)
        for g in range(group):
            col = (hk * group + g) * HEAD_DIM
            o_ref[:, col:col + HEAD_DIM] = o_t[:, g * rows:(g + 1) * rows].T[:, 0:HEAD_DIM].astype(BF16)


def _attn_lat_kernel(sink_ref, q_ref, kp_ref, k0_ref, kn_ref, vp_ref, v0_ref, vn_ref, kc_ref, vc_ref, o_ref):
    n = pl.program_id(1)
    nb = pl.num_programs(1)
    stacked = (N_Q_HEADS // N_KV_HEADS) * BLOCK
    ki = lax.broadcasted_iota(jnp.int32, (BLOCK, stacked), 0)
    qi = lax.broadcasted_iota(jnp.int32, (BLOCK, stacked), 1) % BLOCK
    mask_prev = ki >= qi + jnp.where(n > 0, 0, BLOCK)
    mask_next = ki <= qi - jnp.where(n < nb - 1, 0, BLOCK)
    _attn_heads(q_ref[...],
                [kp_ref[...], k0_ref[...], kn_ref[...], kc_ref[...]],
                [vp_ref[...], v0_ref[...], vn_ref[...], vc_ref[...]],
                [mask_prev, None, mask_next, None], sink_ref, o_ref)


def _attn_ctx_kernel(sink_ref, q_ref, kc_ref, vc_ref, o_ref):
    _attn_heads(q_ref[...], [kc_ref[...]], [vc_ref[...]], [None], sink_ref, o_ref)


def _attn_latent(q, k, vt, kc, vct, sink, seq_len, ctx_len):
    n = q.shape[0]
    nb = seq_len // BLOCK
    batch = n // seq_len
    blk_prev = lambda b, j: b * nb + jnp.maximum(j - 1, 0)
    blk_next = lambda b, j: b * nb + jnp.minimum(j + 1, nb - 1)
    cur = lambda b, j: (b * nb + j, 0)
    kspec = lambda blk: pl.BlockSpec((BLOCK, D_KV), lambda b, j: (blk(b, j), 0))
    vspec = lambda blk: pl.BlockSpec((D_KV, BLOCK), lambda b, j: (0, blk(b, j)))
    blk_cur = lambda b, j: b * nb + j
    return pl.pallas_call(
        _attn_lat_kernel,
        grid=(batch, nb),
        in_specs=[
            pl.BlockSpec(memory_space=pltpu.SMEM),
            pl.BlockSpec((BLOCK, D_ATTN), cur),
            kspec(blk_prev), kspec(blk_cur), kspec(blk_next), vspec(blk_prev), vspec(blk_cur), vspec(blk_next),
            pl.BlockSpec((ctx_len, D_KV), lambda b, j: (b, 0)),
            pl.BlockSpec((D_KV, ctx_len), lambda b, j: (0, b)),
        ],
        out_specs=pl.BlockSpec((BLOCK, D_ATTN), cur),
        out_shape=jax.ShapeDtypeStruct((n, D_ATTN), BF16),
        compiler_params=_cparams("parallel", "parallel"),
        name="attn_latent",
    )(sink, q, k, k, k, vt, vt, vt, kc, vct)


def _attn_context(qc, kc, vct, sink, ctx_len):
    n = qc.shape[0]
    blk = lambda b: (b, 0)
    return pl.pallas_call(
        _attn_ctx_kernel,
        grid=(n // ctx_len,),
        in_specs=[
            pl.BlockSpec(memory_space=pltpu.SMEM),
            pl.BlockSpec((ctx_len, D_ATTN), blk),
            pl.BlockSpec((ctx_len, D_KV), blk),
            pl.BlockSpec((D_KV, ctx_len), lambda b: (0, b)),
        ],
        out_specs=pl.BlockSpec((ctx_len, D_ATTN), blk),
        out_shape=jax.ShapeDtypeStruct((n, D_ATTN), BF16),
        compiler_params=_cparams("parallel"),
        name="attn_context",
    )(sink, qc, kc, vct)


def _mlstm_kernel(ctx_out, nch_c, nch_l, hps,
                  qc_ref, ktc_ref, vc_ref, moc_ref, gc_ref,
                  ql_ref, ktl_ref, vl_ref, mol_ref, gl_ref,
                  gb_ref, ng_ref, *rest):
    if ctx_out:
        yl_ref, yc_ref, af_ref, lff_ref, ab_ref, lfb_ref, cf_ref, cb_ref, hf_ref, hb_ref = rest
    else:
        yl_ref, af_ref, lff_ref, ab_ref, lfb_ref, cf_ref, cb_ref, hf_ref, hb_ref = rest
        yc_ref = None
    lc = MLSTM_CHUNK
    dh = MLSTM_HEAD_DIM
    k_scale = MLSTM_HEAD_DIM ** -0.5
    ti = lax.broadcasted_iota(jnp.int32, (lc, lc), 0)
    si = lax.broadcasted_iota(jnp.int32, (lc, lc), 1)
    lower = si <= ti
    upper = si >= ti
    pre_mat = jnp.where(upper, 1.0, 0.0)
    suf_mat = jnp.where(lower, 1.0, 0.0)
    n_rows = nch_c + nch_l
    pad_rows = -n_rows % 8

    for hh in range(hps):
        gb = gb_ref[hh]

        def gate_rows(kind):
            rows = ([gc_ref[hh, c, kind:kind + 1, :] for c in range(nch_c)]
                    + [gl_ref[hh, c, kind:kind + 1, :] for c in range(nch_l)])
            rows = jnp.concatenate(rows, axis=0) + gb[kind:kind + 1, :]
            if pad_rows:
                rows = jnp.concatenate([rows, jnp.zeros((pad_rows, lc), F32)], axis=0)
            return rows

        lf_f = _log_sigmoid(gate_rows(1))
        lf_b = _log_sigmoid(gate_rows(3))
        a_f = gate_rows(0) - jnp.dot(lf_f, pre_mat, preferred_element_type=F32, precision=HIGHEST)
        a_b = gate_rows(2) - jnp.dot(lf_b, suf_mat, preferred_element_type=F32, precision=HIGHEST)
        for c in range(n_rows):
            af_ref[hh * n_rows + c] = a_f[c:c + 1, :]
            lff_ref[hh * n_rows + c] = lf_f[c:c + 1, :]
            ab_ref[hh * n_rows + c] = a_b[c:c + 1, :]
            lfb_ref[hh * n_rows + c] = lf_b[c:c + 1, :]

    cf_ref[...] = jnp.zeros_like(cf_ref)
    cb_ref[...] = jnp.zeros_like(cb_ref)
    ones_col = jnp.where(lax.broadcasted_iota(jnp.int32, (lc, dh), 1) == 0, 1.0, 0.0).astype(BF16)

    def chunk(q, kt, v, a_row, lf_row, c_ref, hh, m, mask):
        a_mat = jnp.where(mask, jnp.broadcast_to(a_row, (lc, lc)), NEG_INF)
        cm = jnp.max(a_mat, axis=1, keepdims=True)
        b_col = jnp.sum(jnp.where(mask, jnp.broadcast_to(lf_row, (lc, lc)), 0.0), axis=1, keepdims=True)
        mx = jnp.maximum(m, cm)
        mx_last = jnp.maximum(m, jnp.max(a_row, axis=1, keepdims=True))
        w = jnp.exp(a_mat - mx)
        s = jnp.dot(q, kt, preferred_element_type=F32) * k_scale
        p = (s * w).astype(BF16)
        w_s = jnp.exp(a_row - mx_last) * k_scale
        ktw = (kt.astype(F32) * w_s).astype(BF16)
        v_aug = jnp.concatenate([v, ones_col], axis=1)
        both = jnp.dot(jnp.concatenate([p, ktw], axis=0), v_aug, preferred_element_type=F32)
        c_old = c_ref[hh]
        inter = jnp.dot(q, c_old.astype(BF16), preferred_element_type=F32)
        tot = both[0:lc, :] + jnp.exp(m - mx) * inter
        den = tot[:, dh:dh + 1]
        h = tot[:, 0:dh] / jnp.maximum(jnp.abs(den), jnp.exp(-(b_col + mx)))
        c_ref[hh] = jnp.exp(m - mx_last) * c_old + both[lc:lc + dh, :]
        m_new = jnp.sum(lf_row, axis=1, keepdims=True) + mx_last
        return h, m_new

    def step(q_ref, kt_ref, v_ref, row0, c_f, c_b, rf, rb, ms):
        out = []
        for hh in range(hps):
            cols = slice(hh * dh, (hh + 1) * dh)
            base = hh * n_rows + row0
            h_f, m_f = chunk(q_ref[pl.ds(rf, lc), cols], kt_ref[hh, c_f], v_ref[pl.ds(rf, lc), cols],
                             af_ref[base + c_f], lff_ref[base + c_f], cf_ref, hh, ms[2 * hh], lower)
            h_b, m_b = chunk(q_ref[pl.ds(rb, lc), cols], kt_ref[hh, c_b], v_ref[pl.ds(rb, lc), cols],
                             ab_ref[base + c_b], lfb_ref[base + c_b], cb_ref, hh, ms[2 * hh + 1], upper)
            hf_ref[pl.ds(rf, lc), cols] = h_f
            hb_ref[pl.ds(rb, lc), cols] = h_b
            out += [m_f, m_b]
        return tuple(out)

    def finish(n_tok, mo_ref, o_ref):
        for hh in range(hps):
            cols = slice(hh * dh, (hh + 1) * dh)
            h = hf_ref[0:n_tok, cols] + hb_ref[0:n_tok, cols]
            mu = jnp.mean(h, axis=-1, keepdims=True)
            hc = h - mu
            var = jnp.mean(hc * hc, axis=-1, keepdims=True)
            y = hc * lax.rsqrt(var + LN_EPS) * ng_ref[:, cols]
            o_ref[:, cols] = (_sigmoid(mo_ref[:, cols].astype(F32)) * y).astype(BF16)

    ms = tuple(jnp.zeros((1, 1), F32) for _ in range(2 * hps))
    for c in range(nch_c):
        c_b = nch_c - 1 - c
        ms = step(qc_ref, ktc_ref, vc_ref, 0, c, c_b, c * lc, c_b * lc, ms)
    if ctx_out:
        finish(nch_c * lc, moc_ref, yc_ref)

    def body(c, ms):
        c_b = nch_l - 1 - c
        return step(ql_ref, ktl_ref, vl_ref, nch_c, c, c_b,
                    pl.multiple_of(c * lc, lc), pl.multiple_of(c_b * lc, lc), ms)

    lax.fori_loop(0, nch_l, body, ms)
    finish(nch_l * lc, mol_ref, yl_ref)


def _mlstm_branch(ctx_p, lat_p, gate_b, norm_g, seq_len, ctx_len, ctx_out):
    mq_c, kt_c, mv_c, mo_c, g_c = ctx_p
    mq_l, kt_l, mv_l, mo_l, g_l = lat_p
    n_l, n_c = mq_l.shape[0], mq_c.shape[0]
    batch = n_l // seq_len
    lc, dh, hps = MLSTM_CHUNK, MLSTM_HEAD_DIM, MLSTM_HEADS_PER_STEP
    nch_c, nch_l = ctx_len // lc, seq_len // lc

    def stream(t, nch):
        tok = pl.BlockSpec((t, hps * dh), lambda b, h: (b, h))
        return [tok,
                pl.BlockSpec((hps, nch, dh, lc), lambda b, h: (h, b, 0, 0)),
                tok, tok,
                pl.BlockSpec((hps, nch, 4, lc), lambda b, h: (h, b, 0, 0))]

    in_specs = stream(ctx_len, nch_c) + stream(seq_len, nch_l) + [
        pl.BlockSpec((hps, 4, 1), lambda b, h: (h, 0, 0)),
        pl.BlockSpec((1, hps * dh), lambda b, h: (0, h)),
    ]
    out_specs = [pl.BlockSpec((seq_len, hps * dh), lambda b, h: (b, h))]
    out_shape = [jax.ShapeDtypeStruct((n_l, D_MLSTM), BF16)]
    if ctx_out:
        out_specs.append(pl.BlockSpec((ctx_len, hps * dh), lambda b, h: (b, h)))
        out_shape.append(jax.ShapeDtypeStruct((n_c, D_MLSTM), BF16))
    row_scratch = pltpu.VMEM((hps * (nch_c + nch_l), 1, lc), F32)
    outs = pl.pallas_call(
        functools.partial(_mlstm_kernel, ctx_out, nch_c, nch_l, hps),
        grid=(batch, N_MLSTM_HEADS // hps),
        in_specs=in_specs,
        out_specs=out_specs,
        out_shape=out_shape,
        scratch_shapes=[row_scratch, row_scratch, row_scratch, row_scratch,
                        pltpu.VMEM((hps, dh, 2 * dh), F32), pltpu.VMEM((hps, dh, 2 * dh), F32),
                        pltpu.VMEM((seq_len, hps * dh), F32), pltpu.VMEM((seq_len, hps * dh), F32)],
        compiler_params=_cparams("parallel", "parallel"),
        name="mlstm_branch",
    )(mq_c, kt_c, mv_c, mo_c, g_c, mq_l, kt_l, mv_l, mo_l, g_l, gate_b, norm_g)
    return outs if ctx_out else (outs[0], None)


def _route(logits_t, br):
    sc = [_sigmoid(logits_t[e:e + 1, :]) for e in range(N_EXPERTS)]
    sel = [sc[e] + br[e:e + 1, :] for e in range(N_EXPERTS)]
    epg = EXPERTS_PER_GROUP
    group_score = []
    for g in range(N_GROUPS):
        v = sel[g * epg:(g + 1) * epg]
        best = None
        for i in range(epg):
            for j in range(i + 1, epg):
                pair = v[i] + v[j]
                best = pair if best is None else jnp.maximum(best, pair)
        group_score.append(best)
    g_idx = jnp.zeros_like(group_score[0], dtype=jnp.int32)
    best = group_score[0]
    for g in range(1, N_GROUPS):
        better = group_score[g] > best
        g_idx = jnp.where(better, g, g_idx)
        best = jnp.maximum(best, group_score[g])
    chosen = []
    for g in range(N_GROUPS):
        v = sel[g * epg:(g + 1) * epg]
        in_g = g_idx == g
        for i in range(epg):
            rank = jnp.zeros_like(g_idx)
            for j in range(epg):
                if j == i:
                    continue
                ahead = (v[j] >= v[i]) if j < i else (v[j] > v[i])
                rank = rank + jnp.where(ahead, 1, 0)
            chosen.append(in_g & (rank < 2))
    cls = jnp.zeros_like(sc[0])
    w_lo = jnp.zeros_like(sc[0])
    w_hi = jnp.zeros_like(sc[0])
    for g in range(N_GROUPS):
        for pid, (i, j) in enumerate(_PAIRS):
            lo, hi = g * epg + i, g * epg + j
            is_pair = chosen[lo] & chosen[hi]
            cls = jnp.where(is_pair, float(g * len(_PAIRS) + pid), cls)
            w_lo = jnp.where(is_pair, sc[lo], w_lo)
            w_hi = jnp.where(is_pair, sc[hi], w_hi)
    total = w_lo + w_hi
    return cls, w_lo / total, w_hi / total


def _merge_kernel(pre_ln, alpha, ya_ref, yb_ref, yc_ref, bg_ref, h_ref, g1_ref, sc2_ref, sh2_ref,
                  lig_ref, lib_ref, l1g_ref, l1b_ref, wa_ref, wb_ref, wc_ref, wo_ref, wr_ref, br_ref,
                  h1_ref, u2t_ref, route_ref, cnt_ref):
    tm = h_ref.shape[0]

    def branch(y_ref, w_ref, j):
        gate = bg_ref[:, j * 1024:(j + 1) * 1024].astype(F32)
        return gate * jnp.dot(y_ref[...], w_ref[...], preferred_element_type=F32)

    mix = branch(ya_ref, wa_ref, 0) + branch(yb_ref, wb_ref, 1) + branch(yc_ref, wc_ref, 2)
    y = jnp.dot(mix.astype(BF16), wo_ref[...], preferred_element_type=F32)
    h = h_ref[...]
    if pre_ln:
        h = _ln(h, lig_ref[...], lib_ref[...])
    h1 = _ln(alpha * h + g1_ref[...] * y, l1g_ref[...], l1b_ref[...])
    h1_ref[...] = h1
    u2 = h1 * (1.0 + sc2_ref[...]) + sh2_ref[...]
    logits_t = _dot_nt(wr_ref[...], u2, precision=HIGHEST)
    cls, w_lo, w_hi = _route(logits_t, br_ref[...])

    @pl.when(pl.program_id(0) == 0)
    def _():
        cnt_ref[...] = jnp.zeros_like(cnt_ref)

    crow = lax.broadcasted_iota(jnp.int32, (N_CLASS_ROWS, tm), 0).astype(F32)
    onehot = jnp.where(crow == cls, 1.0, 0.0)
    earlier = lax.broadcasted_iota(jnp.int32, (tm, tm), 0) <= lax.broadcasted_iota(jnp.int32, (tm, tm), 1)
    incl = jnp.dot(onehot.astype(BF16), jnp.where(earlier, 1.0, 0.0).astype(BF16), preferred_element_type=F32)
    base = cnt_ref[...]
    rank = jnp.sum(onehot * (incl - 1.0 + base), axis=0, keepdims=True)
    cnt_ref[...] = base + incl[:, tm - 1:tm]

    route_ref[...] = jnp.concatenate([cls, w_lo, w_hi, rank, jnp.zeros((4, tm), F32)], axis=0)
    for s in range(TOKEN_SUBROWS):
        u2t_ref[pl.ds(s, tm, stride=TOKEN_SUBROWS), :] = u2[:, s * LANES:(s + 1) * LANES]


def _merge(ya, yb, yc, bg, h, mod_l, ln_in, ln1, w_a, w_b, w_c, w_o, w_rt, b_r, seq_len, ctx_row, pre_ln, alpha):
    n = ya.shape[0]
    tm = MERGE_TILE
    row = lambda i: (i, 0)
    const = lambda i: (0, 0)
    vec = pl.BlockSpec((1, 1024), const)
    wspec = lambda w: pl.BlockSpec(w.shape, const)
    return pl.pallas_call(
        functools.partial(_merge_kernel, pre_ln, alpha),
        grid=(n // tm,),
        in_specs=[
            pl.BlockSpec((tm, D_CONV), row),
            pl.BlockSpec((tm, D_ATTN), row),
            pl.BlockSpec((tm, D_MLSTM), row),
            pl.BlockSpec((tm, N_BRANCHES * 1024), row),
            pl.BlockSpec((tm, 1024), row),
            _mod_spec(2, tm, seq_len, ctx_row),
            _mod_spec(4, tm, seq_len, ctx_row),
            _mod_spec(3, tm, seq_len, ctx_row),
            vec, vec, vec, vec,
            wspec(w_a), wspec(w_b), wspec(w_c), wspec(w_o), wspec(w_rt), wspec(b_r),
        ],
        out_specs=[
            pl.BlockSpec((tm, 1024), row),
            pl.BlockSpec((tm * TOKEN_SUBROWS, LANES), row),
            pl.BlockSpec((8, tm), lambda i: (0, i)),
        ],
        out_shape=[
            jax.ShapeDtypeStruct((n, 1024), F32),
            jax.ShapeDtypeStruct((n * TOKEN_SUBROWS, LANES), F32),
            jax.ShapeDtypeStruct((8, n), F32),
        ],
        scratch_shapes=[pltpu.VMEM((N_CLASS_ROWS, 1), F32)],
        compiler_params=_cparams("arbitrary"),
        name="merge",
    )(ya, yb, yc, bg, h, mod_l, mod_l, mod_l, ln_in[0], ln_in[1], ln1[0], ln1[1], w_a, w_b, w_c, w_o, w_rt, b_r)


class _TokenGather:
    def __init__(self, idx_ref, tile, src_hbm, buf, sem, slot, priorities=(0, 1)):
        self.idx_ref, self.src_hbm, self.buf, self.sem, self.slot = idx_ref, src_hbm, buf, sem, slot
        self.tokens = buf.shape[1] // TOKEN_SUBROWS
        self.base = tile * self.tokens
        self.priorities = priorities

    def _copy(self, k):
        sub = TOKEN_SUBROWS
        p = pl.multiple_of(self.idx_ref[self.base + k] * sub, sub)
        return pltpu.make_async_copy(self.src_hbm.at[pl.ds(p, sub)], self.buf.at[self.slot, pl.ds(k * sub, sub)],
                                     self.sem.at[self.slot])

    def start(self):
        for k in range(self.tokens):
            self._copy(k).start(priority=self.priorities[k % len(self.priorities)])

    def wait(self):
        for k in range(self.tokens):
            self._copy(k).wait()


def _untile_tokens(ref):
    tokens = ref.shape[0] // TOKEN_SUBROWS
    return jnp.concatenate([ref[pl.ds(s, tokens, stride=TOKEN_SUBROWS), :] for s in range(TOKEN_SUBROWS)], axis=1)


def _experts_kernel(lo_ref, hi_ref, valid_ref, src_ref, x_hbm, wr_ref, wgl_ref, wgh_ref, wdl_ref,
                    wdh_ref, o_ref, buf, sem):
    i = pl.program_id(0)
    last = pl.num_programs(0) - 1
    tm = o_ref.shape[0] // TOKEN_SUBROWS
    slot = i % 2
    nxt = jnp.minimum(i + 1, last)
    used = valid_ref[i] != 0
    prev_used = valid_ref[jnp.maximum(i - 1, 0)] != 0
    gather = functools.partial(_TokenGather, src_ref, src_hbm=x_hbm, buf=buf, sem=sem,
                               priorities=EXPERT_GATHER_PRIORITIES)

    @pl.when(jnp.logical_and(i == 0, used))
    def _():
        gather(tile=0, slot=0).start()

    @pl.when(jnp.logical_and(jnp.logical_and(i > 0, prev_used), jnp.logical_not(used)))
    def _():
        gather(tile=i, slot=slot).wait()

    @pl.when(used)
    def _():
        gather(tile=i, slot=slot).wait()
        gather(tile=nxt, slot=1 - slot).start()
        x32 = _untile_tokens(buf.at[slot])
        x = x32.astype(BF16)

        def affinity(e):
            logit = jnp.sum(x32 * wr_ref[pl.ds(e, 1), :], axis=1, keepdims=True)
            return _sigmoid(logit)

        def expert(wg_ref, wd_ref):
            gu = jnp.dot(x, wg_ref[...], preferred_element_type=F32)
            g_ = gu[:, 0:D_EXPERT]
            act = (g_ * _sigmoid(g_) * gu[:, D_EXPERT:2 * D_EXPERT]).astype(BF16)
            return jnp.dot(act, wd_ref[...], preferred_element_type=F32)

        s_lo, s_hi = affinity(lo_ref[i]), affinity(hi_ref[i])
        total = s_lo + s_hi
        out = (s_lo / total) * expert(wgl_ref, wdl_ref) + (s_hi / total) * expert(wgh_ref, wdh_ref)
        for s in range(TOKEN_SUBROWS):
            o_ref[pl.ds(s, tm, stride=TOKEN_SUBROWS), :] = out[:, s * LANES:(s + 1) * LANES]

    @pl.when(jnp.logical_and(used, i == last))
    def _():
        gather(tile=nxt, slot=1 - slot).wait()

    @pl.when(jnp.logical_not(used))
    def _():
        o_ref[...] = jnp.zeros_like(o_ref)


def _experts(lo, hi, valid, src, u2t, w_rt, w_gu, w_dn):
    tm = EXPERT_TILE
    n_tiles = src.shape[0] // tm
    row = lambda i, *_: (i, 0)
    grid_spec = pltpu.PrefetchScalarGridSpec(
        num_scalar_prefetch=4,
        grid=(n_tiles,),
        in_specs=[
            pl.BlockSpec(memory_space=pl.ANY),
            pl.BlockSpec(w_rt.shape, lambda i, *_: (0, 0)),
            pl.BlockSpec((None, 1024, 2 * D_EXPERT), lambda i, lo, hi, *_: (lo[i], 0, 0)),
            pl.BlockSpec((None, 1024, 2 * D_EXPERT), lambda i, lo, hi, *_: (hi[i], 0, 0)),
            pl.BlockSpec((None, D_EXPERT, 1024), lambda i, lo, hi, *_: (lo[i], 0, 0)),
            pl.BlockSpec((None, D_EXPERT, 1024), lambda i, lo, hi, *_: (hi[i], 0, 0)),
        ],
        out_specs=pl.BlockSpec((tm * TOKEN_SUBROWS, LANES), row),
        scratch_shapes=[pltpu.VMEM((2, tm * TOKEN_SUBROWS, LANES), F32), pltpu.SemaphoreType.DMA((2,))],
    )
    return pl.pallas_call(
        _experts_kernel,
        grid_spec=grid_spec,
        out_shape=jax.ShapeDtypeStruct((n_tiles * tm * TOKEN_SUBROWS, LANES), F32),
        compiler_params=_cparams("arbitrary"),
        name="moe_experts",
    )(lo, hi, valid, src, u2t, w_rt, w_gu, w_gu, w_dn, w_dn)


def _final_kernel(alpha, pos_ref, f_hbm, h1_ref, g2_ref, lg_ref, lb_ref, o_ref, buf, sem):
    i = pl.program_id(0)
    last = pl.num_programs(0) - 1
    slot = i % 2
    nxt = jnp.minimum(i + 1, last)
    gather = functools.partial(_TokenGather, pos_ref, src_hbm=f_hbm, buf=buf, sem=sem)

    @pl.when(i == 0)
    def _():
        gather(tile=0, slot=0).start()

    gather(tile=i, slot=slot).wait()
    gather(tile=nxt, slot=1 - slot).start()
    f = _untile_tokens(buf.at[slot])
    o_ref[...] = _ln(alpha * h1_ref[...] + g2_ref[...] * f, lg_ref[...], lb_ref[...])

    @pl.when(i == last)
    def _():
        gather(tile=nxt, slot=1 - slot).wait()


def _final_ln(h1, pos, fs, mod_l, ln2, seq_len, ctx_row, alpha):
    n = h1.shape[0]
    tm = FINAL_TILE
    row = lambda i, *_: (i, 0)
    const = lambda i, *_: (0, 0)
    grid_spec = pltpu.PrefetchScalarGridSpec(
        num_scalar_prefetch=1,
        grid=(n // tm,),
        in_specs=[
            pl.BlockSpec(memory_space=pl.ANY),
            pl.BlockSpec((tm, 1024), row),
            _mod_spec(5, tm, seq_len, ctx_row),
            pl.BlockSpec((1, 1024), const),
            pl.BlockSpec((1, 1024), const),
        ],
        out_specs=pl.BlockSpec((tm, 1024), row),
        scratch_shapes=[pltpu.VMEM((2, tm * TOKEN_SUBROWS, LANES), F32), pltpu.SemaphoreType.DMA((2,))],
    )
    return pl.pallas_call(
        functools.partial(_final_kernel, alpha),
        grid_spec=grid_spec,
        out_shape=jax.ShapeDtypeStruct((n, 1024), F32),
        compiler_params=_cparams("arbitrary"),
        name="final_ln",
    )(pos, fs, h1, mod_l, ln2[0], ln2[1])


def _sort_plan(route_t, n):
    tm = EXPERT_TILE
    n_tiles = n // tm + N_CLASSES
    cls = route_t[AUX_CLS].astype(jnp.int32)
    rank = route_t[AUX_RANK].astype(jnp.int32)
    onehot = cls[:, None] == jnp.arange(N_CLASSES, dtype=jnp.int32)[None, :]
    counts = jnp.sum(onehot, axis=0, dtype=jnp.int32)
    padded = (counts + tm - 1) // tm * tm
    ends = jnp.cumsum(padded)
    offs = ends - padded
    pos = jnp.sum(jnp.where(onehot, offs[None, :], 0), axis=1) + rank
    tile_ends = ends // tm
    j = jnp.arange(n_tiles, dtype=jnp.int32)
    n_used = tile_ends[-1]
    valid = j < n_used
    tile_cls = jnp.sum(j[:, None] >= tile_ends[None, :], axis=1)
    last_cls = jnp.sum((n_used - 1) >= tile_ends)
    tile_cls = jnp.where(valid, tile_cls, last_cls)
    group, pid = tile_cls // len(_PAIRS), tile_cls % len(_PAIRS)
    pair = jnp.asarray(np.array(_PAIRS, dtype=np.int32))
    lo = group * EXPERTS_PER_GROUP + pair[pid, 0]
    hi = group * EXPERTS_PER_GROUP + pair[pid, 1]
    order = jnp.argsort(cls, stable=True).astype(jnp.int32)
    starts = jnp.cumsum(counts) - counts
    within = (j * tm - offs[tile_cls])[:, None] + jnp.arange(tm, dtype=jnp.int32)[None, :]
    real = valid[:, None] & (within < counts[tile_cls][:, None])
    src = jnp.where(real, order[jnp.where(real, starts[tile_cls][:, None] + within, 0)], 0).reshape(-1)
    return pos.astype(jnp.int32), src.astype(jnp.int32), lo.astype(jnp.int32), hi.astype(jnp.int32), valid.astype(jnp.int32)


def _moe(u2t, route_t, h1, mod_l, ln2, w_rt, w_gu, w_dn, seq_len, ctx_row, alpha):
    n = h1.shape[0]
    pos, src, lo, hi, valid = _sort_plan(route_t, n)
    fs = _experts(lo, hi, valid, src, u2t, w_rt, w_gu, w_dn)
    return _final_ln(h1, pos, fs, mod_l, ln2, seq_len, ctx_row, alpha)


def _rope_swap_index(n_heads):
    idx = np.arange(n_heads * HEAD_DIM)
    within = idx % (HEAD_DIM // 2)
    quarter = HEAD_DIM // 4
    return np.where(within < quarter, idx + quarter, idx - quarter)


def _rope_tables(seq_len, n_heads, scale):
    t = np.arange(seq_len)
    quarter = HEAD_DIM // 4
    inv = ROPE_BASE ** (-np.arange(quarter, dtype=np.float32) / quarter)
    d = np.arange(HEAD_DIM)
    pos = np.where((d // (HEAD_DIM // 2) == 0)[None, :], (t // GRID_W)[:, None], (t % GRID_W)[:, None])
    ang = jnp.asarray(pos.astype(np.float32)) * jnp.asarray(inv[d % quarter])[None, :]
    sign = np.where(d % (HEAD_DIM // 2) < quarter, -1.0, 1.0).astype(np.float32)
    cos = jnp.cos(ang) * scale
    sin = jnp.sin(ang) * (sign * scale)[None, :]
    return jnp.tile(cos, (1, n_heads)), jnp.tile(sin, (1, n_heads))


def _flat_tables(seq_len, n_heads, scale):
    return (jnp.full((seq_len, n_heads * HEAD_DIM), scale, F32), jnp.zeros((seq_len, n_heads * HEAD_DIM), F32))


def _prep_in_weights(w_in_l):
    splits = np.cumsum([2 * D_CONV, D_ATTN, D_KV, D_KV, D_MLSTM, D_MLSTM, D_MLSTM, D_MLSTM, N_GATE_COLS])
    a, q, k, v, mq, mk, mv, mo, mg, bg = jnp.split(w_in_l, splits, axis=1)
    w_main = jnp.concatenate([a, q, q[:, _rope_swap_index(N_Q_HEADS)], k, k[:, _rope_swap_index(N_KV_HEADS)],
                              mq, mv, mo, bg], axis=1).astype(BF16)
    order = np.array([d * 8 + kind * 4 + h for h in range(N_MLSTM_HEADS) for d in range(2) for kind in range(2)])
    w_t = jnp.concatenate([mk.T, mg[:, order].T, v.T], axis=0).astype(BF16)
    return w_main, w_t


def kernel(x, c, ctx, c_ctx, ln_in_g, ln_in_b, w_router, b_router, w_mod, b_mod, w_in, conv_w, conv_b, conv_ln_g,
           conv_ln_b, w_a_out, attn_sink, w_b_out, mlstm_gate_b, mlstm_norm_g, w_c_out, w_out, ln1_g, ln1_b,
           moe_w_gu, moe_w_dn, ln2_g, ln2_b):
    batch, seq_len, d = x.shape
    ctx_len = ctx.shape[1]
    depth = w_in.shape[0]
    alpha = (2.0 * depth) ** 0.25
    ctx_row = batch
    assert d == 1024 and batch < MOD_ROWS
    assert seq_len % MERGE_TILE == 0 and ctx_len % TOKEN_TILE == 0 and (batch * ctx_len) % MERGE_TILE == 0
    assert seq_len % FINAL_TILE == 0 and (batch * ctx_len) % FINAL_TILE == 0 and FINAL_TILE % EXPERT_TILE == 0

    cc = jnp.zeros((MOD_ROWS, d), F32).at[0:batch].set(c).at[batch].set(c_ctx)
    mod = _modulation(cc, w_mod, b_mod).reshape(depth, MOD_ROWS * N_MOD, 1, d)

    attn_scale = HEAD_DIM ** -0.5
    rope_lat = _rope_tables(seq_len, N_Q_HEADS, attn_scale) + _rope_tables(seq_len, N_KV_HEADS, 1.0)
    rope_ctx = _flat_tables(ctx_len, N_Q_HEADS, attn_scale) + _flat_tables(ctx_len, N_KV_HEADS, 1.0)

    vec = lambda t: t.reshape(1, -1)
    ln_in = (vec(ln_in_g), vec(ln_in_b))
    w_rt = w_router.T
    b_r = b_router.reshape(N_EXPERTS, 1)

    h = x.reshape(batch * seq_len, d)
    hc = ctx.reshape(batch * ctx_len, d)
    for l in range(depth):
        need_ctx = l < depth - 1
        pre_ln = l == 0
        mod_l = mod[l]
        w_main, w_t = _prep_in_weights(w_in[l])
        lat = _in_proj(h, ln_in[0], ln_in[1], mod_l, w_main, w_t, rope_lat, seq_len, None, pre_ln)
        cx = _in_proj(hc, ln_in[0], ln_in[1], mod_l, w_main, w_t, rope_ctx, ctx_len, ctx_row, pre_ln)
        a_l, q_l, k_l, v_l, mq_l, mv_l, mo_l, bg_l, kt_l, gt_l = lat
        a_c, q_c, k_c, v_c, mq_c, mv_c, mo_c, bg_c, kt_c, gt_c = cx

        conv_args = (conv_w[l], vec(conv_b[l]), vec(conv_ln_g[l]), vec(conv_ln_b[l]))
        gate_b = jnp.transpose(mlstm_gate_b[l], (2, 0, 1)).reshape(N_MLSTM_HEADS, 4, 1)
        ya = _conv_branch(a_l, *conv_args, seq_len)
        yb = _attn_latent(q_l, k_l, v_l, k_c, v_c, attn_sink[l], seq_len, ctx_len)
        yc, yc_c = _mlstm_branch((mq_c, kt_c, mv_c, mo_c, gt_c), (mq_l, kt_l, mv_l, mo_l, gt_l),
                                 gate_b, vec(mlstm_norm_g[l]), seq_len, ctx_len, need_ctx)

        ln1 = (vec(ln1_g[l]), vec(ln1_b[l]))
        ln2 = (vec(ln2_g[l]), vec(ln2_b[l]))
        w_a, w_b, w_c, w_o = (w.astype(BF16) for w in (w_a_out[l], w_b_out[l], w_c_out[l], w_out[l]))
        w_gu = moe_w_gu[l].astype(BF16)
        w_dn = moe_w_dn[l].astype(BF16)

        h1, pay, route_t = _merge(ya, yb, yc, bg_l, h, mod_l, ln_in, ln1, w_a, w_b, w_c, w_o, w_rt, b_r,
                                  seq_len, None, pre_ln, alpha)
        h = _moe(pay, route_t, h1, mod_l, ln2, w_rt, w_gu, w_dn, seq_len, None, alpha)
        if need_ctx:
            ya_c = _conv_branch(a_c, *conv_args, ctx_len)
            yb_c = _attn_context(q_c, k_c, v_c, attn_sink[l], ctx_len)
            h1c, pay_c, route_tc = _merge(ya_c, yb_c, yc_c, bg_c, hc, mod_l, ln_in, ln1, w_a, w_b, w_c, w_o, w_rt,
                                          b_r, ctx_len, ctx_row, pre_ln, alpha)
            hc = _moe(pay_c, route_tc, h1c, mod_l, ln2, w_rt, w_gu, w_dn, ctx_len, ctx_row, alpha)
    return h.reshape(batch, seq_len, d)
```

```python
import functools

import numpy as np
import jax
import jax.numpy as jnp
from jax import lax
from jax.experimental import pallas as pl
from jax.experimental.pallas import tpu as pltpu

GRID_W = 64
LN_EPS = 1e-5
D_CONV = 512
CONV_WIDTH = 31
N_Q_HEADS = 8
N_KV_HEADS = 2
HEAD_DIM = 64
WINDOW = 128
BLOCK = 128
ROPE_BASE = 10000.0
D_ATTN = N_Q_HEADS * HEAD_DIM
D_KV = N_KV_HEADS * HEAD_DIM
N_MLSTM_HEADS = 4
MLSTM_HEAD_DIM = 128
D_MLSTM = N_MLSTM_HEADS * MLSTM_HEAD_DIM
N_GATE_COLS = 2 * 2 * N_MLSTM_HEADS
N_BRANCHES = 3
N_EXPERTS = 16
N_GROUPS = 4
EXPERTS_PER_GROUP = N_EXPERTS // N_GROUPS
D_EXPERT = 512
N_MOD = 6

LANES = 128
V7X_VMEM_LIMIT_BYTES = 56 * 1024 * 1024

MOD_ROWS = 16
MOD_COL_BLOCK = 512
TOKEN_TILE = 256
EXPERT_TILE = 256
EXPERT_GATHER_PRIORITIES = (1,)
FINAL_TILE = 512
MLSTM_CHUNK = 128
MLSTM_HEADS_PER_STEP = 4
MERGE_TILE = 512
CONV_ROWS = 64
CONV_PAD = 16

_PAIRS = [(i, j) for i in range(EXPERTS_PER_GROUP) for j in range(i + 1, EXPERTS_PER_GROUP)]
N_CLASSES = N_GROUPS * len(_PAIRS)
N_CLASS_ROWS = 32
AUX_CLS, AUX_W_LO, AUX_W_HI, AUX_RANK = 0, 1, 2, 3
TOKEN_SUBROWS = 1024 // LANES

F32 = jnp.float32
BF16 = jnp.bfloat16
HIGHEST = lax.Precision.HIGHEST
NEG_INF = float("-inf")

_C_A = 0
_C_Q = _C_A + 2 * D_CONV
_C_QS = _C_Q + D_ATTN
_C_K = _C_QS + D_ATTN
_C_KS = _C_K + D_KV
_C_MQ = _C_KS + D_KV
_C_MV = _C_MQ + D_MLSTM
_C_MO = _C_MV + D_MLSTM
_C_BG = _C_MO + D_MLSTM
_C_END = _C_BG + N_BRANCHES * 1024
_R_KT = 0
_R_GT = _R_KT + D_MLSTM
_R_VT = _R_GT + N_GATE_COLS


def _cparams(*sem):
    return pltpu.CompilerParams(dimension_semantics=sem, vmem_limit_bytes=V7X_VMEM_LIMIT_BYTES)


def _ln(x, g, b):
    mu = jnp.mean(x, axis=-1, keepdims=True)
    xc = x - mu
    var = jnp.mean(xc * xc, axis=-1, keepdims=True)
    return xc * lax.rsqrt(var + LN_EPS) * g + b


def _sigmoid(x):
    return 0.5 * jnp.tanh(0.5 * x) + 0.5


def _log_sigmoid(x):
    return jnp.minimum(x, 0.0) - jnp.log(1.0 + jnp.exp(-jnp.abs(x)))


def _dot_nt(a, b, precision=None):
    return lax.dot_general(a, b, (((1,), (1,)), ((), ())), preferred_element_type=F32, precision=precision)


def _mod_kernel(c_ref, w_ref, b_ref, o_ref):
    c = c_ref[...]
    s = c * _sigmoid(c)
    o_ref[...] = jnp.dot(s, w_ref[...], preferred_element_type=F32, precision=HIGHEST) + b_ref[...]


def _modulation(cc, w_mod, b_mod):
    depth, d, n = w_mod.shape
    return pl.pallas_call(
        _mod_kernel,
        grid=(depth, n // MOD_COL_BLOCK),
        in_specs=[
            pl.BlockSpec((MOD_ROWS, d), lambda l, j: (0, 0)),
            pl.BlockSpec((None, d, MOD_COL_BLOCK), lambda l, j: (l, 0, j)),
            pl.BlockSpec((None, 1, MOD_COL_BLOCK), lambda l, j: (l, 0, j)),
        ],
        out_specs=pl.BlockSpec((None, MOD_ROWS, MOD_COL_BLOCK), lambda l, j: (l, 0, j)),
        out_shape=jax.ShapeDtypeStruct((depth, MOD_ROWS, n), F32),
        compiler_params=_cparams("parallel", "parallel"),
        name="modulation",
    )(cc, w_mod, b_mod.reshape(depth, 1, n))


def _mod_spec(which, tile, seq_len, ctx_row):
    tiles_per_seq = seq_len // tile
    if ctx_row is None:
        return pl.BlockSpec((None, 1, 1024), lambda i, *_: ((i // tiles_per_seq) * N_MOD + which, 0, 0))
    return pl.BlockSpec((None, 1, 1024), lambda i, *_: (ctx_row * N_MOD + which, 0, 0))


def _in_kernel(pre_ln, x_ref, lg_ref, lb_ref, sc_ref, sh_ref, w_ref, wt_ref, cq_ref, sq_ref, ck_ref, sk_ref,
               a_ref, q_ref, k_ref, v_ref, mq_ref, mv_ref, mo_ref, bg_ref, kt_ref, gt_ref):
    x = x_ref[...]
    if pre_ln:
        x = _ln(x, lg_ref[...], lb_ref[...])
    u = (x * (1.0 + sc_ref[...]) + sh_ref[...]).astype(BF16)

    def seg(lo, hi):
        return jnp.dot(u, w_ref[:, lo:hi], preferred_element_type=F32)

    a_ref[...] = seg(_C_A, _C_Q).astype(BF16)
    q_ref[...] = (seg(_C_Q, _C_QS) * cq_ref[...] + seg(_C_QS, _C_K) * sq_ref[...]).astype(BF16)
    k_ref[...] = (seg(_C_K, _C_KS) * ck_ref[...] + seg(_C_KS, _C_MQ) * sk_ref[...]).astype(BF16)
    v_ref[...] = _dot_nt(wt_ref[_R_VT:_R_VT + D_KV, :], u).astype(BF16)
    mq_ref[...] = seg(_C_MQ, _C_MV).astype(BF16)
    mv_ref[...] = seg(_C_MV, _C_MO).astype(BF16)
    mo_ref[...] = seg(_C_MO, _C_BG).astype(BF16)
    for j in range(N_BRANCHES):
        bg_ref[:, j * 1024:(j + 1) * 1024] = _sigmoid(seg(_C_BG + j * 1024, _C_BG + (j + 1) * 1024)).astype(BF16)
    n_chunks = u.shape[0] // MLSTM_CHUNK
    kt = _dot_nt(wt_ref[_R_KT:_R_KT + D_MLSTM, :], u)
    for h in range(N_MLSTM_HEADS):
        for c in range(n_chunks):
            kt_ref[h, c] = kt[h * MLSTM_HEAD_DIM:(h + 1) * MLSTM_HEAD_DIM,
                              c * MLSTM_CHUNK:(c + 1) * MLSTM_CHUNK].astype(BF16)
    gt = _dot_nt(wt_ref[_R_GT:_R_GT + N_GATE_COLS, :], u)
    for h in range(N_MLSTM_HEADS):
        for c in range(n_chunks):
            gt_ref[h, c] = gt[h * 4:(h + 1) * 4, c * MLSTM_CHUNK:(c + 1) * MLSTM_CHUNK]


def _in_proj(x, ln_g, ln_b, mod_l, w_main, w_t, rope, seq_len, ctx_row, pre_ln):
    n = x.shape[0]
    tm = TOKEN_TILE
    cq, sq, ck, sk = rope
    tps = seq_len // tm
    nch = n // MLSTM_CHUNK
    row = lambda i: (i, 0)
    pos = lambda i: (i % tps, 0)
    const = lambda i: (0, 0)
    out_shape = [
        jax.ShapeDtypeStruct((n, 2 * D_CONV), BF16),
        jax.ShapeDtypeStruct((n, D_ATTN), BF16),
        jax.ShapeDtypeStruct((n, D_KV), BF16),
        jax.ShapeDtypeStruct((D_KV, n), BF16),
        jax.ShapeDtypeStruct((n, D_MLSTM), BF16),
        jax.ShapeDtypeStruct((n, D_MLSTM), BF16),
        jax.ShapeDtypeStruct((n, D_MLSTM), BF16),
        jax.ShapeDtypeStruct((n, N_BRANCHES * 1024), BF16),
        jax.ShapeDtypeStruct((N_MLSTM_HEADS, nch, MLSTM_HEAD_DIM, MLSTM_CHUNK), BF16),
        jax.ShapeDtypeStruct((N_MLSTM_HEADS, nch, 4, MLSTM_CHUNK), F32),
    ]
    cpt = tm // MLSTM_CHUNK
    out_specs = [
        pl.BlockSpec((tm, 2 * D_CONV), row),
        pl.BlockSpec((tm, D_ATTN), row),
        pl.BlockSpec((tm, D_KV), row),
        pl.BlockSpec((D_KV, tm), lambda i: (0, i)),
        pl.BlockSpec((tm, D_MLSTM), row),
        pl.BlockSpec((tm, D_MLSTM), row),
        pl.BlockSpec((tm, D_MLSTM), row),
        pl.BlockSpec((tm, N_BRANCHES * 1024), row),
        pl.BlockSpec((N_MLSTM_HEADS, cpt, MLSTM_HEAD_DIM, MLSTM_CHUNK), lambda i: (0, i, 0, 0)),
        pl.BlockSpec((N_MLSTM_HEADS, cpt, 4, MLSTM_CHUNK), lambda i: (0, i, 0, 0)),
    ]
    in_specs = [
        pl.BlockSpec((tm, 1024), row),
        pl.BlockSpec((1, 1024), const),
        pl.BlockSpec((1, 1024), const),
        _mod_spec(1, tm, seq_len, ctx_row),
        _mod_spec(0, tm, seq_len, ctx_row),
        pl.BlockSpec(w_main.shape, const, pipeline_mode=pl.Buffered(1)),
        pl.BlockSpec(w_t.shape, const, pipeline_mode=pl.Buffered(1)),
        pl.BlockSpec((tm, D_ATTN), pos),
        pl.BlockSpec((tm, D_ATTN), pos),
        pl.BlockSpec((tm, D_KV), pos),
        pl.BlockSpec((tm, D_KV), pos),
    ]
    return pl.pallas_call(
        functools.partial(_in_kernel, pre_ln),
        grid=(n // tm,),
        in_specs=in_specs,
        out_specs=out_specs,
        out_shape=out_shape,
        compiler_params=_cparams("parallel"),
        name="in_proj",
    )(x, ln_g, ln_b, mod_l, mod_l, w_main, w_t, cq, sq, ck, sk)


def _conv_kernel(a_ref, w_ref, cb_ref, g_ref, b_ref, o_ref, upad_ref):
    t = a_ref.shape[0]
    zeros = jnp.zeros((CONV_PAD, D_CONV), F32)
    upad_ref[0:CONV_PAD, :] = zeros
    upad_ref[CONV_PAD + t:2 * CONV_PAD + t, :] = zeros
    val = a_ref[:, 0:D_CONV].astype(F32)
    gate = a_ref[:, D_CONV:2 * D_CONV].astype(F32)
    upad_ref[CONV_PAD:CONV_PAD + t, :] = val * _sigmoid(gate)
    half = CONV_WIDTH // 2

    def body(c, carry):
        r0 = pl.multiple_of(c * CONV_ROWS, CONV_ROWS)
        n_win = CONV_ROWS + 2 * CONV_PAD
        win = upad_ref[pl.ds(r0, n_win), :]
        acc = jnp.zeros((CONV_ROWS, D_CONV), F32) + cb_ref[...]
        for res in range(8):
            rolled = win if res == 0 else pltpu.roll(win, shift=n_win - res, axis=0)
            for k in range(CONV_WIDTH):
                off = CONV_PAD - half + k
                if off % 8 == res:
                    acc = acc + rolled[off - res:off - res + CONV_ROWS, :] * w_ref[k:k + 1, :]
        y = _ln(acc, g_ref[...], b_ref[...])
        o_ref[pl.ds(r0, CONV_ROWS), :] = (y * _sigmoid(y)).astype(BF16)
        return carry

    lax.fori_loop(0, t // CONV_ROWS, body, 0)


def _conv_branch(a_in, conv_w, conv_b, ln_g, ln_b, seq_len):
    n = a_in.shape[0]
    const = lambda b: (0, 0)
    return pl.pallas_call(
        _conv_kernel,
        grid=(n // seq_len,),
        in_specs=[
            pl.BlockSpec((seq_len, 2 * D_CONV), lambda b: (b, 0)),
            pl.BlockSpec((CONV_WIDTH, D_CONV), const),
            pl.BlockSpec((1, D_CONV), const),
            pl.BlockSpec((1, D_CONV), const),
            pl.BlockSpec((1, D_CONV), const),
        ],
        out_specs=pl.BlockSpec((seq_len, D_CONV), lambda b: (b, 0)),
        out_shape=jax.ShapeDtypeStruct((n, D_CONV), BF16),
        scratch_shapes=[pltpu.VMEM((seq_len + 2 * CONV_PAD, D_CONV), F32)],
        compiler_params=_cparams("parallel"),
        name="conv_branch",
    )(a_in, conv_w, conv_b, ln_g, ln_b)


def _attn_heads(q, keys, vals_t, masks, sink_ref, o_ref):
    rows = q.shape[0]
    group = N_Q_HEADS // N_KV_HEADS
    for hk in range(N_KV_HEADS):
        lo = hk * HEAD_DIM
        qs = jnp.concatenate([q[:, (hk * group + g) * HEAD_DIM:(hk * group + g + 1) * HEAD_DIM]
                              for g in range(group)], axis=0)
        sink = jnp.concatenate([jnp.full((1, rows), sink_ref[hk * group + g], F32) for g in range(group)], axis=1)
        scores = []
        m = sink
        for kk, mask in zip(keys, masks):
            s = _dot_nt(kk[:, lo:lo + HEAD_DIM], qs)
            if mask is not None:
                s = jnp.where(mask, s, NEG_INF)
            scores.append(s)
            m = jnp.maximum(m, jnp.max(s, axis=0, keepdims=True))
        acc = jnp.zeros((2 * HEAD_DIM, rows * group), F32)
        for s, vt in zip(scores, vals_t):
            n_k = s.shape[0]
            p = jnp.exp(s - m).astype(BF16)
            ones_rows = jnp.where(lax.broadcasted_iota(jnp.int32, (HEAD_DIM, n_k), 0) == 0, 1.0, 0.0).astype(BF16)
            v_aug = jnp.concatenate([vt[lo:lo + HEAD_DIM, :], ones_rows], axis=0)
            acc = acc + jnp.dot(v_aug, p, preferred_element_type=F32)
        denom = acc[HEAD_DIM:HEAD_DIM + 1, :] + jnp.exp(sink - m)
        o_t = acc * (1.0 / denom)
        for g in range(group):
            col = (hk * group + g) * HEAD_DIM
            o_ref[:, col:col + HEAD_DIM] = o_t[:, g * rows:(g + 1) * rows].T[:, 0:HEAD_DIM].astype(BF16)


def _attn_lat_kernel(sink_ref, q_ref, kp_ref, k0_ref, kn_ref, vp_ref, v0_ref, vn_ref, kc_ref, vc_ref, o_ref):
    n = pl.program_id(1)
    nb = pl.num_programs(1)
    stacked = (N_Q_HEADS // N_KV_HEADS) * BLOCK
    ki = lax.broadcasted_iota(jnp.int32, (BLOCK, stacked), 0)
    qi = lax.broadcasted_iota(jnp.int32, (BLOCK, stacked), 1) % BLOCK
    mask_prev = ki >= qi + jnp.where(n > 0, 0, BLOCK)
    mask_next = ki <= qi - jnp.where(n < nb - 1, 0, BLOCK)
    _attn_heads(q_ref[...],
                [kp_ref[...], k0_ref[...], kn_ref[...], kc_ref[...]],
                [vp_ref[...], v0_ref[...], vn_ref[...], vc_ref[...]],
                [mask_prev, None, mask_next, None], sink_ref, o_ref)


def _attn_ctx_kernel(sink_ref, q_ref, kc_ref, vc_ref, o_ref):
    _attn_heads(q_ref[...], [kc_ref[...]], [vc_ref[...]], [None], sink_ref, o_ref)


def _attn_latent(q, k, vt, kc, vct, sink, seq_len, ctx_len):
    n = q.shape[0]
    nb = seq_len // BLOCK
    batch = n // seq_len
    blk_prev = lambda b, j: b * nb + jnp.maximum(j - 1, 0)
    blk_next = lambda b, j: b * nb + jnp.minimum(j + 1, nb - 1)
    cur = lambda b, j: (b * nb + j, 0)
    kspec = lambda blk: pl.BlockSpec((BLOCK, D_KV), lambda b, j: (blk(b, j), 0))
    vspec = lambda blk: pl.BlockSpec((D_KV, BLOCK), lambda b, j: (0, blk(b, j)))
    blk_cur = lambda b, j: b * nb + j
    return pl.pallas_call(
        _attn_lat_kernel,
        grid=(batch, nb),
        in_specs=[
            pl.BlockSpec(memory_space=pltpu.SMEM),
            pl.BlockSpec((BLOCK, D_ATTN), cur),
            kspec(blk_prev), kspec(blk_cur), kspec(blk_next), vspec(blk_prev), vspec(blk_cur), vspec(blk_next),
            pl.BlockSpec((ctx_len, D_KV), lambda b, j: (b, 0)),
            pl.BlockSpec((D_KV, ctx_len), lambda b, j: (0, b)),
        ],
        out_specs=pl.BlockSpec((BLOCK, D_ATTN), cur),
        out_shape=jax.ShapeDtypeStruct((n, D_ATTN), BF16),
        compiler_params=_cparams("parallel", "parallel"),
        name="attn_latent",
    )(sink, q, k, k, k, vt, vt, vt, kc, vct)


def _attn_context(qc, kc, vct, sink, ctx_len):
    n = qc.shape[0]
    blk = lambda b: (b, 0)
    return pl.pallas_call(
        _attn_ctx_kernel,
        grid=(n // ctx_len,),
        in_specs=[
            pl.BlockSpec(memory_space=pltpu.SMEM),
            pl.BlockSpec((ctx_len, D_ATTN), blk),
            pl.BlockSpec((ctx_len, D_KV), blk),
            pl.BlockSpec((D_KV, ctx_len), lambda b: (0, b)),
        ],
        out_specs=pl.BlockSpec((ctx_len, D_ATTN), blk),
        out_shape=jax.ShapeDtypeStruct((n, D_ATTN), BF16),
        compiler_params=_cparams("parallel"),
        name="attn_context",
    )(sink, qc, kc, vct)


def _mlstm_kernel(ctx_out, nch_c, nch_l, hps,
                  qc_ref, ktc_ref, vc_ref, moc_ref, gc_ref,
                  ql_ref, ktl_ref, vl_ref, mol_ref, gl_ref,
                  gb_ref, ng_ref, *rest):
    if ctx_out:
        yl_ref, yc_ref, af_ref, lff_ref, ab_ref, lfb_ref, cf_ref, cb_ref, hf_ref, hb_ref = rest
    else:
        yl_ref, af_ref, lff_ref, ab_ref, lfb_ref, cf_ref, cb_ref, hf_ref, hb_ref = rest
        yc_ref = None
    lc = MLSTM_CHUNK
    dh = MLSTM_HEAD_DIM
    k_scale = MLSTM_HEAD_DIM ** -0.5
    ti = lax.broadcasted_iota(jnp.int32, (lc, lc), 0)
    si = lax.broadcasted_iota(jnp.int32, (lc, lc), 1)
    lower = si <= ti
    upper = si >= ti
    pre_mat = jnp.where(upper, 1.0, 0.0)
    suf_mat = jnp.where(lower, 1.0, 0.0)
    n_rows = nch_c + nch_l
    pad_rows = -n_rows % 8

    for hh in range(hps):
        gb = gb_ref[hh]

        def gate_rows(kind):
            rows = ([gc_ref[hh, c, kind:kind + 1, :] for c in range(nch_c)]
                    + [gl_ref[hh, c, kind:kind + 1, :] for c in range(nch_l)])
            rows = jnp.concatenate(rows, axis=0) + gb[kind:kind + 1, :]
            if pad_rows:
                rows = jnp.concatenate([rows, jnp.zeros((pad_rows, lc), F32)], axis=0)
            return rows

        lf_f = _log_sigmoid(gate_rows(1))
        lf_b = _log_sigmoid(gate_rows(3))
        a_f = gate_rows(0) - jnp.dot(lf_f, pre_mat, preferred_element_type=F32, precision=HIGHEST)
        a_b = gate_rows(2) - jnp.dot(lf_b, suf_mat, preferred_element_type=F32, precision=HIGHEST)
        for c in range(n_rows):
            af_ref[hh * n_rows + c] = a_f[c:c + 1, :]
            lff_ref[hh * n_rows + c] = lf_f[c:c + 1, :]
            ab_ref[hh * n_rows + c] = a_b[c:c + 1, :]
            lfb_ref[hh * n_rows + c] = lf_b[c:c + 1, :]

    cf_ref[...] = jnp.zeros_like(cf_ref)
    cb_ref[...] = jnp.zeros_like(cb_ref)
    ones_col = jnp.where(lax.broadcasted_iota(jnp.int32, (lc, dh), 1) == 0, 1.0, 0.0).astype(BF16)

    def chunk(q, kt, v, a_row, lf_row, c_ref, hh, m, mask):
        a_mat = jnp.where(mask, jnp.broadcast_to(a_row, (lc, lc)), NEG_INF)
        cm = jnp.max(a_mat, axis=1, keepdims=True)
        b_col = jnp.sum(jnp.where(mask, jnp.broadcast_to(lf_row, (lc, lc)), 0.0), axis=1, keepdims=True)
        mx = jnp.maximum(m, cm)
        mx_last = jnp.maximum(m, jnp.max(a_row, axis=1, keepdims=True))
        w = jnp.exp(a_mat - mx)
        s = jnp.dot(q, kt, preferred_element_type=F32) * k_scale
        p = (s * w).astype(BF16)
        w_s = jnp.exp(a_row - mx_last) * k_scale
        ktw = (kt.astype(F32) * w_s).astype(BF16)
        v_aug = jnp.concatenate([v, ones_col], axis=1)
        both = jnp.dot(jnp.concatenate([p, ktw], axis=0), v_aug, preferred_element_type=F32)
        c_old = c_ref[hh]
        inter = jnp.dot(q, c_old.astype(BF16), preferred_element_type=F32)
        tot = both[0:lc, :] + jnp.exp(m - mx) * inter
        den = tot[:, dh:dh + 1]
        h = tot[:, 0:dh] / jnp.maximum(jnp.abs(den), jnp.exp(-(b_col + mx)))
        c_ref[hh] = jnp.exp(m - mx_last) * c_old + both[lc:lc + dh, :]
        m_new = jnp.sum(lf_row, axis=1, keepdims=True) + mx_last
        return h, m_new

    def step(q_ref, kt_ref, v_ref, row0, c_f, c_b, rf, rb, ms):
        out = []
        for hh in range(hps):
            cols = slice(hh * dh, (hh + 1) * dh)
            base = hh * n_rows + row0
            h_f, m_f = chunk(q_ref[pl.ds(rf, lc), cols], kt_ref[hh, c_f], v_ref[pl.ds(rf, lc), cols],
                             af_ref[base + c_f], lff_ref[base + c_f], cf_ref, hh, ms[2 * hh], lower)
            h_b, m_b = chunk(q_ref[pl.ds(rb, lc), cols], kt_ref[hh, c_b], v_ref[pl.ds(rb, lc), cols],
                             ab_ref[base + c_b], lfb_ref[base + c_b], cb_ref, hh, ms[2 * hh + 1], upper)
            hf_ref[pl.ds(rf, lc), cols] = h_f
            hb_ref[pl.ds(rb, lc), cols] = h_b
            out += [m_f, m_b]
        return tuple(out)

    def finish(n_tok, mo_ref, o_ref):
        for hh in range(hps):
            cols = slice(hh * dh, (hh + 1) * dh)
            h = hf_ref[0:n_tok, cols] + hb_ref[0:n_tok, cols]
            mu = jnp.mean(h, axis=-1, keepdims=True)
            hc = h - mu
            var = jnp.mean(hc * hc, axis=-1, keepdims=True)
            y = hc * lax.rsqrt(var + LN_EPS) * ng_ref[:, cols]
            o_ref[:, cols] = (_sigmoid(mo_ref[:, cols].astype(F32)) * y).astype(BF16)

    ms = tuple(jnp.zeros((1, 1), F32) for _ in range(2 * hps))
    for c in range(nch_c):
        c_b = nch_c - 1 - c
        ms = step(qc_ref, ktc_ref, vc_ref, 0, c, c_b, c * lc, c_b * lc, ms)
    if ctx_out:
        finish(nch_c * lc, moc_ref, yc_ref)

    def body(c, ms):
        c_b = nch_l - 1 - c
        return step(ql_ref, ktl_ref, vl_ref, nch_c, c, c_b,
                    pl.multiple_of(c * lc, lc), pl.multiple_of(c_b * lc, lc), ms)

    lax.fori_loop(0, nch_l, body, ms)
    finish(nch_l * lc, mol_ref, yl_ref)


def _mlstm_branch(ctx_p, lat_p, gate_b, norm_g, seq_len, ctx_len, ctx_out):
    mq_c, kt_c, mv_c, mo_c, g_c = ctx_p
    mq_l, kt_l, mv_l, mo_l, g_l = lat_p
    n_l, n_c = mq_l.shape[0], mq_c.shape[0]
    batch = n_l // seq_len
    lc, dh, hps = MLSTM_CHUNK, MLSTM_HEAD_DIM, MLSTM_HEADS_PER_STEP
    nch_c, nch_l = ctx_len // lc, seq_len // lc

    def stream(t, nch):
        tok = pl.BlockSpec((t, hps * dh), lambda b, h: (b, h))
        return [tok,
                pl.BlockSpec((hps, nch, dh, lc), lambda b, h: (h, b, 0, 0)),
                tok, tok,
                pl.BlockSpec((hps, nch, 4, lc), lambda b, h: (h, b, 0, 0))]

    in_specs = stream(ctx_len, nch_c) + stream(seq_len, nch_l) + [
        pl.BlockSpec((hps, 4, 1), lambda b, h: (h, 0, 0)),
        pl.BlockSpec((1, hps * dh), lambda b, h: (0, h)),
    ]
    out_specs = [pl.BlockSpec((seq_len, hps * dh), lambda b, h: (b, h))]
    out_shape = [jax.ShapeDtypeStruct((n_l, D_MLSTM), BF16)]
    if ctx_out:
        out_specs.append(pl.BlockSpec((ctx_len, hps * dh), lambda b, h: (b, h)))
        out_shape.append(jax.ShapeDtypeStruct((n_c, D_MLSTM), BF16))
    row_scratch = pltpu.VMEM((hps * (nch_c + nch_l), 1, lc), F32)
    outs = pl.pallas_call(
        functools.partial(_mlstm_kernel, ctx_out, nch_c, nch_l, hps),
        grid=(batch, N_MLSTM_HEADS // hps),
        in_specs=in_specs,
        out_specs=out_specs,
        out_shape=out_shape,
        scratch_shapes=[row_scratch, row_scratch, row_scratch, row_scratch,
                        pltpu.VMEM((hps, dh, 2 * dh), F32), pltpu.VMEM((hps, dh, 2 * dh), F32),
                        pltpu.VMEM((seq_len, hps * dh), F32), pltpu.VMEM((seq_len, hps * dh), F32)],
        compiler_params=_cparams("parallel", "parallel"),
        name="mlstm_branch",
    )(mq_c, kt_c, mv_c, mo_c, g_c, mq_l, kt_l, mv_l, mo_l, g_l, gate_b, norm_g)
    return outs if ctx_out else (outs[0], None)


def _route(logits_t, br):
    sc = [_sigmoid(logits_t[e:e + 1, :]) for e in range(N_EXPERTS)]
    sel = [sc[e] + br[e:e + 1, :] for e in range(N_EXPERTS)]
    epg = EXPERTS_PER_GROUP
    group_score = []
    for g in range(N_GROUPS):
        v = sel[g * epg:(g + 1) * epg]
        best = None
        for i in range(epg):
            for j in range(i + 1, epg):
                pair = v[i] + v[j]
                best = pair if best is None else jnp.maximum(best, pair)
        group_score.append(best)
    g_idx = jnp.zeros_like(group_score[0], dtype=jnp.int32)
    best = group_score[0]
    for g in range(1, N_GROUPS):
        better = group_score[g] > best
        g_idx = jnp.where(better, g, g_idx)
        best = jnp.maximum(best, group_score[g])
    chosen = []
    for g in range(N_GROUPS):
        v = sel[g * epg:(g + 1) * epg]
        in_g = g_idx == g
        for i in range(epg):
            rank = jnp.zeros_like(g_idx)
            for j in range(epg):
                if j == i:
                    continue
                ahead = (v[j] >= v[i]) if j < i else (v[j] > v[i])
                rank = rank + jnp.where(ahead, 1, 0)
            chosen.append(in_g & (rank < 2))
    cls = jnp.zeros_like(sc[0])
    w_lo = jnp.zeros_like(sc[0])
    w_hi = jnp.zeros_like(sc[0])
    for g in range(N_GROUPS):
        for pid, (i, j) in enumerate(_PAIRS):
            lo, hi = g * epg + i, g * epg + j
            is_pair = chosen[lo] & chosen[hi]
            cls = jnp.where(is_pair, float(g * len(_PAIRS) + pid), cls)
            w_lo = jnp.where(is_pair, sc[lo], w_lo)
            w_hi = jnp.where(is_pair, sc[hi], w_hi)
    total = w_lo + w_hi
    return cls, w_lo / total, w_hi / total


def _merge_kernel(pre_ln, alpha, ya_ref, yb_ref, yc_ref, bg_ref, h_ref, g1_ref, sc2_ref, sh2_ref,
                  lig_ref, lib_ref, l1g_ref, l1b_ref, wa_ref, wb_ref, wc_ref, wo_ref, wr_ref, br_ref,
                  h1_ref, u2t_ref, route_ref, cnt_ref):
    tm = h_ref.shape[0]

    def branch(y_ref, w_ref, j):
        gate = bg_ref[:, j * 1024:(j + 1) * 1024].astype(F32)
        return gate * jnp.dot(y_ref[...], w_ref[...], preferred_element_type=F32)

    mix = branch(ya_ref, wa_ref, 0) + branch(yb_ref, wb_ref, 1) + branch(yc_ref, wc_ref, 2)
    y = jnp.dot(mix.astype(BF16), wo_ref[...], preferred_element_type=F32)
    h = h_ref[...]
    if pre_ln:
        h = _ln(h, lig_ref[...], lib_ref[...])
    h1 = _ln(alpha * h + g1_ref[...] * y, l1g_ref[...], l1b_ref[...])
    h1_ref[...] = h1
    u2 = h1 * (1.0 + sc2_ref[...]) + sh2_ref[...]
    logits_t = _dot_nt(wr_ref[...], u2, precision=HIGHEST)
    cls, w_lo, w_hi = _route(logits_t, br_ref[...])

    @pl.when(pl.program_id(0) == 0)
    def _():
        cnt_ref[...] = jnp.zeros_like(cnt_ref)

    crow = lax.broadcasted_iota(jnp.int32, (N_CLASS_ROWS, tm), 0).astype(F32)
    onehot = jnp.where(crow == cls, 1.0, 0.0)
    earlier = lax.broadcasted_iota(jnp.int32, (tm, tm), 0) <= lax.broadcasted_iota(jnp.int32, (tm, tm), 1)
    incl = jnp.dot(onehot.astype(BF16), jnp.where(earlier, 1.0, 0.0).astype(BF16), preferred_element_type=F32)
    base = cnt_ref[...]
    rank = jnp.sum(onehot * (incl - 1.0 + base), axis=0, keepdims=True)
    cnt_ref[...] = base + incl[:, tm - 1:tm]

    route_ref[...] = jnp.concatenate([cls, w_lo, w_hi, rank, jnp.zeros((4, tm), F32)], axis=0)
    for s in range(TOKEN_SUBROWS):
        u2t_ref[pl.ds(s, tm, stride=TOKEN_SUBROWS), :] = u2[:, s * LANES:(s + 1) * LANES]


def _merge(ya, yb, yc, bg, h, mod_l, ln_in, ln1, w_a, w_b, w_c, w_o, w_rt, b_r, seq_len, ctx_row, pre_ln, alpha):
    n = ya.shape[0]
    tm = MERGE_TILE
    row = lambda i: (i, 0)
    const = lambda i: (0, 0)
    vec = pl.BlockSpec((1, 1024), const)
    wspec = lambda w: pl.BlockSpec(w.shape, const)
    return pl.pallas_call(
        functools.partial(_merge_kernel, pre_ln, alpha),
        grid=(n // tm,),
        in_specs=[
            pl.BlockSpec((tm, D_CONV), row),
            pl.BlockSpec((tm, D_ATTN), row),
            pl.BlockSpec((tm, D_MLSTM), row),
            pl.BlockSpec((tm, N_BRANCHES * 1024), row),
            pl.BlockSpec((tm, 1024), row),
            _mod_spec(2, tm, seq_len, ctx_row),
            _mod_spec(4, tm, seq_len, ctx_row),
            _mod_spec(3, tm, seq_len, ctx_row),
            vec, vec, vec, vec,
            wspec(w_a), wspec(w_b), wspec(w_c), wspec(w_o), wspec(w_rt), wspec(b_r),
        ],
        out_specs=[
            pl.BlockSpec((tm, 1024), row),
            pl.BlockSpec((tm * TOKEN_SUBROWS, LANES), row),
            pl.BlockSpec((8, tm), lambda i: (0, i)),
        ],
        out_shape=[
            jax.ShapeDtypeStruct((n, 1024), F32),
            jax.ShapeDtypeStruct((n * TOKEN_SUBROWS, LANES), F32),
            jax.ShapeDtypeStruct((8, n), F32),
        ],
        scratch_shapes=[pltpu.VMEM((N_CLASS_ROWS, 1), F32)],
        compiler_params=_cparams("arbitrary"),
        name="merge",
    )(ya, yb, yc, bg, h, mod_l, mod_l, mod_l, ln_in[0], ln_in[1], ln1[0], ln1[1], w_a, w_b, w_c, w_o, w_rt, b_r)


class _TokenGather:
    def __init__(self, idx_ref, tile, src_hbm, buf, sem, slot, priorities=(0, 1)):
        self.idx_ref, self.src_hbm, self.buf, self.sem, self.slot = idx_ref, src_hbm, buf, sem, slot
        self.tokens = buf.shape[1] // TOKEN_SUBROWS
        self.base = tile * self.tokens
        self.priorities = priorities

    def _copy(self, k):
        sub = TOKEN_SUBROWS
        p = pl.multiple_of(self.idx_ref[self.base + k] * sub, sub)
        return pltpu.make_async_copy(self.src_hbm.at[pl.ds(p, sub)], self.buf.at[self.slot, pl.ds(k * sub, sub)],
                                     self.sem.at[self.slot])

    def start(self, part=0, parts=1):
        share = self.tokens // parts
        for k in range(part * share, (part + 1) * share):
            self._copy(k).start(priority=self.priorities[k % len(self.priorities)])

    def wait(self):
        for k in range(self.tokens):
            self._copy(k).wait()


def _untile_tokens(ref):
    tokens = ref.shape[0] // TOKEN_SUBROWS
    return jnp.concatenate([ref[pl.ds(s, tokens, stride=TOKEN_SUBROWS), :] for s in range(TOKEN_SUBROWS)], axis=1)


def _experts_kernel(lo_ref, hi_ref, valid_ref, src_ref, x_hbm, wr_ref, wgl_ref, wgh_ref, wdl_ref,
                    wdh_ref, o_ref, buf, sem, x_ref, act_ref, acc_ref, gate_ref):
    i = pl.program_id(0)
    last = pl.num_programs(0) - 1
    tm = o_ref.shape[0] // TOKEN_SUBROWS
    slot = i % 2
    nxt = jnp.minimum(i + 1, last)
    used = valid_ref[i] != 0
    prev_used = valid_ref[jnp.maximum(i - 1, 0)] != 0
    gather = functools.partial(_TokenGather, src_ref, src_hbm=x_hbm, buf=buf, sem=sem,
                               priorities=EXPERT_GATHER_PRIORITIES)

    @pl.when(jnp.logical_and(i == 0, used))
    def _():
        gather(tile=0, slot=0).start()

    @pl.when(jnp.logical_and(jnp.logical_and(i > 0, prev_used), jnp.logical_not(used)))
    def _():
        gather(tile=i, slot=slot).wait()

    def gate_up(wg_ref):
        gu = jnp.dot(x_ref[...], wg_ref[...], preferred_element_type=F32)
        g_ = gu[:, 0:D_EXPERT]
        act_ref[...] = (g_ * _sigmoid(g_) * gu[:, D_EXPERT:2 * D_EXPERT]).astype(BF16)

    @pl.when(used)
    def _():
        gather(tile=i, slot=slot).wait()
        gather(tile=nxt, slot=1 - slot).start(0, 4)
        x32 = _untile_tokens(buf.at[slot])
        x_ref[...] = x32.astype(BF16)

        def affinity(e):
            logit = jnp.sum(x32 * wr_ref[pl.ds(e, 1), :], axis=1, keepdims=True)
            return _sigmoid(logit)

        s_lo, s_hi = affinity(lo_ref[i]), affinity(hi_ref[i])
        total = s_lo + s_hi
        gate_ref[0] = s_lo / total
        gate_ref[1] = s_hi / total
        gate_up(wgl_ref)

    @pl.when(used)
    def _():
        gather(tile=nxt, slot=1 - slot).start(1, 4)
        acc_ref[...] = gate_ref[0] * jnp.dot(act_ref[...], wdl_ref[...], preferred_element_type=F32)

    @pl.when(used)
    def _():
        gather(tile=nxt, slot=1 - slot).start(2, 4)
        gate_up(wgh_ref)

    @pl.when(used)
    def _():
        gather(tile=nxt, slot=1 - slot).start(3, 4)
        out = acc_ref[...] + gate_ref[1] * jnp.dot(act_ref[...], wdh_ref[...], preferred_element_type=F32)
        for s in range(TOKEN_SUBROWS):
            o_ref[pl.ds(s, tm, stride=TOKEN_SUBROWS), :] = out[:, s * LANES:(s + 1) * LANES]

    @pl.when(jnp.logical_and(used, i == last))
    def _():
        gather(tile=nxt, slot=1 - slot).wait()

    @pl.when(jnp.logical_not(used))
    def _():
        o_ref[...] = jnp.zeros_like(o_ref)


def _experts(lo, hi, valid, src, u2t, w_rt, w_gu, w_dn):
    tm = EXPERT_TILE
    n_tiles = src.shape[0] // tm
    row = lambda i, *_: (i, 0)
    grid_spec = pltpu.PrefetchScalarGridSpec(
        num_scalar_prefetch=4,
        grid=(n_tiles,),
        in_specs=[
            pl.BlockSpec(memory_space=pl.ANY),
            pl.BlockSpec(w_rt.shape, lambda i, *_: (0, 0)),
            pl.BlockSpec((None, 1024, 2 * D_EXPERT), lambda i, lo, hi, *_: (lo[i], 0, 0)),
            pl.BlockSpec((None, 1024, 2 * D_EXPERT), lambda i, lo, hi, *_: (hi[i], 0, 0)),
            pl.BlockSpec((None, D_EXPERT, 1024), lambda i, lo, hi, *_: (lo[i], 0, 0)),
            pl.BlockSpec((None, D_EXPERT, 1024), lambda i, lo, hi, *_: (hi[i], 0, 0)),
        ],
        out_specs=pl.BlockSpec((tm * TOKEN_SUBROWS, LANES), row),
        scratch_shapes=[pltpu.VMEM((2, tm * TOKEN_SUBROWS, LANES), F32), pltpu.SemaphoreType.DMA((2,)),
                        pltpu.VMEM((tm, 1024), BF16), pltpu.VMEM((tm, D_EXPERT), BF16),
                        pltpu.VMEM((tm, 1024), F32), pltpu.VMEM((2, tm, 1), F32)],
    )
    return pl.pallas_call(
        _experts_kernel,
        grid_spec=grid_spec,
        out_shape=jax.ShapeDtypeStruct((n_tiles * tm * TOKEN_SUBROWS, LANES), F32),
        compiler_params=_cparams("arbitrary"),
        name="moe_experts",
    )(lo, hi, valid, src, u2t, w_rt, w_gu, w_gu, w_dn, w_dn)


def _final_kernel(alpha, pos_ref, f_hbm, h1_ref, g2_ref, lg_ref, lb_ref, o_ref, buf, sem):
    i = pl.program_id(0)
    last = pl.num_programs(0) - 1
    slot = i % 2
    nxt = jnp.minimum(i + 1, last)
    gather = functools.partial(_TokenGather, pos_ref, src_hbm=f_hbm, buf=buf, sem=sem)

    @pl.when(i == 0)
    def _():
        gather(tile=0, slot=0).start()

    gather(tile=i, slot=slot).wait()
    gather(tile=nxt, slot=1 - slot).start()
    f = _untile_tokens(buf.at[slot])
    o_ref[...] = _ln(alpha * h1_ref[...] + g2_ref[...] * f, lg_ref[...], lb_ref[...])

    @pl.when(i == last)
    def _():
        gather(tile=nxt, slot=1 - slot).wait()


def _final_ln(h1, pos, fs, mod_l, ln2, seq_len, ctx_row, alpha):
    n = h1.shape[0]
    tm = FINAL_TILE
    row = lambda i, *_: (i, 0)
    const = lambda i, *_: (0, 0)
    grid_spec = pltpu.PrefetchScalarGridSpec(
        num_scalar_prefetch=1,
        grid=(n // tm,),
        in_specs=[
            pl.BlockSpec(memory_space=pl.ANY),
            pl.BlockSpec((tm, 1024), row),
            _mod_spec(5, tm, seq_len, ctx_row),
            pl.BlockSpec((1, 1024), const),
            pl.BlockSpec((1, 1024), const),
        ],
        out_specs=pl.BlockSpec((tm, 1024), row),
        scratch_shapes=[pltpu.VMEM((2, tm * TOKEN_SUBROWS, LANES), F32), pltpu.SemaphoreType.DMA((2,))],
    )
    return pl.pallas_call(
        functools.partial(_final_kernel, alpha),
        grid_spec=grid_spec,
        out_shape=jax.ShapeDtypeStruct((n, 1024), F32),
        compiler_params=_cparams("arbitrary"),
        name="final_ln",
    )(pos, fs, h1, mod_l, ln2[0], ln2[1])


def _sort_plan(route_t, n):
    tm = EXPERT_TILE
    n_tiles = n // tm + N_CLASSES
    cls = route_t[AUX_CLS].astype(jnp.int32)
    rank = route_t[AUX_RANK].astype(jnp.int32)
    onehot = cls[:, None] == jnp.arange(N_CLASSES, dtype=jnp.int32)[None, :]
    counts = jnp.sum(onehot, axis=0, dtype=jnp.int32)
    padded = (counts + tm - 1) // tm * tm
    ends = jnp.cumsum(padded)
    offs = ends - padded
    pos = jnp.sum(jnp.where(onehot, offs[None, :], 0), axis=1) + rank
    tile_ends = ends // tm
    j = jnp.arange(n_tiles, dtype=jnp.int32)
    n_used = tile_ends[-1]
    valid = j < n_used
    tile_cls = jnp.sum(j[:, None] >= tile_ends[None, :], axis=1)
    last_cls = jnp.sum((n_used - 1) >= tile_ends)
    tile_cls = jnp.where(valid, tile_cls, last_cls)
    group, pid = tile_cls // len(_PAIRS), tile_cls % len(_PAIRS)
    pair = jnp.asarray(np.array(_PAIRS, dtype=np.int32))
    lo = group * EXPERTS_PER_GROUP + pair[pid, 0]
    hi = group * EXPERTS_PER_GROUP + pair[pid, 1]
    order = jnp.argsort(cls, stable=True).astype(jnp.int32)
    starts = jnp.cumsum(counts) - counts
    within = (j * tm - offs[tile_cls])[:, None] + jnp.arange(tm, dtype=jnp.int32)[None, :]
    real = valid[:, None] & (within < counts[tile_cls][:, None])
    src = jnp.where(real, order[jnp.where(real, starts[tile_cls][:, None] + within, 0)], 0).reshape(-1)
    return pos.astype(jnp.int32), src.astype(jnp.int32), lo.astype(jnp.int32), hi.astype(jnp.int32), valid.astype(jnp.int32)


def _moe(u2t, route_t, h1, mod_l, ln2, w_rt, w_gu, w_dn, seq_len, ctx_row, alpha):
    n = h1.shape[0]
    pos, src, lo, hi, valid = _sort_plan(route_t, n)
    fs = _experts(lo, hi, valid, src, u2t, w_rt, w_gu, w_dn)
    return _final_ln(h1, pos, fs, mod_l, ln2, seq_len, ctx_row, alpha)


def _rope_swap_index(n_heads):
    idx = np.arange(n_heads * HEAD_DIM)
    within = idx % (HEAD_DIM // 2)
    quarter = HEAD_DIM // 4
    return np.where(within < quarter, idx + quarter, idx - quarter)


def _rope_tables(seq_len, n_heads, scale):
    t = np.arange(seq_len)
    quarter = HEAD_DIM // 4
    inv = ROPE_BASE ** (-np.arange(quarter, dtype=np.float32) / quarter)
    d = np.arange(HEAD_DIM)
    pos = np.where((d // (HEAD_DIM // 2) == 0)[None, :], (t // GRID_W)[:, None], (t % GRID_W)[:, None])
    ang = jnp.asarray(pos.astype(np.float32)) * jnp.asarray(inv[d % quarter])[None, :]
    sign = np.where(d % (HEAD_DIM // 2) < quarter, -1.0, 1.0).astype(np.float32)
    cos = jnp.cos(ang) * scale
    sin = jnp.sin(ang) * (sign * scale)[None, :]
    return jnp.tile(cos, (1, n_heads)), jnp.tile(sin, (1, n_heads))


def _flat_tables(seq_len, n_heads, scale):
    return (jnp.full((seq_len, n_heads * HEAD_DIM), scale, F32), jnp.zeros((seq_len, n_heads * HEAD_DIM), F32))


def _prep_in_weights(w_in_l):
    splits = np.cumsum([2 * D_CONV, D_ATTN, D_KV, D_KV, D_MLSTM, D_MLSTM, D_MLSTM, D_MLSTM, N_GATE_COLS])
    a, q, k, v, mq, mk, mv, mo, mg, bg = jnp.split(w_in_l, splits, axis=1)
    w_main = jnp.concatenate([a, q, q[:, _rope_swap_index(N_Q_HEADS)], k, k[:, _rope_swap_index(N_KV_HEADS)],
                              mq, mv, mo, bg], axis=1).astype(BF16)
    order = np.array([d * 8 + kind * 4 + h for h in range(N_MLSTM_HEADS) for d in range(2) for kind in range(2)])
    w_t = jnp.concatenate([mk.T, mg[:, order].T, v.T], axis=0).astype(BF16)
    return w_main, w_t


def kernel(x, c, ctx, c_ctx, ln_in_g, ln_in_b, w_router, b_router, w_mod, b_mod, w_in, conv_w, conv_b, conv_ln_g,
           conv_ln_b, w_a_out, attn_sink, w_b_out, mlstm_gate_b, mlstm_norm_g, w_c_out, w_out, ln1_g, ln1_b,
           moe_w_gu, moe_w_dn, ln2_g, ln2_b):
    batch, seq_len, d = x.shape
    ctx_len = ctx.shape[1]
    depth = w_in.shape[0]
    alpha = (2.0 * depth) ** 0.25
    ctx_row = batch
    assert d == 1024 and batch < MOD_ROWS
    assert seq_len % MERGE_TILE == 0 and ctx_len % TOKEN_TILE == 0 and (batch * ctx_len) % MERGE_TILE == 0
    assert seq_len % FINAL_TILE == 0 and (batch * ctx_len) % FINAL_TILE == 0 and FINAL_TILE % EXPERT_TILE == 0

    cc = jnp.zeros((MOD_ROWS, d), F32).at[0:batch].set(c).at[batch].set(c_ctx)
    mod = _modulation(cc, w_mod, b_mod).reshape(depth, MOD_ROWS * N_MOD, 1, d)

    attn_scale = HEAD_DIM ** -0.5
    rope_lat = _rope_tables(seq_len, N_Q_HEADS, attn_scale) + _rope_tables(seq_len, N_KV_HEADS, 1.0)
    rope_ctx = _flat_tables(ctx_len, N_Q_HEADS, attn_scale) + _flat_tables(ctx_len, N_KV_HEADS, 1.0)

    vec = lambda t: t.reshape(1, -1)
    ln_in = (vec(ln_in_g), vec(ln_in_b))
    w_rt = w_router.T
    b_r = b_router.reshape(N_EXPERTS, 1)

    h = x.reshape(batch * seq_len, d)
    hc = ctx.reshape(batch * ctx_len, d)
    for l in range(depth):
        need_ctx = l < depth - 1
        pre_ln = l == 0
        mod_l = mod[l]
        w_main, w_t = _prep_in_weights(w_in[l])
        lat = _in_proj(h, ln_in[0], ln_in[1], mod_l, w_main, w_t, rope_lat, seq_len, None, pre_ln)
        cx = _in_proj(hc, ln_in[0], ln_in[1], mod_l, w_main, w_t, rope_ctx, ctx_len, ctx_row, pre_ln)
        a_l, q_l, k_l, v_l, mq_l, mv_l, mo_l, bg_l, kt_l, gt_l = lat
        a_c, q_c, k_c, v_c, mq_c, mv_c, mo_c, bg_c, kt_c, gt_c = cx

        conv_args = (conv_w[l], vec(conv_b[l]), vec(conv_ln_g[l]), vec(conv_ln_b[l]))
        gate_b = jnp.transpose(mlstm_gate_b[l], (2, 0, 1)).reshape(N_MLSTM_HEADS, 4, 1)
        ya = _conv_branch(a_l, *conv_args, seq_len)
        yb = _attn_latent(q_l, k_l, v_l, k_c, v_c, attn_sink[l], seq_len, ctx_len)
        yc, yc_c = _mlstm_branch((mq_c, kt_c, mv_c, mo_c, gt_c), (mq_l, kt_l, mv_l, mo_l, gt_l),
                                 gate_b, vec(mlstm_norm_g[l]), seq_len, ctx_len, need_ctx)

        ln1 = (vec(ln1_g[l]), vec(ln1_b[l]))
        ln2 = (vec(ln2_g[l]), vec(ln2_b[l]))
        w_a, w_b, w_c, w_o = (w.astype(BF16) for w in (w_a_out[l], w_b_out[l], w_c_out[l], w_out[l]))
        w_gu = moe_w_gu[l].astype(BF16)
        w_dn = moe_w_dn[l].astype(BF16)

        h1, pay, route_t = _merge(ya, yb, yc, bg_l, h, mod_l, ln_in, ln1, w_a, w_b, w_c, w_o, w_rt, b_r,
                                  seq_len, None, pre_ln, alpha)
        h = _moe(pay, route_t, h1, mod_l, ln2, w_rt, w_gu, w_dn, seq_len, None, alpha)
        if need_ctx:
            ya_c = _conv_branch(a_c, *conv_args, ctx_len)
            yb_c = _attn_context(q_c, k_c, v_c, attn_sink[l], ctx_len)
            h1c, pay_c, route_tc = _merge(ya_c, yb_c, yc_c, bg_c, hc, mod_l, ln_in, ln1, w_a, w_b, w_c, w_o, w_rt,
                                          b_r, ctx_len, ctx_row, pre_ln, alpha)
            hc = _moe(pay_c, route_tc, h1c, mod_l, ln2, w_rt, w_gu, w_dn, ctx_len, ctx_row, alpha)
    return h.reshape(batch, seq_len, d)
```

```python
import functools

import numpy as np
import jax
import jax.numpy as jnp
from jax import lax
from jax.experimental import pallas as pl
from jax.experimental.pallas import tpu as pltpu

GRID_W = 64
LN_EPS = 1e-5
D_CONV = 512
CONV_WIDTH = 31
N_Q_HEADS = 8
N_KV_HEADS = 2
HEAD_DIM = 64
WINDOW = 128
BLOCK = 128
ROPE_BASE = 10000.0
D_ATTN = N_Q_HEADS * HEAD_DIM
D_KV = N_KV_HEADS * HEAD_DIM
N_MLSTM_HEADS = 4
MLSTM_HEAD_DIM = 128
D_MLSTM = N_MLSTM_HEADS * MLSTM_HEAD_DIM
N_GATE_COLS = 2 * 2 * N_MLSTM_HEADS
N_BRANCHES = 3
N_EXPERTS = 16
N_GROUPS = 4
EXPERTS_PER_GROUP = N_EXPERTS // N_GROUPS
D_EXPERT = 512
N_MOD = 6

LANES = 128
V7X_VMEM_LIMIT_BYTES = 56 * 1024 * 1024

MOD_ROWS = 16
MOD_COL_BLOCK = 512
TOKEN_TILE = 256
EXPERT_TILE = 256
FINAL_TILE = 512
MLSTM_CHUNK = 128
MLSTM_HEADS_PER_STEP = 4
MERGE_TILE = 512
CONV_ROWS = 64
CONV_PAD = 16

_PAIRS = [(i, j) for i in range(EXPERTS_PER_GROUP) for j in range(i + 1, EXPERTS_PER_GROUP)]
N_CLASSES = N_GROUPS * len(_PAIRS)
N_CLASS_ROWS = 32
AUX_CLS, AUX_W_LO, AUX_W_HI, AUX_RANK = 0, 1, 2, 3
TOKEN_SUBROWS = 1024 // LANES

F32 = jnp.float32
BF16 = jnp.bfloat16
HIGHEST = lax.Precision.HIGHEST
NEG_INF = float("-inf")

_C_A = 0
_C_Q = _C_A + 2 * D_CONV
_C_QS = _C_Q + D_ATTN
_C_K = _C_QS + D_ATTN
_C_KS = _C_K + D_KV
_C_MQ = _C_KS + D_KV
_C_MV = _C_MQ + D_MLSTM
_C_MO = _C_MV + D_MLSTM
_C_BG = _C_MO + D_MLSTM
_C_END = _C_BG + N_BRANCHES * 1024
_R_KT = 0
_R_GT = _R_KT + D_MLSTM
_R_VT = _R_GT + N_GATE_COLS


def _cparams(*sem):
    return pltpu.CompilerParams(dimension_semantics=sem, vmem_limit_bytes=V7X_VMEM_LIMIT_BYTES)


def _ln(x, g, b):
    mu = jnp.mean(x, axis=-1, keepdims=True)
    xc = x - mu
    var = jnp.mean(xc * xc, axis=-1, keepdims=True)
    return xc * lax.rsqrt(var + LN_EPS) * g + b


def _sigmoid(x):
    return 0.5 * jnp.tanh(0.5 * x) + 0.5


def _log_sigmoid(x):
    return jnp.minimum(x, 0.0) - jnp.log(1.0 + jnp.exp(-jnp.abs(x)))


def _dot_nt(a, b, precision=None):
    return lax.dot_general(a, b, (((1,), (1,)), ((), ())), preferred_element_type=F32, precision=precision)


def _mod_kernel(c_ref, w_ref, b_ref, o_ref):
    c = c_ref[...]
    s = c * _sigmoid(c)
    o_ref[...] = jnp.dot(s, w_ref[...], preferred_element_type=F32, precision=HIGHEST) + b_ref[...]


def _modulation(cc, w_mod, b_mod):
    depth, d, n = w_mod.shape
    return pl.pallas_call(
        _mod_kernel,
        grid=(depth, n // MOD_COL_BLOCK),
        in_specs=[
            pl.BlockSpec((MOD_ROWS, d), lambda l, j: (0, 0)),
            pl.BlockSpec((None, d, MOD_COL_BLOCK), lambda l, j: (l, 0, j)),
            pl.BlockSpec((None, 1, MOD_COL_BLOCK), lambda l, j: (l, 0, j)),
        ],
        out_specs=pl.BlockSpec((None, MOD_ROWS, MOD_COL_BLOCK), lambda l, j: (l, 0, j)),
        out_shape=jax.ShapeDtypeStruct((depth, MOD_ROWS, n), F32),
        compiler_params=_cparams("parallel", "parallel"),
        name="modulation",
    )(cc, w_mod, b_mod.reshape(depth, 1, n))


def _mod_spec(which, tile, seq_len, ctx_row):
    tiles_per_seq = seq_len // tile
    if ctx_row is None:
        return pl.BlockSpec((None, 1, 1024), lambda i, *_: ((i // tiles_per_seq) * N_MOD + which, 0, 0))
    return pl.BlockSpec((None, 1, 1024), lambda i, *_: (ctx_row * N_MOD + which, 0, 0))


def _in_kernel(pre_ln, x_ref, lg_ref, lb_ref, sc_ref, sh_ref, w_ref, wt_ref, cq_ref, sq_ref, ck_ref, sk_ref,
               a_ref, q_ref, k_ref, v_ref, mq_ref, mv_ref, mo_ref, bg_ref, kt_ref, gt_ref):
    x = x_ref[...]
    if pre_ln:
        x = _ln(x, lg_ref[...], lb_ref[...])
    u = (x * (1.0 + sc_ref[...]) + sh_ref[...]).astype(BF16)

    def seg(lo, hi):
        return jnp.dot(u, w_ref[:, lo:hi], preferred_element_type=F32)

    a_ref[...] = seg(_C_A, _C_Q).astype(BF16)
    q_ref[...] = (seg(_C_Q, _C_QS) * cq_ref[...] + seg(_C_QS, _C_K) * sq_ref[...]).astype(BF16)
    k_ref[...] = (seg(_C_K, _C_KS) * ck_ref[...] + seg(_C_KS, _C_MQ) * sk_ref[...]).astype(BF16)
    v_ref[...] = _dot_nt(wt_ref[_R_VT:_R_VT + D_KV, :], u).astype(BF16)
    mq_ref[...] = seg(_C_MQ, _C_MV).astype(BF16)
    mv_ref[...] = seg(_C_MV, _C_MO).astype(BF16)
    mo_ref[...] = seg(_C_MO, _C_BG).astype(BF16)
    for j in range(N_BRANCHES):
        bg_ref[:, j * 1024:(j + 1) * 1024] = _sigmoid(seg(_C_BG + j * 1024, _C_BG + (j + 1) * 1024)).astype(BF16)
    n_chunks = u.shape[0] // MLSTM_CHUNK
    kt = _dot_nt(wt_ref[_R_KT:_R_KT + D_MLSTM, :], u)
    for h in range(N_MLSTM_HEADS):
        for c in range(n_chunks):
            kt_ref[h, c] = kt[h * MLSTM_HEAD_DIM:(h + 1) * MLSTM_HEAD_DIM,
                              c * MLSTM_CHUNK:(c + 1) * MLSTM_CHUNK].astype(BF16)
    gt = _dot_nt(wt_ref[_R_GT:_R_GT + N_GATE_COLS, :], u)
    for h in range(N_MLSTM_HEADS):
        for c in range(n_chunks):
            gt_ref[h, c] = gt[h * 4:(h + 1) * 4, c * MLSTM_CHUNK:(c + 1) * MLSTM_CHUNK]


def _in_proj(x, ln_g, ln_b, mod_l, w_main, w_t, rope, seq_len, ctx_row, pre_ln):
    n = x.shape[0]
    tm = TOKEN_TILE
    cq, sq, ck, sk = rope
    tps = seq_len // tm
    nch = n // MLSTM_CHUNK
    row = lambda i: (i, 0)
    pos = lambda i: (i % tps, 0)
    const = lambda i: (0, 0)
    out_shape = [
        jax.ShapeDtypeStruct((n, 2 * D_CONV), BF16),
        jax.ShapeDtypeStruct((n, D_ATTN), BF16),
        jax.ShapeDtypeStruct((n, D_KV), BF16),
        jax.ShapeDtypeStruct((D_KV, n), BF16),
        jax.ShapeDtypeStruct((n, D_MLSTM), BF16),
        jax.ShapeDtypeStruct((n, D_MLSTM), BF16),
        jax.ShapeDtypeStruct((n, D_MLSTM), BF16),
        jax.ShapeDtypeStruct((n, N_BRANCHES * 1024), BF16),
        jax.ShapeDtypeStruct((N_MLSTM_HEADS, nch, MLSTM_HEAD_DIM, MLSTM_CHUNK), BF16),
        jax.ShapeDtypeStruct((N_MLSTM_HEADS, nch, 4, MLSTM_CHUNK), F32),
    ]
    cpt = tm // MLSTM_CHUNK
    out_specs = [
        pl.BlockSpec((tm, 2 * D_CONV), row),
        pl.BlockSpec((tm, D_ATTN), row),
        pl.BlockSpec((tm, D_KV), row),
        pl.BlockSpec((D_KV, tm), lambda i: (0, i)),
        pl.BlockSpec((tm, D_MLSTM), row),
        pl.BlockSpec((tm, D_MLSTM), row),
        pl.BlockSpec((tm, D_MLSTM), row),
        pl.BlockSpec((tm, N_BRANCHES * 1024), row),
        pl.BlockSpec((N_MLSTM_HEADS, cpt, MLSTM_HEAD_DIM, MLSTM_CHUNK), lambda i: (0, i, 0, 0)),
        pl.BlockSpec((N_MLSTM_HEADS, cpt, 4, MLSTM_CHUNK), lambda i: (0, i, 0, 0)),
    ]
    in_specs = [
        pl.BlockSpec((tm, 1024), row),
        pl.BlockSpec((1, 1024), const),
        pl.BlockSpec((1, 1024), const),
        _mod_spec(1, tm, seq_len, ctx_row),
        _mod_spec(0, tm, seq_len, ctx_row),
        pl.BlockSpec(w_main.shape, const, pipeline_mode=pl.Buffered(1)),
        pl.BlockSpec(w_t.shape, const, pipeline_mode=pl.Buffered(1)),
        pl.BlockSpec((tm, D_ATTN), pos),
        pl.BlockSpec((tm, D_ATTN), pos),
        pl.BlockSpec((tm, D_KV), pos),
        pl.BlockSpec((tm, D_KV), pos),
    ]
    return pl.pallas_call(
        functools.partial(_in_kernel, pre_ln),
        grid=(n // tm,),
        in_specs=in_specs,
        out_specs=out_specs,
        out_shape=out_shape,
        compiler_params=_cparams("parallel"),
        name="in_proj",
    )(x, ln_g, ln_b, mod_l, mod_l, w_main, w_t, cq, sq, ck, sk)


def _conv_kernel(a_ref, w_ref, cb_ref, g_ref, b_ref, o_ref, upad_ref):
    t = a_ref.shape[0]
    zeros = jnp.zeros((CONV_PAD, D_CONV), F32)
    upad_ref[0:CONV_PAD, :] = zeros
    upad_ref[CONV_PAD + t:2 * CONV_PAD + t, :] = zeros
    val = a_ref[:, 0:D_CONV].astype(F32)
    gate = a_ref[:, D_CONV:2 * D_CONV].astype(F32)
    upad_ref[CONV_PAD:CONV_PAD + t, :] = val * _sigmoid(gate)
    half = CONV_WIDTH // 2

    def body(c, carry):
        r0 = pl.multiple_of(c * CONV_ROWS, CONV_ROWS)
        n_win = CONV_ROWS + 2 * CONV_PAD
        win = upad_ref[pl.ds(r0, n_win), :]
        acc = jnp.zeros((CONV_ROWS, D_CONV), F32) + cb_ref[...]
        for res in range(8):
            rolled = win if res == 0 else pltpu.roll(win, shift=n_win - res, axis=0)
            for k in range(CONV_WIDTH):
                off = CONV_PAD - half + k
                if off % 8 == res:
                    acc = acc + rolled[off - res:off - res + CONV_ROWS, :] * w_ref[k:k + 1, :]
        y = _ln(acc, g_ref[...], b_ref[...])
        o_ref[pl.ds(r0, CONV_ROWS), :] = (y * _sigmoid(y)).astype(BF16)
        return carry

    lax.fori_loop(0, t // CONV_ROWS, body, 0)


def _conv_branch(a_in, conv_w, conv_b, ln_g, ln_b, seq_len):
    n = a_in.shape[0]
    const = lambda b: (0, 0)
    return pl.pallas_call(
        _conv_kernel,
        grid=(n // seq_len,),
        in_specs=[
            pl.BlockSpec((seq_len, 2 * D_CONV), lambda b: (b, 0)),
            pl.BlockSpec((CONV_WIDTH, D_CONV), const),
            pl.BlockSpec((1, D_CONV), const),
            pl.BlockSpec((1, D_CONV), const),
            pl.BlockSpec((1, D_CONV), const),
        ],
        out_specs=pl.BlockSpec((seq_len, D_CONV), lambda b: (b, 0)),
        out_shape=jax.ShapeDtypeStruct((n, D_CONV), BF16),
        scratch_shapes=[pltpu.VMEM((seq_len + 2 * CONV_PAD, D_CONV), F32)],
        compiler_params=_cparams("parallel"),
        name="conv_branch",
    )(a_in, conv_w, conv_b, ln_g, ln_b)


def _attn_heads(q, keys, vals_t, masks, sink_ref, o_ref):
    rows = q.shape[0]
    group = N_Q_HEADS // N_KV_HEADS
    for hk in range(N_KV_HEADS):
        lo = hk * HEAD_DIM
        qs = jnp.concatenate([q[:, (hk * group + g) * HEAD_DIM:(hk * group + g + 1) * HEAD_DIM]
                              for g in range(group)], axis=0)
        sink = jnp.concatenate([jnp.full((1, rows), sink_ref[hk * group + g], F32) for g in range(group)], axis=1)
        scores = []
        m = sink
        for kk, mask in zip(keys, masks):
            s = _dot_nt(kk[:, lo:lo + HEAD_DIM], qs)
            if mask is not None:
                s = jnp.where(mask, s, NEG_INF)
            scores.append(s)
            m = jnp.maximum(m, jnp.max(s, axis=0, keepdims=True))
        acc = jnp.zeros((2 * HEAD_DIM, rows * group), F32)
        for s, vt in zip(scores, vals_t):
            n_k = s.shape[0]
            p = jnp.exp(s - m).astype(BF16)
            ones_rows = jnp.where(lax.broadcasted_iota(jnp.int32, (HEAD_DIM, n_k), 0) == 0, 1.0, 0.0).astype(BF16)
            v_aug = jnp.concatenate([vt[lo:lo + HEAD_DIM, :], ones_rows], axis=0)
            acc = acc + jnp.dot(v_aug, p, preferred_element_type=F32)
        denom = acc[HEAD_DIM:HEAD_DIM + 1, :] + jnp.exp(sink - m)
        o_t = acc * (1.0 / denom)
        for g in range(group):
            col = (hk * group + g) * HEAD_DIM
            o_ref[:, col:col + HEAD_DIM] = o_t[:, g * rows:(g + 1) * rows].T[:, 0:HEAD_DIM].astype(BF16)


def _attn_lat_kernel(sink_ref, q_ref, kp_ref, k0_ref, kn_ref, vp_ref, v0_ref, vn_ref, kc_ref, vc_ref, o_ref):
    n = pl.program_id(1)
    nb = pl.num_programs(1)
    stacked = (N_Q_HEADS // N_KV_HEADS) * BLOCK
    ki = lax.broadcasted_iota(jnp.int32, (BLOCK, stacked), 0)
    qi = lax.broadcasted_iota(jnp.int32, (BLOCK, stacked), 1) % BLOCK
    mask_prev = ki >= qi + jnp.where(n > 0, 0, BLOCK)
    mask_next = ki <= qi - jnp.where(n < nb - 1, 0, BLOCK)
    _attn_heads(q_ref[...],
                [kp_ref[...], k0_ref[...], kn_ref[...], kc_ref[...]],
                [vp_ref[...], v0_ref[...], vn_ref[...], vc_ref[...]],
                [mask_prev, None, mask_next, None], sink_ref, o_ref)


def _attn_ctx_kernel(sink_ref, q_ref, kc_ref, vc_ref, o_ref):
    _attn_heads(q_ref[...], [kc_ref[...]], [vc_ref[...]], [None], sink_ref, o_ref)


def _attn_latent(q, k, vt, kc, vct, sink, seq_len, ctx_len):
    n = q.shape[0]
    nb = seq_len // BLOCK
    batch = n // seq_len
    blk_prev = lambda b, j: b * nb + jnp.maximum(j - 1, 0)
    blk_next = lambda b, j: b * nb + jnp.minimum(j + 1, nb - 1)
    cur = lambda b, j: (b * nb + j, 0)
    kspec = lambda blk: pl.BlockSpec((BLOCK, D_KV), lambda b, j: (blk(b, j), 0))
    vspec = lambda blk: pl.BlockSpec((D_KV, BLOCK), lambda b, j: (0, blk(b, j)))
    blk_cur = lambda b, j: b * nb + j
    return pl.pallas_call(
        _attn_lat_kernel,
        grid=(batch, nb),
        in_specs=[
            pl.BlockSpec(memory_space=pltpu.SMEM),
            pl.BlockSpec((BLOCK, D_ATTN), cur),
            kspec(blk_prev), kspec(blk_cur), kspec(blk_next), vspec(blk_prev), vspec(blk_cur), vspec(blk_next),
            pl.BlockSpec((ctx_len, D_KV), lambda b, j: (b, 0)),
            pl.BlockSpec((D_KV, ctx_len), lambda b, j: (0, b)),
        ],
        out_specs=pl.BlockSpec((BLOCK, D_ATTN), cur),
        out_shape=jax.ShapeDtypeStruct((n, D_ATTN), BF16),
        compiler_params=_cparams("parallel", "parallel"),
        name="attn_latent",
    )(sink, q, k, k, k, vt, vt, vt, kc, vct)


def _attn_context(qc, kc, vct, sink, ctx_len):
    n = qc.shape[0]
    blk = lambda b: (b, 0)
    return pl.pallas_call(
        _attn_ctx_kernel,
        grid=(n // ctx_len,),
        in_specs=[
            pl.BlockSpec(memory_space=pltpu.SMEM),
            pl.BlockSpec((ctx_len, D_ATTN), blk),
            pl.BlockSpec((ctx_len, D_KV), blk),
            pl.BlockSpec((D_KV, ctx_len), lambda b: (0, b)),
        ],
        out_specs=pl.BlockSpec((ctx_len, D_ATTN), blk),
        out_shape=jax.ShapeDtypeStruct((n, D_ATTN), BF16),
        compiler_params=_cparams("parallel"),
        name="attn_context",
    )(sink, qc, kc, vct)


def _mlstm_kernel(ctx_out, nch_c, nch_l, hps,
                  qc_ref, ktc_ref, vc_ref, moc_ref, gc_ref,
                  ql_ref, ktl_ref, vl_ref, mol_ref, gl_ref,
                  gb_ref, ng_ref, *rest):
    if ctx_out:
        yl_ref, yc_ref, af_ref, lff_ref, ab_ref, lfb_ref, cf_ref, cb_ref, hf_ref, hb_ref = rest
    else:
        yl_ref, af_ref, lff_ref, ab_ref, lfb_ref, cf_ref, cb_ref, hf_ref, hb_ref = rest
        yc_ref = None
    lc = MLSTM_CHUNK
    dh = MLSTM_HEAD_DIM
    k_scale = MLSTM_HEAD_DIM ** -0.5
    ti = lax.broadcasted_iota(jnp.int32, (lc, lc), 0)
    si = lax.broadcasted_iota(jnp.int32, (lc, lc), 1)
    lower = si <= ti
    upper = si >= ti
    pre_mat = jnp.where(upper, 1.0, 0.0)
    suf_mat = jnp.where(lower, 1.0, 0.0)
    n_rows = nch_c + nch_l
    pad_rows = -n_rows % 8

    for hh in range(hps):
        gb = gb_ref[hh]

        def gate_rows(kind):
            rows = ([gc_ref[hh, c, kind:kind + 1, :] for c in range(nch_c)]
                    + [gl_ref[hh, c, kind:kind + 1, :] for c in range(nch_l)])
            rows = jnp.concatenate(rows, axis=0) + gb[kind:kind + 1, :]
            if pad_rows:
                rows = jnp.concatenate([rows, jnp.zeros((pad_rows, lc), F32)], axis=0)
            return rows

        lf_f = _log_sigmoid(gate_rows(1))
        lf_b = _log_sigmoid(gate_rows(3))
        a_f = gate_rows(0) - jnp.dot(lf_f, pre_mat, preferred_element_type=F32, precision=HIGHEST)
        a_b = gate_rows(2) - jnp.dot(lf_b, suf_mat, preferred_element_type=F32, precision=HIGHEST)
        for c in range(n_rows):
            af_ref[hh * n_rows + c] = a_f[c:c + 1, :]
            lff_ref[hh * n_rows + c] = lf_f[c:c + 1, :]
            ab_ref[hh * n_rows + c] = a_b[c:c + 1, :]
            lfb_ref[hh * n_rows + c] = lf_b[c:c + 1, :]

    cf_ref[...] = jnp.zeros_like(cf_ref)
    cb_ref[...] = jnp.zeros_like(cb_ref)
    ones_col = jnp.where(lax.broadcasted_iota(jnp.int32, (lc, dh), 1) == 0, 1.0, 0.0).astype(BF16)

    def chunk(q, kt, v, a_row, lf_row, c_ref, hh, m, mask):
        a_mat = jnp.where(mask, jnp.broadcast_to(a_row, (lc, lc)), NEG_INF)
        cm = jnp.max(a_mat, axis=1, keepdims=True)
        b_col = jnp.sum(jnp.where(mask, jnp.broadcast_to(lf_row, (lc, lc)), 0.0), axis=1, keepdims=True)
        mx = jnp.maximum(m, cm)
        mx_last = jnp.maximum(m, jnp.max(a_row, axis=1, keepdims=True))
        w = jnp.exp(a_mat - mx)
        s = jnp.dot(q, kt, preferred_element_type=F32) * k_scale
        p = (s * w).astype(BF16)
        w_s = jnp.exp(a_row - mx_last) * k_scale
        ktw = (kt.astype(F32) * w_s).astype(BF16)
        v_aug = jnp.concatenate([v, ones_col], axis=1)
        both = jnp.dot(jnp.concatenate([p, ktw], axis=0), v_aug, preferred_element_type=F32)
        c_old = c_ref[hh]
        inter = jnp.dot(q, c_old.astype(BF16), preferred_element_type=F32)
        tot = both[0:lc, :] + jnp.exp(m - mx) * inter
        den = tot[:, dh:dh + 1]
        h = tot[:, 0:dh] / jnp.maximum(jnp.abs(den), jnp.exp(-(b_col + mx)))
        c_ref[hh] = jnp.exp(m - mx_last) * c_old + both[lc:lc + dh, :]
        m_new = jnp.sum(lf_row, axis=1, keepdims=True) + mx_last
        return h, m_new

    def step(q_ref, kt_ref, v_ref, row0, c_f, c_b, rf, rb, ms):
        out = []
        for hh in range(hps):
            cols = slice(hh * dh, (hh + 1) * dh)
            base = hh * n_rows + row0
            h_f, m_f = chunk(q_ref[pl.ds(rf, lc), cols], kt_ref[hh, c_f], v_ref[pl.ds(rf, lc), cols],
                             af_ref[base + c_f], lff_ref[base + c_f], cf_ref, hh, ms[2 * hh], lower)
            h_b, m_b = chunk(q_ref[pl.ds(rb, lc), cols], kt_ref[hh, c_b], v_ref[pl.ds(rb, lc), cols],
                             ab_ref[base + c_b], lfb_ref[base + c_b], cb_ref, hh, ms[2 * hh + 1], upper)
            hf_ref[pl.ds(rf, lc), cols] = h_f
            hb_ref[pl.ds(rb, lc), cols] = h_b
            out += [m_f, m_b]
        return tuple(out)

    def finish(n_tok, mo_ref, o_ref):
        for hh in range(hps):
            cols = slice(hh * dh, (hh + 1) * dh)
            h = hf_ref[0:n_tok, cols] + hb_ref[0:n_tok, cols]
            mu = jnp.mean(h, axis=-1, keepdims=True)
            hc = h - mu
            var = jnp.mean(hc * hc, axis=-1, keepdims=True)
            y = hc * lax.rsqrt(var + LN_EPS) * ng_ref[:, cols]
            o_ref[:, cols] = (_sigmoid(mo_ref[:, cols].astype(F32)) * y).astype(BF16)

    ms = tuple(jnp.zeros((1, 1), F32) for _ in range(2 * hps))
    for c in range(nch_c):
        c_b = nch_c - 1 - c
        ms = step(qc_ref, ktc_ref, vc_ref, 0, c, c_b, c * lc, c_b * lc, ms)
    if ctx_out:
        finish(nch_c * lc, moc_ref, yc_ref)

    def body(c, ms):
        c_b = nch_l - 1 - c
        return step(ql_ref, ktl_ref, vl_ref, nch_c, c, c_b,
                    pl.multiple_of(c * lc, lc), pl.multiple_of(c_b * lc, lc), ms)

    lax.fori_loop(0, nch_l, body, ms)
    finish(nch_l * lc, mol_ref, yl_ref)


def _mlstm_branch(ctx_p, lat_p, gate_b, norm_g, seq_len, ctx_len, ctx_out):
    mq_c, kt_c, mv_c, mo_c, g_c = ctx_p
    mq_l, kt_l, mv_l, mo_l, g_l = lat_p
    n_l, n_c = mq_l.shape[0], mq_c.shape[0]
    batch = n_l // seq_len
    lc, dh, hps = MLSTM_CHUNK, MLSTM_HEAD_DIM, MLSTM_HEADS_PER_STEP
    nch_c, nch_l = ctx_len // lc, seq_len // lc

    def stream(t, nch):
        tok = pl.BlockSpec((t, hps * dh), lambda b, h: (b, h))
        return [tok,
                pl.BlockSpec((hps, nch, dh, lc), lambda b, h: (h, b, 0, 0)),
                tok, tok,
                pl.BlockSpec((hps, nch, 4, lc), lambda b, h: (h, b, 0, 0))]

    in_specs = stream(ctx_len, nch_c) + stream(seq_len, nch_l) + [
        pl.BlockSpec((hps, 4, 1), lambda b, h: (h, 0, 0)),
        pl.BlockSpec((1, hps * dh), lambda b, h: (0, h)),
    ]
    out_specs = [pl.BlockSpec((seq_len, hps * dh), lambda b, h: (b, h))]
    out_shape = [jax.ShapeDtypeStruct((n_l, D_MLSTM), BF16)]
    if ctx_out:
        out_specs.append(pl.BlockSpec((ctx_len, hps * dh), lambda b, h: (b, h)))
        out_shape.append(jax.ShapeDtypeStruct((n_c, D_MLSTM), BF16))
    row_scratch = pltpu.VMEM((hps * (nch_c + nch_l), 1, lc), F32)
    outs = pl.pallas_call(
        functools.partial(_mlstm_kernel, ctx_out, nch_c, nch_l, hps),
        grid=(batch, N_MLSTM_HEADS // hps),
        in_specs=in_specs,
        out_specs=out_specs,
        out_shape=out_shape,
        scratch_shapes=[row_scratch, row_scratch, row_scratch, row_scratch,
                        pltpu.VMEM((hps, dh, 2 * dh), F32), pltpu.VMEM((hps, dh, 2 * dh), F32),
                        pltpu.VMEM((seq_len, hps * dh), F32), pltpu.VMEM((seq_len, hps * dh), F32)],
        compiler_params=_cparams("parallel", "parallel"),
        name="mlstm_branch",
    )(mq_c, kt_c, mv_c, mo_c, g_c, mq_l, kt_l, mv_l, mo_l, g_l, gate_b, norm_g)
    return outs if ctx_out else (outs[0], None)


def _route(logits_t, br):
    sc = [_sigmoid(logits_t[e:e + 1, :]) for e in range(N_EXPERTS)]
    sel = [sc[e] + br[e:e + 1, :] for e in range(N_EXPERTS)]
    epg = EXPERTS_PER_GROUP
    group_score = []
    for g in range(N_GROUPS):
        v = sel[g * epg:(g + 1) * epg]
        best = None
        for i in range(epg):
            for j in range(i + 1, epg):
                pair = v[i] + v[j]
                best = pair if best is None else jnp.maximum(best, pair)
        group_score.append(best)
    g_idx = jnp.zeros_like(group_score[0], dtype=jnp.int32)
    best = group_score[0]
    for g in range(1, N_GROUPS):
        better = group_score[g] > best
        g_idx = jnp.where(better, g, g_idx)
        best = jnp.maximum(best, group_score[g])
    chosen = []
    for g in range(N_GROUPS):
        v = sel[g * epg:(g + 1) * epg]
        in_g = g_idx == g
        for i in range(epg):
            rank = jnp.zeros_like(g_idx)
            for j in range(epg):
                if j == i:
                    continue
                ahead = (v[j] >= v[i]) if j < i else (v[j] > v[i])
                rank = rank + jnp.where(ahead, 1, 0)
            chosen.append(in_g & (rank < 2))
    cls = jnp.zeros_like(sc[0])
    w_lo = jnp.zeros_like(sc[0])
    w_hi = jnp.zeros_like(sc[0])
    for g in range(N_GROUPS):
        for pid, (i, j) in enumerate(_PAIRS):
            lo, hi = g * epg + i, g * epg + j
            is_pair = chosen[lo] & chosen[hi]
            cls = jnp.where(is_pair, float(g * len(_PAIRS) + pid), cls)
            w_lo = jnp.where(is_pair, sc[lo], w_lo)
            w_hi = jnp.where(is_pair, sc[hi], w_hi)
    total = w_lo + w_hi
    return cls, w_lo / total, w_hi / total


def _merge_kernel(pre_ln, alpha, ya_ref, yb_ref, yc_ref, bg_ref, h_ref, g1_ref, sc2_ref, sh2_ref,
                  lig_ref, lib_ref, l1g_ref, l1b_ref, wa_ref, wb_ref, wc_ref, wo_ref, wr_ref, br_ref,
                  h1_ref, u2t_ref, route_ref, cnt_ref):
    tm = h_ref.shape[0]

    def branch(y_ref, w_ref, j):
        gate = bg_ref[:, j * 1024:(j + 1) * 1024].astype(F32)
        return gate * jnp.dot(y_ref[...], w_ref[...], preferred_element_type=F32)

    mix = branch(ya_ref, wa_ref, 0) + branch(yb_ref, wb_ref, 1) + branch(yc_ref, wc_ref, 2)
    y = jnp.dot(mix.astype(BF16), wo_ref[...], preferred_element_type=F32)
    h = h_ref[...]
    if pre_ln:
        h = _ln(h, lig_ref[...], lib_ref[...])
    h1 = _ln(alpha * h + g1_ref[...] * y, l1g_ref[...], l1b_ref[...])
    h1_ref[...] = h1
    u2 = h1 * (1.0 + sc2_ref[...]) + sh2_ref[...]
    logits_t = _dot_nt(wr_ref[...], u2, precision=HIGHEST)
    cls, w_lo, w_hi = _route(logits_t, br_ref[...])

    @pl.when(pl.program_id(0) == 0)
    def _():
        cnt_ref[...] = jnp.zeros_like(cnt_ref)

    crow = lax.broadcasted_iota(jnp.int32, (N_CLASS_ROWS, tm), 0).astype(F32)
    onehot = jnp.where(crow == cls, 1.0, 0.0)
    earlier = lax.broadcasted_iota(jnp.int32, (tm, tm), 0) <= lax.broadcasted_iota(jnp.int32, (tm, tm), 1)
    incl = jnp.dot(onehot.astype(BF16), jnp.where(earlier, 1.0, 0.0).astype(BF16), preferred_element_type=F32)
    base = cnt_ref[...]
    rank = jnp.sum(onehot * (incl - 1.0 + base), axis=0, keepdims=True)
    cnt_ref[...] = base + incl[:, tm - 1:tm]

    route_ref[...] = jnp.concatenate([cls, w_lo, w_hi, rank, jnp.zeros((4, tm), F32)], axis=0)
    for s in range(TOKEN_SUBROWS):
        u2t_ref[pl.ds(s, tm, stride=TOKEN_SUBROWS), :] = u2[:, s * LANES:(s + 1) * LANES]


def _merge(ya, yb, yc, bg, h, mod_l, ln_in, ln1, w_a, w_b, w_c, w_o, w_rt, b_r, seq_len, ctx_row, pre_ln, alpha):
    n = ya.shape[0]
    tm = MERGE_TILE
    row = lambda i: (i, 0)
    const = lambda i: (0, 0)
    vec = pl.BlockSpec((1, 1024), const)
    wspec = lambda w: pl.BlockSpec(w.shape, const)
    return pl.pallas_call(
        functools.partial(_merge_kernel, pre_ln, alpha),
        grid=(n // tm,),
        in_specs=[
            pl.BlockSpec((tm, D_CONV), row),
            pl.BlockSpec((tm, D_ATTN), row),
            pl.BlockSpec((tm, D_MLSTM), row),
            pl.BlockSpec((tm, N_BRANCHES * 1024), row),
            pl.BlockSpec((tm, 1024), row),
            _mod_spec(2, tm, seq_len, ctx_row),
            _mod_spec(4, tm, seq_len, ctx_row),
            _mod_spec(3, tm, seq_len, ctx_row),
            vec, vec, vec, vec,
            wspec(w_a), wspec(w_b), wspec(w_c), wspec(w_o), wspec(w_rt), wspec(b_r),
        ],
        out_specs=[
            pl.BlockSpec((tm, 1024), row),
            pl.BlockSpec((tm * TOKEN_SUBROWS, LANES), row),
            pl.BlockSpec((8, tm), lambda i: (0, i)),
        ],
        out_shape=[
            jax.ShapeDtypeStruct((n, 1024), F32),
            jax.ShapeDtypeStruct((n * TOKEN_SUBROWS, LANES), F32),
            jax.ShapeDtypeStruct((8, n), F32),
        ],
        scratch_shapes=[pltpu.VMEM((N_CLASS_ROWS, 1), F32)],
        compiler_params=_cparams("arbitrary"),
        name="merge",
    )(ya, yb, yc, bg, h, mod_l, mod_l, mod_l, ln_in[0], ln_in[1], ln1[0], ln1[1], w_a, w_b, w_c, w_o, w_rt, b_r)


class _TokenGather:
    def __init__(self, idx_ref, base, src_hbm, buf, sem, slot):
        self.idx_ref, self.base, self.src_hbm, self.buf, self.sem, self.slot = idx_ref, base, src_hbm, buf, sem, slot
        self.tokens = buf.shape[1] // TOKEN_SUBROWS
        self.last = idx_ref.shape[0] - 1

    def _copy(self, k):
        sub = TOKEN_SUBROWS
        p = pl.multiple_of(self.idx_ref[jnp.minimum(self.base + k, self.last)] * sub, sub)
        return pltpu.make_async_copy(self.src_hbm.at[pl.ds(p, sub)], self.buf.at[self.slot, pl.ds(k * sub, sub)],
                                     self.sem.at[self.slot])

    def start(self):
        for k in range(self.tokens):
            self._copy(k).start(priority=k % 2)

    def wait(self):
        for k in range(self.tokens):
            self._copy(k).wait()


def _untile_tokens(ref):
    tokens = ref.shape[0] // TOKEN_SUBROWS
    return jnp.concatenate([ref[pl.ds(s, tokens, stride=TOKEN_SUBROWS), :] for s in range(TOKEN_SUBROWS)], axis=1)


def _group_kernel(lo_ref, hi_ref, base_ref, order_ref, x_hbm, wr_ref, xs_ref, gate_ref, buf, sem):
    i = pl.program_id(0)
    last = pl.num_programs(0) - 1
    slot = i % 2
    nxt = jnp.minimum(i + 1, last)
    gather = functools.partial(_TokenGather, order_ref, src_hbm=x_hbm, buf=buf, sem=sem)

    @pl.when(i == 0)
    def _():
        gather(base=base_ref[0], slot=0).start()

    gather(base=base_ref[i], slot=slot).wait()
    gather(base=base_ref[nxt], slot=1 - slot).start()
    x32 = _untile_tokens(buf.at[slot])

    def affinity(e):
        logit = jnp.sum(x32 * wr_ref[pl.ds(e, 1), :], axis=1, keepdims=True)
        return _sigmoid(logit)

    s_lo, s_hi = affinity(lo_ref[i]), affinity(hi_ref[i])
    total = s_lo + s_hi
    xs_ref[...] = x32.astype(BF16)
    gate_ref[...] = jnp.concatenate([s_lo / total, s_hi / total], axis=1)

    @pl.when(i == last)
    def _():
        gather(base=base_ref[nxt], slot=1 - slot).wait()


def _group_tokens(lo, hi, base, order, u2t, w_rt):
    tm = EXPERT_TILE
    n_tiles = base.shape[0]
    row = lambda i, *_: (i, 0)
    grid_spec = pltpu.PrefetchScalarGridSpec(
        num_scalar_prefetch=4,
        grid=(n_tiles,),
        in_specs=[
            pl.BlockSpec(memory_space=pl.ANY),
            pl.BlockSpec(w_rt.shape, lambda i, *_: (0, 0)),
        ],
        out_specs=[pl.BlockSpec((tm, 1024), row), pl.BlockSpec((tm, 2), row)],
        scratch_shapes=[pltpu.VMEM((2, tm * TOKEN_SUBROWS, LANES), F32), pltpu.SemaphoreType.DMA((2,))],
    )
    return pl.pallas_call(
        _group_kernel,
        grid_spec=grid_spec,
        out_shape=[jax.ShapeDtypeStruct((n_tiles * tm, 1024), BF16), jax.ShapeDtypeStruct((n_tiles * tm, 2), F32)],
        compiler_params=_cparams("arbitrary"),
        name="group_tokens",
    )(lo, hi, base, order, u2t, w_rt)


def _experts_kernel(lo_ref, hi_ref, valid_ref, x_ref, gate_ref, wgl_ref, wgh_ref, wdl_ref, wdh_ref, o_ref):
    i = pl.program_id(0)
    tm = x_ref.shape[0]

    @pl.when(valid_ref[i] != 0)
    def _():
        x = x_ref[...]

        def expert(wg_ref, wd_ref):
            gu = jnp.dot(x, wg_ref[...], preferred_element_type=F32)
            g_ = gu[:, 0:D_EXPERT]
            act = (g_ * _sigmoid(g_) * gu[:, D_EXPERT:2 * D_EXPERT]).astype(BF16)
            return jnp.dot(act, wd_ref[...], preferred_element_type=F32)

        out = gate_ref[:, 0:1] * expert(wgl_ref, wdl_ref) + gate_ref[:, 1:2] * expert(wgh_ref, wdh_ref)
        for s in range(TOKEN_SUBROWS):
            o_ref[pl.ds(s, tm, stride=TOKEN_SUBROWS), :] = out[:, s * LANES:(s + 1) * LANES]

    @pl.when(valid_ref[i] == 0)
    def _():
        o_ref[...] = jnp.zeros_like(o_ref)


def _experts(lo, hi, valid, xs, gates, w_gu, w_dn):
    tm = EXPERT_TILE
    n_tiles = xs.shape[0] // tm
    row = lambda i, *_: (i, 0)
    grid_spec = pltpu.PrefetchScalarGridSpec(
        num_scalar_prefetch=3,
        grid=(n_tiles,),
        in_specs=[
            pl.BlockSpec((tm, 1024), row),
            pl.BlockSpec((tm, 2), row),
            pl.BlockSpec((None, 1024, 2 * D_EXPERT), lambda i, lo, hi, *_: (lo[i], 0, 0)),
            pl.BlockSpec((None, 1024, 2 * D_EXPERT), lambda i, lo, hi, *_: (hi[i], 0, 0)),
            pl.BlockSpec((None, D_EXPERT, 1024), lambda i, lo, hi, *_: (lo[i], 0, 0)),
            pl.BlockSpec((None, D_EXPERT, 1024), lambda i, lo, hi, *_: (hi[i], 0, 0)),
        ],
        out_specs=pl.BlockSpec((tm * TOKEN_SUBROWS, LANES), row),
    )
    return pl.pallas_call(
        _experts_kernel,
        grid_spec=grid_spec,
        out_shape=jax.ShapeDtypeStruct((n_tiles * tm * TOKEN_SUBROWS, LANES), F32),
        compiler_params=_cparams("arbitrary"),
        name="moe_experts",
    )(lo, hi, valid, xs, gates, w_gu, w_gu, w_dn, w_dn)


def _final_kernel(alpha, pos_ref, f_hbm, h1_ref, g2_ref, lg_ref, lb_ref, o_ref, buf, sem):
    i = pl.program_id(0)
    last = pl.num_programs(0) - 1
    slot = i % 2
    nxt = jnp.minimum(i + 1, last)
    gather = functools.partial(_TokenGather, pos_ref, src_hbm=f_hbm, buf=buf, sem=sem)

    tm = o_ref.shape[0]

    @pl.when(i == 0)
    def _():
        gather(base=0, slot=0).start()

    gather(base=i * tm, slot=slot).wait()
    gather(base=nxt * tm, slot=1 - slot).start()
    f = _untile_tokens(buf.at[slot])
    o_ref[...] = _ln(alpha * h1_ref[...] + g2_ref[...] * f, lg_ref[...], lb_ref[...])

    @pl.when(i == last)
    def _():
        gather(base=nxt * tm, slot=1 - slot).wait()


def _final_ln(h1, pos, fs, mod_l, ln2, seq_len, ctx_row, alpha):
    n = h1.shape[0]
    tm = FINAL_TILE
    row = lambda i, *_: (i, 0)
    const = lambda i, *_: (0, 0)
    grid_spec = pltpu.PrefetchScalarGridSpec(
        num_scalar_prefetch=1,
        grid=(n // tm,),
        in_specs=[
            pl.BlockSpec(memory_space=pl.ANY),
            pl.BlockSpec((tm, 1024), row),
            _mod_spec(5, tm, seq_len, ctx_row),
            pl.BlockSpec((1, 1024), const),
            pl.BlockSpec((1, 1024), const),
        ],
        out_specs=pl.BlockSpec((tm, 1024), row),
        scratch_shapes=[pltpu.VMEM((2, tm * TOKEN_SUBROWS, LANES), F32), pltpu.SemaphoreType.DMA((2,))],
    )
    return pl.pallas_call(
        functools.partial(_final_kernel, alpha),
        grid_spec=grid_spec,
        out_shape=jax.ShapeDtypeStruct((n, 1024), F32),
        compiler_params=_cparams("arbitrary"),
        name="final_ln",
    )(pos, fs, h1, mod_l, ln2[0], ln2[1])


def _sort_plan(route_t, n):
    tm = EXPERT_TILE
    n_tiles = n // tm + N_CLASSES
    cls = route_t[AUX_CLS].astype(jnp.int32)
    rank = route_t[AUX_RANK].astype(jnp.int32)
    onehot = cls[:, None] == jnp.arange(N_CLASSES, dtype=jnp.int32)[None, :]
    counts = jnp.sum(onehot, axis=0, dtype=jnp.int32)
    padded = (counts + tm - 1) // tm * tm
    ends = jnp.cumsum(padded)
    offs = ends - padded
    pos = jnp.sum(jnp.where(onehot, offs[None, :], 0), axis=1) + rank
    tile_ends = ends // tm
    j = jnp.arange(n_tiles, dtype=jnp.int32)
    n_used = tile_ends[-1]
    valid = j < n_used
    tile_cls = jnp.sum(j[:, None] >= tile_ends[None, :], axis=1)
    last_cls = jnp.sum((n_used - 1) >= tile_ends)
    tile_cls = jnp.where(valid, tile_cls, last_cls)
    group, pid = tile_cls // len(_PAIRS), tile_cls % len(_PAIRS)
    pair = jnp.asarray(np.array(_PAIRS, dtype=np.int32))
    lo = group * EXPERTS_PER_GROUP + pair[pid, 0]
    hi = group * EXPERTS_PER_GROUP + pair[pid, 1]
    order = jnp.argsort(cls, stable=True).astype(jnp.int32)
    starts = jnp.cumsum(counts) - counts
    base = jnp.where(valid, starts[tile_cls] + j * tm - offs[tile_cls], 0)
    i32 = lambda t: t.astype(jnp.int32)
    return i32(pos), order, i32(base), i32(lo), i32(hi), i32(valid)


def _moe(u2t, route_t, h1, mod_l, ln2, w_rt, w_gu, w_dn, seq_len, ctx_row, alpha):
    n = h1.shape[0]
    pos, order, base, lo, hi, valid = _sort_plan(route_t, n)
    xs, gates = _group_tokens(lo, hi, base, order, u2t, w_rt)
    fs = _experts(lo, hi, valid, xs, gates, w_gu, w_dn)
    return _final_ln(h1, pos, fs, mod_l, ln2, seq_len, ctx_row, alpha)


def _rope_swap_index(n_heads):
    idx = np.arange(n_heads * HEAD_DIM)
    within = idx % (HEAD_DIM // 2)
    quarter = HEAD_DIM // 4
    return np.where(within < quarter, idx + quarter, idx - quarter)


def _rope_tables(seq_len, n_heads, scale):
    t = np.arange(seq_len)
    quarter = HEAD_DIM // 4
    inv = ROPE_BASE ** (-np.arange(quarter, dtype=np.float32) / quarter)
    d = np.arange(HEAD_DIM)
    pos = np.where((d // (HEAD_DIM // 2) == 0)[None, :], (t // GRID_W)[:, None], (t % GRID_W)[:, None])
    ang = jnp.asarray(pos.astype(np.float32)) * jnp.asarray(inv[d % quarter])[None, :]
    sign = np.where(d % (HEAD_DIM // 2) < quarter, -1.0, 1.0).astype(np.float32)
    cos = jnp.cos(ang) * scale
    sin = jnp.sin(ang) * (sign * scale)[None, :]
    return jnp.tile(cos, (1, n_heads)), jnp.tile(sin, (1, n_heads))


def _flat_tables(seq_len, n_heads, scale):
    return (jnp.full((seq_len, n_heads * HEAD_DIM), scale, F32), jnp.zeros((seq_len, n_heads * HEAD_DIM), F32))


def _prep_in_weights(w_in_l):
    splits = np.cumsum([2 * D_CONV, D_ATTN, D_KV, D_KV, D_MLSTM, D_MLSTM, D_MLSTM, D_MLSTM, N_GATE_COLS])
    a, q, k, v, mq, mk, mv, mo, mg, bg = jnp.split(w_in_l, splits, axis=1)
    w_main = jnp.concatenate([a, q, q[:, _rope_swap_index(N_Q_HEADS)], k, k[:, _rope_swap_index(N_KV_HEADS)],
                              mq, mv, mo, bg], axis=1).astype(BF16)
    order = np.array([d * 8 + kind * 4 + h for h in range(N_MLSTM_HEADS) for d in range(2) for kind in range(2)])
    w_t = jnp.concatenate([mk.T, mg[:, order].T, v.T], axis=0).astype(BF16)
    return w_main, w_t


def kernel(x, c, ctx, c_ctx, ln_in_g, ln_in_b, w_router, b_router, w_mod, b_mod, w_in, conv_w, conv_b, conv_ln_g,
           conv_ln_b, w_a_out, attn_sink, w_b_out, mlstm_gate_b, mlstm_norm_g, w_c_out, w_out, ln1_g, ln1_b,
           moe_w_gu, moe_w_dn, ln2_g, ln2_b):
    batch, seq_len, d = x.shape
    ctx_len = ctx.shape[1]
    depth = w_in.shape[0]
    alpha = (2.0 * depth) ** 0.25
    ctx_row = batch
    assert d == 1024 and batch < MOD_ROWS
    assert seq_len % MERGE_TILE == 0 and ctx_len % TOKEN_TILE == 0 and (batch * ctx_len) % MERGE_TILE == 0
    assert seq_len % FINAL_TILE == 0 and (batch * ctx_len) % FINAL_TILE == 0 and FINAL_TILE % EXPERT_TILE == 0

    cc = jnp.zeros((MOD_ROWS, d), F32).at[0:batch].set(c).at[batch].set(c_ctx)
    mod = _modulation(cc, w_mod, b_mod).reshape(depth, MOD_ROWS * N_MOD, 1, d)

    attn_scale = HEAD_DIM ** -0.5
    rope_lat = _rope_tables(seq_len, N_Q_HEADS, attn_scale) + _rope_tables(seq_len, N_KV_HEADS, 1.0)
    rope_ctx = _flat_tables(ctx_len, N_Q_HEADS, attn_scale) + _flat_tables(ctx_len, N_KV_HEADS, 1.0)

    vec = lambda t: t.reshape(1, -1)
    ln_in = (vec(ln_in_g), vec(ln_in_b))
    w_rt = w_router.T
    b_r = b_router.reshape(N_EXPERTS, 1)

    h = x.reshape(batch * seq_len, d)
    hc = ctx.reshape(batch * ctx_len, d)
    for l in range(depth):
        need_ctx = l < depth - 1
        pre_ln = l == 0
        mod_l = mod[l]
        w_main, w_t = _prep_in_weights(w_in[l])
        lat = _in_proj(h, ln_in[0], ln_in[1], mod_l, w_main, w_t, rope_lat, seq_len, None, pre_ln)
        cx = _in_proj(hc, ln_in[0], ln_in[1], mod_l, w_main, w_t, rope_ctx, ctx_len, ctx_row, pre_ln)
        a_l, q_l, k_l, v_l, mq_l, mv_l, mo_l, bg_l, kt_l, gt_l = lat
        a_c, q_c, k_c, v_c, mq_c, mv_c, mo_c, bg_c, kt_c, gt_c = cx

        conv_args = (conv_w[l], vec(conv_b[l]), vec(conv_ln_g[l]), vec(conv_ln_b[l]))
        gate_b = jnp.transpose(mlstm_gate_b[l], (2, 0, 1)).reshape(N_MLSTM_HEADS, 4, 1)
        ya = _conv_branch(a_l, *conv_args, seq_len)
        yb = _attn_latent(q_l, k_l, v_l, k_c, v_c, attn_sink[l], seq_len, ctx_len)
        yc, yc_c = _mlstm_branch((mq_c, kt_c, mv_c, mo_c, gt_c), (mq_l, kt_l, mv_l, mo_l, gt_l),
                                 gate_b, vec(mlstm_norm_g[l]), seq_len, ctx_len, need_ctx)

        ln1 = (vec(ln1_g[l]), vec(ln1_b[l]))
        ln2 = (vec(ln2_g[l]), vec(ln2_b[l]))
        w_a, w_b, w_c, w_o = (w.astype(BF16) for w in (w_a_out[l], w_b_out[l], w_c_out[l], w_out[l]))
        w_gu = moe_w_gu[l].astype(BF16)
        w_dn = moe_w_dn[l].astype(BF16)

        h1, pay, route_t = _merge(ya, yb, yc, bg_l, h, mod_l, ln_in, ln1, w_a, w_b, w_c, w_o, w_rt, b_r,
                                  seq_len, None, pre_ln, alpha)
        h = _moe(pay, route_t, h1, mod_l, ln2, w_rt, w_gu, w_dn, seq_len, None, alpha)
        if need_ctx:
            ya_c = _conv_branch(a_c, *conv_args, ctx_len)
            yb_c = _attn_context(q_c, k_c, v_c, attn_sink[l], ctx_len)
            h1c, pay_c, route_tc = _merge(ya_c, yb_c, yc_c, bg_c, hc, mod_l, ln_in, ln1, w_a, w_b, w_c, w_o, w_rt,
                                          b_r, ctx_len, ctx_row, pre_ln, alpha)
            hc = _moe(pay_c, route_tc, h1c, mod_l, ln2, w_rt, w_gu, w_dn, ctx_len, ctx_row, alpha)
    return h.reshape(batch, seq_len, d)
```

```python
import functools

import numpy as np
import jax
import jax.numpy as jnp
from jax import lax
from jax.experimental import pallas as pl
from jax.experimental.pallas import tpu as pltpu

GRID_W = 64
LN_EPS = 1e-5
D_CONV = 512
CONV_WIDTH = 31
N_Q_HEADS = 8
N_KV_HEADS = 2
HEAD_DIM = 64
WINDOW = 128
BLOCK = 128
ROPE_BASE = 10000.0
D_ATTN = N_Q_HEADS * HEAD_DIM
D_KV = N_KV_HEADS * HEAD_DIM
N_MLSTM_HEADS = 4
MLSTM_HEAD_DIM = 128
D_MLSTM = N_MLSTM_HEADS * MLSTM_HEAD_DIM
N_GATE_COLS = 2 * 2 * N_MLSTM_HEADS
N_BRANCHES = 3
N_EXPERTS = 16
N_GROUPS = 4
EXPERTS_PER_GROUP = N_EXPERTS // N_GROUPS
D_EXPERT = 512
N_MOD = 6

LANES = 128
V7X_VMEM_LIMIT_BYTES = 56 * 1024 * 1024

MOD_ROWS = 16
MOD_COL_BLOCK = 512
TOKEN_TILE = 512
EXPERT_TILE = 256
GROUP_COPY_PARTS = 4
FINAL_TILE = 512
MLSTM_CHUNK = 128
MLSTM_HEADS_PER_STEP = 4
MERGE_TILE = 512
CONV_ROWS = 64
CONV_PAD = 16

_PAIRS = [(i, j) for i in range(EXPERTS_PER_GROUP) for j in range(i + 1, EXPERTS_PER_GROUP)]
N_CLASSES = N_GROUPS * len(_PAIRS)
N_CLASS_ROWS = 32
AUX_CLS, AUX_W_LO, AUX_W_HI, AUX_RANK = 0, 1, 2, 3
TOKEN_SUBROWS = 1024 // LANES

F32 = jnp.float32
BF16 = jnp.bfloat16
HIGHEST = lax.Precision.HIGHEST
NEG_INF = float("-inf")

_C_A = 0
_C_Q = _C_A + 2 * D_CONV
_C_QS = _C_Q + D_ATTN
_C_K = _C_QS + D_ATTN
_C_KS = _C_K + D_KV
_C_MQ = _C_KS + D_KV
_C_MV = _C_MQ + D_MLSTM
_C_MO = _C_MV + D_MLSTM
_C_BG = _C_MO + D_MLSTM
_C_END = _C_BG + N_BRANCHES * 1024
_R_KT = 0
_R_GT = _R_KT + D_MLSTM
_R_VT = _R_GT + N_GATE_COLS


def _cparams(*sem):
    return pltpu.CompilerParams(dimension_semantics=sem, vmem_limit_bytes=V7X_VMEM_LIMIT_BYTES)


def _ln(x, g, b):
    mu = jnp.mean(x, axis=-1, keepdims=True)
    xc = x - mu
    var = jnp.mean(xc * xc, axis=-1, keepdims=True)
    return xc * lax.rsqrt(var + LN_EPS) * g + b


def _sigmoid(x):
    return 0.5 * jnp.tanh(0.5 * x) + 0.5


def _log_sigmoid(x):
    return jnp.minimum(x, 0.0) - jnp.log(1.0 + jnp.exp(-jnp.abs(x)))


def _dot_nt(a, b, precision=None):
    return lax.dot_general(a, b, (((1,), (1,)), ((), ())), preferred_element_type=F32, precision=precision)


def _mod_kernel(c_ref, w_ref, b_ref, o_ref):
    c = c_ref[...]
    s = c * _sigmoid(c)
    o_ref[...] = jnp.dot(s, w_ref[...], preferred_element_type=F32, precision=HIGHEST) + b_ref[...]


def _modulation(cc, w_mod, b_mod):
    depth, d, n = w_mod.shape
    return pl.pallas_call(
        _mod_kernel,
        grid=(depth, n // MOD_COL_BLOCK),
        in_specs=[
            pl.BlockSpec((MOD_ROWS, d), lambda l, j: (0, 0)),
            pl.BlockSpec((None, d, MOD_COL_BLOCK), lambda l, j: (l, 0, j)),
            pl.BlockSpec((None, 1, MOD_COL_BLOCK), lambda l, j: (l, 0, j)),
        ],
        out_specs=pl.BlockSpec((None, MOD_ROWS, MOD_COL_BLOCK), lambda l, j: (l, 0, j)),
        out_shape=jax.ShapeDtypeStruct((depth, MOD_ROWS, n), F32),
        compiler_params=_cparams("parallel", "parallel"),
        name="modulation",
    )(cc, w_mod, b_mod.reshape(depth, 1, n))


def _mod_spec(which, tile, seq_len, ctx_row):
    tiles_per_seq = seq_len // tile
    if ctx_row is None:
        return pl.BlockSpec((None, 1, 1024), lambda i, *_: ((i // tiles_per_seq) * N_MOD + which, 0, 0))
    return pl.BlockSpec((None, 1, 1024), lambda i, *_: (ctx_row * N_MOD + which, 0, 0))


def _in_kernel(pre_ln, x_ref, lg_ref, lb_ref, sc_ref, sh_ref, w_ref, wt_ref, cq_ref, sq_ref, ck_ref, sk_ref,
               a_ref, q_ref, k_ref, v_ref, mq_ref, mv_ref, mo_ref, bg_ref, kt_ref, gt_ref):
    x = x_ref[...]
    if pre_ln:
        x = _ln(x, lg_ref[...], lb_ref[...])
    u = (x * (1.0 + sc_ref[...]) + sh_ref[...]).astype(BF16)

    def seg(lo, hi):
        return jnp.dot(u, w_ref[:, lo:hi], preferred_element_type=F32)

    a_ref[...] = seg(_C_A, _C_Q).astype(BF16)
    q_ref[...] = (seg(_C_Q, _C_QS) * cq_ref[...] + seg(_C_QS, _C_K) * sq_ref[...]).astype(BF16)
    k_ref[...] = (seg(_C_K, _C_KS) * ck_ref[...] + seg(_C_KS, _C_MQ) * sk_ref[...]).astype(BF16)
    v_ref[...] = _dot_nt(wt_ref[_R_VT:_R_VT + D_KV, :], u).astype(BF16)
    mq_ref[...] = seg(_C_MQ, _C_MV).astype(BF16)
    mv_ref[...] = seg(_C_MV, _C_MO).astype(BF16)
    mo_ref[...] = seg(_C_MO, _C_BG).astype(BF16)
    for j in range(N_BRANCHES):
        bg_ref[:, j * 1024:(j + 1) * 1024] = _sigmoid(seg(_C_BG + j * 1024, _C_BG + (j + 1) * 1024)).astype(BF16)
    n_chunks = u.shape[0] // MLSTM_CHUNK
    kt = _dot_nt(wt_ref[_R_KT:_R_KT + D_MLSTM, :], u)
    for h in range(N_MLSTM_HEADS):
        for c in range(n_chunks):
            kt_ref[h, c] = kt[h * MLSTM_HEAD_DIM:(h + 1) * MLSTM_HEAD_DIM,
                              c * MLSTM_CHUNK:(c + 1) * MLSTM_CHUNK].astype(BF16)
    gt = _dot_nt(wt_ref[_R_GT:_R_GT + N_GATE_COLS, :], u)
    for h in range(N_MLSTM_HEADS):
        for c in range(n_chunks):
            gt_ref[h, c] = gt[h * 4:(h + 1) * 4, c * MLSTM_CHUNK:(c + 1) * MLSTM_CHUNK]


def _in_proj(x, ln_g, ln_b, mod_l, w_main, w_t, rope, seq_len, ctx_row, pre_ln):
    n = x.shape[0]
    tm = min(TOKEN_TILE, seq_len)
    cq, sq, ck, sk = rope
    tps = seq_len // tm
    nch = n // MLSTM_CHUNK
    row = lambda i: (i, 0)
    pos = lambda i: (i % tps, 0)
    const = lambda i: (0, 0)
    out_shape = [
        jax.ShapeDtypeStruct((n, 2 * D_CONV), BF16),
        jax.ShapeDtypeStruct((n, D_ATTN), BF16),
        jax.ShapeDtypeStruct((n, D_KV), BF16),
        jax.ShapeDtypeStruct((D_KV, n), BF16),
        jax.ShapeDtypeStruct((n, D_MLSTM), BF16),
        jax.ShapeDtypeStruct((n, D_MLSTM), BF16),
        jax.ShapeDtypeStruct((n, D_MLSTM), BF16),
        jax.ShapeDtypeStruct((n, N_BRANCHES * 1024), BF16),
        jax.ShapeDtypeStruct((N_MLSTM_HEADS, nch, MLSTM_HEAD_DIM, MLSTM_CHUNK), BF16),
        jax.ShapeDtypeStruct((N_MLSTM_HEADS, nch, 4, MLSTM_CHUNK), F32),
    ]
    cpt = tm // MLSTM_CHUNK
    out_specs = [
        pl.BlockSpec((tm, 2 * D_CONV), row),
        pl.BlockSpec((tm, D_ATTN), row),
        pl.BlockSpec((tm, D_KV), row),
        pl.BlockSpec((D_KV, tm), lambda i: (0, i)),
        pl.BlockSpec((tm, D_MLSTM), row),
        pl.BlockSpec((tm, D_MLSTM), row),
        pl.BlockSpec((tm, D_MLSTM), row),
        pl.BlockSpec((tm, N_BRANCHES * 1024), row),
        pl.BlockSpec((N_MLSTM_HEADS, cpt, MLSTM_HEAD_DIM, MLSTM_CHUNK), lambda i: (0, i, 0, 0)),
        pl.BlockSpec((N_MLSTM_HEADS, cpt, 4, MLSTM_CHUNK), lambda i: (0, i, 0, 0)),
    ]
    in_specs = [
        pl.BlockSpec((tm, 1024), row),
        pl.BlockSpec((1, 1024), const),
        pl.BlockSpec((1, 1024), const),
        _mod_spec(1, tm, seq_len, ctx_row),
        _mod_spec(0, tm, seq_len, ctx_row),
        pl.BlockSpec(w_main.shape, const, pipeline_mode=pl.Buffered(1)),
        pl.BlockSpec(w_t.shape, const, pipeline_mode=pl.Buffered(1)),
        pl.BlockSpec((tm, D_ATTN), pos),
        pl.BlockSpec((tm, D_ATTN), pos),
        pl.BlockSpec((tm, D_KV), pos),
        pl.BlockSpec((tm, D_KV), pos),
    ]
    return pl.pallas_call(
        functools.partial(_in_kernel, pre_ln),
        grid=(n // tm,),
        in_specs=in_specs,
        out_specs=out_specs,
        out_shape=out_shape,
        compiler_params=_cparams("parallel"),
        name="in_proj",
    )(x, ln_g, ln_b, mod_l, mod_l, w_main, w_t, cq, sq, ck, sk)


def _conv_kernel(a_ref, w_ref, cb_ref, g_ref, b_ref, o_ref, upad_ref):
    t = a_ref.shape[0]
    zeros = jnp.zeros((CONV_PAD, D_CONV), F32)
    upad_ref[0:CONV_PAD, :] = zeros
    upad_ref[CONV_PAD + t:2 * CONV_PAD + t, :] = zeros
    val = a_ref[:, 0:D_CONV].astype(F32)
    gate = a_ref[:, D_CONV:2 * D_CONV].astype(F32)
    upad_ref[CONV_PAD:CONV_PAD + t, :] = val * _sigmoid(gate)
    half = CONV_WIDTH // 2

    def body(c, carry):
        r0 = pl.multiple_of(c * CONV_ROWS, CONV_ROWS)
        n_win = CONV_ROWS + 2 * CONV_PAD
        win = upad_ref[pl.ds(r0, n_win), :]
        acc = jnp.zeros((CONV_ROWS, D_CONV), F32) + cb_ref[...]
        for res in range(8):
            rolled = win if res == 0 else pltpu.roll(win, shift=n_win - res, axis=0)
            for k in range(CONV_WIDTH):
                off = CONV_PAD - half + k
                if off % 8 == res:
                    acc = acc + rolled[off - res:off - res + CONV_ROWS, :] * w_ref[k:k + 1, :]
        y = _ln(acc, g_ref[...], b_ref[...])
        o_ref[pl.ds(r0, CONV_ROWS), :] = (y * _sigmoid(y)).astype(BF16)
        return carry

    lax.fori_loop(0, t // CONV_ROWS, body, 0)


def _conv_branch(a_in, conv_w, conv_b, ln_g, ln_b, seq_len):
    n = a_in.shape[0]
    const = lambda b: (0, 0)
    return pl.pallas_call(
        _conv_kernel,
        grid=(n // seq_len,),
        in_specs=[
            pl.BlockSpec((seq_len, 2 * D_CONV), lambda b: (b, 0)),
            pl.BlockSpec((CONV_WIDTH, D_CONV), const),
            pl.BlockSpec((1, D_CONV), const),
            pl.BlockSpec((1, D_CONV), const),
            pl.BlockSpec((1, D_CONV), const),
        ],
        out_specs=pl.BlockSpec((seq_len, D_CONV), lambda b: (b, 0)),
        out_shape=jax.ShapeDtypeStruct((n, D_CONV), BF16),
        scratch_shapes=[pltpu.VMEM((seq_len + 2 * CONV_PAD, D_CONV), F32)],
        compiler_params=_cparams("parallel"),
        name="conv_branch",
    )(a_in, conv_w, conv_b, ln_g, ln_b)


def _attn_heads(q, keys, vals_t, masks, sink_ref, o_ref):
    rows = q.shape[0]
    group = N_Q_HEADS // N_KV_HEADS
    for hk in range(N_KV_HEADS):
        lo = hk * HEAD_DIM
        qs = jnp.concatenate([q[:, (hk * group + g) * HEAD_DIM:(hk * group + g + 1) * HEAD_DIM]
                              for g in range(group)], axis=0)
        sink = jnp.concatenate([jnp.full((1, rows), sink_ref[hk * group + g], F32) for g in range(group)], axis=1)
        scores = []
        m = sink
        for kk, mask in zip(keys, masks):
            s = _dot_nt(kk[:, lo:lo + HEAD_DIM], qs)
            if mask is not None:
                s = jnp.where(mask, s, NEG_INF)
            scores.append(s)
            m = jnp.maximum(m, jnp.max(s, axis=0, keepdims=True))
        acc = jnp.zeros((2 * HEAD_DIM, rows * group), F32)
        for s, vt in zip(scores, vals_t):
            n_k = s.shape[0]
            p = jnp.exp(s - m).astype(BF16)
            ones_rows = jnp.where(lax.broadcasted_iota(jnp.int32, (HEAD_DIM, n_k), 0) == 0, 1.0, 0.0).astype(BF16)
            v_aug = jnp.concatenate([vt[lo:lo + HEAD_DIM, :], ones_rows], axis=0)
            acc = acc + jnp.dot(v_aug, p, preferred_element_type=F32)
        denom = acc[HEAD_DIM:HEAD_DIM + 1, :] + jnp.exp(sink - m)
        o_t = acc * (1.0 / denom)
        for g in range(group):
            col = (hk * group + g) * HEAD_DIM
            o_ref[:, col:col + HEAD_DIM] = o_t[:, g * rows:(g + 1) * rows].T[:, 0:HEAD_DIM].astype(BF16)


def _attn_lat_kernel(sink_ref, q_ref, kp_ref, k0_ref, kn_ref, vp_ref, v0_ref, vn_ref, kc_ref, vc_ref, o_ref):
    n = pl.program_id(1)
    nb = pl.num_programs(1)
    stacked = (N_Q_HEADS // N_KV_HEADS) * BLOCK
    ki = lax.broadcasted_iota(jnp.int32, (BLOCK, stacked), 0)
    qi = lax.broadcasted_iota(jnp.int32, (BLOCK, stacked), 1) % BLOCK
    mask_prev = ki >= qi + jnp.where(n > 0, 0, BLOCK)
    mask_next = ki <= qi - jnp.where(n < nb - 1, 0, BLOCK)
    _attn_heads(q_ref[...],
                [kp_ref[...], k0_ref[...], kn_ref[...], kc_ref[...]],
                [vp_ref[...], v0_ref[...], vn_ref[...], vc_ref[...]],
                [mask_prev, None, mask_next, None], sink_ref, o_ref)


def _attn_ctx_kernel(sink_ref, q_ref, kc_ref, vc_ref, o_ref):
    _attn_heads(q_ref[...], [kc_ref[...]], [vc_ref[...]], [None], sink_ref, o_ref)


def _attn_latent(q, k, vt, kc, vct, sink, seq_len, ctx_len):
    n = q.shape[0]
    nb = seq_len // BLOCK
    batch = n // seq_len
    blk_prev = lambda b, j: b * nb + jnp.maximum(j - 1, 0)
    blk_next = lambda b, j: b * nb + jnp.minimum(j + 1, nb - 1)
    cur = lambda b, j: (b * nb + j, 0)
    kspec = lambda blk: pl.BlockSpec((BLOCK, D_KV), lambda b, j: (blk(b, j), 0))
    vspec = lambda blk: pl.BlockSpec((D_KV, BLOCK), lambda b, j: (0, blk(b, j)))
    blk_cur = lambda b, j: b * nb + j
    return pl.pallas_call(
        _attn_lat_kernel,
        grid=(batch, nb),
        in_specs=[
            pl.BlockSpec(memory_space=pltpu.SMEM),
            pl.BlockSpec((BLOCK, D_ATTN), cur),
            kspec(blk_prev), kspec(blk_cur), kspec(blk_next), vspec(blk_prev), vspec(blk_cur), vspec(blk_next),
            pl.BlockSpec((ctx_len, D_KV), lambda b, j: (b, 0)),
            pl.BlockSpec((D_KV, ctx_len), lambda b, j: (0, b)),
        ],
        out_specs=pl.BlockSpec((BLOCK, D_ATTN), cur),
        out_shape=jax.ShapeDtypeStruct((n, D_ATTN), BF16),
        compiler_params=_cparams("parallel", "parallel"),
        name="attn_latent",
    )(sink, q, k, k, k, vt, vt, vt, kc, vct)


def _attn_context(qc, kc, vct, sink, ctx_len):
    n = qc.shape[0]
    blk = lambda b: (b, 0)
    return pl.pallas_call(
        _attn_ctx_kernel,
        grid=(n // ctx_len,),
        in_specs=[
            pl.BlockSpec(memory_space=pltpu.SMEM),
            pl.BlockSpec((ctx_len, D_ATTN), blk),
            pl.BlockSpec((ctx_len, D_KV), blk),
            pl.BlockSpec((D_KV, ctx_len), lambda b: (0, b)),
        ],
        out_specs=pl.BlockSpec((ctx_len, D_ATTN), blk),
        out_shape=jax.ShapeDtypeStruct((n, D_ATTN), BF16),
        compiler_params=_cparams("parallel"),
        name="attn_context",
    )(sink, qc, kc, vct)


def _mlstm_kernel(ctx_out, nch_c, nch_l, hps,
                  qc_ref, ktc_ref, vc_ref, moc_ref, gc_ref,
                  ql_ref, ktl_ref, vl_ref, mol_ref, gl_ref,
                  gb_ref, ng_ref, *rest):
    if ctx_out:
        yl_ref, yc_ref, af_ref, lff_ref, ab_ref, lfb_ref, cf_ref, cb_ref, hf_ref, hb_ref = rest
    else:
        yl_ref, af_ref, lff_ref, ab_ref, lfb_ref, cf_ref, cb_ref, hf_ref, hb_ref = rest
        yc_ref = None
    lc = MLSTM_CHUNK
    dh = MLSTM_HEAD_DIM
    k_scale = MLSTM_HEAD_DIM ** -0.5
    ti = lax.broadcasted_iota(jnp.int32, (lc, lc), 0)
    si = lax.broadcasted_iota(jnp.int32, (lc, lc), 1)
    lower = si <= ti
    upper = si >= ti
    pre_mat = jnp.where(upper, 1.0, 0.0)
    suf_mat = jnp.where(lower, 1.0, 0.0)
    n_rows = nch_c + nch_l
    pad_rows = -n_rows % 8

    for hh in range(hps):
        gb = gb_ref[hh]

        def gate_rows(kind):
            rows = ([gc_ref[hh, c, kind:kind + 1, :] for c in range(nch_c)]
                    + [gl_ref[hh, c, kind:kind + 1, :] for c in range(nch_l)])
            rows = jnp.concatenate(rows, axis=0) + gb[kind:kind + 1, :]
            if pad_rows:
                rows = jnp.concatenate([rows, jnp.zeros((pad_rows, lc), F32)], axis=0)
            return rows

        lf_f = _log_sigmoid(gate_rows(1))
        lf_b = _log_sigmoid(gate_rows(3))
        a_f = gate_rows(0) - jnp.dot(lf_f, pre_mat, preferred_element_type=F32, precision=HIGHEST)
        a_b = gate_rows(2) - jnp.dot(lf_b, suf_mat, preferred_element_type=F32, precision=HIGHEST)
        for c in range(n_rows):
            af_ref[hh * n_rows + c] = a_f[c:c + 1, :]
            lff_ref[hh * n_rows + c] = lf_f[c:c + 1, :]
            ab_ref[hh * n_rows + c] = a_b[c:c + 1, :]
            lfb_ref[hh * n_rows + c] = lf_b[c:c + 1, :]

    cf_ref[...] = jnp.zeros_like(cf_ref)
    cb_ref[...] = jnp.zeros_like(cb_ref)
    ones_col = jnp.where(lax.broadcasted_iota(jnp.int32, (lc, dh), 1) == 0, 1.0, 0.0).astype(BF16)

    def chunk(q, kt, v, a_row, lf_row, c_ref, hh, m, mask):
        a_mat = jnp.where(mask, jnp.broadcast_to(a_row, (lc, lc)), NEG_INF)
        cm = jnp.max(a_mat, axis=1, keepdims=True)
        b_col = jnp.sum(jnp.where(mask, jnp.broadcast_to(lf_row, (lc, lc)), 0.0), axis=1, keepdims=True)
        mx = jnp.maximum(m, cm)
        mx_last = jnp.maximum(m, jnp.max(a_row, axis=1, keepdims=True))
        w = jnp.exp(a_mat - mx)
        s = jnp.dot(q, kt, preferred_element_type=F32) * k_scale
        p = (s * w).astype(BF16)
        w_s = jnp.exp(a_row - mx_last) * k_scale
        ktw = (kt.astype(F32) * w_s).astype(BF16)
        v_aug = jnp.concatenate([v, ones_col], axis=1)
        both = jnp.dot(jnp.concatenate([p, ktw], axis=0), v_aug, preferred_element_type=F32)
        c_old = c_ref[hh]
        inter = jnp.dot(q, c_old.astype(BF16), preferred_element_type=F32)
        tot = both[0:lc, :] + jnp.exp(m - mx) * inter
        den = tot[:, dh:dh + 1]
        h = tot[:, 0:dh] / jnp.maximum(jnp.abs(den), jnp.exp(-(b_col + mx)))
        c_ref[hh] = jnp.exp(m - mx_last) * c_old + both[lc:lc + dh, :]
        m_new = jnp.sum(lf_row, axis=1, keepdims=True) + mx_last
        return h, m_new

    def step(q_ref, kt_ref, v_ref, row0, c_f, c_b, rf, rb, ms):
        out = []
        for hh in range(hps):
            cols = slice(hh * dh, (hh + 1) * dh)
            base = hh * n_rows + row0
            h_f, m_f = chunk(q_ref[pl.ds(rf, lc), cols], kt_ref[hh, c_f], v_ref[pl.ds(rf, lc), cols],
                             af_ref[base + c_f], lff_ref[base + c_f], cf_ref, hh, ms[2 * hh], lower)
            h_b, m_b = chunk(q_ref[pl.ds(rb, lc), cols], kt_ref[hh, c_b], v_ref[pl.ds(rb, lc), cols],
                             ab_ref[base + c_b], lfb_ref[base + c_b], cb_ref, hh, ms[2 * hh + 1], upper)
            hf_ref[pl.ds(rf, lc), cols] = h_f
            hb_ref[pl.ds(rb, lc), cols] = h_b
            out += [m_f, m_b]
        return tuple(out)

    def finish(n_tok, mo_ref, o_ref):
        for hh in range(hps):
            cols = slice(hh * dh, (hh + 1) * dh)
            h = hf_ref[0:n_tok, cols] + hb_ref[0:n_tok, cols]
            mu = jnp.mean(h, axis=-1, keepdims=True)
            hc = h - mu
            var = jnp.mean(hc * hc, axis=-1, keepdims=True)
            y = hc * lax.rsqrt(var + LN_EPS) * ng_ref[:, cols]
            o_ref[:, cols] = (_sigmoid(mo_ref[:, cols].astype(F32)) * y).astype(BF16)

    ms = tuple(jnp.zeros((1, 1), F32) for _ in range(2 * hps))
    for c in range(nch_c):
        c_b = nch_c - 1 - c
        ms = step(qc_ref, ktc_ref, vc_ref, 0, c, c_b, c * lc, c_b * lc, ms)
    if ctx_out:
        finish(nch_c * lc, moc_ref, yc_ref)

    def body(c, ms):
        c_b = nch_l - 1 - c
        return step(ql_ref, ktl_ref, vl_ref, nch_c, c, c_b,
                    pl.multiple_of(c * lc, lc), pl.multiple_of(c_b * lc, lc), ms)

    lax.fori_loop(0, nch_l, body, ms)
    finish(nch_l * lc, mol_ref, yl_ref)


def _mlstm_branch(ctx_p, lat_p, gate_b, norm_g, seq_len, ctx_len, ctx_out):
    mq_c, kt_c, mv_c, mo_c, g_c = ctx_p
    mq_l, kt_l, mv_l, mo_l, g_l = lat_p
    n_l, n_c = mq_l.shape[0], mq_c.shape[0]
    batch = n_l // seq_len
    lc, dh, hps = MLSTM_CHUNK, MLSTM_HEAD_DIM, MLSTM_HEADS_PER_STEP
    nch_c, nch_l = ctx_len // lc, seq_len // lc

    def stream(t, nch):
        tok = pl.BlockSpec((t, hps * dh), lambda b, h: (b, h))
        return [tok,
                pl.BlockSpec((hps, nch, dh, lc), lambda b, h: (h, b, 0, 0)),
                tok, tok,
                pl.BlockSpec((hps, nch, 4, lc), lambda b, h: (h, b, 0, 0))]

    in_specs = stream(ctx_len, nch_c) + stream(seq_len, nch_l) + [
        pl.BlockSpec((hps, 4, 1), lambda b, h: (h, 0, 0)),
        pl.BlockSpec((1, hps * dh), lambda b, h: (0, h)),
    ]
    out_specs = [pl.BlockSpec((seq_len, hps * dh), lambda b, h: (b, h))]
    out_shape = [jax.ShapeDtypeStruct((n_l, D_MLSTM), BF16)]
    if ctx_out:
        out_specs.append(pl.BlockSpec((ctx_len, hps * dh), lambda b, h: (b, h)))
        out_shape.append(jax.ShapeDtypeStruct((n_c, D_MLSTM), BF16))
    row_scratch = pltpu.VMEM((hps * (nch_c + nch_l), 1, lc), F32)
    outs = pl.pallas_call(
        functools.partial(_mlstm_kernel, ctx_out, nch_c, nch_l, hps),
        grid=(batch, N_MLSTM_HEADS // hps),
        in_specs=in_specs,
        out_specs=out_specs,
        out_shape=out_shape,
        scratch_shapes=[row_scratch, row_scratch, row_scratch, row_scratch,
                        pltpu.VMEM((hps, dh, 2 * dh), F32), pltpu.VMEM((hps, dh, 2 * dh), F32),
                        pltpu.VMEM((seq_len, hps * dh), F32), pltpu.VMEM((seq_len, hps * dh), F32)],
        compiler_params=_cparams("parallel", "parallel"),
        name="mlstm_branch",
    )(mq_c, kt_c, mv_c, mo_c, g_c, mq_l, kt_l, mv_l, mo_l, g_l, gate_b, norm_g)
    return outs if ctx_out else (outs[0], None)


def _route(logits_t, br):
    sc = [_sigmoid(logits_t[e:e + 1, :]) for e in range(N_EXPERTS)]
    sel = [sc[e] + br[e:e + 1, :] for e in range(N_EXPERTS)]
    epg = EXPERTS_PER_GROUP
    group_score = []
    for g in range(N_GROUPS):
        v = sel[g * epg:(g + 1) * epg]
        best = None
        for i in range(epg):
            for j in range(i + 1, epg):
                pair = v[i] + v[j]
                best = pair if best is None else jnp.maximum(best, pair)
        group_score.append(best)
    g_idx = jnp.zeros_like(group_score[0], dtype=jnp.int32)
    best = group_score[0]
    for g in range(1, N_GROUPS):
        better = group_score[g] > best
        g_idx = jnp.where(better, g, g_idx)
        best = jnp.maximum(best, group_score[g])
    chosen = []
    for g in range(N_GROUPS):
        v = sel[g * epg:(g + 1) * epg]
        in_g = g_idx == g
        for i in range(epg):
            rank = jnp.zeros_like(g_idx)
            for j in range(epg):
                if j == i:
                    continue
                ahead = (v[j] >= v[i]) if j < i else (v[j] > v[i])
                rank = rank + jnp.where(ahead, 1, 0)
            chosen.append(in_g & (rank < 2))
    cls = jnp.zeros_like(sc[0])
    w_lo = jnp.zeros_like(sc[0])
    w_hi = jnp.zeros_like(sc[0])
    for g in range(N_GROUPS):
        for pid, (i, j) in enumerate(_PAIRS):
            lo, hi = g * epg + i, g * epg + j
            is_pair = chosen[lo] & chosen[hi]
            cls = jnp.where(is_pair, float(g * len(_PAIRS) + pid), cls)
            w_lo = jnp.where(is_pair, sc[lo], w_lo)
            w_hi = jnp.where(is_pair, sc[hi], w_hi)
    total = w_lo + w_hi
    return cls, w_lo / total, w_hi / total


def _merge_kernel(pre_ln, alpha, ya_ref, yb_ref, yc_ref, bg_ref, h_ref, g1_ref, sc2_ref, sh2_ref,
                  lig_ref, lib_ref, l1g_ref, l1b_ref, wa_ref, wb_ref, wc_ref, wo_ref, wr_ref, br_ref,
                  h1_ref, u2t_ref, route_ref, cnt_ref):
    tm = h_ref.shape[0]

    def branch(y_ref, w_ref, j):
        gate = bg_ref[:, j * 1024:(j + 1) * 1024].astype(F32)
        return gate * jnp.dot(y_ref[...], w_ref[...], preferred_element_type=F32)

    mix = branch(ya_ref, wa_ref, 0) + branch(yb_ref, wb_ref, 1) + branch(yc_ref, wc_ref, 2)
    y = jnp.dot(mix.astype(BF16), wo_ref[...], preferred_element_type=F32)
    h = h_ref[...]
    if pre_ln:
        h = _ln(h, lig_ref[...], lib_ref[...])
    h1 = _ln(alpha * h + g1_ref[...] * y, l1g_ref[...], l1b_ref[...])
    h1_ref[...] = h1
    u2 = h1 * (1.0 + sc2_ref[...]) + sh2_ref[...]
    logits_t = _dot_nt(wr_ref[...], u2, precision=HIGHEST)
    cls, w_lo, w_hi = _route(logits_t, br_ref[...])

    @pl.when(pl.program_id(0) == 0)
    def _():
        cnt_ref[...] = jnp.zeros_like(cnt_ref)

    crow = lax.broadcasted_iota(jnp.int32, (N_CLASS_ROWS, tm), 0).astype(F32)
    onehot = jnp.where(crow == cls, 1.0, 0.0)
    earlier = lax.broadcasted_iota(jnp.int32, (tm, tm), 0) <= lax.broadcasted_iota(jnp.int32, (tm, tm), 1)
    incl = jnp.dot(onehot.astype(BF16), jnp.where(earlier, 1.0, 0.0).astype(BF16), preferred_element_type=F32)
    base = cnt_ref[...]
    rank = jnp.sum(onehot * (incl - 1.0 + base), axis=0, keepdims=True)
    cnt_ref[...] = base + incl[:, tm - 1:tm]

    route_ref[...] = jnp.concatenate([cls, w_lo, w_hi, rank, jnp.zeros((4, tm), F32)], axis=0)
    for s in range(TOKEN_SUBROWS):
        u2t_ref[pl.ds(s, tm, stride=TOKEN_SUBROWS), :] = u2[:, s * LANES:(s + 1) * LANES]


def _merge(ya, yb, yc, bg, h, mod_l, ln_in, ln1, w_a, w_b, w_c, w_o, w_rt, b_r, seq_len, ctx_row, pre_ln, alpha):
    n = ya.shape[0]
    tm = MERGE_TILE
    row = lambda i: (i, 0)
    const = lambda i: (0, 0)
    vec = pl.BlockSpec((1, 1024), const)
    wspec = lambda w: pl.BlockSpec(w.shape, const)
    return pl.pallas_call(
        functools.partial(_merge_kernel, pre_ln, alpha),
        grid=(n // tm,),
        in_specs=[
            pl.BlockSpec((tm, D_CONV), row),
            pl.BlockSpec((tm, D_ATTN), row),
            pl.BlockSpec((tm, D_MLSTM), row),
            pl.BlockSpec((tm, N_BRANCHES * 1024), row),
            pl.BlockSpec((tm, 1024), row),
            _mod_spec(2, tm, seq_len, ctx_row),
            _mod_spec(4, tm, seq_len, ctx_row),
            _mod_spec(3, tm, seq_len, ctx_row),
            vec, vec, vec, vec,
            wspec(w_a), wspec(w_b), wspec(w_c), wspec(w_o), wspec(w_rt), wspec(b_r),
        ],
        out_specs=[
            pl.BlockSpec((tm, 1024), row),
            pl.BlockSpec((tm * TOKEN_SUBROWS, LANES), row),
            pl.BlockSpec((8, tm), lambda i: (0, i)),
        ],
        out_shape=[
            jax.ShapeDtypeStruct((n, 1024), F32),
            jax.ShapeDtypeStruct((n * TOKEN_SUBROWS, LANES), F32),
            jax.ShapeDtypeStruct((8, n), F32),
        ],
        scratch_shapes=[pltpu.VMEM((N_CLASS_ROWS, 1), F32)],
        compiler_params=_cparams("arbitrary"),
        name="merge",
    )(ya, yb, yc, bg, h, mod_l, mod_l, mod_l, ln_in[0], ln_in[1], ln1[0], ln1[1], w_a, w_b, w_c, w_o, w_rt, b_r)


class _TokenGather:
    def __init__(self, idx_ref, base, src_hbm, buf, sem, slot):
        self.idx_ref, self.base, self.src_hbm, self.buf, self.sem, self.slot = idx_ref, base, src_hbm, buf, sem, slot
        self.tokens = buf.shape[1] // TOKEN_SUBROWS
        self.last = idx_ref.shape[0] - 1

    def _copy(self, k):
        sub = TOKEN_SUBROWS
        p = pl.multiple_of(self.idx_ref[jnp.minimum(self.base + k, self.last)] * sub, sub)
        return pltpu.make_async_copy(self.src_hbm.at[pl.ds(p, sub)], self.buf.at[self.slot, pl.ds(k * sub, sub)],
                                     self.sem.at[self.slot])

    def _share(self, part, parts):
        share = self.tokens // parts
        return range(part * share, (part + 1) * share)

    def start(self, part=0, parts=1):
        for k in self._share(part, parts):
            self._copy(k).start(priority=k % 2)

    def wait(self, part=0, parts=1):
        for k in self._share(part, parts):
            self._copy(k).wait()


def _untile_tokens(ref):
    tokens = ref.shape[0] // TOKEN_SUBROWS
    return jnp.concatenate([ref[pl.ds(s, tokens, stride=TOKEN_SUBROWS), :] for s in range(TOKEN_SUBROWS)], axis=1)


def _group_kernel(lo_ref, hi_ref, base_ref, cnt_ref, order_ref, x_hbm, wr_ref, xs_ref, gate_ref, buf, sem):
    i = pl.program_id(0)
    last = pl.num_programs(0) - 1
    slot = i % 2
    nxt = jnp.minimum(i + 1, last)
    gather = functools.partial(_TokenGather, order_ref, src_hbm=x_hbm, buf=buf, sem=sem)
    parts = GROUP_COPY_PARTS
    share = (buf.shape[1] // TOKEN_SUBROWS) // parts

    @pl.when(i == 0)
    def _():
        buf[...] = jnp.zeros_like(buf)

    for p in range(parts):
        @pl.when(jnp.logical_and(i == 0, cnt_ref[0] > p * share))
        def _():
            gather(base=base_ref[0], slot=0).start(p, parts)

    for p in range(parts):
        @pl.when(cnt_ref[i] > p * share)
        def _():
            gather(base=base_ref[i], slot=slot).wait(p, parts)

    for p in range(parts):
        @pl.when(cnt_ref[nxt] > p * share)
        def _():
            gather(base=base_ref[nxt], slot=1 - slot).start(p, parts)

    x32 = _untile_tokens(buf.at[slot])

    def affinity(e):
        logit = jnp.sum(x32 * wr_ref[pl.ds(e, 1), :], axis=1, keepdims=True)
        return _sigmoid(logit)

    s_lo, s_hi = affinity(lo_ref[i]), affinity(hi_ref[i])
    total = s_lo + s_hi
    xs_ref[...] = x32.astype(BF16)
    gate_ref[...] = jnp.concatenate([s_lo / total, s_hi / total], axis=1)

    for p in range(parts):
        @pl.when(jnp.logical_and(i == last, cnt_ref[nxt] > p * share))
        def _():
            gather(base=base_ref[nxt], slot=1 - slot).wait(p, parts)


def _group_tokens(lo, hi, base, cnt, order, u2t, w_rt):
    tm = EXPERT_TILE
    n_tiles = base.shape[0]
    row = lambda i, *_: (i, 0)
    grid_spec = pltpu.PrefetchScalarGridSpec(
        num_scalar_prefetch=5,
        grid=(n_tiles,),
        in_specs=[
            pl.BlockSpec(memory_space=pl.ANY),
            pl.BlockSpec(w_rt.shape, lambda i, *_: (0, 0)),
        ],
        out_specs=[pl.BlockSpec((tm, 1024), row), pl.BlockSpec((tm, 2), row)],
        scratch_shapes=[pltpu.VMEM((2, tm * TOKEN_SUBROWS, LANES), F32), pltpu.SemaphoreType.DMA((2,))],
    )
    return pl.pallas_call(
        _group_kernel,
        grid_spec=grid_spec,
        out_shape=[jax.ShapeDtypeStruct((n_tiles * tm, 1024), BF16), jax.ShapeDtypeStruct((n_tiles * tm, 2), F32)],
        compiler_params=_cparams("arbitrary"),
        name="group_tokens",
    )(lo, hi, base, cnt, order, u2t, w_rt)


def _experts_kernel(lo_ref, hi_ref, valid_ref, x_ref, gate_ref, wgl_ref, wgh_ref, wdl_ref, wdh_ref, o_ref):
    i = pl.program_id(0)
    tm = x_ref.shape[0]

    @pl.when(valid_ref[i] != 0)
    def _():
        x = x_ref[...]

        def expert(wg_ref, wd_ref):
            gu = jnp.dot(x, wg_ref[...], preferred_element_type=F32)
            g_ = gu[:, 0:D_EXPERT]
            act = (g_ * _sigmoid(g_) * gu[:, D_EXPERT:2 * D_EXPERT]).astype(BF16)
            return jnp.dot(act, wd_ref[...], preferred_element_type=F32)

        out = gate_ref[:, 0:1] * expert(wgl_ref, wdl_ref) + gate_ref[:, 1:2] * expert(wgh_ref, wdh_ref)
        for s in range(TOKEN_SUBROWS):
            o_ref[pl.ds(s, tm, stride=TOKEN_SUBROWS), :] = out[:, s * LANES:(s + 1) * LANES]

    @pl.when(valid_ref[i] == 0)
    def _():
        o_ref[...] = jnp.zeros_like(o_ref)


def _experts(lo, hi, valid, xs, gates, w_gu, w_dn):
    tm = EXPERT_TILE
    n_tiles = xs.shape[0] // tm
    row = lambda i, *_: (i, 0)
    grid_spec = pltpu.PrefetchScalarGridSpec(
        num_scalar_prefetch=3,
        grid=(n_tiles,),
        in_specs=[
            pl.BlockSpec((tm, 1024), row),
            pl.BlockSpec((tm, 2), row),
            pl.BlockSpec((None, 1024, 2 * D_EXPERT), lambda i, lo, hi, *_: (lo[i], 0, 0)),
            pl.BlockSpec((None, 1024, 2 * D_EXPERT), lambda i, lo, hi, *_: (hi[i], 0, 0)),
            pl.BlockSpec((None, D_EXPERT, 1024), lambda i, lo, hi, *_: (lo[i], 0, 0)),
            pl.BlockSpec((None, D_EXPERT, 1024), lambda i, lo, hi, *_: (hi[i], 0, 0)),
        ],
        out_specs=pl.BlockSpec((tm * TOKEN_SUBROWS, LANES), row),
    )
    return pl.pallas_call(
        _experts_kernel,
        grid_spec=grid_spec,
        out_shape=jax.ShapeDtypeStruct((n_tiles * tm * TOKEN_SUBROWS, LANES), F32),
        compiler_params=_cparams("arbitrary"),
        name="moe_experts",
    )(lo, hi, valid, xs, gates, w_gu, w_gu, w_dn, w_dn)


def _final_kernel(alpha, pos_ref, f_hbm, h1_ref, g2_ref, lg_ref, lb_ref, o_ref, buf, sem):
    i = pl.program_id(0)
    last = pl.num_programs(0) - 1
    slot = i % 2
    nxt = jnp.minimum(i + 1, last)
    gather = functools.partial(_TokenGather, pos_ref, src_hbm=f_hbm, buf=buf, sem=sem)

    tm = o_ref.shape[0]

    @pl.when(i == 0)
    def _():
        gather(base=0, slot=0).start()

    gather(base=i * tm, slot=slot).wait()
    gather(base=nxt * tm, slot=1 - slot).start()
    f = _untile_tokens(buf.at[slot])
    o_ref[...] = _ln(alpha * h1_ref[...] + g2_ref[...] * f, lg_ref[...], lb_ref[...])

    @pl.when(i == last)
    def _():
        gather(base=nxt * tm, slot=1 - slot).wait()


def _final_ln(h1, pos, fs, mod_l, ln2, seq_len, ctx_row, alpha):
    n = h1.shape[0]
    tm = FINAL_TILE
    row = lambda i, *_: (i, 0)
    const = lambda i, *_: (0, 0)
    grid_spec = pltpu.PrefetchScalarGridSpec(
        num_scalar_prefetch=1,
        grid=(n // tm,),
        in_specs=[
            pl.BlockSpec(memory_space=pl.ANY),
            pl.BlockSpec((tm, 1024), row),
            _mod_spec(5, tm, seq_len, ctx_row),
            pl.BlockSpec((1, 1024), const),
            pl.BlockSpec((1, 1024), const),
        ],
        out_specs=pl.BlockSpec((tm, 1024), row),
        scratch_shapes=[pltpu.VMEM((2, tm * TOKEN_SUBROWS, LANES), F32), pltpu.SemaphoreType.DMA((2,))],
    )
    return pl.pallas_call(
        functools.partial(_final_kernel, alpha),
        grid_spec=grid_spec,
        out_shape=jax.ShapeDtypeStruct((n, 1024), F32),
        compiler_params=_cparams("arbitrary"),
        name="final_ln",
    )(pos, fs, h1, mod_l, ln2[0], ln2[1])


def _sort_plan(route_t, n):
    tm = EXPERT_TILE
    n_tiles = n // tm + N_CLASSES
    cls = route_t[AUX_CLS].astype(jnp.int32)
    rank = route_t[AUX_RANK].astype(jnp.int32)
    onehot = cls[:, None] == jnp.arange(N_CLASSES, dtype=jnp.int32)[None, :]
    counts = jnp.sum(onehot, axis=0, dtype=jnp.int32)
    padded = (counts + tm - 1) // tm * tm
    ends = jnp.cumsum(padded)
    offs = ends - padded
    pos = jnp.sum(jnp.where(onehot, offs[None, :], 0), axis=1) + rank
    tile_ends = ends // tm
    j = jnp.arange(n_tiles, dtype=jnp.int32)
    n_used = tile_ends[-1]
    valid = j < n_used
    tile_cls = jnp.sum(j[:, None] >= tile_ends[None, :], axis=1)
    last_cls = jnp.sum((n_used - 1) >= tile_ends)
    tile_cls = jnp.where(valid, tile_cls, last_cls)
    group, pid = tile_cls // len(_PAIRS), tile_cls % len(_PAIRS)
    pair = jnp.asarray(np.array(_PAIRS, dtype=np.int32))
    lo = group * EXPERTS_PER_GROUP + pair[pid, 0]
    hi = group * EXPERTS_PER_GROUP + pair[pid, 1]
    order = jnp.argsort(cls, stable=True).astype(jnp.int32)
    starts = jnp.cumsum(counts) - counts
    first = j * tm - offs[tile_cls]
    base = jnp.where(valid, starts[tile_cls] + first, 0)
    cnt = jnp.where(valid, jnp.clip(counts[tile_cls] - first, 0, tm), 0)
    i32 = lambda t: t.astype(jnp.int32)
    return i32(pos), order, i32(base), i32(cnt), i32(lo), i32(hi), i32(valid)


def _moe(u2t, route_t, h1, mod_l, ln2, w_rt, w_gu, w_dn, seq_len, ctx_row, alpha):
    n = h1.shape[0]
    pos, order, base, cnt, lo, hi, valid = _sort_plan(route_t, n)
    xs, gates = _group_tokens(lo, hi, base, cnt, order, u2t, w_rt)
    fs = _experts(lo, hi, valid, xs, gates, w_gu, w_dn)
    return _final_ln(h1, pos, fs, mod_l, ln2, seq_len, ctx_row, alpha)


def _rope_swap_index(n_heads):
    idx = np.arange(n_heads * HEAD_DIM)
    within = idx % (HEAD_DIM // 2)
    quarter = HEAD_DIM // 4
    return np.where(within < quarter, idx + quarter, idx - quarter)


def _rope_tables(seq_len, n_heads, scale):
    t = np.arange(seq_len)
    quarter = HEAD_DIM // 4
    inv = ROPE_BASE ** (-np.arange(quarter, dtype=np.float32) / quarter)
    d = np.arange(HEAD_DIM)
    pos = np.where((d // (HEAD_DIM // 2) == 0)[None, :], (t // GRID_W)[:, None], (t % GRID_W)[:, None])
    ang = jnp.asarray(pos.astype(np.float32)) * jnp.asarray(inv[d % quarter])[None, :]
    sign = np.where(d % (HEAD_DIM // 2) < quarter, -1.0, 1.0).astype(np.float32)
    cos = jnp.cos(ang) * scale
    sin = jnp.sin(ang) * (sign * scale)[None, :]
    return jnp.tile(cos, (1, n_heads)), jnp.tile(sin, (1, n_heads))


def _flat_tables(seq_len, n_heads, scale):
    return (jnp.full((seq_len, n_heads * HEAD_DIM), scale, F32), jnp.zeros((seq_len, n_heads * HEAD_DIM), F32))


def _prep_in_weights(w_in_l):
    splits = np.cumsum([2 * D_CONV, D_ATTN, D_KV, D_KV, D_MLSTM, D_MLSTM, D_MLSTM, D_MLSTM, N_GATE_COLS])
    a, q, k, v, mq, mk, mv, mo, mg, bg = jnp.split(w_in_l, splits, axis=1)
    w_main = jnp.concatenate([a, q, q[:, _rope_swap_index(N_Q_HEADS)], k, k[:, _rope_swap_index(N_KV_HEADS)],
                              mq, mv, mo, bg], axis=1).astype(BF16)
    order = np.array([d * 8 + kind * 4 + h for h in range(N_MLSTM_HEADS) for d in range(2) for kind in range(2)])
    w_t = jnp.concatenate([mk.T, mg[:, order].T, v.T], axis=0).astype(BF16)
    return w_main, w_t


def kernel(x, c, ctx, c_ctx, ln_in_g, ln_in_b, w_router, b_router, w_mod, b_mod, w_in, conv_w, conv_b, conv_ln_g,
           conv_ln_b, w_a_out, attn_sink, w_b_out, mlstm_gate_b, mlstm_norm_g, w_c_out, w_out, ln1_g, ln1_b,
           moe_w_gu, moe_w_dn, ln2_g, ln2_b):
    batch, seq_len, d = x.shape
    ctx_len = ctx.shape[1]
    depth = w_in.shape[0]
    alpha = (2.0 * depth) ** 0.25
    ctx_row = batch
    assert d == 1024 and batch < MOD_ROWS
    assert seq_len % MERGE_TILE == 0 and seq_len % TOKEN_TILE == 0 and (batch * ctx_len) % MERGE_TILE == 0
    assert ctx_len % MLSTM_CHUNK == 0 and ctx_len % 8 == 0
    assert seq_len % FINAL_TILE == 0 and (batch * ctx_len) % FINAL_TILE == 0 and FINAL_TILE % EXPERT_TILE == 0

    cc = jnp.zeros((MOD_ROWS, d), F32).at[0:batch].set(c).at[batch].set(c_ctx)
    mod = _modulation(cc, w_mod, b_mod).reshape(depth, MOD_ROWS * N_MOD, 1, d)

    attn_scale = HEAD_DIM ** -0.5
    rope_lat = _rope_tables(seq_len, N_Q_HEADS, attn_scale) + _rope_tables(seq_len, N_KV_HEADS, 1.0)
    rope_ctx = _flat_tables(ctx_len, N_Q_HEADS, attn_scale) + _flat_tables(ctx_len, N_KV_HEADS, 1.0)

    vec = lambda t: t.reshape(1, -1)
    ln_in = (vec(ln_in_g), vec(ln_in_b))
    w_rt = w_router.T
    b_r = b_router.reshape(N_EXPERTS, 1)

    h = x.reshape(batch * seq_len, d)
    hc = ctx.reshape(batch * ctx_len, d)
    for l in range(depth):
        need_ctx = l < depth - 1
        pre_ln = l == 0
        mod_l = mod[l]
        w_main, w_t = _prep_in_weights(w_in[l])
        lat = _in_proj(h, ln_in[0], ln_in[1], mod_l, w_main, w_t, rope_lat, seq_len, None, pre_ln)
        cx = _in_proj(hc, ln_in[0], ln_in[1], mod_l, w_main, w_t, rope_ctx, ctx_len, ctx_row, pre_ln)
        a_l, q_l, k_l, v_l, mq_l, mv_l, mo_l, bg_l, kt_l, gt_l = lat
        a_c, q_c, k_c, v_c, mq_c, mv_c, mo_c, bg_c, kt_c, gt_c = cx

        conv_args = (conv_w[l], vec(conv_b[l]), vec(conv_ln_g[l]), vec(conv_ln_b[l]))
        gate_b = jnp.transpose(mlstm_gate_b[l], (2, 0, 1)).reshape(N_MLSTM_HEADS, 4, 1)
        ya = _conv_branch(a_l, *conv_args, seq_len)
        yb = _attn_latent(q_l, k_l, v_l, k_c, v_c, attn_sink[l], seq_len, ctx_len)
        yc, yc_c = _mlstm_branch((mq_c, kt_c, mv_c, mo_c, gt_c), (mq_l, kt_l, mv_l, mo_l, gt_l),
                                 gate_b, vec(mlstm_norm_g[l]), seq_len, ctx_len, need_ctx)

        ln1 = (vec(ln1_g[l]), vec(ln1_b[l]))
        ln2 = (vec(ln2_g[l]), vec(ln2_b[l]))
        w_a, w_b, w_c, w_o = (w.astype(BF16) for w in (w_a_out[l], w_b_out[l], w_c_out[l], w_out[l]))
        w_gu = moe_w_gu[l].astype(BF16)
        w_dn = moe_w_dn[l].astype(BF16)

        h1, pay, route_t = _merge(ya, yb, yc, bg_l, h, mod_l, ln_in, ln1, w_a, w_b, w_c, w_o, w_rt, b_r,
                                  seq_len, None, pre_ln, alpha)
        h = _moe(pay, route_t, h1, mod_l, ln2, w_rt, w_gu, w_dn, seq_len, None, alpha)
        if need_ctx:
            ya_c = _conv_branch(a_c, *conv_args, ctx_len)
            yb_c = _attn_context(q_c, k_c, v_c, attn_sink[l], ctx_len)
            h1c, pay_c, route_tc = _merge(ya_c, yb_c, yc_c, bg_c, hc, mod_l, ln_in, ln1, w_a, w_b, w_c, w_o, w_rt,
                                          b_r, ctx_len, ctx_row, pre_ln, alpha)
            hc = _moe(pay_c, route_tc, h1c, mod_l, ln2, w_rt, w_gu, w_dn, ctx_len, ctx_row, alpha)
    return h.reshape(batch, seq_len, d)
```

```python
import functools

import numpy as np
import jax
import jax.numpy as jnp
from jax import lax
from jax.experimental import pallas as pl
from jax.experimental.pallas import tpu as pltpu

GRID_W = 64
LN_EPS = 1e-5
D_CONV = 512
CONV_WIDTH = 31
N_Q_HEADS = 8
N_KV_HEADS = 2
HEAD_DIM = 64
WINDOW = 128
BLOCK = 128
ROPE_BASE = 10000.0
D_ATTN = N_Q_HEADS * HEAD_DIM
D_KV = N_KV_HEADS * HEAD_DIM
N_MLSTM_HEADS = 4
MLSTM_HEAD_DIM = 128
D_MLSTM = N_MLSTM_HEADS * MLSTM_HEAD_DIM
N_GATE_COLS = 2 * 2 * N_MLSTM_HEADS
N_BRANCHES = 3
N_EXPERTS = 16
N_GROUPS = 4
EXPERTS_PER_GROUP = N_EXPERTS // N_GROUPS
D_EXPERT = 512
N_MOD = 6

LANES = 128
V7X_VMEM_LIMIT_BYTES = 56 * 1024 * 1024

MOD_ROWS = 16
MOD_COL_BLOCK = 512
TOKEN_TILE = 512
EXPERT_TILE = 256
GROUP_COPY_PARTS = 4
FINAL_TILE = 512
MLSTM_CHUNK = 128
MLSTM_HEADS_PER_STEP = 4
MERGE_TILE = 512
CONV_ROWS = 64
CONV_PAD = 16

_PAIRS = [(i, j) for i in range(EXPERTS_PER_GROUP) for j in range(i + 1, EXPERTS_PER_GROUP)]
N_CLASSES = N_GROUPS * len(_PAIRS)
N_CLASS_ROWS = 32
AUX_CLS, AUX_W_LO, AUX_W_HI, AUX_RANK = 0, 1, 2, 3
TOKEN_SUBROWS = 1024 // LANES

F32 = jnp.float32
BF16 = jnp.bfloat16
HIGHEST = lax.Precision.HIGHEST
NEG_INF = float("-inf")

_C_A = 0
_C_Q = _C_A + 2 * D_CONV
_C_QS = _C_Q + D_ATTN
_C_K = _C_QS + D_ATTN
_C_KS = _C_K + D_KV
_C_MQ = _C_KS + D_KV
_C_MV = _C_MQ + D_MLSTM
_C_MO = _C_MV + D_MLSTM
_C_BG = _C_MO + D_MLSTM
_C_END = _C_BG + N_BRANCHES * 1024
_R_KT = 0
_R_GT = _R_KT + D_MLSTM
_R_VT = _R_GT + N_GATE_COLS


def _cparams(*sem):
    return pltpu.CompilerParams(dimension_semantics=sem, vmem_limit_bytes=V7X_VMEM_LIMIT_BYTES)


def _ln(x, g, b):
    mu = jnp.mean(x, axis=-1, keepdims=True)
    xc = x - mu
    var = jnp.mean(xc * xc, axis=-1, keepdims=True)
    return xc * lax.rsqrt(var + LN_EPS) * g + b


def _sigmoid(x):
    return 0.5 * jnp.tanh(0.5 * x) + 0.5


def _log_sigmoid(x):
    return jnp.minimum(x, 0.0) - jnp.log(1.0 + jnp.exp(-jnp.abs(x)))


def _dot_nt(a, b, precision=None):
    return lax.dot_general(a, b, (((1,), (1,)), ((), ())), preferred_element_type=F32, precision=precision)


def _mod_kernel(c_ref, w_ref, b_ref, o_ref):
    c = c_ref[...]
    s = c * _sigmoid(c)
    o_ref[...] = jnp.dot(s, w_ref[...], preferred_element_type=F32, precision=HIGHEST) + b_ref[...]


def _modulation(cc, w_mod, b_mod):
    depth, d, n = w_mod.shape
    return pl.pallas_call(
        _mod_kernel,
        grid=(depth, n // MOD_COL_BLOCK),
        in_specs=[
            pl.BlockSpec((MOD_ROWS, d), lambda l, j: (0, 0)),
            pl.BlockSpec((None, d, MOD_COL_BLOCK), lambda l, j: (l, 0, j)),
            pl.BlockSpec((None, 1, MOD_COL_BLOCK), lambda l, j: (l, 0, j)),
        ],
        out_specs=pl.BlockSpec((None, MOD_ROWS, MOD_COL_BLOCK), lambda l, j: (l, 0, j)),
        out_shape=jax.ShapeDtypeStruct((depth, MOD_ROWS, n), F32),
        compiler_params=_cparams("parallel", "parallel"),
        name="modulation",
    )(cc, w_mod, b_mod.reshape(depth, 1, n))


def _mod_spec(which, tile, seq_len, ctx_row):
    tiles_per_seq = seq_len // tile
    if ctx_row is None:
        return pl.BlockSpec((None, 1, 1024), lambda i, *_: ((i // tiles_per_seq) * N_MOD + which, 0, 0))
    return pl.BlockSpec((None, 1, 1024), lambda i, *_: (ctx_row * N_MOD + which, 0, 0))


def _in_kernel(pre_ln, x_ref, lg_ref, lb_ref, sc_ref, sh_ref, w_ref, wt_ref, cq_ref, sq_ref, ck_ref, sk_ref,
               a_ref, q_ref, k_ref, v_ref, mq_ref, mv_ref, mo_ref, bg_ref, kt_ref, gt_ref):
    x = x_ref[...]
    if pre_ln:
        x = _ln(x, lg_ref[...], lb_ref[...])
    u = (x * (1.0 + sc_ref[...]) + sh_ref[...]).astype(BF16)

    def seg(lo, hi):
        return jnp.dot(u, w_ref[:, lo:hi], preferred_element_type=F32)

    a_ref[...] = seg(_C_A, _C_Q).astype(BF16)
    q_ref[...] = (seg(_C_Q, _C_QS) * cq_ref[...] + seg(_C_QS, _C_K) * sq_ref[...]).astype(BF16)
    k_ref[...] = (seg(_C_K, _C_KS) * ck_ref[...] + seg(_C_KS, _C_MQ) * sk_ref[...]).astype(BF16)
    v_ref[...] = _dot_nt(wt_ref[_R_VT:_R_VT + D_KV, :], u).astype(BF16)
    mq_ref[...] = seg(_C_MQ, _C_MV).astype(BF16)
    mv_ref[...] = seg(_C_MV, _C_MO).astype(BF16)
    mo_ref[...] = seg(_C_MO, _C_BG).astype(BF16)
    for j in range(N_BRANCHES):
        bg_ref[:, j * 1024:(j + 1) * 1024] = _sigmoid(seg(_C_BG + j * 1024, _C_BG + (j + 1) * 1024)).astype(BF16)
    n_chunks = u.shape[0] // MLSTM_CHUNK
    kt = _dot_nt(wt_ref[_R_KT:_R_KT + D_MLSTM, :], u)
    for h in range(N_MLSTM_HEADS):
        for c in range(n_chunks):
            kt_ref[h, c] = kt[h * MLSTM_HEAD_DIM:(h + 1) * MLSTM_HEAD_DIM,
                              c * MLSTM_CHUNK:(c + 1) * MLSTM_CHUNK].astype(BF16)
    gt = _dot_nt(wt_ref[_R_GT:_R_GT + N_GATE_COLS, :], u)
    for h in range(N_MLSTM_HEADS):
        for c in range(n_chunks):
            gt_ref[h, c] = gt[h * 4:(h + 1) * 4, c * MLSTM_CHUNK:(c + 1) * MLSTM_CHUNK]


def _in_proj(x, ln_g, ln_b, mod_l, w_main, w_t, rope, seq_len, ctx_row, pre_ln):
    n = x.shape[0]
    tm = min(TOKEN_TILE, seq_len)
    cq, sq, ck, sk = rope
    tps = seq_len // tm
    nch = n // MLSTM_CHUNK
    row = lambda i: (i, 0)
    pos = lambda i: (i % tps, 0)
    const = lambda i: (0, 0)
    out_shape = [
        jax.ShapeDtypeStruct((n, 2 * D_CONV), BF16),
        jax.ShapeDtypeStruct((n, D_ATTN), BF16),
        jax.ShapeDtypeStruct((n, D_KV), BF16),
        jax.ShapeDtypeStruct((D_KV, n), BF16),
        jax.ShapeDtypeStruct((n, D_MLSTM), BF16),
        jax.ShapeDtypeStruct((n, D_MLSTM), BF16),
        jax.ShapeDtypeStruct((n, D_MLSTM), BF16),
        jax.ShapeDtypeStruct((n, N_BRANCHES * 1024), BF16),
        jax.ShapeDtypeStruct((N_MLSTM_HEADS, nch, MLSTM_HEAD_DIM, MLSTM_CHUNK), BF16),
        jax.ShapeDtypeStruct((N_MLSTM_HEADS, nch, 4, MLSTM_CHUNK), F32),
    ]
    cpt = tm // MLSTM_CHUNK
    out_specs = [
        pl.BlockSpec((tm, 2 * D_CONV), row),
        pl.BlockSpec((tm, D_ATTN), row),
        pl.BlockSpec((tm, D_KV), row),
        pl.BlockSpec((D_KV, tm), lambda i: (0, i)),
        pl.BlockSpec((tm, D_MLSTM), row),
        pl.BlockSpec((tm, D_MLSTM), row),
        pl.BlockSpec((tm, D_MLSTM), row),
        pl.BlockSpec((tm, N_BRANCHES * 1024), row),
        pl.BlockSpec((N_MLSTM_HEADS, cpt, MLSTM_HEAD_DIM, MLSTM_CHUNK), lambda i: (0, i, 0, 0)),
        pl.BlockSpec((N_MLSTM_HEADS, cpt, 4, MLSTM_CHUNK), lambda i: (0, i, 0, 0)),
    ]
    in_specs = [
        pl.BlockSpec((tm, 1024), row),
        pl.BlockSpec((1, 1024), const),
        pl.BlockSpec((1, 1024), const),
        _mod_spec(1, tm, seq_len, ctx_row),
        _mod_spec(0, tm, seq_len, ctx_row),
        pl.BlockSpec(w_main.shape, const, pipeline_mode=pl.Buffered(1)),
        pl.BlockSpec(w_t.shape, const, pipeline_mode=pl.Buffered(1)),
        pl.BlockSpec((tm, D_ATTN), pos),
        pl.BlockSpec((tm, D_ATTN), pos),
        pl.BlockSpec((tm, D_KV), pos),
        pl.BlockSpec((tm, D_KV), pos),
    ]
    return pl.pallas_call(
        functools.partial(_in_kernel, pre_ln),
        grid=(n // tm,),
        in_specs=in_specs,
        out_specs=out_specs,
        out_shape=out_shape,
        compiler_params=_cparams("parallel"),
        name="in_proj",
    )(x, ln_g, ln_b, mod_l, mod_l, w_main, w_t, cq, sq, ck, sk)


def _conv_kernel(a_ref, w_ref, cb_ref, g_ref, b_ref, o_ref, upad_ref):
    t = a_ref.shape[0]
    zeros = jnp.zeros((CONV_PAD, D_CONV), F32)
    upad_ref[0:CONV_PAD, :] = zeros
    upad_ref[CONV_PAD + t:2 * CONV_PAD + t, :] = zeros
    val = a_ref[:, 0:D_CONV].astype(F32)
    gate = a_ref[:, D_CONV:2 * D_CONV].astype(F32)
    upad_ref[CONV_PAD:CONV_PAD + t, :] = val * _sigmoid(gate)
    half = CONV_WIDTH // 2

    def body(c, carry):
        r0 = pl.multiple_of(c * CONV_ROWS, CONV_ROWS)
        n_win = CONV_ROWS + 2 * CONV_PAD
        win = upad_ref[pl.ds(r0, n_win), :]
        acc = jnp.zeros((CONV_ROWS, D_CONV), F32) + cb_ref[...]
        for res in range(8):
            rolled = win if res == 0 else pltpu.roll(win, shift=n_win - res, axis=0)
            for k in range(CONV_WIDTH):
                off = CONV_PAD - half + k
                if off % 8 == res:
                    acc = acc + rolled[off - res:off - res + CONV_ROWS, :] * w_ref[k:k + 1, :]
        y = _ln(acc, g_ref[...], b_ref[...])
        o_ref[pl.ds(r0, CONV_ROWS), :] = (y * _sigmoid(y)).astype(BF16)
        return carry

    lax.fori_loop(0, t // CONV_ROWS, body, 0)


def _conv_branch(a_in, conv_w, conv_b, ln_g, ln_b, seq_len):
    n = a_in.shape[0]
    const = lambda b: (0, 0)
    return pl.pallas_call(
        _conv_kernel,
        grid=(n // seq_len,),
        in_specs=[
            pl.BlockSpec((seq_len, 2 * D_CONV), lambda b: (b, 0)),
            pl.BlockSpec((CONV_WIDTH, D_CONV), const),
            pl.BlockSpec((1, D_CONV), const),
            pl.BlockSpec((1, D_CONV), const),
            pl.BlockSpec((1, D_CONV), const),
        ],
        out_specs=pl.BlockSpec((seq_len, D_CONV), lambda b: (b, 0)),
        out_shape=jax.ShapeDtypeStruct((n, D_CONV), BF16),
        scratch_shapes=[pltpu.VMEM((seq_len + 2 * CONV_PAD, D_CONV), F32)],
        compiler_params=_cparams("parallel"),
        name="conv_branch",
    )(a_in, conv_w, conv_b, ln_g, ln_b)


def _attn_heads(q, keys, vals_t, masks, sink_ref, o_ref):
    rows = q.shape[0]
    group = N_Q_HEADS // N_KV_HEADS
    for hk in range(N_KV_HEADS):
        lo = hk * HEAD_DIM
        qs = jnp.concatenate([q[:, (hk * group + g) * HEAD_DIM:(hk * group + g + 1) * HEAD_DIM]
                              for g in range(group)], axis=0)
        sink = jnp.concatenate([jnp.full((1, rows), sink_ref[hk * group + g], F32) for g in range(group)], axis=1)
        scores = []
        m = sink
        for kk, mask in zip(keys, masks):
            s = _dot_nt(kk[:, lo:lo + HEAD_DIM], qs)
            if mask is not None:
                s = jnp.where(mask, s, NEG_INF)
            scores.append(s)
            m = jnp.maximum(m, jnp.max(s, axis=0, keepdims=True))
        acc = jnp.zeros((2 * HEAD_DIM, rows * group), F32)
        for s, vt in zip(scores, vals_t):
            n_k = s.shape[0]
            p = jnp.exp(s - m).astype(BF16)
            ones_rows = jnp.where(lax.broadcasted_iota(jnp.int32, (HEAD_DIM, n_k), 0) == 0, 1.0, 0.0).astype(BF16)
            v_aug = jnp.concatenate([vt[lo:lo + HEAD_DIM, :], ones_rows], axis=0)
            acc = acc + jnp.dot(v_aug, p, preferred_element_type=F32)
        denom = acc[HEAD_DIM:HEAD_DIM + 1, :] + jnp.exp(sink - m)
        o_t = acc * (1.0 / denom)
        for g in range(group):
            col = (hk * group + g) * HEAD_DIM
            o_ref[:, col:col + HEAD_DIM] = o_t[:, g * rows:(g + 1) * rows].T[:, 0:HEAD_DIM].astype(BF16)


def _attn_lat_kernel(sink_ref, q_ref, kp_ref, k0_ref, kn_ref, vp_ref, v0_ref, vn_ref, kc_ref, vc_ref, o_ref):
    n = pl.program_id(1)
    nb = pl.num_programs(1)
    stacked = (N_Q_HEADS // N_KV_HEADS) * BLOCK
    ki = lax.broadcasted_iota(jnp.int32, (BLOCK, stacked), 0)
    qi = lax.broadcasted_iota(jnp.int32, (BLOCK, stacked), 1) % BLOCK
    mask_prev = ki >= qi + jnp.where(n > 0, 0, BLOCK)
    mask_next = ki <= qi - jnp.where(n < nb - 1, 0, BLOCK)
    _attn_heads(q_ref[...],
                [kp_ref[...], k0_ref[...], kn_ref[...], kc_ref[...]],
                [vp_ref[...], v0_ref[...], vn_ref[...], vc_ref[...]],
                [mask_prev, None, mask_next, None], sink_ref, o_ref)


def _attn_ctx_kernel(sink_ref, q_ref, kc_ref, vc_ref, o_ref):
    _attn_heads(q_ref[...], [kc_ref[...]], [vc_ref[...]], [None], sink_ref, o_ref)


def _attn_latent(q, k, vt, kc, vct, sink, seq_len, ctx_len):
    n = q.shape[0]
    nb = seq_len // BLOCK
    batch = n // seq_len
    blk_prev = lambda b, j: b * nb + jnp.maximum(j - 1, 0)
    blk_next = lambda b, j: b * nb + jnp.minimum(j + 1, nb - 1)
    cur = lambda b, j: (b * nb + j, 0)
    kspec = lambda blk: pl.BlockSpec((BLOCK, D_KV), lambda b, j: (blk(b, j), 0))
    vspec = lambda blk: pl.BlockSpec((D_KV, BLOCK), lambda b, j: (0, blk(b, j)))
    blk_cur = lambda b, j: b * nb + j
    return pl.pallas_call(
        _attn_lat_kernel,
        grid=(batch, nb),
        in_specs=[
            pl.BlockSpec(memory_space=pltpu.SMEM),
            pl.BlockSpec((BLOCK, D_ATTN), cur),
            kspec(blk_prev), kspec(blk_cur), kspec(blk_next), vspec(blk_prev), vspec(blk_cur), vspec(blk_next),
            pl.BlockSpec((ctx_len, D_KV), lambda b, j: (b, 0)),
            pl.BlockSpec((D_KV, ctx_len), lambda b, j: (0, b)),
        ],
        out_specs=pl.BlockSpec((BLOCK, D_ATTN), cur),
        out_shape=jax.ShapeDtypeStruct((n, D_ATTN), BF16),
        compiler_params=_cparams("parallel", "parallel"),
        name="attn_latent",
    )(sink, q, k, k, k, vt, vt, vt, kc, vct)


def _attn_context(qc, kc, vct, sink, ctx_len):
    n = qc.shape[0]
    blk = lambda b: (b, 0)
    return pl.pallas_call(
        _attn_ctx_kernel,
        grid=(n // ctx_len,),
        in_specs=[
            pl.BlockSpec(memory_space=pltpu.SMEM),
            pl.BlockSpec((ctx_len, D_ATTN), blk),
            pl.BlockSpec((ctx_len, D_KV), blk),
            pl.BlockSpec((D_KV, ctx_len), lambda b: (0, b)),
        ],
        out_specs=pl.BlockSpec((ctx_len, D_ATTN), blk),
        out_shape=jax.ShapeDtypeStruct((n, D_ATTN), BF16),
        compiler_params=_cparams("parallel"),
        name="attn_context",
    )(sink, qc, kc, vct)


def _mlstm_kernel(ctx_out, nch_c, nch_l, hps,
                  qc_ref, ktc_ref, vc_ref, moc_ref, gc_ref,
                  ql_ref, ktl_ref, vl_ref, mol_ref, gl_ref,
                  gb_ref, ng_ref, *rest):
    if ctx_out:
        yl_ref, yc_ref, af_ref, lff_ref, ab_ref, lfb_ref, cf_ref, cb_ref, hf_ref, hb_ref = rest
    else:
        yl_ref, af_ref, lff_ref, ab_ref, lfb_ref, cf_ref, cb_ref, hf_ref, hb_ref = rest
        yc_ref = None
    lc = MLSTM_CHUNK
    dh = MLSTM_HEAD_DIM
    k_scale = MLSTM_HEAD_DIM ** -0.5
    ti = lax.broadcasted_iota(jnp.int32, (lc, lc), 0)
    si = lax.broadcasted_iota(jnp.int32, (lc, lc), 1)
    lower = si <= ti
    upper = si >= ti
    pre_mat = jnp.where(upper, 1.0, 0.0)
    suf_mat = jnp.where(lower, 1.0, 0.0)
    n_rows = nch_c + nch_l
    pad_rows = -n_rows % 8

    for hh in range(hps):
        gb = gb_ref[hh]

        def gate_rows(kind):
            rows = ([gc_ref[hh, c, kind:kind + 1, :] for c in range(nch_c)]
                    + [gl_ref[hh, c, kind:kind + 1, :] for c in range(nch_l)])
            rows = jnp.concatenate(rows, axis=0) + gb[kind:kind + 1, :]
            if pad_rows:
                rows = jnp.concatenate([rows, jnp.zeros((pad_rows, lc), F32)], axis=0)
            return rows

        lf_f = _log_sigmoid(gate_rows(1))
        lf_b = _log_sigmoid(gate_rows(3))
        a_f = gate_rows(0) - jnp.dot(lf_f, pre_mat, preferred_element_type=F32, precision=HIGHEST)
        a_b = gate_rows(2) - jnp.dot(lf_b, suf_mat, preferred_element_type=F32, precision=HIGHEST)
        for c in range(n_rows):
            af_ref[hh * n_rows + c] = a_f[c:c + 1, :]
            lff_ref[hh * n_rows + c] = lf_f[c:c + 1, :]
            ab_ref[hh * n_rows + c] = a_b[c:c + 1, :]
            lfb_ref[hh * n_rows + c] = lf_b[c:c + 1, :]

    cf_ref[...] = jnp.zeros_like(cf_ref)
    cb_ref[...] = jnp.zeros_like(cb_ref)
    ones_col = jnp.where(lax.broadcasted_iota(jnp.int32, (lc, dh), 1) == 0, 1.0, 0.0).astype(BF16)

    def chunk(q, kt, v, a_row, lf_row, c_ref, hh, m, mask):
        a_mat = jnp.where(mask, jnp.broadcast_to(a_row, (lc, lc)), NEG_INF)
        cm = jnp.max(a_mat, axis=1, keepdims=True)
        b_col = jnp.sum(jnp.where(mask, jnp.broadcast_to(lf_row, (lc, lc)), 0.0), axis=1, keepdims=True)
        mx = jnp.maximum(m, cm)
        mx_last = jnp.maximum(m, jnp.max(a_row, axis=1, keepdims=True))
        w = jnp.exp(a_mat - mx)
        s = jnp.dot(q, kt, preferred_element_type=F32) * k_scale
        p = (s * w).astype(BF16)
        w_s = jnp.exp(a_row - mx_last) * k_scale
        ktw = (kt.astype(F32) * w_s).astype(BF16)
        v_aug = jnp.concatenate([v, ones_col], axis=1)
        both = jnp.dot(jnp.concatenate([p, ktw], axis=0), v_aug, preferred_element_type=F32)
        c_old = c_ref[hh]
        inter = jnp.dot(q, c_old.astype(BF16), preferred_element_type=F32)
        tot = both[0:lc, :] + jnp.exp(m - mx) * inter
        den = tot[:, dh:dh + 1]
        h = tot[:, 0:dh] / jnp.maximum(jnp.abs(den), jnp.exp(-(b_col + mx)))
        c_ref[hh] = jnp.exp(m - mx_last) * c_old + both[lc:lc + dh, :]
        m_new = jnp.sum(lf_row, axis=1, keepdims=True) + mx_last
        return h, m_new

    def step(q_ref, kt_ref, v_ref, row0, c_f, c_b, rf, rb, ms):
        out = []
        for hh in range(hps):
            cols = slice(hh * dh, (hh + 1) * dh)
            base = hh * n_rows + row0
            h_f, m_f = chunk(q_ref[pl.ds(rf, lc), cols], kt_ref[hh, c_f], v_ref[pl.ds(rf, lc), cols],
                             af_ref[base + c_f], lff_ref[base + c_f], cf_ref, hh, ms[2 * hh], lower)
            h_b, m_b = chunk(q_ref[pl.ds(rb, lc), cols], kt_ref[hh, c_b], v_ref[pl.ds(rb, lc), cols],
                             ab_ref[base + c_b], lfb_ref[base + c_b], cb_ref, hh, ms[2 * hh + 1], upper)
            hf_ref[pl.ds(rf, lc), cols] = h_f
            hb_ref[pl.ds(rb, lc), cols] = h_b
            out += [m_f, m_b]
        return tuple(out)

    def finish(n_tok, mo_ref, o_ref):
        for hh in range(hps):
            cols = slice(hh * dh, (hh + 1) * dh)
            h = hf_ref[0:n_tok, cols] + hb_ref[0:n_tok, cols]
            mu = jnp.mean(h, axis=-1, keepdims=True)
            hc = h - mu
            var = jnp.mean(hc * hc, axis=-1, keepdims=True)
            y = hc * lax.rsqrt(var + LN_EPS) * ng_ref[:, cols]
            o_ref[:, cols] = (_sigmoid(mo_ref[:, cols].astype(F32)) * y).astype(BF16)

    ms = tuple(jnp.zeros((1, 1), F32) for _ in range(2 * hps))
    for c in range(nch_c):
        c_b = nch_c - 1 - c
        ms = step(qc_ref, ktc_ref, vc_ref, 0, c, c_b, c * lc, c_b * lc, ms)
    if ctx_out:
        finish(nch_c * lc, moc_ref, yc_ref)

    def body(c, ms):
        c_b = nch_l - 1 - c
        return step(ql_ref, ktl_ref, vl_ref, nch_c, c, c_b,
                    pl.multiple_of(c * lc, lc), pl.multiple_of(c_b * lc, lc), ms)

    lax.fori_loop(0, nch_l, body, ms)
    finish(nch_l * lc, mol_ref, yl_ref)


def _mlstm_branch(ctx_p, lat_p, gate_b, norm_g, seq_len, ctx_len, ctx_out):
    mq_c, kt_c, mv_c, mo_c, g_c = ctx_p
    mq_l, kt_l, mv_l, mo_l, g_l = lat_p
    n_l, n_c = mq_l.shape[0], mq_c.shape[0]
    batch = n_l // seq_len
    lc, dh, hps = MLSTM_CHUNK, MLSTM_HEAD_DIM, MLSTM_HEADS_PER_STEP
    nch_c, nch_l = ctx_len // lc, seq_len // lc

    def stream(t, nch):
        tok = pl.BlockSpec((t, hps * dh), lambda b, h: (b, h))
        return [tok,
                pl.BlockSpec((hps, nch, dh, lc), lambda b, h: (h, b, 0, 0)),
                tok, tok,
                pl.BlockSpec((hps, nch, 4, lc), lambda b, h: (h, b, 0, 0))]

    in_specs = stream(ctx_len, nch_c) + stream(seq_len, nch_l) + [
        pl.BlockSpec((hps, 4, 1), lambda b, h: (h, 0, 0)),
        pl.BlockSpec((1, hps * dh), lambda b, h: (0, h)),
    ]
    out_specs = [pl.BlockSpec((seq_len, hps * dh), lambda b, h: (b, h))]
    out_shape = [jax.ShapeDtypeStruct((n_l, D_MLSTM), BF16)]
    if ctx_out:
        out_specs.append(pl.BlockSpec((ctx_len, hps * dh), lambda b, h: (b, h)))
        out_shape.append(jax.ShapeDtypeStruct((n_c, D_MLSTM), BF16))
    row_scratch = pltpu.VMEM((hps * (nch_c + nch_l), 1, lc), F32)
    outs = pl.pallas_call(
        functools.partial(_mlstm_kernel, ctx_out, nch_c, nch_l, hps),
        grid=(batch, N_MLSTM_HEADS // hps),
        in_specs=in_specs,
        out_specs=out_specs,
        out_shape=out_shape,
        scratch_shapes=[row_scratch, row_scratch, row_scratch, row_scratch,
                        pltpu.VMEM((hps, dh, 2 * dh), F32), pltpu.VMEM((hps, dh, 2 * dh), F32),
                        pltpu.VMEM((seq_len, hps * dh), F32), pltpu.VMEM((seq_len, hps * dh), F32)],
        compiler_params=_cparams("parallel", "parallel"),
        name="mlstm_branch",
    )(mq_c, kt_c, mv_c, mo_c, g_c, mq_l, kt_l, mv_l, mo_l, g_l, gate_b, norm_g)
    return outs if ctx_out else (outs[0], None)


def _route(logits_t, br):
    sc = [_sigmoid(logits_t[e:e + 1, :]) for e in range(N_EXPERTS)]
    sel = [sc[e] + br[e:e + 1, :] for e in range(N_EXPERTS)]
    epg = EXPERTS_PER_GROUP
    group_score = []
    for g in range(N_GROUPS):
        v = sel[g * epg:(g + 1) * epg]
        best = None
        for i in range(epg):
            for j in range(i + 1, epg):
                pair = v[i] + v[j]
                best = pair if best is None else jnp.maximum(best, pair)
        group_score.append(best)
    g_idx = jnp.zeros_like(group_score[0], dtype=jnp.int32)
    best = group_score[0]
    for g in range(1, N_GROUPS):
        better = group_score[g] > best
        g_idx = jnp.where(better, g, g_idx)
        best = jnp.maximum(best, group_score[g])
    chosen = []
    for g in range(N_GROUPS):
        v = sel[g * epg:(g + 1) * epg]
        in_g = g_idx == g
        for i in range(epg):
            rank = jnp.zeros_like(g_idx)
            for j in range(epg):
                if j == i:
                    continue
                ahead = (v[j] >= v[i]) if j < i else (v[j] > v[i])
                rank = rank + jnp.where(ahead, 1, 0)
            chosen.append(in_g & (rank < 2))
    cls = jnp.zeros_like(sc[0])
    w_lo = jnp.zeros_like(sc[0])
    w_hi = jnp.zeros_like(sc[0])
    for g in range(N_GROUPS):
        for pid, (i, j) in enumerate(_PAIRS):
            lo, hi = g * epg + i, g * epg + j
            is_pair = chosen[lo] & chosen[hi]
            cls = jnp.where(is_pair, float(g * len(_PAIRS) + pid), cls)
            w_lo = jnp.where(is_pair, sc[lo], w_lo)
            w_hi = jnp.where(is_pair, sc[hi], w_hi)
    total = w_lo + w_hi
    return cls, w_lo / total, w_hi / total


def _merge_kernel(pre_ln, alpha, ya_ref, yb_ref, yc_ref, bg_ref, h_ref, g1_ref, sc2_ref, sh2_ref,
                  lig_ref, lib_ref, l1g_ref, l1b_ref, wa_ref, wb_ref, wc_ref, wo_ref, wr_ref, br_ref,
                  h1_ref, u2t_ref, route_ref, cnt_ref, tri_ref):
    tm = h_ref.shape[0]

    def branch(y_ref, w_ref, j):
        gate = bg_ref[:, j * 1024:(j + 1) * 1024].astype(F32)
        return gate * jnp.dot(y_ref[...], w_ref[...], preferred_element_type=F32)

    mix = branch(ya_ref, wa_ref, 0) + branch(yb_ref, wb_ref, 1) + branch(yc_ref, wc_ref, 2)
    y = jnp.dot(mix.astype(BF16), wo_ref[...], preferred_element_type=F32)
    h = h_ref[...]
    if pre_ln:
        h = _ln(h, lig_ref[...], lib_ref[...])
    h1 = _ln(alpha * h + g1_ref[...] * y, l1g_ref[...], l1b_ref[...])
    h1_ref[...] = h1
    u2 = h1 * (1.0 + sc2_ref[...]) + sh2_ref[...]
    logits_t = _dot_nt(wr_ref[...], u2, precision=HIGHEST)
    cls, w_lo, w_hi = _route(logits_t, br_ref[...])

    @pl.when(pl.program_id(0) == 0)
    def _():
        cnt_ref[...] = jnp.zeros_like(cnt_ref)
        earlier = lax.broadcasted_iota(jnp.int32, (tm, tm), 0) <= lax.broadcasted_iota(jnp.int32, (tm, tm), 1)
        tri_ref[...] = jnp.where(earlier, 1.0, 0.0).astype(BF16)

    crow = lax.broadcasted_iota(jnp.int32, (N_CLASS_ROWS, tm), 0).astype(F32)
    onehot = jnp.where(crow == cls, 1.0, 0.0)
    incl = jnp.dot(onehot.astype(BF16), tri_ref[...], preferred_element_type=F32)
    base = cnt_ref[...]
    rank = jnp.sum(onehot * (incl - 1.0 + base), axis=0, keepdims=True)
    cnt_ref[...] = base + incl[:, tm - 1:tm]

    route_ref[...] = jnp.concatenate([cls, w_lo, w_hi, rank, jnp.zeros((4, tm), F32)], axis=0)
    for s in range(TOKEN_SUBROWS):
        u2t_ref[pl.ds(s, tm, stride=TOKEN_SUBROWS), :] = u2[:, s * LANES:(s + 1) * LANES]


def _merge(ya, yb, yc, bg, h, mod_l, ln_in, ln1, w_a, w_b, w_c, w_o, w_rt, b_r, seq_len, ctx_row, pre_ln, alpha):
    n = ya.shape[0]
    tm = MERGE_TILE
    row = lambda i: (i, 0)
    const = lambda i: (0, 0)
    vec = pl.BlockSpec((1, 1024), const)
    wspec = lambda w: pl.BlockSpec(w.shape, const)
    return pl.pallas_call(
        functools.partial(_merge_kernel, pre_ln, alpha),
        grid=(n // tm,),
        in_specs=[
            pl.BlockSpec((tm, D_CONV), row),
            pl.BlockSpec((tm, D_ATTN), row),
            pl.BlockSpec((tm, D_MLSTM), row),
            pl.BlockSpec((tm, N_BRANCHES * 1024), row),
            pl.BlockSpec((tm, 1024), row),
            _mod_spec(2, tm, seq_len, ctx_row),
            _mod_spec(4, tm, seq_len, ctx_row),
            _mod_spec(3, tm, seq_len, ctx_row),
            vec, vec, vec, vec,
            wspec(w_a), wspec(w_b), wspec(w_c), wspec(w_o), wspec(w_rt), wspec(b_r),
        ],
        out_specs=[
            pl.BlockSpec((tm, 1024), row),
            pl.BlockSpec((tm * TOKEN_SUBROWS, LANES), row),
            pl.BlockSpec((8, tm), lambda i: (0, i)),
        ],
        out_shape=[
            jax.ShapeDtypeStruct((n, 1024), F32),
            jax.ShapeDtypeStruct((n * TOKEN_SUBROWS, LANES), F32),
            jax.ShapeDtypeStruct((8, n), F32),
        ],
        scratch_shapes=[pltpu.VMEM((N_CLASS_ROWS, 1), F32), pltpu.VMEM((tm, tm), BF16)],
        compiler_params=_cparams("arbitrary"),
        name="merge",
    )(ya, yb, yc, bg, h, mod_l, mod_l, mod_l, ln_in[0], ln_in[1], ln1[0], ln1[1], w_a, w_b, w_c, w_o, w_rt, b_r)


class _TokenGather:
    def __init__(self, idx_ref, base, src_hbm, buf, sem, slot):
        self.idx_ref, self.base, self.src_hbm, self.buf, self.sem, self.slot = idx_ref, base, src_hbm, buf, sem, slot
        self.tokens = buf.shape[1] // TOKEN_SUBROWS
        self.last = idx_ref.shape[0] - 1

    def _copy(self, k):
        sub = TOKEN_SUBROWS
        p = pl.multiple_of(self.idx_ref[jnp.minimum(self.base + k, self.last)] * sub, sub)
        return pltpu.make_async_copy(self.src_hbm.at[pl.ds(p, sub)], self.buf.at[self.slot, pl.ds(k * sub, sub)],
                                     self.sem.at[self.slot])

    def _share(self, part, parts):
        share = self.tokens // parts
        return range(part * share, (part + 1) * share)

    def start(self, part=0, parts=1):
        for k in self._share(part, parts):
            self._copy(k).start(priority=k % 2)

    def wait(self, part=0, parts=1):
        for k in self._share(part, parts):
            self._copy(k).wait()


def _untile_tokens(ref):
    tokens = ref.shape[0] // TOKEN_SUBROWS
    return jnp.concatenate([ref[pl.ds(s, tokens, stride=TOKEN_SUBROWS), :] for s in range(TOKEN_SUBROWS)], axis=1)


def _group_kernel(lo_ref, hi_ref, base_ref, cnt_ref, order_ref, x_hbm, wr_ref, xs_ref, gate_ref, buf, sem):
    i = pl.program_id(0)
    last = pl.num_programs(0) - 1
    slot = i % 2
    nxt = jnp.minimum(i + 1, last)
    gather = functools.partial(_TokenGather, order_ref, src_hbm=x_hbm, buf=buf, sem=sem)
    parts = GROUP_COPY_PARTS
    share = (buf.shape[1] // TOKEN_SUBROWS) // parts

    @pl.when(i == 0)
    def _():
        buf[...] = jnp.zeros_like(buf)

    for p in range(parts):
        @pl.when(jnp.logical_and(i == 0, cnt_ref[0] > p * share))
        def _():
            gather(base=base_ref[0], slot=0).start(p, parts)

    for p in range(parts):
        @pl.when(cnt_ref[i] > p * share)
        def _():
            gather(base=base_ref[i], slot=slot).wait(p, parts)

    for p in range(parts):
        @pl.when(cnt_ref[nxt] > p * share)
        def _():
            gather(base=base_ref[nxt], slot=1 - slot).start(p, parts)

    x32 = _untile_tokens(buf.at[slot])

    def affinity(e):
        logit = jnp.sum(x32 * wr_ref[pl.ds(e, 1), :], axis=1, keepdims=True)
        return _sigmoid(logit)

    s_lo, s_hi = affinity(lo_ref[i]), affinity(hi_ref[i])
    total = s_lo + s_hi
    xs_ref[...] = x32.astype(BF16)
    gate_ref[...] = jnp.concatenate([s_lo / total, s_hi / total], axis=1)

    for p in range(parts):
        @pl.when(jnp.logical_and(i == last, cnt_ref[nxt] > p * share))
        def _():
            gather(base=base_ref[nxt], slot=1 - slot).wait(p, parts)


def _group_tokens(lo, hi, base, cnt, order, u2t, w_rt):
    tm = EXPERT_TILE
    n_tiles = base.shape[0]
    row = lambda i, *_: (i, 0)
    grid_spec = pltpu.PrefetchScalarGridSpec(
        num_scalar_prefetch=5,
        grid=(n_tiles,),
        in_specs=[
            pl.BlockSpec(memory_space=pl.ANY),
            pl.BlockSpec(w_rt.shape, lambda i, *_: (0, 0)),
        ],
        out_specs=[pl.BlockSpec((tm, 1024), row), pl.BlockSpec((tm, 2), row)],
        scratch_shapes=[pltpu.VMEM((2, tm * TOKEN_SUBROWS, LANES), F32), pltpu.SemaphoreType.DMA((2,))],
    )
    return pl.pallas_call(
        _group_kernel,
        grid_spec=grid_spec,
        out_shape=[jax.ShapeDtypeStruct((n_tiles * tm, 1024), BF16), jax.ShapeDtypeStruct((n_tiles * tm, 2), F32)],
        compiler_params=_cparams("arbitrary"),
        name="group_tokens",
    )(lo, hi, base, cnt, order, u2t, w_rt)


def _experts_kernel(lo_ref, hi_ref, valid_ref, x_ref, gate_ref, wgl_ref, wgh_ref, wdl_ref, wdh_ref, o_ref):
    i = pl.program_id(0)
    tm = x_ref.shape[0]

    @pl.when(valid_ref[i] != 0)
    def _():
        x = x_ref[...]

        def expert(wg_ref, wd_ref):
            gu = jnp.dot(x, wg_ref[...].astype(BF16), preferred_element_type=F32)
            g_ = gu[:, 0:D_EXPERT]
            act = (g_ * _sigmoid(g_) * gu[:, D_EXPERT:2 * D_EXPERT]).astype(BF16)
            return jnp.dot(act, wd_ref[...].astype(BF16), preferred_element_type=F32)

        out = gate_ref[:, 0:1] * expert(wgl_ref, wdl_ref) + gate_ref[:, 1:2] * expert(wgh_ref, wdh_ref)
        for s in range(TOKEN_SUBROWS):
            o_ref[pl.ds(s, tm, stride=TOKEN_SUBROWS), :] = out[:, s * LANES:(s + 1) * LANES]

    @pl.when(valid_ref[i] == 0)
    def _():
        o_ref[...] = jnp.zeros_like(o_ref)


def _experts(lo, hi, valid, xs, gates, w_gu, w_dn, layer):
    tm = EXPERT_TILE
    n_tiles = xs.shape[0] // tm
    row = lambda i, *_: (i, 0)
    grid_spec = pltpu.PrefetchScalarGridSpec(
        num_scalar_prefetch=3,
        grid=(n_tiles,),
        in_specs=[
            pl.BlockSpec((tm, 1024), row),
            pl.BlockSpec((tm, 2), row),
            pl.BlockSpec((None, None, 1024, 2 * D_EXPERT), lambda i, lo, hi, *_: (layer, lo[i], 0, 0)),
            pl.BlockSpec((None, None, 1024, 2 * D_EXPERT), lambda i, lo, hi, *_: (layer, hi[i], 0, 0)),
            pl.BlockSpec((None, None, D_EXPERT, 1024), lambda i, lo, hi, *_: (layer, lo[i], 0, 0)),
            pl.BlockSpec((None, None, D_EXPERT, 1024), lambda i, lo, hi, *_: (layer, hi[i], 0, 0)),
        ],
        out_specs=pl.BlockSpec((tm * TOKEN_SUBROWS, LANES), row),
    )
    return pl.pallas_call(
        _experts_kernel,
        grid_spec=grid_spec,
        out_shape=jax.ShapeDtypeStruct((n_tiles * tm * TOKEN_SUBROWS, LANES), F32),
        compiler_params=_cparams("arbitrary"),
        name="moe_experts",
    )(lo, hi, valid, xs, gates, w_gu, w_gu, w_dn, w_dn)


def _final_kernel(alpha, pos_ref, f_hbm, h1_ref, g2_ref, lg_ref, lb_ref, o_ref, buf, sem):
    i = pl.program_id(0)
    last = pl.num_programs(0) - 1
    slot = i % 2
    nxt = jnp.minimum(i + 1, last)
    gather = functools.partial(_TokenGather, pos_ref, src_hbm=f_hbm, buf=buf, sem=sem)

    tm = o_ref.shape[0]

    @pl.when(i == 0)
    def _():
        gather(base=0, slot=0).start()

    gather(base=i * tm, slot=slot).wait()
    gather(base=nxt * tm, slot=1 - slot).start()
    f = _untile_tokens(buf.at[slot])
    o_ref[...] = _ln(alpha * h1_ref[...] + g2_ref[...] * f, lg_ref[...], lb_ref[...])

    @pl.when(i == last)
    def _():
        gather(base=nxt * tm, slot=1 - slot).wait()


def _final_ln(h1, pos, fs, mod_l, ln2, seq_len, ctx_row, alpha):
    n = h1.shape[0]
    tm = FINAL_TILE
    row = lambda i, *_: (i, 0)
    const = lambda i, *_: (0, 0)
    grid_spec = pltpu.PrefetchScalarGridSpec(
        num_scalar_prefetch=1,
        grid=(n // tm,),
        in_specs=[
            pl.BlockSpec(memory_space=pl.ANY),
            pl.BlockSpec((tm, 1024), row),
            _mod_spec(5, tm, seq_len, ctx_row),
            pl.BlockSpec((1, 1024), const),
            pl.BlockSpec((1, 1024), const),
        ],
        out_specs=pl.BlockSpec((tm, 1024), row),
        scratch_shapes=[pltpu.VMEM((2, tm * TOKEN_SUBROWS, LANES), F32), pltpu.SemaphoreType.DMA((2,))],
    )
    return pl.pallas_call(
        functools.partial(_final_kernel, alpha),
        grid_spec=grid_spec,
        out_shape=jax.ShapeDtypeStruct((n, 1024), F32),
        compiler_params=_cparams("arbitrary"),
        name="final_ln",
    )(pos, fs, h1, mod_l, ln2[0], ln2[1])


def _sort_plan(route_t, n):
    tm = EXPERT_TILE
    n_tiles = n // tm + N_CLASSES
    cls = route_t[AUX_CLS].astype(jnp.int32)
    rank = route_t[AUX_RANK].astype(jnp.int32)
    onehot = cls[:, None] == jnp.arange(N_CLASSES, dtype=jnp.int32)[None, :]
    counts = jnp.sum(onehot, axis=0, dtype=jnp.int32)
    padded = (counts + tm - 1) // tm * tm
    ends = jnp.cumsum(padded)
    offs = ends - padded
    pos = jnp.sum(jnp.where(onehot, offs[None, :], 0), axis=1) + rank
    tile_ends = ends // tm
    j = jnp.arange(n_tiles, dtype=jnp.int32)
    n_used = tile_ends[-1]
    valid = j < n_used
    tile_cls = jnp.sum(j[:, None] >= tile_ends[None, :], axis=1)
    last_cls = jnp.sum((n_used - 1) >= tile_ends)
    tile_cls = jnp.where(valid, tile_cls, last_cls)
    group, pid = tile_cls // len(_PAIRS), tile_cls % len(_PAIRS)
    pair = jnp.asarray(np.array(_PAIRS, dtype=np.int32))
    lo = group * EXPERTS_PER_GROUP + pair[pid, 0]
    hi = group * EXPERTS_PER_GROUP + pair[pid, 1]
    order = jnp.argsort(cls, stable=True).astype(jnp.int32)
    starts = jnp.cumsum(counts) - counts
    first = j * tm - offs[tile_cls]
    base = jnp.where(valid, starts[tile_cls] + first, 0)
    cnt = jnp.where(valid, jnp.clip(counts[tile_cls] - first, 0, tm), 0)
    i32 = lambda t: t.astype(jnp.int32)
    return i32(pos), order, i32(base), i32(cnt), i32(lo), i32(hi), i32(valid)


def _moe(u2t, route_t, h1, mod_l, ln2, w_rt, w_gu, w_dn, layer, seq_len, ctx_row, alpha):
    n = h1.shape[0]
    pos, order, base, cnt, lo, hi, valid = _sort_plan(route_t, n)
    xs, gates = _group_tokens(lo, hi, base, cnt, order, u2t, w_rt)
    fs = _experts(lo, hi, valid, xs, gates, w_gu, w_dn, layer)
    return _final_ln(h1, pos, fs, mod_l, ln2, seq_len, ctx_row, alpha)


def _rope_swap_index(n_heads):
    idx = np.arange(n_heads * HEAD_DIM)
    within = idx % (HEAD_DIM // 2)
    quarter = HEAD_DIM // 4
    return np.where(within < quarter, idx + quarter, idx - quarter)


def _rope_tables(seq_len, n_heads, scale):
    t = np.arange(seq_len)
    quarter = HEAD_DIM // 4
    inv = ROPE_BASE ** (-np.arange(quarter, dtype=np.float32) / quarter)
    d = np.arange(HEAD_DIM)
    pos = np.where((d // (HEAD_DIM // 2) == 0)[None, :], (t // GRID_W)[:, None], (t % GRID_W)[:, None])
    ang = jnp.asarray(pos.astype(np.float32)) * jnp.asarray(inv[d % quarter])[None, :]
    sign = np.where(d % (HEAD_DIM // 2) < quarter, -1.0, 1.0).astype(np.float32)
    cos = jnp.cos(ang) * scale
    sin = jnp.sin(ang) * (sign * scale)[None, :]
    return jnp.tile(cos, (1, n_heads)), jnp.tile(sin, (1, n_heads))


def _flat_tables(seq_len, n_heads, scale):
    return (jnp.full((seq_len, n_heads * HEAD_DIM), scale, F32), jnp.zeros((seq_len, n_heads * HEAD_DIM), F32))


def _prep_in_weights(w_in_l):
    splits = np.cumsum([2 * D_CONV, D_ATTN, D_KV, D_KV, D_MLSTM, D_MLSTM, D_MLSTM, D_MLSTM, N_GATE_COLS])
    a, q, k, v, mq, mk, mv, mo, mg, bg = jnp.split(w_in_l, splits, axis=1)
    w_main = jnp.concatenate([a, q, q[:, _rope_swap_index(N_Q_HEADS)], k, k[:, _rope_swap_index(N_KV_HEADS)],
                              mq, mv, mo, bg], axis=1).astype(BF16)
    order = np.array([d * 8 + kind * 4 + h for h in range(N_MLSTM_HEADS) for d in range(2) for kind in range(2)])
    w_t = jnp.concatenate([mk.T, mg[:, order].T, v.T], axis=0).astype(BF16)
    return w_main, w_t


def kernel(x, c, ctx, c_ctx, ln_in_g, ln_in_b, w_router, b_router, w_mod, b_mod, w_in, conv_w, conv_b, conv_ln_g,
           conv_ln_b, w_a_out, attn_sink, w_b_out, mlstm_gate_b, mlstm_norm_g, w_c_out, w_out, ln1_g, ln1_b,
           moe_w_gu, moe_w_dn, ln2_g, ln2_b):
    batch, seq_len, d = x.shape
    ctx_len = ctx.shape[1]
    depth = w_in.shape[0]
    alpha = (2.0 * depth) ** 0.25
    ctx_row = batch
    assert d == 1024 and batch < MOD_ROWS
    assert seq_len % MERGE_TILE == 0 and seq_len % TOKEN_TILE == 0 and (batch * ctx_len) % MERGE_TILE == 0
    assert ctx_len % MLSTM_CHUNK == 0 and ctx_len % 8 == 0
    assert seq_len % FINAL_TILE == 0 and (batch * ctx_len) % FINAL_TILE == 0 and FINAL_TILE % EXPERT_TILE == 0

    cc = jnp.zeros((MOD_ROWS, d), F32).at[0:batch].set(c).at[batch].set(c_ctx)
    mod = _modulation(cc, w_mod, b_mod).reshape(depth, MOD_ROWS * N_MOD, 1, d)

    attn_scale = HEAD_DIM ** -0.5
    rope_lat = _rope_tables(seq_len, N_Q_HEADS, attn_scale) + _rope_tables(seq_len, N_KV_HEADS, 1.0)
    rope_ctx = _flat_tables(ctx_len, N_Q_HEADS, attn_scale) + _flat_tables(ctx_len, N_KV_HEADS, 1.0)

    vec = lambda t: t.reshape(1, -1)
    ln_in = (vec(ln_in_g), vec(ln_in_b))
    w_rt = w_router.T
    b_r = b_router.reshape(N_EXPERTS, 1)

    h = x.reshape(batch * seq_len, d)
    hc = ctx.reshape(batch * ctx_len, d)
    for l in range(depth):
        need_ctx = l < depth - 1
        pre_ln = l == 0
        mod_l = mod[l]
        w_main, w_t = _prep_in_weights(w_in[l])
        lat = _in_proj(h, ln_in[0], ln_in[1], mod_l, w_main, w_t, rope_lat, seq_len, None, pre_ln)
        cx = _in_proj(hc, ln_in[0], ln_in[1], mod_l, w_main, w_t, rope_ctx, ctx_len, ctx_row, pre_ln)
        a_l, q_l, k_l, v_l, mq_l, mv_l, mo_l, bg_l, kt_l, gt_l = lat
        a_c, q_c, k_c, v_c, mq_c, mv_c, mo_c, bg_c, kt_c, gt_c = cx

        conv_args = (conv_w[l], vec(conv_b[l]), vec(conv_ln_g[l]), vec(conv_ln_b[l]))
        gate_b = jnp.transpose(mlstm_gate_b[l], (2, 0, 1)).reshape(N_MLSTM_HEADS, 4, 1)
        ya = _conv_branch(a_l, *conv_args, seq_len)
        yb = _attn_latent(q_l, k_l, v_l, k_c, v_c, attn_sink[l], seq_len, ctx_len)
        yc, yc_c = _mlstm_branch((mq_c, kt_c, mv_c, mo_c, gt_c), (mq_l, kt_l, mv_l, mo_l, gt_l),
                                 gate_b, vec(mlstm_norm_g[l]), seq_len, ctx_len, need_ctx)

        ln1 = (vec(ln1_g[l]), vec(ln1_b[l]))
        ln2 = (vec(ln2_g[l]), vec(ln2_b[l]))
        w_a, w_b, w_c, w_o = (w.astype(BF16) for w in (w_a_out[l], w_b_out[l], w_c_out[l], w_out[l]))

        h1, pay, route_t = _merge(ya, yb, yc, bg_l, h, mod_l, ln_in, ln1, w_a, w_b, w_c, w_o, w_rt, b_r,
                                  seq_len, None, pre_ln, alpha)
        h = _moe(pay, route_t, h1, mod_l, ln2, w_rt, moe_w_gu, moe_w_dn, l, seq_len, None, alpha)
        if need_ctx:
            ya_c = _conv_branch(a_c, *conv_args, ctx_len)
            yb_c = _attn_context(q_c, k_c, v_c, attn_sink[l], ctx_len)
            h1c, pay_c, route_tc = _merge(ya_c, yb_c, yc_c, bg_c, hc, mod_l, ln_in, ln1, w_a, w_b, w_c, w_o, w_rt,
                                          b_r, ctx_len, ctx_row, pre_ln, alpha)
            hc = _moe(pay_c, route_tc, h1c, mod_l, ln2, w_rt, moe_w_gu, moe_w_dn, l, ctx_len, ctx_row, alpha)
    return h.reshape(batch, seq_len, d)
```

```python
import functools

import numpy as np
import jax
import jax.numpy as jnp
from jax import lax
from jax.experimental import pallas as pl
from jax.experimental.pallas import tpu as pltpu

GRID_W = 64
LN_EPS = 1e-5
D_CONV = 512
CONV_WIDTH = 31
N_Q_HEADS = 8
N_KV_HEADS = 2
HEAD_DIM = 64
WINDOW = 128
BLOCK = 128
ROPE_BASE = 10000.0
D_ATTN = N_Q_HEADS * HEAD_DIM
D_KV = N_KV_HEADS * HEAD_DIM
N_MLSTM_HEADS = 4
MLSTM_HEAD_DIM = 128
D_MLSTM = N_MLSTM_HEADS * MLSTM_HEAD_DIM
N_GATE_COLS = 2 * 2 * N_MLSTM_HEADS
N_BRANCHES = 3
N_EXPERTS = 16
N_GROUPS = 4
EXPERTS_PER_GROUP = N_EXPERTS // N_GROUPS
D_EXPERT = 512
N_MOD = 6

LANES = 128
V7X_VMEM_LIMIT_BYTES = 56 * 1024 * 1024

MOD_ROWS = 16
MOD_COL_BLOCK = 512
TOKEN_TILE = 512
EXPERT_TILE = 256
GROUP_COPY_PARTS = 4
FINAL_TILE = 512
MLSTM_CHUNK = 128
MLSTM_HEADS_PER_STEP = 4
MERGE_TILE = 512
CONV_ROWS = 64
CONV_PAD = 16

_PAIRS = [(i, j) for i in range(EXPERTS_PER_GROUP) for j in range(i + 1, EXPERTS_PER_GROUP)]
N_CLASSES = N_GROUPS * len(_PAIRS)
N_CLASS_ROWS = 32
AUX_CLS, AUX_W_LO, AUX_W_HI, AUX_RANK = 0, 1, 2, 3
TOKEN_SUBROWS = 1024 // LANES

F32 = jnp.float32
BF16 = jnp.bfloat16
HIGHEST = lax.Precision.HIGHEST
NEG_INF = float("-inf")

_C_A = 0
_C_Q = _C_A + 2 * D_CONV
_C_K = _C_Q + D_ATTN
_C_MQ = _C_K + D_KV
_C_MV = _C_MQ + D_MLSTM
_C_MO = _C_MV + D_MLSTM
_C_BG = _C_MO + D_MLSTM
_C_END = _C_BG + N_BRANCHES * 1024
_R_KT = 0
_R_GT = _R_KT + D_MLSTM
_R_VT = _R_GT + N_GATE_COLS


def _cparams(*sem):
    return pltpu.CompilerParams(dimension_semantics=sem, vmem_limit_bytes=V7X_VMEM_LIMIT_BYTES)


def _ln(x, g, b):
    mu = jnp.mean(x, axis=-1, keepdims=True)
    xc = x - mu
    var = jnp.mean(xc * xc, axis=-1, keepdims=True)
    return xc * lax.rsqrt(var + LN_EPS) * g + b


def _sigmoid(x):
    return 0.5 * jnp.tanh(0.5 * x) + 0.5


def _log_sigmoid(x):
    return jnp.minimum(x, 0.0) - jnp.log(1.0 + jnp.exp(-jnp.abs(x)))


def _dot_nt(a, b, precision=None):
    return lax.dot_general(a, b, (((1,), (1,)), ((), ())), preferred_element_type=F32, precision=precision)


def _mod_kernel(c_ref, w_ref, b_ref, o_ref):
    c = c_ref[...]
    s = c * _sigmoid(c)
    o_ref[...] = jnp.dot(s, w_ref[...], preferred_element_type=F32, precision=HIGHEST) + b_ref[...]


def _modulation(cc, w_mod, b_mod):
    depth, d, n = w_mod.shape
    return pl.pallas_call(
        _mod_kernel,
        grid=(depth, n // MOD_COL_BLOCK),
        in_specs=[
            pl.BlockSpec((MOD_ROWS, d), lambda l, j: (0, 0)),
            pl.BlockSpec((None, d, MOD_COL_BLOCK), lambda l, j: (l, 0, j)),
            pl.BlockSpec((None, 1, MOD_COL_BLOCK), lambda l, j: (l, 0, j)),
        ],
        out_specs=pl.BlockSpec((None, MOD_ROWS, MOD_COL_BLOCK), lambda l, j: (l, 0, j)),
        out_shape=jax.ShapeDtypeStruct((depth, MOD_ROWS, n), F32),
        compiler_params=_cparams("parallel", "parallel"),
        name="modulation",
    )(cc, w_mod, b_mod.reshape(depth, 1, n))


def _mod_spec(which, tile, seq_len, ctx_row):
    tiles_per_seq = seq_len // tile
    if ctx_row is None:
        return pl.BlockSpec((None, 1, 1024), lambda i, *_: ((i // tiles_per_seq) * N_MOD + which, 0, 0))
    return pl.BlockSpec((None, 1, 1024), lambda i, *_: (ctx_row * N_MOD + which, 0, 0))


def _in_kernel(pre_ln, x_ref, lg_ref, lb_ref, sc_ref, sh_ref, w_ref, wt_ref, cq_ref, sq_ref, ck_ref, sk_ref,
               a_ref, q_ref, k_ref, v_ref, mq_ref, mv_ref, mo_ref, bg_ref, kt_ref, gt_ref):
    x = x_ref[...]
    if pre_ln:
        x = _ln(x, lg_ref[...], lb_ref[...])
    u = (x * (1.0 + sc_ref[...]) + sh_ref[...]).astype(BF16)

    def seg(lo, hi):
        return jnp.dot(u, w_ref[:, lo:hi], preferred_element_type=F32)

    a_ref[...] = seg(_C_A, _C_Q).astype(BF16)
    def rotary(t, cos, sin):
        n = t.shape[1]
        quarter = HEAD_DIM // 4
        lane = lax.broadcasted_iota(jnp.int32, t.shape, 1)
        up = pltpu.roll(t, shift=n - quarter, axis=1)
        down = pltpu.roll(t, shift=quarter, axis=1)
        partner = jnp.where(lane % (2 * quarter) < quarter, up, down)
        return (t * cos + partner * sin).astype(BF16)

    q_ref[...] = rotary(seg(_C_Q, _C_K), cq_ref[...], sq_ref[...])
    k_ref[...] = rotary(seg(_C_K, _C_MQ), ck_ref[...], sk_ref[...])
    v_ref[...] = _dot_nt(wt_ref[_R_VT:_R_VT + D_KV, :], u).astype(BF16)
    mq_ref[...] = seg(_C_MQ, _C_MV).astype(BF16)
    mv_ref[...] = seg(_C_MV, _C_MO).astype(BF16)
    mo_ref[...] = seg(_C_MO, _C_BG).astype(BF16)
    for j in range(N_BRANCHES):
        bg_ref[:, j * 1024:(j + 1) * 1024] = _sigmoid(seg(_C_BG + j * 1024, _C_BG + (j + 1) * 1024)).astype(BF16)
    n_chunks = u.shape[0] // MLSTM_CHUNK
    kt = _dot_nt(wt_ref[_R_KT:_R_KT + D_MLSTM, :], u)
    for h in range(N_MLSTM_HEADS):
        for c in range(n_chunks):
            kt_ref[h, c] = kt[h * MLSTM_HEAD_DIM:(h + 1) * MLSTM_HEAD_DIM,
                              c * MLSTM_CHUNK:(c + 1) * MLSTM_CHUNK].astype(BF16)
    gt = _dot_nt(wt_ref[_R_GT:_R_GT + N_GATE_COLS, :], u)
    for h in range(N_MLSTM_HEADS):
        for c in range(n_chunks):
            gt_ref[h, c] = gt[h * 4:(h + 1) * 4, c * MLSTM_CHUNK:(c + 1) * MLSTM_CHUNK]


def _in_proj(x, ln_g, ln_b, mod_l, w_main, w_t, rope, seq_len, ctx_row, pre_ln):
    n = x.shape[0]
    tm = min(TOKEN_TILE, seq_len)
    cq, sq, ck, sk = rope
    tps = seq_len // tm
    nch = n // MLSTM_CHUNK
    row = lambda i: (i, 0)
    pos = lambda i: (i % tps, 0)
    const = lambda i: (0, 0)
    out_shape = [
        jax.ShapeDtypeStruct((n, 2 * D_CONV), BF16),
        jax.ShapeDtypeStruct((n, D_ATTN), BF16),
        jax.ShapeDtypeStruct((n, D_KV), BF16),
        jax.ShapeDtypeStruct((D_KV, n), BF16),
        jax.ShapeDtypeStruct((n, D_MLSTM), BF16),
        jax.ShapeDtypeStruct((n, D_MLSTM), BF16),
        jax.ShapeDtypeStruct((n, D_MLSTM), BF16),
        jax.ShapeDtypeStruct((n, N_BRANCHES * 1024), BF16),
        jax.ShapeDtypeStruct((N_MLSTM_HEADS, nch, MLSTM_HEAD_DIM, MLSTM_CHUNK), BF16),
        jax.ShapeDtypeStruct((N_MLSTM_HEADS, nch, 4, MLSTM_CHUNK), F32),
    ]
    cpt = tm // MLSTM_CHUNK
    out_specs = [
        pl.BlockSpec((tm, 2 * D_CONV), row),
        pl.BlockSpec((tm, D_ATTN), row),
        pl.BlockSpec((tm, D_KV), row),
        pl.BlockSpec((D_KV, tm), lambda i: (0, i)),
        pl.BlockSpec((tm, D_MLSTM), row),
        pl.BlockSpec((tm, D_MLSTM), row),
        pl.BlockSpec((tm, D_MLSTM), row),
        pl.BlockSpec((tm, N_BRANCHES * 1024), row),
        pl.BlockSpec((N_MLSTM_HEADS, cpt, MLSTM_HEAD_DIM, MLSTM_CHUNK), lambda i: (0, i, 0, 0)),
        pl.BlockSpec((N_MLSTM_HEADS, cpt, 4, MLSTM_CHUNK), lambda i: (0, i, 0, 0)),
    ]
    in_specs = [
        pl.BlockSpec((tm, 1024), row),
        pl.BlockSpec((1, 1024), const),
        pl.BlockSpec((1, 1024), const),
        _mod_spec(1, tm, seq_len, ctx_row),
        _mod_spec(0, tm, seq_len, ctx_row),
        pl.BlockSpec(w_main.shape, const, pipeline_mode=pl.Buffered(1)),
        pl.BlockSpec(w_t.shape, const, pipeline_mode=pl.Buffered(1)),
        pl.BlockSpec((tm, D_ATTN), pos),
        pl.BlockSpec((tm, D_ATTN), pos),
        pl.BlockSpec((tm, D_KV), pos),
        pl.BlockSpec((tm, D_KV), pos),
    ]
    return pl.pallas_call(
        functools.partial(_in_kernel, pre_ln),
        grid=(n // tm,),
        in_specs=in_specs,
        out_specs=out_specs,
        out_shape=out_shape,
        compiler_params=_cparams("parallel"),
        name="in_proj",
    )(x, ln_g, ln_b, mod_l, mod_l, w_main, w_t, cq, sq, ck, sk)


def _conv_kernel(a_ref, w_ref, cb_ref, g_ref, b_ref, o_ref, upad_ref):
    t = a_ref.shape[0]
    zeros = jnp.zeros((CONV_PAD, D_CONV), F32)
    upad_ref[0:CONV_PAD, :] = zeros
    upad_ref[CONV_PAD + t:2 * CONV_PAD + t, :] = zeros
    val = a_ref[:, 0:D_CONV].astype(F32)
    gate = a_ref[:, D_CONV:2 * D_CONV].astype(F32)
    upad_ref[CONV_PAD:CONV_PAD + t, :] = val * _sigmoid(gate)
    half = CONV_WIDTH // 2

    def body(c, carry):
        r0 = pl.multiple_of(c * CONV_ROWS, CONV_ROWS)
        n_win = CONV_ROWS + 2 * CONV_PAD
        win = upad_ref[pl.ds(r0, n_win), :]
        acc = jnp.zeros((CONV_ROWS, D_CONV), F32) + cb_ref[...]
        for res in range(8):
            rolled = win if res == 0 else pltpu.roll(win, shift=n_win - res, axis=0)
            for k in range(CONV_WIDTH):
                off = CONV_PAD - half + k
                if off % 8 == res:
                    acc = acc + rolled[off - res:off - res + CONV_ROWS, :] * w_ref[k:k + 1, :]
        y = _ln(acc, g_ref[...], b_ref[...])
        o_ref[pl.ds(r0, CONV_ROWS), :] = (y * _sigmoid(y)).astype(BF16)
        return carry

    lax.fori_loop(0, t // CONV_ROWS, body, 0)


def _conv_branch(a_in, conv_w, conv_b, ln_g, ln_b, seq_len):
    n = a_in.shape[0]
    const = lambda b: (0, 0)
    return pl.pallas_call(
        _conv_kernel,
        grid=(n // seq_len,),
        in_specs=[
            pl.BlockSpec((seq_len, 2 * D_CONV), lambda b: (b, 0)),
            pl.BlockSpec((CONV_WIDTH, D_CONV), const),
            pl.BlockSpec((1, D_CONV), const),
            pl.BlockSpec((1, D_CONV), const),
            pl.BlockSpec((1, D_CONV), const),
        ],
        out_specs=pl.BlockSpec((seq_len, D_CONV), lambda b: (b, 0)),
        out_shape=jax.ShapeDtypeStruct((n, D_CONV), BF16),
        scratch_shapes=[pltpu.VMEM((seq_len + 2 * CONV_PAD, D_CONV), F32)],
        compiler_params=_cparams("parallel"),
        name="conv_branch",
    )(a_in, conv_w, conv_b, ln_g, ln_b)


def _attn_heads(q, keys, vals_t, masks, sink_ref, o_ref):
    rows = q.shape[0]
    group = N_Q_HEADS // N_KV_HEADS
    for hk in range(N_KV_HEADS):
        lo = hk * HEAD_DIM
        qs = jnp.concatenate([q[:, (hk * group + g) * HEAD_DIM:(hk * group + g + 1) * HEAD_DIM]
                              for g in range(group)], axis=0)
        sink = jnp.concatenate([jnp.full((1, rows), sink_ref[hk * group + g], F32) for g in range(group)], axis=1)
        scores = []
        m = sink
        for kk, mask in zip(keys, masks):
            s = _dot_nt(kk[:, lo:lo + HEAD_DIM], qs)
            if mask is not None:
                s = jnp.where(mask, s, NEG_INF)
            scores.append(s)
            m = jnp.maximum(m, jnp.max(s, axis=0, keepdims=True))
        acc = jnp.zeros((2 * HEAD_DIM, rows * group), F32)
        for s, vt in zip(scores, vals_t):
            n_k = s.shape[0]
            p = jnp.exp(s - m).astype(BF16)
            ones_rows = jnp.where(lax.broadcasted_iota(jnp.int32, (HEAD_DIM, n_k), 0) == 0, 1.0, 0.0).astype(BF16)
            v_aug = jnp.concatenate([vt[lo:lo + HEAD_DIM, :], ones_rows], axis=0)
            acc = acc + jnp.dot(v_aug, p, preferred_element_type=F32)
        denom = acc[HEAD_DIM:HEAD_DIM + 1, :] + jnp.exp(sink - m)
        o_t = acc * (1.0 / denom)
        for g in range(group):
            col = (hk * group + g) * HEAD_DIM
            o_ref[:, col:col + HEAD_DIM] = o_t[:, g * rows:(g + 1) * rows].T[:, 0:HEAD_DIM].astype(BF16)


def _attn_lat_kernel(sink_ref, q_ref, kp_ref, k0_ref, kn_ref, vp_ref, v0_ref, vn_ref, kc_ref, vc_ref, o_ref):
    n = pl.program_id(1)
    nb = pl.num_programs(1)
    stacked = (N_Q_HEADS // N_KV_HEADS) * BLOCK
    ki = lax.broadcasted_iota(jnp.int32, (BLOCK, stacked), 0)
    qi = lax.broadcasted_iota(jnp.int32, (BLOCK, stacked), 1) % BLOCK
    mask_prev = ki >= qi + jnp.where(n > 0, 0, BLOCK)
    mask_next = ki <= qi - jnp.where(n < nb - 1, 0, BLOCK)
    _attn_heads(q_ref[...],
                [kp_ref[...], k0_ref[...], kn_ref[...], kc_ref[...]],
                [vp_ref[...], v0_ref[...], vn_ref[...], vc_ref[...]],
                [mask_prev, None, mask_next, None], sink_ref, o_ref)


def _attn_ctx_kernel(sink_ref, q_ref, kc_ref, vc_ref, o_ref):
    _attn_heads(q_ref[...], [kc_ref[...]], [vc_ref[...]], [None], sink_ref, o_ref)


def _attn_latent(q, k, vt, kc, vct, sink, seq_len, ctx_len):
    n = q.shape[0]
    nb = seq_len // BLOCK
    batch = n // seq_len
    blk_prev = lambda b, j: b * nb + jnp.maximum(j - 1, 0)
    blk_next = lambda b, j: b * nb + jnp.minimum(j + 1, nb - 1)
    cur = lambda b, j: (b * nb + j, 0)
    kspec = lambda blk: pl.BlockSpec((BLOCK, D_KV), lambda b, j: (blk(b, j), 0))
    vspec = lambda blk: pl.BlockSpec((D_KV, BLOCK), lambda b, j: (0, blk(b, j)))
    blk_cur = lambda b, j: b * nb + j
    return pl.pallas_call(
        _attn_lat_kernel,
        grid=(batch, nb),
        in_specs=[
            pl.BlockSpec(memory_space=pltpu.SMEM),
            pl.BlockSpec((BLOCK, D_ATTN), cur),
            kspec(blk_prev), kspec(blk_cur), kspec(blk_next), vspec(blk_prev), vspec(blk_cur), vspec(blk_next),
            pl.BlockSpec((ctx_len, D_KV), lambda b, j: (b, 0)),
            pl.BlockSpec((D_KV, ctx_len), lambda b, j: (0, b)),
        ],
        out_specs=pl.BlockSpec((BLOCK, D_ATTN), cur),
        out_shape=jax.ShapeDtypeStruct((n, D_ATTN), BF16),
        compiler_params=_cparams("parallel", "parallel"),
        name="attn_latent",
    )(sink, q, k, k, k, vt, vt, vt, kc, vct)


def _attn_context(qc, kc, vct, sink, ctx_len):
    n = qc.shape[0]
    blk = lambda b: (b, 0)
    return pl.pallas_call(
        _attn_ctx_kernel,
        grid=(n // ctx_len,),
        in_specs=[
            pl.BlockSpec(memory_space=pltpu.SMEM),
            pl.BlockSpec((ctx_len, D_ATTN), blk),
            pl.BlockSpec((ctx_len, D_KV), blk),
            pl.BlockSpec((D_KV, ctx_len), lambda b: (0, b)),
        ],
        out_specs=pl.BlockSpec((ctx_len, D_ATTN), blk),
        out_shape=jax.ShapeDtypeStruct((n, D_ATTN), BF16),
        compiler_params=_cparams("parallel"),
        name="attn_context",
    )(sink, qc, kc, vct)


def _mlstm_kernel(ctx_out, nch_c, nch_l, hps,
                  qc_ref, ktc_ref, vc_ref, moc_ref, gc_ref,
                  ql_ref, ktl_ref, vl_ref, mol_ref, gl_ref,
                  gb_ref, ng_ref, *rest):
    if ctx_out:
        yl_ref, yc_ref, af_ref, lff_ref, ab_ref, lfb_ref, cf_ref, cb_ref, hf_ref, hb_ref = rest
    else:
        yl_ref, af_ref, lff_ref, ab_ref, lfb_ref, cf_ref, cb_ref, hf_ref, hb_ref = rest
        yc_ref = None
    lc = MLSTM_CHUNK
    dh = MLSTM_HEAD_DIM
    k_scale = MLSTM_HEAD_DIM ** -0.5
    ti = lax.broadcasted_iota(jnp.int32, (lc, lc), 0)
    si = lax.broadcasted_iota(jnp.int32, (lc, lc), 1)
    lower = si <= ti
    upper = si >= ti
    pre_mat = jnp.where(upper, 1.0, 0.0)
    suf_mat = jnp.where(lower, 1.0, 0.0)
    n_rows = nch_c + nch_l
    pad_rows = -n_rows % 8

    for hh in range(hps):
        gb = gb_ref[hh]

        def gate_rows(kind):
            rows = ([gc_ref[hh, c, kind:kind + 1, :] for c in range(nch_c)]
                    + [gl_ref[hh, c, kind:kind + 1, :] for c in range(nch_l)])
            rows = jnp.concatenate(rows, axis=0) + gb[kind:kind + 1, :]
            if pad_rows:
                rows = jnp.concatenate([rows, jnp.zeros((pad_rows, lc), F32)], axis=0)
            return rows

        lf_f = _log_sigmoid(gate_rows(1))
        lf_b = _log_sigmoid(gate_rows(3))
        a_f = gate_rows(0) - jnp.dot(lf_f, pre_mat, preferred_element_type=F32, precision=HIGHEST)
        a_b = gate_rows(2) - jnp.dot(lf_b, suf_mat, preferred_element_type=F32, precision=HIGHEST)
        for c in range(n_rows):
            af_ref[hh * n_rows + c] = a_f[c:c + 1, :]
            lff_ref[hh * n_rows + c] = lf_f[c:c + 1, :]
            ab_ref[hh * n_rows + c] = a_b[c:c + 1, :]
            lfb_ref[hh * n_rows + c] = lf_b[c:c + 1, :]

    cf_ref[...] = jnp.zeros_like(cf_ref)
    cb_ref[...] = jnp.zeros_like(cb_ref)
    ones_col = jnp.where(lax.broadcasted_iota(jnp.int32, (lc, dh), 1) == 0, 1.0, 0.0).astype(BF16)

    def chunk(q, kt, v, a_row, lf_row, c_ref, hh, m, mask):
        a_mat = jnp.where(mask, jnp.broadcast_to(a_row, (lc, lc)), NEG_INF)
        cm = jnp.max(a_mat, axis=1, keepdims=True)
        b_col = jnp.sum(jnp.where(mask, jnp.broadcast_to(lf_row, (lc, lc)), 0.0), axis=1, keepdims=True)
        mx = jnp.maximum(m, cm)
        mx_last = jnp.maximum(m, jnp.max(a_row, axis=1, keepdims=True))
        w = jnp.exp(a_mat - mx)
        s = jnp.dot(q, kt, preferred_element_type=F32) * k_scale
        p = (s * w).astype(BF16)
        w_s = jnp.exp(a_row - mx_last) * k_scale
        ktw = (kt.astype(F32) * w_s).astype(BF16)
        v_aug = jnp.concatenate([v, ones_col], axis=1)
        both = jnp.dot(jnp.concatenate([p, ktw], axis=0), v_aug, preferred_element_type=F32)
        c_old = c_ref[hh]
        inter = jnp.dot(q, c_old.astype(BF16), preferred_element_type=F32)
        tot = both[0:lc, :] + jnp.exp(m - mx) * inter
        den = tot[:, dh:dh + 1]
        h = tot[:, 0:dh] / jnp.maximum(jnp.abs(den), jnp.exp(-(b_col + mx)))
        c_ref[hh] = jnp.exp(m - mx_last) * c_old + both[lc:lc + dh, :]
        m_new = jnp.sum(lf_row, axis=1, keepdims=True) + mx_last
        return h, m_new

    def step(q_ref, kt_ref, v_ref, row0, c_f, c_b, rf, rb, ms):
        out = []
        for hh in range(hps):
            cols = slice(hh * dh, (hh + 1) * dh)
            base = hh * n_rows + row0
            h_f, m_f = chunk(q_ref[pl.ds(rf, lc), cols], kt_ref[hh, c_f], v_ref[pl.ds(rf, lc), cols],
                             af_ref[base + c_f], lff_ref[base + c_f], cf_ref, hh, ms[2 * hh], lower)
            h_b, m_b = chunk(q_ref[pl.ds(rb, lc), cols], kt_ref[hh, c_b], v_ref[pl.ds(rb, lc), cols],
                             ab_ref[base + c_b], lfb_ref[base + c_b], cb_ref, hh, ms[2 * hh + 1], upper)
            hf_ref[pl.ds(rf, lc), cols] = h_f
            hb_ref[pl.ds(rb, lc), cols] = h_b
            out += [m_f, m_b]
        return tuple(out)

    def finish(n_tok, mo_ref, o_ref):
        for hh in range(hps):
            cols = slice(hh * dh, (hh + 1) * dh)
            h = hf_ref[0:n_tok, cols] + hb_ref[0:n_tok, cols]
            mu = jnp.mean(h, axis=-1, keepdims=True)
            hc = h - mu
            var = jnp.mean(hc * hc, axis=-1, keepdims=True)
            y = hc * lax.rsqrt(var + LN_EPS) * ng_ref[:, cols]
            o_ref[:, cols] = (_sigmoid(mo_ref[:, cols].astype(F32)) * y).astype(BF16)

    ms = tuple(jnp.zeros((1, 1), F32) for _ in range(2 * hps))
    for c in range(nch_c):
        c_b = nch_c - 1 - c
        ms = step(qc_ref, ktc_ref, vc_ref, 0, c, c_b, c * lc, c_b * lc, ms)
    if ctx_out:
        finish(nch_c * lc, moc_ref, yc_ref)

    def body(c, ms):
        c_b = nch_l - 1 - c
        return step(ql_ref, ktl_ref, vl_ref, nch_c, c, c_b,
                    pl.multiple_of(c * lc, lc), pl.multiple_of(c_b * lc, lc), ms)

    lax.fori_loop(0, nch_l, body, ms)
    finish(nch_l * lc, mol_ref, yl_ref)


def _mlstm_branch(ctx_p, lat_p, gate_b, norm_g, seq_len, ctx_len, ctx_out):
    mq_c, kt_c, mv_c, mo_c, g_c = ctx_p
    mq_l, kt_l, mv_l, mo_l, g_l = lat_p
    n_l, n_c = mq_l.shape[0], mq_c.shape[0]
    batch = n_l // seq_len
    lc, dh, hps = MLSTM_CHUNK, MLSTM_HEAD_DIM, MLSTM_HEADS_PER_STEP
    nch_c, nch_l = ctx_len // lc, seq_len // lc

    def stream(t, nch):
        tok = pl.BlockSpec((t, hps * dh), lambda b, h: (b, h))
        return [tok,
                pl.BlockSpec((hps, nch, dh, lc), lambda b, h: (h, b, 0, 0)),
                tok, tok,
                pl.BlockSpec((hps, nch, 4, lc), lambda b, h: (h, b, 0, 0))]

    in_specs = stream(ctx_len, nch_c) + stream(seq_len, nch_l) + [
        pl.BlockSpec((hps, 4, 1), lambda b, h: (h, 0, 0)),
        pl.BlockSpec((1, hps * dh), lambda b, h: (0, h)),
    ]
    out_specs = [pl.BlockSpec((seq_len, hps * dh), lambda b, h: (b, h))]
    out_shape = [jax.ShapeDtypeStruct((n_l, D_MLSTM), BF16)]
    if ctx_out:
        out_specs.append(pl.BlockSpec((ctx_len, hps * dh), lambda b, h: (b, h)))
        out_shape.append(jax.ShapeDtypeStruct((n_c, D_MLSTM), BF16))
    row_scratch = pltpu.VMEM((hps * (nch_c + nch_l), 1, lc), F32)
    outs = pl.pallas_call(
        functools.partial(_mlstm_kernel, ctx_out, nch_c, nch_l, hps),
        grid=(batch, N_MLSTM_HEADS // hps),
        in_specs=in_specs,
        out_specs=out_specs,
        out_shape=out_shape,
        scratch_shapes=[row_scratch, row_scratch, row_scratch, row_scratch,
                        pltpu.VMEM((hps, dh, 2 * dh), F32), pltpu.VMEM((hps, dh, 2 * dh), F32),
                        pltpu.VMEM((seq_len, hps * dh), F32), pltpu.VMEM((seq_len, hps * dh), F32)],
        compiler_params=_cparams("parallel", "parallel"),
        name="mlstm_branch",
    )(mq_c, kt_c, mv_c, mo_c, g_c, mq_l, kt_l, mv_l, mo_l, g_l, gate_b, norm_g)
    return outs if ctx_out else (outs[0], None)


def _route(logits_t, br):
    sc = [_sigmoid(logits_t[e:e + 1, :]) for e in range(N_EXPERTS)]
    sel = [sc[e] + br[e:e + 1, :] for e in range(N_EXPERTS)]
    epg = EXPERTS_PER_GROUP
    group_score = []
    for g in range(N_GROUPS):
        v = sel[g * epg:(g + 1) * epg]
        best = None
        for i in range(epg):
            for j in range(i + 1, epg):
                pair = v[i] + v[j]
                best = pair if best is None else jnp.maximum(best, pair)
        group_score.append(best)
    g_idx = jnp.zeros_like(group_score[0], dtype=jnp.int32)
    best = group_score[0]
    for g in range(1, N_GROUPS):
        better = group_score[g] > best
        g_idx = jnp.where(better, g, g_idx)
        best = jnp.maximum(best, group_score[g])
    chosen = []
    for g in range(N_GROUPS):
        v = sel[g * epg:(g + 1) * epg]
        in_g = g_idx == g
        for i in range(epg):
            rank = jnp.zeros_like(g_idx)
            for j in range(epg):
                if j == i:
                    continue
                ahead = (v[j] >= v[i]) if j < i else (v[j] > v[i])
                rank = rank + jnp.where(ahead, 1, 0)
            chosen.append(in_g & (rank < 2))
    cls = jnp.zeros_like(sc[0])
    w_lo = jnp.zeros_like(sc[0])
    w_hi = jnp.zeros_like(sc[0])
    for g in range(N_GROUPS):
        for pid, (i, j) in enumerate(_PAIRS):
            lo, hi = g * epg + i, g * epg + j
            is_pair = chosen[lo] & chosen[hi]
            cls = jnp.where(is_pair, float(g * len(_PAIRS) + pid), cls)
            w_lo = jnp.where(is_pair, sc[lo], w_lo)
            w_hi = jnp.where(is_pair, sc[hi], w_hi)
    total = w_lo + w_hi
    return cls, w_lo / total, w_hi / total


def _merge_kernel(pre_ln, alpha, ya_ref, yb_ref, yc_ref, bg_ref, h_ref, g1_ref, sc2_ref, sh2_ref,
                  lig_ref, lib_ref, l1g_ref, l1b_ref, wa_ref, wb_ref, wc_ref, wo_ref, wr_ref, br_ref,
                  h1_ref, u2t_ref, route_ref, cnt_ref, tri_ref):
    tm = h_ref.shape[0]

    def branch(y_ref, w_ref, j):
        gate = bg_ref[:, j * 1024:(j + 1) * 1024].astype(F32)
        return gate * jnp.dot(y_ref[...], w_ref[...], preferred_element_type=F32)

    mix = branch(ya_ref, wa_ref, 0) + branch(yb_ref, wb_ref, 1) + branch(yc_ref, wc_ref, 2)
    y = jnp.dot(mix.astype(BF16), wo_ref[...], preferred_element_type=F32)
    h = h_ref[...]
    if pre_ln:
        h = _ln(h, lig_ref[...], lib_ref[...])
    h1 = _ln(alpha * h + g1_ref[...] * y, l1g_ref[...], l1b_ref[...])
    h1_ref[...] = h1
    u2 = h1 * (1.0 + sc2_ref[...]) + sh2_ref[...]
    logits_t = _dot_nt(wr_ref[...], u2, precision=HIGHEST)
    cls, w_lo, w_hi = _route(logits_t, br_ref[...])

    @pl.when(pl.program_id(0) == 0)
    def _():
        cnt_ref[...] = jnp.zeros_like(cnt_ref)
        earlier = lax.broadcasted_iota(jnp.int32, (tm, tm), 0) <= lax.broadcasted_iota(jnp.int32, (tm, tm), 1)
        tri_ref[...] = jnp.where(earlier, 1.0, 0.0).astype(BF16)

    crow = lax.broadcasted_iota(jnp.int32, (N_CLASS_ROWS, tm), 0).astype(F32)
    onehot = jnp.where(crow == cls, 1.0, 0.0)
    incl = jnp.dot(onehot.astype(BF16), tri_ref[...], preferred_element_type=F32)
    base = cnt_ref[...]
    rank = jnp.sum(onehot * (incl - 1.0 + base), axis=0, keepdims=True)
    cnt_ref[...] = base + incl[:, tm - 1:tm]

    route_ref[...] = jnp.concatenate([cls, w_lo, w_hi, rank, jnp.zeros((4, tm), F32)], axis=0)
    for s in range(TOKEN_SUBROWS):
        u2t_ref[pl.ds(s, tm, stride=TOKEN_SUBROWS), :] = u2[:, s * LANES:(s + 1) * LANES]


def _merge(ya, yb, yc, bg, h, mod_l, ln_in, ln1, w_a, w_b, w_c, w_o, w_rt, b_r, seq_len, ctx_row, pre_ln, alpha):
    n = ya.shape[0]
    tm = MERGE_TILE
    row = lambda i: (i, 0)
    const = lambda i: (0, 0)
    vec = pl.BlockSpec((1, 1024), const)
    wspec = lambda w: pl.BlockSpec(w.shape, const)
    return pl.pallas_call(
        functools.partial(_merge_kernel, pre_ln, alpha),
        grid=(n // tm,),
        in_specs=[
            pl.BlockSpec((tm, D_CONV), row),
            pl.BlockSpec((tm, D_ATTN), row),
            pl.BlockSpec((tm, D_MLSTM), row),
            pl.BlockSpec((tm, N_BRANCHES * 1024), row),
            pl.BlockSpec((tm, 1024), row),
            _mod_spec(2, tm, seq_len, ctx_row),
            _mod_spec(4, tm, seq_len, ctx_row),
            _mod_spec(3, tm, seq_len, ctx_row),
            vec, vec, vec, vec,
            wspec(w_a), wspec(w_b), wspec(w_c), wspec(w_o), wspec(w_rt), wspec(b_r),
        ],
        out_specs=[
            pl.BlockSpec((tm, 1024), row),
            pl.BlockSpec((tm * TOKEN_SUBROWS, LANES), row),
            pl.BlockSpec((8, tm), lambda i: (0, i)),
        ],
        out_shape=[
            jax.ShapeDtypeStruct((n, 1024), F32),
            jax.ShapeDtypeStruct((n * TOKEN_SUBROWS, LANES), F32),
            jax.ShapeDtypeStruct((8, n), F32),
        ],
        scratch_shapes=[pltpu.VMEM((N_CLASS_ROWS, 1), F32), pltpu.VMEM((tm, tm), BF16)],
        compiler_params=_cparams("arbitrary"),
        name="merge",
    )(ya, yb, yc, bg, h, mod_l, mod_l, mod_l, ln_in[0], ln_in[1], ln1[0], ln1[1], w_a, w_b, w_c, w_o, w_rt, b_r)


class _TokenGather:
    def __init__(self, idx_ref, base, src_hbm, buf, sem, slot):
        self.idx_ref, self.base, self.src_hbm, self.buf, self.sem, self.slot = idx_ref, base, src_hbm, buf, sem, slot
        self.tokens = buf.shape[1] // TOKEN_SUBROWS
        self.last = idx_ref.shape[0] - 1

    def _copy(self, k):
        sub = TOKEN_SUBROWS
        p = pl.multiple_of(self.idx_ref[jnp.minimum(self.base + k, self.last)] * sub, sub)
        return pltpu.make_async_copy(self.src_hbm.at[pl.ds(p, sub)], self.buf.at[self.slot, pl.ds(k * sub, sub)],
                                     self.sem.at[self.slot])

    def _share(self, part, parts):
        share = self.tokens // parts
        return range(part * share, (part + 1) * share)

    def start(self, part=0, parts=1):
        for k in self._share(part, parts):
            self._copy(k).start(priority=k % 2)

    def wait(self, part=0, parts=1):
        for k in self._share(part, parts):
            self._copy(k).wait()


def _untile_tokens(ref):
    tokens = ref.shape[0] // TOKEN_SUBROWS
    return jnp.concatenate([ref[pl.ds(s, tokens, stride=TOKEN_SUBROWS), :] for s in range(TOKEN_SUBROWS)], axis=1)


def _group_kernel(lo_ref, hi_ref, base_ref, cnt_ref, order_ref, x_hbm, wr_ref, xs_ref, gate_ref, buf, sem):
    i = pl.program_id(0)
    last = pl.num_programs(0) - 1
    slot = i % 2
    nxt = jnp.minimum(i + 1, last)
    gather = functools.partial(_TokenGather, order_ref, src_hbm=x_hbm, buf=buf, sem=sem)
    parts = GROUP_COPY_PARTS
    share = (buf.shape[1] // TOKEN_SUBROWS) // parts

    @pl.when(i == 0)
    def _():
        buf[...] = jnp.zeros_like(buf)

    for p in range(parts):
        @pl.when(jnp.logical_and(i == 0, cnt_ref[0] > p * share))
        def _():
            gather(base=base_ref[0], slot=0).start(p, parts)

    for p in range(parts):
        @pl.when(cnt_ref[i] > p * share)
        def _():
            gather(base=base_ref[i], slot=slot).wait(p, parts)

    for p in range(parts):
        @pl.when(cnt_ref[nxt] > p * share)
        def _():
            gather(base=base_ref[nxt], slot=1 - slot).start(p, parts)

    x32 = _untile_tokens(buf.at[slot])

    def affinity(e):
        logit = jnp.sum(x32 * wr_ref[pl.ds(e, 1), :], axis=1, keepdims=True)
        return _sigmoid(logit)

    s_lo, s_hi = affinity(lo_ref[i]), affinity(hi_ref[i])
    total = s_lo + s_hi
    xs_ref[...] = x32.astype(BF16)
    gate_ref[...] = jnp.concatenate([s_lo / total, s_hi / total], axis=1)

    for p in range(parts):
        @pl.when(jnp.logical_and(i == last, cnt_ref[nxt] > p * share))
        def _():
            gather(base=base_ref[nxt], slot=1 - slot).wait(p, parts)


def _group_tokens(lo, hi, base, cnt, order, u2t, w_rt):
    tm = EXPERT_TILE
    n_tiles = base.shape[0]
    row = lambda i, *_: (i, 0)
    grid_spec = pltpu.PrefetchScalarGridSpec(
        num_scalar_prefetch=5,
        grid=(n_tiles,),
        in_specs=[
            pl.BlockSpec(memory_space=pl.ANY),
            pl.BlockSpec(w_rt.shape, lambda i, *_: (0, 0)),
        ],
        out_specs=[pl.BlockSpec((tm, 1024), row), pl.BlockSpec((tm, 2), row)],
        scratch_shapes=[pltpu.VMEM((2, tm * TOKEN_SUBROWS, LANES), F32), pltpu.SemaphoreType.DMA((2,))],
    )
    return pl.pallas_call(
        _group_kernel,
        grid_spec=grid_spec,
        out_shape=[jax.ShapeDtypeStruct((n_tiles * tm, 1024), BF16), jax.ShapeDtypeStruct((n_tiles * tm, 2), F32)],
        compiler_params=_cparams("arbitrary"),
        name="group_tokens",
    )(lo, hi, base, cnt, order, u2t, w_rt)


def _experts_kernel(lo_ref, hi_ref, valid_ref, x_ref, gate_ref, wgl_ref, wgh_ref, wdl_ref, wdh_ref, o_ref):
    i = pl.program_id(0)
    tm = x_ref.shape[0]

    @pl.when(valid_ref[i] != 0)
    def _():
        x = x_ref[...]

        def expert(wg_ref, wd_ref):
            gu = jnp.dot(x, wg_ref[...].astype(BF16), preferred_element_type=F32)
            g_ = gu[:, 0:D_EXPERT]
            act = (g_ * _sigmoid(g_) * gu[:, D_EXPERT:2 * D_EXPERT]).astype(BF16)
            return jnp.dot(act, wd_ref[...].astype(BF16), preferred_element_type=F32)

        out = gate_ref[:, 0:1] * expert(wgl_ref, wdl_ref) + gate_ref[:, 1:2] * expert(wgh_ref, wdh_ref)
        for s in range(TOKEN_SUBROWS):
            o_ref[pl.ds(s, tm, stride=TOKEN_SUBROWS), :] = out[:, s * LANES:(s + 1) * LANES]

    @pl.when(valid_ref[i] == 0)
    def _():
        o_ref[...] = jnp.zeros_like(o_ref)


def _experts(lo, hi, valid, xs, gates, w_gu, w_dn, layer):
    tm = EXPERT_TILE
    n_tiles = xs.shape[0] // tm
    row = lambda i, *_: (i, 0)
    grid_spec = pltpu.PrefetchScalarGridSpec(
        num_scalar_prefetch=3,
        grid=(n_tiles,),
        in_specs=[
            pl.BlockSpec((tm, 1024), row),
            pl.BlockSpec((tm, 2), row),
            pl.BlockSpec((None, None, 1024, 2 * D_EXPERT), lambda i, lo, hi, *_: (layer, lo[i], 0, 0)),
            pl.BlockSpec((None, None, 1024, 2 * D_EXPERT), lambda i, lo, hi, *_: (layer, hi[i], 0, 0)),
            pl.BlockSpec((None, None, D_EXPERT, 1024), lambda i, lo, hi, *_: (layer, lo[i], 0, 0)),
            pl.BlockSpec((None, None, D_EXPERT, 1024), lambda i, lo, hi, *_: (layer, hi[i], 0, 0)),
        ],
        out_specs=pl.BlockSpec((tm * TOKEN_SUBROWS, LANES), row),
    )
    return pl.pallas_call(
        _experts_kernel,
        grid_spec=grid_spec,
        out_shape=jax.ShapeDtypeStruct((n_tiles * tm * TOKEN_SUBROWS, LANES), F32),
        compiler_params=_cparams("arbitrary"),
        name="moe_experts",
    )(lo, hi, valid, xs, gates, w_gu, w_gu, w_dn, w_dn)


def _final_kernel(alpha, pos_ref, f_hbm, h1_ref, g2_ref, lg_ref, lb_ref, o_ref, buf, sem):
    i = pl.program_id(0)
    last = pl.num_programs(0) - 1
    slot = i % 2
    nxt = jnp.minimum(i + 1, last)
    gather = functools.partial(_TokenGather, pos_ref, src_hbm=f_hbm, buf=buf, sem=sem)

    tm = o_ref.shape[0]

    @pl.when(i == 0)
    def _():
        gather(base=0, slot=0).start()

    gather(base=i * tm, slot=slot).wait()
    gather(base=nxt * tm, slot=1 - slot).start()
    f = _untile_tokens(buf.at[slot])
    o_ref[...] = _ln(alpha * h1_ref[...] + g2_ref[...] * f, lg_ref[...], lb_ref[...])

    @pl.when(i == last)
    def _():
        gather(base=nxt * tm, slot=1 - slot).wait()


def _final_ln(h1, pos, fs, mod_l, ln2, seq_len, ctx_row, alpha):
    n = h1.shape[0]
    tm = FINAL_TILE
    row = lambda i, *_: (i, 0)
    const = lambda i, *_: (0, 0)
    grid_spec = pltpu.PrefetchScalarGridSpec(
        num_scalar_prefetch=1,
        grid=(n // tm,),
        in_specs=[
            pl.BlockSpec(memory_space=pl.ANY),
            pl.BlockSpec((tm, 1024), row),
            _mod_spec(5, tm, seq_len, ctx_row),
            pl.BlockSpec((1, 1024), const),
            pl.BlockSpec((1, 1024), const),
        ],
        out_specs=pl.BlockSpec((tm, 1024), row),
        scratch_shapes=[pltpu.VMEM((2, tm * TOKEN_SUBROWS, LANES), F32), pltpu.SemaphoreType.DMA((2,))],
    )
    return pl.pallas_call(
        functools.partial(_final_kernel, alpha),
        grid_spec=grid_spec,
        out_shape=jax.ShapeDtypeStruct((n, 1024), F32),
        compiler_params=_cparams("arbitrary"),
        name="final_ln",
    )(pos, fs, h1, mod_l, ln2[0], ln2[1])


def _sort_plan(route_t, n):
    tm = EXPERT_TILE
    n_tiles = n // tm + N_CLASSES
    cls = route_t[AUX_CLS].astype(jnp.int32)
    rank = route_t[AUX_RANK].astype(jnp.int32)
    onehot = cls[:, None] == jnp.arange(N_CLASSES, dtype=jnp.int32)[None, :]
    counts = jnp.sum(onehot, axis=0, dtype=jnp.int32)
    padded = (counts + tm - 1) // tm * tm
    ends = jnp.cumsum(padded)
    offs = ends - padded
    pos = jnp.sum(jnp.where(onehot, offs[None, :], 0), axis=1) + rank
    tile_ends = ends // tm
    j = jnp.arange(n_tiles, dtype=jnp.int32)
    n_used = tile_ends[-1]
    valid = j < n_used
    tile_cls = jnp.sum(j[:, None] >= tile_ends[None, :], axis=1)
    last_cls = jnp.sum((n_used - 1) >= tile_ends)
    tile_cls = jnp.where(valid, tile_cls, last_cls)
    group, pid = tile_cls // len(_PAIRS), tile_cls % len(_PAIRS)
    pair = jnp.asarray(np.array(_PAIRS, dtype=np.int32))
    lo = group * EXPERTS_PER_GROUP + pair[pid, 0]
    hi = group * EXPERTS_PER_GROUP + pair[pid, 1]
    order = jnp.argsort(cls, stable=True).astype(jnp.int32)
    starts = jnp.cumsum(counts) - counts
    first = j * tm - offs[tile_cls]
    base = jnp.where(valid, starts[tile_cls] + first, 0)
    cnt = jnp.where(valid, jnp.clip(counts[tile_cls] - first, 0, tm), 0)
    i32 = lambda t: t.astype(jnp.int32)
    return i32(pos), order, i32(base), i32(cnt), i32(lo), i32(hi), i32(valid)


def _moe(u2t, route_t, h1, mod_l, ln2, w_rt, w_gu, w_dn, layer, seq_len, ctx_row, alpha):
    n = h1.shape[0]
    pos, order, base, cnt, lo, hi, valid = _sort_plan(route_t, n)
    xs, gates = _group_tokens(lo, hi, base, cnt, order, u2t, w_rt)
    fs = _experts(lo, hi, valid, xs, gates, w_gu, w_dn, layer)
    return _final_ln(h1, pos, fs, mod_l, ln2, seq_len, ctx_row, alpha)


def _rope_tables(seq_len, n_heads, scale):
    t = np.arange(seq_len)
    quarter = HEAD_DIM // 4
    inv = ROPE_BASE ** (-np.arange(quarter, dtype=np.float32) / quarter)
    d = np.arange(HEAD_DIM)
    pos = np.where((d // (HEAD_DIM // 2) == 0)[None, :], (t // GRID_W)[:, None], (t % GRID_W)[:, None])
    ang = jnp.asarray(pos.astype(np.float32)) * jnp.asarray(inv[d % quarter])[None, :]
    sign = np.where(d % (HEAD_DIM // 2) < quarter, -1.0, 1.0).astype(np.float32)
    cos = jnp.cos(ang) * scale
    sin = jnp.sin(ang) * (sign * scale)[None, :]
    return jnp.tile(cos, (1, n_heads)), jnp.tile(sin, (1, n_heads))


def _flat_tables(seq_len, n_heads, scale):
    return (jnp.full((seq_len, n_heads * HEAD_DIM), scale, F32), jnp.zeros((seq_len, n_heads * HEAD_DIM), F32))


def _prep_in_weights(w_in_l):
    splits = np.cumsum([2 * D_CONV, D_ATTN, D_KV, D_KV, D_MLSTM, D_MLSTM, D_MLSTM, D_MLSTM, N_GATE_COLS])
    a, q, k, v, mq, mk, mv, mo, mg, bg = jnp.split(w_in_l, splits, axis=1)
    w_main = jnp.concatenate([a, q, k, mq, mv, mo, bg], axis=1).astype(BF16)
    order = np.array([d * 8 + kind * 4 + h for h in range(N_MLSTM_HEADS) for d in range(2) for kind in range(2)])
    w_t = jnp.concatenate([mk.T, mg[:, order].T, v.T], axis=0).astype(BF16)
    return w_main, w_t


def kernel(x, c, ctx, c_ctx, ln_in_g, ln_in_b, w_router, b_router, w_mod, b_mod, w_in, conv_w, conv_b, conv_ln_g,
           conv_ln_b, w_a_out, attn_sink, w_b_out, mlstm_gate_b, mlstm_norm_g, w_c_out, w_out, ln1_g, ln1_b,
           moe_w_gu, moe_w_dn, ln2_g, ln2_b):
    batch, seq_len, d = x.shape
    ctx_len = ctx.shape[1]
    depth = w_in.shape[0]
    alpha = (2.0 * depth) ** 0.25
    ctx_row = batch
    assert d == 1024 and batch < MOD_ROWS
    assert seq_len % MERGE_TILE == 0 and seq_len % TOKEN_TILE == 0 and (batch * ctx_len) % MERGE_TILE == 0
    assert ctx_len % MLSTM_CHUNK == 0 and ctx_len % 8 == 0
    assert seq_len % FINAL_TILE == 0 and (batch * ctx_len) % FINAL_TILE == 0 and FINAL_TILE % EXPERT_TILE == 0

    cc = jnp.zeros((MOD_ROWS, d), F32).at[0:batch].set(c).at[batch].set(c_ctx)
    mod = _modulation(cc, w_mod, b_mod).reshape(depth, MOD_ROWS * N_MOD, 1, d)

    attn_scale = HEAD_DIM ** -0.5
    rope_lat = _rope_tables(seq_len, N_Q_HEADS, attn_scale) + _rope_tables(seq_len, N_KV_HEADS, 1.0)
    rope_ctx = _flat_tables(ctx_len, N_Q_HEADS, attn_scale) + _flat_tables(ctx_len, N_KV_HEADS, 1.0)

    vec = lambda t: t.reshape(1, -1)
    ln_in = (vec(ln_in_g), vec(ln_in_b))
    w_rt = w_router.T
    b_r = b_router.reshape(N_EXPERTS, 1)

    h = x.reshape(batch * seq_len, d)
    hc = ctx.reshape(batch * ctx_len, d)
    for l in range(depth):
        need_ctx = l < depth - 1
        pre_ln = l == 0
        mod_l = mod[l]
        w_main, w_t = _prep_in_weights(w_in[l])
        lat = _in_proj(h, ln_in[0], ln_in[1], mod_l, w_main, w_t, rope_lat, seq_len, None, pre_ln)
        cx = _in_proj(hc, ln_in[0], ln_in[1], mod_l, w_main, w_t, rope_ctx, ctx_len, ctx_row, pre_ln)
        a_l, q_l, k_l, v_l, mq_l, mv_l, mo_l, bg_l, kt_l, gt_l = lat
        a_c, q_c, k_c, v_c, mq_c, mv_c, mo_c, bg_c, kt_c, gt_c = cx

        conv_args = (conv_w[l], vec(conv_b[l]), vec(conv_ln_g[l]), vec(conv_ln_b[l]))
        gate_b = jnp.transpose(mlstm_gate_b[l], (2, 0, 1)).reshape(N_MLSTM_HEADS, 4, 1)
        ya = _conv_branch(a_l, *conv_args, seq_len)
        yb = _attn_latent(q_l, k_l, v_l, k_c, v_c, attn_sink[l], seq_len, ctx_len)
        yc, yc_c = _mlstm_branch((mq_c, kt_c, mv_c, mo_c, gt_c), (mq_l, kt_l, mv_l, mo_l, gt_l),
                                 gate_b, vec(mlstm_norm_g[l]), seq_len, ctx_len, need_ctx)

        ln1 = (vec(ln1_g[l]), vec(ln1_b[l]))
        ln2 = (vec(ln2_g[l]), vec(ln2_b[l]))
        w_a, w_b, w_c, w_o = (w.astype(BF16) for w in (w_a_out[l], w_b_out[l], w_c_out[l], w_out[l]))

        h1, pay, route_t = _merge(ya, yb, yc, bg_l, h, mod_l, ln_in, ln1, w_a, w_b, w_c, w_o, w_rt, b_r,
                                  seq_len, None, pre_ln, alpha)
        h = _moe(pay, route_t, h1, mod_l, ln2, w_rt, moe_w_gu, moe_w_dn, l, seq_len, None, alpha)
        if need_ctx:
            ya_c = _conv_branch(a_c, *conv_args, ctx_len)
            yb_c = _attn_context(q_c, k_c, v_c, attn_sink[l], ctx_len)
            h1c, pay_c, route_tc = _merge(ya_c, yb_c, yc_c, bg_c, hc, mod_l, ln_in, ln1, w_a, w_b, w_c, w_o, w_rt,
                                          b_r, ctx_len, ctx_row, pre_ln, alpha)
            hc = _moe(pay_c, route_tc, h1c, mod_l, ln2, w_rt, moe_w_gu, moe_w_dn, l, ctx_len, ctx_row, alpha)
    return h.reshape(batch, seq_len, d)
```

```python
import functools

import numpy as np
import jax
import jax.numpy as jnp
from jax import lax
from jax.experimental import pallas as pl
from jax.experimental.pallas import tpu as pltpu

GRID_W = 64
LN_EPS = 1e-5
D_CONV = 512
CONV_WIDTH = 31
N_Q_HEADS = 8
N_KV_HEADS = 2
HEAD_DIM = 64
WINDOW = 128
BLOCK = 128
ROPE_BASE = 10000.0
D_ATTN = N_Q_HEADS * HEAD_DIM
D_KV = N_KV_HEADS * HEAD_DIM
N_MLSTM_HEADS = 4
MLSTM_HEAD_DIM = 128
D_MLSTM = N_MLSTM_HEADS * MLSTM_HEAD_DIM
N_GATE_COLS = 2 * 2 * N_MLSTM_HEADS
N_BRANCHES = 3
N_EXPERTS = 16
N_GROUPS = 4
EXPERTS_PER_GROUP = N_EXPERTS // N_GROUPS
D_EXPERT = 512
N_MOD = 6

LANES = 128
V7X_VMEM_LIMIT_BYTES = 56 * 1024 * 1024

MOD_ROWS = 16
MOD_COL_BLOCK = 1536
TOKEN_TILE = 512
EXPERT_TILE = 256
GROUP_COPY_PARTS = 4
FINAL_TILE = 512
MLSTM_CHUNK = 128
MLSTM_HEADS_PER_STEP = 4
MERGE_TILE = 512
CONV_ROWS = 64
CONV_PAD = 16

_PAIRS = [(i, j) for i in range(EXPERTS_PER_GROUP) for j in range(i + 1, EXPERTS_PER_GROUP)]
N_CLASSES = N_GROUPS * len(_PAIRS)
N_CLASS_ROWS = 32
AUX_CLS, AUX_W_LO, AUX_W_HI, AUX_RANK = 0, 1, 2, 3
TOKEN_SUBROWS = 1024 // LANES

F32 = jnp.float32
BF16 = jnp.bfloat16
HIGHEST = lax.Precision.HIGHEST
NEG_INF = float("-inf")

_C_A = 0
_C_Q = _C_A + 2 * D_CONV
_C_K = _C_Q + D_ATTN
_C_MQ = _C_K + D_KV
_C_MV = _C_MQ + D_MLSTM
_C_MO = _C_MV + D_MLSTM
_C_BG = _C_MO + D_MLSTM
_C_END = _C_BG + N_BRANCHES * 1024
_R_KT = 0
_R_GT = _R_KT + D_MLSTM
_R_VT = _R_GT + N_GATE_COLS


def _cparams(*sem):
    return pltpu.CompilerParams(dimension_semantics=sem, vmem_limit_bytes=V7X_VMEM_LIMIT_BYTES)


def _ln(x, g, b):
    mu = jnp.mean(x, axis=-1, keepdims=True)
    xc = x - mu
    var = jnp.mean(xc * xc, axis=-1, keepdims=True)
    return xc * lax.rsqrt(var + LN_EPS) * g + b


def _sigmoid(x):
    return 0.5 * jnp.tanh(0.5 * x) + 0.5


def _log_sigmoid(x):
    return jnp.minimum(x, 0.0) - jnp.log(1.0 + jnp.exp(-jnp.abs(x)))


def _dot_nt(a, b, precision=None):
    return lax.dot_general(a, b, (((1,), (1,)), ((), ())), preferred_element_type=F32, precision=precision)


def _mod_kernel(c_ref, w_ref, b_ref, o_ref):
    c = c_ref[...]
    s = c * _sigmoid(c)
    o_ref[...] = jnp.dot(s, w_ref[...], preferred_element_type=F32, precision=HIGHEST) + b_ref[...]


def _modulation(cc, w_mod, b_mod):
    depth, d, n = w_mod.shape
    return pl.pallas_call(
        _mod_kernel,
        grid=(depth, n // MOD_COL_BLOCK),
        in_specs=[
            pl.BlockSpec((MOD_ROWS, d), lambda l, j: (0, 0)),
            pl.BlockSpec((None, d, MOD_COL_BLOCK), lambda l, j: (l, 0, j)),
            pl.BlockSpec((None, 1, MOD_COL_BLOCK), lambda l, j: (l, 0, j)),
        ],
        out_specs=pl.BlockSpec((None, MOD_ROWS, MOD_COL_BLOCK), lambda l, j: (l, 0, j)),
        out_shape=jax.ShapeDtypeStruct((depth, MOD_ROWS, n), F32),
        compiler_params=_cparams("parallel", "parallel"),
        name="modulation",
    )(cc, w_mod, b_mod.reshape(depth, 1, n))


def _mod_spec(which, tile, seq_len, ctx_row):
    tiles_per_seq = seq_len // tile
    if ctx_row is None:
        return pl.BlockSpec((None, 1, 1024), lambda i, *_: ((i // tiles_per_seq) * N_MOD + which, 0, 0))
    return pl.BlockSpec((None, 1, 1024), lambda i, *_: (ctx_row * N_MOD + which, 0, 0))


def _in_kernel(pre_ln, full, x_ref, lg_ref, lb_ref, sc_ref, sh_ref, w_ref, wt_ref, cq_ref, sq_ref, ck_ref, sk_ref,
               *out_refs):
    if full:
        a_ref, q_ref, k_ref, v_ref, mq_ref, mv_ref, mo_ref, bg_ref, kt_ref, gt_ref = out_refs
    else:
        k_ref, v_ref, mq_ref, mv_ref, mo_ref, kt_ref, gt_ref = out_refs
    x = x_ref[...]
    if pre_ln:
        x = _ln(x, lg_ref[...], lb_ref[...])
    u = (x * (1.0 + sc_ref[...]) + sh_ref[...]).astype(BF16)

    def seg(lo, hi):
        return jnp.dot(u, w_ref[:, lo:hi], preferred_element_type=F32)

    def rotary(t, cos, sin):
        n = t.shape[1]
        quarter = HEAD_DIM // 4
        lane = lax.broadcasted_iota(jnp.int32, t.shape, 1)
        up = pltpu.roll(t, shift=n - quarter, axis=1)
        down = pltpu.roll(t, shift=quarter, axis=1)
        partner = jnp.where(lane % (2 * quarter) < quarter, up, down)
        return (t * cos + partner * sin).astype(BF16)

    if full:
        a_ref[...] = seg(_C_A, _C_Q).astype(BF16)
        q_ref[...] = rotary(seg(_C_Q, _C_K), cq_ref[...], sq_ref[...])
        for j in range(N_BRANCHES):
            bg_ref[:, j * 1024:(j + 1) * 1024] = _sigmoid(seg(_C_BG + j * 1024, _C_BG + (j + 1) * 1024)).astype(BF16)
    k_ref[...] = rotary(seg(_C_K, _C_MQ), ck_ref[...], sk_ref[...])
    v_ref[...] = _dot_nt(wt_ref[_R_VT:_R_VT + D_KV, :], u).astype(BF16)
    mq_ref[...] = seg(_C_MQ, _C_MV).astype(BF16)
    mv_ref[...] = seg(_C_MV, _C_MO).astype(BF16)
    mo_ref[...] = seg(_C_MO, _C_BG).astype(BF16)
    n_chunks = u.shape[0] // MLSTM_CHUNK
    kt = _dot_nt(wt_ref[_R_KT:_R_KT + D_MLSTM, :], u)
    for h in range(N_MLSTM_HEADS):
        for c in range(n_chunks):
            kt_ref[h, c] = kt[h * MLSTM_HEAD_DIM:(h + 1) * MLSTM_HEAD_DIM,
                              c * MLSTM_CHUNK:(c + 1) * MLSTM_CHUNK].astype(BF16)
    gt = _dot_nt(wt_ref[_R_GT:_R_GT + N_GATE_COLS, :], u)
    for h in range(N_MLSTM_HEADS):
        for c in range(n_chunks):
            gt_ref[h, c] = gt[h * 4:(h + 1) * 4, c * MLSTM_CHUNK:(c + 1) * MLSTM_CHUNK]


def _in_proj(x, ln_g, ln_b, mod_l, w_main, w_t, rope, seq_len, ctx_row, pre_ln, full=True):
    n = x.shape[0]
    tm = min(TOKEN_TILE, seq_len)
    cq, sq, ck, sk = rope
    tps = seq_len // tm
    nch = n // MLSTM_CHUNK
    row = lambda i: (i, 0)
    pos = lambda i: (i % tps, 0)
    const = lambda i: (0, 0)
    out_shape = [
        jax.ShapeDtypeStruct((n, 2 * D_CONV), BF16),
        jax.ShapeDtypeStruct((n, D_ATTN), BF16),
        jax.ShapeDtypeStruct((n, D_KV), BF16),
        jax.ShapeDtypeStruct((D_KV, n), BF16),
        jax.ShapeDtypeStruct((n, D_MLSTM), BF16),
        jax.ShapeDtypeStruct((n, D_MLSTM), BF16),
        jax.ShapeDtypeStruct((n, D_MLSTM), BF16),
        jax.ShapeDtypeStruct((n, N_BRANCHES * 1024), BF16),
        jax.ShapeDtypeStruct((N_MLSTM_HEADS, nch, MLSTM_HEAD_DIM, MLSTM_CHUNK), BF16),
        jax.ShapeDtypeStruct((N_MLSTM_HEADS, nch, 4, MLSTM_CHUNK), F32),
    ]
    cpt = tm // MLSTM_CHUNK
    out_specs = [
        pl.BlockSpec((tm, 2 * D_CONV), row),
        pl.BlockSpec((tm, D_ATTN), row),
        pl.BlockSpec((tm, D_KV), row),
        pl.BlockSpec((D_KV, tm), lambda i: (0, i)),
        pl.BlockSpec((tm, D_MLSTM), row),
        pl.BlockSpec((tm, D_MLSTM), row),
        pl.BlockSpec((tm, D_MLSTM), row),
        pl.BlockSpec((tm, N_BRANCHES * 1024), row),
        pl.BlockSpec((N_MLSTM_HEADS, cpt, MLSTM_HEAD_DIM, MLSTM_CHUNK), lambda i: (0, i, 0, 0)),
        pl.BlockSpec((N_MLSTM_HEADS, cpt, 4, MLSTM_CHUNK), lambda i: (0, i, 0, 0)),
    ]
    in_specs = [
        pl.BlockSpec((tm, 1024), row),
        pl.BlockSpec((1, 1024), const),
        pl.BlockSpec((1, 1024), const),
        _mod_spec(1, tm, seq_len, ctx_row),
        _mod_spec(0, tm, seq_len, ctx_row),
        pl.BlockSpec(w_main.shape, const, pipeline_mode=pl.Buffered(1)),
        pl.BlockSpec(w_t.shape, const, pipeline_mode=pl.Buffered(1)),
        pl.BlockSpec((tm, D_ATTN), pos),
        pl.BlockSpec((tm, D_ATTN), pos),
        pl.BlockSpec((tm, D_KV), pos),
        pl.BlockSpec((tm, D_KV), pos),
    ]
    skipped = () if full else (0, 1, 7)
    keep = [j for j in range(len(out_shape)) if j not in skipped]
    outs = pl.pallas_call(
        functools.partial(_in_kernel, pre_ln, full),
        grid=(n // tm,),
        in_specs=in_specs,
        out_specs=[out_specs[j] for j in keep],
        out_shape=[out_shape[j] for j in keep],
        compiler_params=_cparams("parallel"),
        name="in_proj",
    )(x, ln_g, ln_b, mod_l, mod_l, w_main, w_t, cq, sq, ck, sk)
    result = [None] * len(out_shape)
    for j, o in zip(keep, outs):
        result[j] = o
    return result


def _conv_kernel(a_ref, w_ref, cb_ref, g_ref, b_ref, o_ref, upad_ref):
    t = a_ref.shape[0]
    zeros = jnp.zeros((CONV_PAD, D_CONV), F32)
    upad_ref[0:CONV_PAD, :] = zeros
    upad_ref[CONV_PAD + t:2 * CONV_PAD + t, :] = zeros
    val = a_ref[:, 0:D_CONV].astype(F32)
    gate = a_ref[:, D_CONV:2 * D_CONV].astype(F32)
    upad_ref[CONV_PAD:CONV_PAD + t, :] = val * _sigmoid(gate)
    half = CONV_WIDTH // 2

    def body(c, carry):
        r0 = pl.multiple_of(c * CONV_ROWS, CONV_ROWS)
        n_win = CONV_ROWS + 2 * CONV_PAD
        win = upad_ref[pl.ds(r0, n_win), :]
        acc = jnp.zeros((CONV_ROWS, D_CONV), F32) + cb_ref[...]
        for res in range(8):
            rolled = win if res == 0 else pltpu.roll(win, shift=n_win - res, axis=0)
            for k in range(CONV_WIDTH):
                off = CONV_PAD - half + k
                if off % 8 == res:
                    acc = acc + rolled[off - res:off - res + CONV_ROWS, :] * w_ref[k:k + 1, :]
        y = _ln(acc, g_ref[...], b_ref[...])
        o_ref[pl.ds(r0, CONV_ROWS), :] = (y * _sigmoid(y)).astype(BF16)
        return carry

    lax.fori_loop(0, t // CONV_ROWS, body, 0)


def _conv_branch(a_in, conv_w, conv_b, ln_g, ln_b, seq_len):
    n = a_in.shape[0]
    const = lambda b: (0, 0)
    return pl.pallas_call(
        _conv_kernel,
        grid=(n // seq_len,),
        in_specs=[
            pl.BlockSpec((seq_len, 2 * D_CONV), lambda b: (b, 0)),
            pl.BlockSpec((CONV_WIDTH, D_CONV), const),
            pl.BlockSpec((1, D_CONV), const),
            pl.BlockSpec((1, D_CONV), const),
            pl.BlockSpec((1, D_CONV), const),
        ],
        out_specs=pl.BlockSpec((seq_len, D_CONV), lambda b: (b, 0)),
        out_shape=jax.ShapeDtypeStruct((n, D_CONV), BF16),
        scratch_shapes=[pltpu.VMEM((seq_len + 2 * CONV_PAD, D_CONV), F32)],
        compiler_params=_cparams("parallel"),
        name="conv_branch",
    )(a_in, conv_w, conv_b, ln_g, ln_b)


def _attn_heads(q, keys, vals_t, masks, sink_ref, o_ref):
    rows = q.shape[0]
    group = N_Q_HEADS // N_KV_HEADS
    for hk in range(N_KV_HEADS):
        lo = hk * HEAD_DIM
        qs = jnp.concatenate([q[:, (hk * group + g) * HEAD_DIM:(hk * group + g + 1) * HEAD_DIM]
                              for g in range(group)], axis=0)
        sink = jnp.concatenate([jnp.full((1, rows), sink_ref[hk * group + g], F32) for g in range(group)], axis=1)
        scores = []
        m = sink
        for kk, mask in zip(keys, masks):
            s = _dot_nt(kk[:, lo:lo + HEAD_DIM], qs)
            if mask is not None:
                s = jnp.where(mask, s, NEG_INF)
            scores.append(s)
            m = jnp.maximum(m, jnp.max(s, axis=0, keepdims=True))
        acc = jnp.zeros((2 * HEAD_DIM, rows * group), F32)
        for s, vt in zip(scores, vals_t):
            n_k = s.shape[0]
            p = jnp.exp(s - m).astype(BF16)
            ones_rows = jnp.where(lax.broadcasted_iota(jnp.int32, (HEAD_DIM, n_k), 0) == 0, 1.0, 0.0).astype(BF16)
            v_aug = jnp.concatenate([vt[lo:lo + HEAD_DIM, :], ones_rows], axis=0)
            acc = acc + jnp.dot(v_aug, p, preferred_element_type=F32)
        denom = acc[HEAD_DIM:HEAD_DIM + 1, :] + jnp.exp(sink - m)
        o_t = acc * (1.0 / denom)
        for g in range(group):
            col = (hk * group + g) * HEAD_DIM
            o_ref[:, col:col + HEAD_DIM] = o_t[:, g * rows:(g + 1) * rows].T[:, 0:HEAD_DIM].astype(BF16)


def _attn_lat_kernel(sink_ref, q_ref, kp_ref, k0_ref, kn_ref, vp_ref, v0_ref, vn_ref, kc_ref, vc_ref, o_ref):
    n = pl.program_id(1)
    nb = pl.num_programs(1)
    stacked = (N_Q_HEADS // N_KV_HEADS) * BLOCK
    ki = lax.broadcasted_iota(jnp.int32, (BLOCK, stacked), 0)
    qi = lax.broadcasted_iota(jnp.int32, (BLOCK, stacked), 1) % BLOCK
    mask_prev = ki >= qi + jnp.where(n > 0, 0, BLOCK)
    mask_next = ki <= qi - jnp.where(n < nb - 1, 0, BLOCK)
    _attn_heads(q_ref[...],
                [kp_ref[...], k0_ref[...], kn_ref[...], kc_ref[...]],
                [vp_ref[...], v0_ref[...], vn_ref[...], vc_ref[...]],
                [mask_prev, None, mask_next, None], sink_ref, o_ref)


def _attn_ctx_kernel(sink_ref, q_ref, kc_ref, vc_ref, o_ref):
    _attn_heads(q_ref[...], [kc_ref[...]], [vc_ref[...]], [None], sink_ref, o_ref)


def _attn_latent(q, k, vt, kc, vct, sink, seq_len, ctx_len):
    n = q.shape[0]
    nb = seq_len // BLOCK
    batch = n // seq_len
    blk_prev = lambda b, j: b * nb + jnp.maximum(j - 1, 0)
    blk_next = lambda b, j: b * nb + jnp.minimum(j + 1, nb - 1)
    cur = lambda b, j: (b * nb + j, 0)
    kspec = lambda blk: pl.BlockSpec((BLOCK, D_KV), lambda b, j: (blk(b, j), 0))
    vspec = lambda blk: pl.BlockSpec((D_KV, BLOCK), lambda b, j: (0, blk(b, j)))
    blk_cur = lambda b, j: b * nb + j
    return pl.pallas_call(
        _attn_lat_kernel,
        grid=(batch, nb),
        in_specs=[
            pl.BlockSpec(memory_space=pltpu.SMEM),
            pl.BlockSpec((BLOCK, D_ATTN), cur),
            kspec(blk_prev), kspec(blk_cur), kspec(blk_next), vspec(blk_prev), vspec(blk_cur), vspec(blk_next),
            pl.BlockSpec((ctx_len, D_KV), lambda b, j: (b, 0)),
            pl.BlockSpec((D_KV, ctx_len), lambda b, j: (0, b)),
        ],
        out_specs=pl.BlockSpec((BLOCK, D_ATTN), cur),
        out_shape=jax.ShapeDtypeStruct((n, D_ATTN), BF16),
        compiler_params=_cparams("parallel", "parallel"),
        name="attn_latent",
    )(sink, q, k, k, k, vt, vt, vt, kc, vct)


def _attn_context(qc, kc, vct, sink, ctx_len):
    n = qc.shape[0]
    blk = lambda b: (b, 0)
    return pl.pallas_call(
        _attn_ctx_kernel,
        grid=(n // ctx_len,),
        in_specs=[
            pl.BlockSpec(memory_space=pltpu.SMEM),
            pl.BlockSpec((ctx_len, D_ATTN), blk),
            pl.BlockSpec((ctx_len, D_KV), blk),
            pl.BlockSpec((D_KV, ctx_len), lambda b: (0, b)),
        ],
        out_specs=pl.BlockSpec((ctx_len, D_ATTN), blk),
        out_shape=jax.ShapeDtypeStruct((n, D_ATTN), BF16),
        compiler_params=_cparams("parallel"),
        name="attn_context",
    )(sink, qc, kc, vct)


def _mlstm_kernel(ctx_out, nch_c, nch_l, hps,
                  qc_ref, ktc_ref, vc_ref, moc_ref, gc_ref,
                  ql_ref, ktl_ref, vl_ref, mol_ref, gl_ref,
                  gb_ref, ng_ref, *rest):
    if ctx_out:
        yl_ref, yc_ref, af_ref, lff_ref, ab_ref, lfb_ref, cf_ref, cb_ref, hf_ref, hb_ref = rest
    else:
        yl_ref, af_ref, lff_ref, ab_ref, lfb_ref, cf_ref, cb_ref, hf_ref, hb_ref = rest
        yc_ref = None
    lc = MLSTM_CHUNK
    dh = MLSTM_HEAD_DIM
    k_scale = MLSTM_HEAD_DIM ** -0.5
    ti = lax.broadcasted_iota(jnp.int32, (lc, lc), 0)
    si = lax.broadcasted_iota(jnp.int32, (lc, lc), 1)
    lower = si <= ti
    upper = si >= ti
    pre_mat = jnp.where(upper, 1.0, 0.0)
    suf_mat = jnp.where(lower, 1.0, 0.0)
    n_rows = nch_c + nch_l
    pad_rows = -n_rows % 8

    for hh in range(hps):
        gb = gb_ref[hh]

        def gate_rows(kind):
            rows = ([gc_ref[hh, c, kind:kind + 1, :] for c in range(nch_c)]
                    + [gl_ref[hh, c, kind:kind + 1, :] for c in range(nch_l)])
            rows = jnp.concatenate(rows, axis=0) + gb[kind:kind + 1, :]
            if pad_rows:
                rows = jnp.concatenate([rows, jnp.zeros((pad_rows, lc), F32)], axis=0)
            return rows

        lf_f = _log_sigmoid(gate_rows(1))
        lf_b = _log_sigmoid(gate_rows(3))
        a_f = gate_rows(0) - jnp.dot(lf_f, pre_mat, preferred_element_type=F32, precision=HIGHEST)
        a_b = gate_rows(2) - jnp.dot(lf_b, suf_mat, preferred_element_type=F32, precision=HIGHEST)
        for c in range(n_rows):
            af_ref[hh * n_rows + c] = a_f[c:c + 1, :]
            lff_ref[hh * n_rows + c] = lf_f[c:c + 1, :]
            ab_ref[hh * n_rows + c] = a_b[c:c + 1, :]
            lfb_ref[hh * n_rows + c] = lf_b[c:c + 1, :]

    cf_ref[...] = jnp.zeros_like(cf_ref)
    cb_ref[...] = jnp.zeros_like(cb_ref)
    ones_col = jnp.where(lax.broadcasted_iota(jnp.int32, (lc, dh), 1) == 0, 1.0, 0.0).astype(BF16)

    def chunk(q, kt, v, a_row, lf_row, c_ref, hh, m, mask):
        a_mat = jnp.where(mask, jnp.broadcast_to(a_row, (lc, lc)), NEG_INF)
        cm = jnp.max(a_mat, axis=1, keepdims=True)
        b_col = jnp.sum(jnp.where(mask, jnp.broadcast_to(lf_row, (lc, lc)), 0.0), axis=1, keepdims=True)
        mx = jnp.maximum(m, cm)
        mx_last = jnp.maximum(m, jnp.max(a_row, axis=1, keepdims=True))
        w = jnp.exp(a_mat - mx)
        s = jnp.dot(q, kt, preferred_element_type=F32) * k_scale
        p = (s * w).astype(BF16)
        w_s = jnp.exp(a_row - mx_last) * k_scale
        ktw = (kt.astype(F32) * w_s).astype(BF16)
        v_aug = jnp.concatenate([v, ones_col], axis=1)
        both = jnp.dot(jnp.concatenate([p, ktw], axis=0), v_aug, preferred_element_type=F32)
        c_old = c_ref[hh]
        inter = jnp.dot(q, c_old.astype(BF16), preferred_element_type=F32)
        tot = both[0:lc, :] + jnp.exp(m - mx) * inter
        den = tot[:, dh:dh + 1]
        h = tot[:, 0:dh] / jnp.maximum(jnp.abs(den), jnp.exp(-(b_col + mx)))
        c_ref[hh] = jnp.exp(m - mx_last) * c_old + both[lc:lc + dh, :]
        m_new = jnp.sum(lf_row, axis=1, keepdims=True) + mx_last
        return h, m_new

    def step(q_ref, kt_ref, v_ref, row0, c_f, c_b, rf, rb, ms):
        out = []
        for hh in range(hps):
            cols = slice(hh * dh, (hh + 1) * dh)
            base = hh * n_rows + row0
            h_f, m_f = chunk(q_ref[pl.ds(rf, lc), cols], kt_ref[hh, c_f], v_ref[pl.ds(rf, lc), cols],
                             af_ref[base + c_f], lff_ref[base + c_f], cf_ref, hh, ms[2 * hh], lower)
            h_b, m_b = chunk(q_ref[pl.ds(rb, lc), cols], kt_ref[hh, c_b], v_ref[pl.ds(rb, lc), cols],
                             ab_ref[base + c_b], lfb_ref[base + c_b], cb_ref, hh, ms[2 * hh + 1], upper)
            hf_ref[pl.ds(rf, lc), cols] = h_f
            hb_ref[pl.ds(rb, lc), cols] = h_b
            out += [m_f, m_b]
        return tuple(out)

    def finish(n_tok, mo_ref, o_ref):
        for hh in range(hps):
            cols = slice(hh * dh, (hh + 1) * dh)
            h = hf_ref[0:n_tok, cols] + hb_ref[0:n_tok, cols]
            mu = jnp.mean(h, axis=-1, keepdims=True)
            hc = h - mu
            var = jnp.mean(hc * hc, axis=-1, keepdims=True)
            y = hc * lax.rsqrt(var + LN_EPS) * ng_ref[:, cols]
            o_ref[:, cols] = (_sigmoid(mo_ref[:, cols].astype(F32)) * y).astype(BF16)

    ms = tuple(jnp.zeros((1, 1), F32) for _ in range(2 * hps))
    for c in range(nch_c):
        c_b = nch_c - 1 - c
        ms = step(qc_ref, ktc_ref, vc_ref, 0, c, c_b, c * lc, c_b * lc, ms)
    if ctx_out:
        finish(nch_c * lc, moc_ref, yc_ref)

    def body(c, ms):
        c_b = nch_l - 1 - c
        return step(ql_ref, ktl_ref, vl_ref, nch_c, c, c_b,
                    pl.multiple_of(c * lc, lc), pl.multiple_of(c_b * lc, lc), ms)

    lax.fori_loop(0, nch_l, body, ms)
    finish(nch_l * lc, mol_ref, yl_ref)


def _mlstm_branch(ctx_p, lat_p, gate_b, norm_g, seq_len, ctx_len, ctx_out):
    mq_c, kt_c, mv_c, mo_c, g_c = ctx_p
    mq_l, kt_l, mv_l, mo_l, g_l = lat_p
    n_l, n_c = mq_l.shape[0], mq_c.shape[0]
    batch = n_l // seq_len
    lc, dh, hps = MLSTM_CHUNK, MLSTM_HEAD_DIM, MLSTM_HEADS_PER_STEP
    nch_c, nch_l = ctx_len // lc, seq_len // lc

    def stream(t, nch):
        tok = pl.BlockSpec((t, hps * dh), lambda b, h: (b, h))
        return [tok,
                pl.BlockSpec((hps, nch, dh, lc), lambda b, h: (h, b, 0, 0)),
                tok, tok,
                pl.BlockSpec((hps, nch, 4, lc), lambda b, h: (h, b, 0, 0))]

    in_specs = stream(ctx_len, nch_c) + stream(seq_len, nch_l) + [
        pl.BlockSpec((hps, 4, 1), lambda b, h: (h, 0, 0)),
        pl.BlockSpec((1, hps * dh), lambda b, h: (0, h)),
    ]
    out_specs = [pl.BlockSpec((seq_len, hps * dh), lambda b, h: (b, h))]
    out_shape = [jax.ShapeDtypeStruct((n_l, D_MLSTM), BF16)]
    if ctx_out:
        out_specs.append(pl.BlockSpec((ctx_len, hps * dh), lambda b, h: (b, h)))
        out_shape.append(jax.ShapeDtypeStruct((n_c, D_MLSTM), BF16))
    row_scratch = pltpu.VMEM((hps * (nch_c + nch_l), 1, lc), F32)
    outs = pl.pallas_call(
        functools.partial(_mlstm_kernel, ctx_out, nch_c, nch_l, hps),
        grid=(batch, N_MLSTM_HEADS // hps),
        in_specs=in_specs,
        out_specs=out_specs,
        out_shape=out_shape,
        scratch_shapes=[row_scratch, row_scratch, row_scratch, row_scratch,
                        pltpu.VMEM((hps, dh, 2 * dh), F32), pltpu.VMEM((hps, dh, 2 * dh), F32),
                        pltpu.VMEM((seq_len, hps * dh), F32), pltpu.VMEM((seq_len, hps * dh), F32)],
        compiler_params=_cparams("parallel", "parallel"),
        name="mlstm_branch",
    )(mq_c, kt_c, mv_c, mo_c, g_c, mq_l, kt_l, mv_l, mo_l, g_l, gate_b, norm_g)
    return outs if ctx_out else (outs[0], None)


def _route(logits_t, br):
    sc = [_sigmoid(logits_t[e:e + 1, :]) for e in range(N_EXPERTS)]
    sel = [sc[e] + br[e:e + 1, :] for e in range(N_EXPERTS)]
    epg = EXPERTS_PER_GROUP
    group_score = []
    for g in range(N_GROUPS):
        v = sel[g * epg:(g + 1) * epg]
        best = None
        for i in range(epg):
            for j in range(i + 1, epg):
                pair = v[i] + v[j]
                best = pair if best is None else jnp.maximum(best, pair)
        group_score.append(best)
    g_idx = jnp.zeros_like(group_score[0], dtype=jnp.int32)
    best = group_score[0]
    for g in range(1, N_GROUPS):
        better = group_score[g] > best
        g_idx = jnp.where(better, g, g_idx)
        best = jnp.maximum(best, group_score[g])
    chosen = []
    for g in range(N_GROUPS):
        v = sel[g * epg:(g + 1) * epg]
        in_g = g_idx == g
        for i in range(epg):
            rank = jnp.zeros_like(g_idx)
            for j in range(epg):
                if j == i:
                    continue
                ahead = (v[j] >= v[i]) if j < i else (v[j] > v[i])
                rank = rank + jnp.where(ahead, 1, 0)
            chosen.append(in_g & (rank < 2))
    cls = jnp.zeros_like(sc[0])
    w_lo = jnp.zeros_like(sc[0])
    w_hi = jnp.zeros_like(sc[0])
    for g in range(N_GROUPS):
        for pid, (i, j) in enumerate(_PAIRS):
            lo, hi = g * epg + i, g * epg + j
            is_pair = chosen[lo] & chosen[hi]
            cls = jnp.where(is_pair, float(g * len(_PAIRS) + pid), cls)
            w_lo = jnp.where(is_pair, sc[lo], w_lo)
            w_hi = jnp.where(is_pair, sc[hi], w_hi)
    total = w_lo + w_hi
    return cls, w_lo / total, w_hi / total


def _merge_kernel(pre_ln, alpha, ya_ref, yb_ref, yc_ref, bg_ref, h_ref, g1_ref, sc2_ref, sh2_ref,
                  lig_ref, lib_ref, l1g_ref, l1b_ref, wa_ref, wb_ref, wc_ref, wo_ref, wr_ref, br_ref,
                  h1_ref, u2t_ref, route_ref, cnt_ref, tri_ref):
    tm = h_ref.shape[0]

    def branch(y_ref, w_ref, j):
        gate = bg_ref[:, j * 1024:(j + 1) * 1024].astype(F32)
        return gate * jnp.dot(y_ref[...], w_ref[...], preferred_element_type=F32)

    mix = branch(ya_ref, wa_ref, 0) + branch(yb_ref, wb_ref, 1) + branch(yc_ref, wc_ref, 2)
    y = jnp.dot(mix.astype(BF16), wo_ref[...], preferred_element_type=F32)
    h = h_ref[...]
    if pre_ln:
        h = _ln(h, lig_ref[...], lib_ref[...])
    h1 = _ln(alpha * h + g1_ref[...] * y, l1g_ref[...], l1b_ref[...])
    h1_ref[...] = h1
    u2 = h1 * (1.0 + sc2_ref[...]) + sh2_ref[...]
    logits_t = _dot_nt(wr_ref[...], u2, precision=HIGHEST)
    cls, w_lo, w_hi = _route(logits_t, br_ref[...])

    @pl.when(pl.program_id(0) == 0)
    def _():
        cnt_ref[...] = jnp.zeros_like(cnt_ref)
        earlier = lax.broadcasted_iota(jnp.int32, (tm, tm), 0) <= lax.broadcasted_iota(jnp.int32, (tm, tm), 1)
        tri_ref[...] = jnp.where(earlier, 1.0, 0.0).astype(BF16)

    crow = lax.broadcasted_iota(jnp.int32, (N_CLASS_ROWS, tm), 0).astype(F32)
    onehot = jnp.where(crow == cls, 1.0, 0.0)
    incl = jnp.dot(onehot.astype(BF16), tri_ref[...], preferred_element_type=F32)
    base = cnt_ref[...]
    rank = jnp.sum(onehot * (incl - 1.0 + base), axis=0, keepdims=True)
    cnt_ref[...] = base + incl[:, tm - 1:tm]

    route_ref[...] = jnp.concatenate([cls, w_lo, w_hi, rank, jnp.zeros((4, tm), F32)], axis=0)
    for s in range(TOKEN_SUBROWS):
        u2t_ref[pl.ds(s, tm, stride=TOKEN_SUBROWS), :] = u2[:, s * LANES:(s + 1) * LANES]


def _merge(ya, yb, yc, bg, h, mod_l, ln_in, ln1, w_a, w_b, w_c, w_o, w_rt, b_r, seq_len, ctx_row, pre_ln, alpha):
    n = ya.shape[0]
    tm = MERGE_TILE
    row = lambda i: (i, 0)
    const = lambda i: (0, 0)
    vec = pl.BlockSpec((1, 1024), const)
    wspec = lambda w: pl.BlockSpec(w.shape, const)
    return pl.pallas_call(
        functools.partial(_merge_kernel, pre_ln, alpha),
        grid=(n // tm,),
        in_specs=[
            pl.BlockSpec((tm, D_CONV), row),
            pl.BlockSpec((tm, D_ATTN), row),
            pl.BlockSpec((tm, D_MLSTM), row),
            pl.BlockSpec((tm, N_BRANCHES * 1024), row),
            pl.BlockSpec((tm, 1024), row),
            _mod_spec(2, tm, seq_len, ctx_row),
            _mod_spec(4, tm, seq_len, ctx_row),
            _mod_spec(3, tm, seq_len, ctx_row),
            vec, vec, vec, vec,
            wspec(w_a), wspec(w_b), wspec(w_c), wspec(w_o), wspec(w_rt), wspec(b_r),
        ],
        out_specs=[
            pl.BlockSpec((tm, 1024), row),
            pl.BlockSpec((tm * TOKEN_SUBROWS, LANES), row),
            pl.BlockSpec((8, tm), lambda i: (0, i)),
        ],
        out_shape=[
            jax.ShapeDtypeStruct((n, 1024), F32),
            jax.ShapeDtypeStruct((n * TOKEN_SUBROWS, LANES), F32),
            jax.ShapeDtypeStruct((8, n), F32),
        ],
        scratch_shapes=[pltpu.VMEM((N_CLASS_ROWS, 1), F32), pltpu.VMEM((tm, tm), BF16)],
        compiler_params=_cparams("arbitrary"),
        name="merge",
    )(ya, yb, yc, bg, h, mod_l, mod_l, mod_l, ln_in[0], ln_in[1], ln1[0], ln1[1], w_a, w_b, w_c, w_o, w_rt, b_r)


class _TokenGather:
    def __init__(self, idx_ref, base, src_hbm, buf, sem, slot):
        self.idx_ref, self.base, self.src_hbm, self.buf, self.sem, self.slot = idx_ref, base, src_hbm, buf, sem, slot
        self.tokens = buf.shape[1] // TOKEN_SUBROWS
        self.last = idx_ref.shape[0] - 1

    def _copy(self, k):
        sub = TOKEN_SUBROWS
        p = pl.multiple_of(self.idx_ref[jnp.minimum(self.base + k, self.last)] * sub, sub)
        return pltpu.make_async_copy(self.src_hbm.at[pl.ds(p, sub)], self.buf.at[self.slot, pl.ds(k * sub, sub)],
                                     self.sem.at[self.slot])

    def _share(self, part, parts):
        share = self.tokens // parts
        return range(part * share, (part + 1) * share)

    def start(self, part=0, parts=1):
        for k in self._share(part, parts):
            self._copy(k).start(priority=k % 2)

    def wait(self, part=0, parts=1):
        for k in self._share(part, parts):
            self._copy(k).wait()


def _untile_tokens(ref):
    tokens = ref.shape[0] // TOKEN_SUBROWS
    return jnp.concatenate([ref[pl.ds(s, tokens, stride=TOKEN_SUBROWS), :] for s in range(TOKEN_SUBROWS)], axis=1)


def _group_kernel(lo_ref, hi_ref, base_ref, cnt_ref, order_ref, x_hbm, wr_ref, xs_ref, gate_ref, buf, sem):
    i = pl.program_id(0)
    last = pl.num_programs(0) - 1
    slot = i % 2
    nxt = jnp.minimum(i + 1, last)
    gather = functools.partial(_TokenGather, order_ref, src_hbm=x_hbm, buf=buf, sem=sem)
    parts = GROUP_COPY_PARTS
    share = (buf.shape[1] // TOKEN_SUBROWS) // parts

    @pl.when(i == 0)
    def _():
        buf[...] = jnp.zeros_like(buf)

    for p in range(parts):
        @pl.when(jnp.logical_and(i == 0, cnt_ref[0] > p * share))
        def _():
            gather(base=base_ref[0], slot=0).start(p, parts)

    for p in range(parts):
        @pl.when(cnt_ref[i] > p * share)
        def _():
            gather(base=base_ref[i], slot=slot).wait(p, parts)

    for p in range(parts):
        @pl.when(cnt_ref[nxt] > p * share)
        def _():
            gather(base=base_ref[nxt], slot=1 - slot).start(p, parts)

    x32 = _untile_tokens(buf.at[slot])

    def affinity(e):
        logit = jnp.sum(x32 * wr_ref[pl.ds(e, 1), :], axis=1, keepdims=True)
        return _sigmoid(logit)

    s_lo, s_hi = affinity(lo_ref[i]), affinity(hi_ref[i])
    total = s_lo + s_hi
    xs_ref[...] = x32.astype(BF16)
    gate_ref[...] = jnp.concatenate([s_lo / total, s_hi / total], axis=1)

    for p in range(parts):
        @pl.when(jnp.logical_and(i == last, cnt_ref[nxt] > p * share))
        def _():
            gather(base=base_ref[nxt], slot=1 - slot).wait(p, parts)


def _group_tokens(lo, hi, base, cnt, order, u2t, w_rt):
    tm = EXPERT_TILE
    n_tiles = base.shape[0]
    row = lambda i, *_: (i, 0)
    grid_spec = pltpu.PrefetchScalarGridSpec(
        num_scalar_prefetch=5,
        grid=(n_tiles,),
        in_specs=[
            pl.BlockSpec(memory_space=pl.ANY),
            pl.BlockSpec(w_rt.shape, lambda i, *_: (0, 0)),
        ],
        out_specs=[pl.BlockSpec((tm, 1024), row), pl.BlockSpec((tm, 2), row)],
        scratch_shapes=[pltpu.VMEM((2, tm * TOKEN_SUBROWS, LANES), F32), pltpu.SemaphoreType.DMA((2,))],
    )
    return pl.pallas_call(
        _group_kernel,
        grid_spec=grid_spec,
        out_shape=[jax.ShapeDtypeStruct((n_tiles * tm, 1024), BF16), jax.ShapeDtypeStruct((n_tiles * tm, 2), F32)],
        compiler_params=_cparams("arbitrary"),
        name="group_tokens",
    )(lo, hi, base, cnt, order, u2t, w_rt)


def _experts_kernel(lo_ref, hi_ref, valid_ref, x_ref, gate_ref, wgl_ref, wgh_ref, wdl_ref, wdh_ref, o_ref):
    i = pl.program_id(0)
    tm = x_ref.shape[0]

    @pl.when(valid_ref[i] != 0)
    def _():
        x = x_ref[...]

        def expert(wg_ref, wd_ref):
            gu = jnp.dot(x, wg_ref[...].astype(BF16), preferred_element_type=F32)
            g_ = gu[:, 0:D_EXPERT]
            act = (g_ * _sigmoid(g_) * gu[:, D_EXPERT:2 * D_EXPERT]).astype(BF16)
            return jnp.dot(act, wd_ref[...].astype(BF16), preferred_element_type=F32)

        out = gate_ref[:, 0:1] * expert(wgl_ref, wdl_ref) + gate_ref[:, 1:2] * expert(wgh_ref, wdh_ref)
        for s in range(TOKEN_SUBROWS):
            o_ref[pl.ds(s, tm, stride=TOKEN_SUBROWS), :] = out[:, s * LANES:(s + 1) * LANES]

    @pl.when(valid_ref[i] == 0)
    def _():
        o_ref[...] = jnp.zeros_like(o_ref)


def _experts(lo, hi, valid, xs, gates, w_gu, w_dn, layer):
    tm = EXPERT_TILE
    n_tiles = xs.shape[0] // tm
    row = lambda i, *_: (i, 0)
    grid_spec = pltpu.PrefetchScalarGridSpec(
        num_scalar_prefetch=3,
        grid=(n_tiles,),
        in_specs=[
            pl.BlockSpec((tm, 1024), row),
            pl.BlockSpec((tm, 2), row),
            pl.BlockSpec((None, None, 1024, 2 * D_EXPERT), lambda i, lo, hi, *_: (layer, lo[i], 0, 0)),
            pl.BlockSpec((None, None, 1024, 2 * D_EXPERT), lambda i, lo, hi, *_: (layer, hi[i], 0, 0)),
            pl.BlockSpec((None, None, D_EXPERT, 1024), lambda i, lo, hi, *_: (layer, lo[i], 0, 0)),
            pl.BlockSpec((None, None, D_EXPERT, 1024), lambda i, lo, hi, *_: (layer, hi[i], 0, 0)),
        ],
        out_specs=pl.BlockSpec((tm * TOKEN_SUBROWS, LANES), row),
    )
    return pl.pallas_call(
        _experts_kernel,
        grid_spec=grid_spec,
        out_shape=jax.ShapeDtypeStruct((n_tiles * tm * TOKEN_SUBROWS, LANES), F32),
        compiler_params=_cparams("arbitrary"),
        name="moe_experts",
    )(lo, hi, valid, xs, gates, w_gu, w_gu, w_dn, w_dn)


def _final_kernel(alpha, pos_ref, f_hbm, h1_ref, g2_ref, lg_ref, lb_ref, o_ref, buf, sem):
    i = pl.program_id(0)
    last = pl.num_programs(0) - 1
    slot = i % 2
    nxt = jnp.minimum(i + 1, last)
    gather = functools.partial(_TokenGather, pos_ref, src_hbm=f_hbm, buf=buf, sem=sem)

    tm = o_ref.shape[0]

    @pl.when(i == 0)
    def _():
        gather(base=0, slot=0).start()

    gather(base=i * tm, slot=slot).wait()
    gather(base=nxt * tm, slot=1 - slot).start()
    f = _untile_tokens(buf.at[slot])
    o_ref[...] = _ln(alpha * h1_ref[...] + g2_ref[...] * f, lg_ref[...], lb_ref[...])

    @pl.when(i == last)
    def _():
        gather(base=nxt * tm, slot=1 - slot).wait()


def _final_ln(h1, pos, fs, mod_l, ln2, seq_len, ctx_row, alpha):
    n = h1.shape[0]
    tm = FINAL_TILE
    row = lambda i, *_: (i, 0)
    const = lambda i, *_: (0, 0)
    grid_spec = pltpu.PrefetchScalarGridSpec(
        num_scalar_prefetch=1,
        grid=(n // tm,),
        in_specs=[
            pl.BlockSpec(memory_space=pl.ANY),
            pl.BlockSpec((tm, 1024), row),
            _mod_spec(5, tm, seq_len, ctx_row),
            pl.BlockSpec((1, 1024), const),
            pl.BlockSpec((1, 1024), const),
        ],
        out_specs=pl.BlockSpec((tm, 1024), row),
        scratch_shapes=[pltpu.VMEM((2, tm * TOKEN_SUBROWS, LANES), F32), pltpu.SemaphoreType.DMA((2,))],
    )
    return pl.pallas_call(
        functools.partial(_final_kernel, alpha),
        grid_spec=grid_spec,
        out_shape=jax.ShapeDtypeStruct((n, 1024), F32),
        compiler_params=_cparams("arbitrary"),
        name="final_ln",
    )(pos, fs, h1, mod_l, ln2[0], ln2[1])


def _sort_plan(route_t, n):
    tm = EXPERT_TILE
    n_tiles = n // tm + N_CLASSES
    cls = route_t[AUX_CLS].astype(jnp.int32)
    rank = route_t[AUX_RANK].astype(jnp.int32)
    onehot = cls[:, None] == jnp.arange(N_CLASSES, dtype=jnp.int32)[None, :]
    counts = jnp.sum(onehot, axis=0, dtype=jnp.int32)
    padded = (counts + tm - 1) // tm * tm
    ends = jnp.cumsum(padded)
    offs = ends - padded
    pos = jnp.sum(jnp.where(onehot, offs[None, :], 0), axis=1) + rank
    tile_ends = ends // tm
    j = jnp.arange(n_tiles, dtype=jnp.int32)
    n_used = tile_ends[-1]
    valid = j < n_used
    tile_cls = jnp.sum(j[:, None] >= tile_ends[None, :], axis=1)
    last_cls = jnp.sum((n_used - 1) >= tile_ends)
    tile_cls = jnp.where(valid, tile_cls, last_cls)
    group, pid = tile_cls // len(_PAIRS), tile_cls % len(_PAIRS)
    pair = jnp.asarray(np.array(_PAIRS, dtype=np.int32))
    lo = group * EXPERTS_PER_GROUP + pair[pid, 0]
    hi = group * EXPERTS_PER_GROUP + pair[pid, 1]
    order = jnp.argsort(cls, stable=True).astype(jnp.int32)
    starts = jnp.cumsum(counts) - counts
    first = j * tm - offs[tile_cls]
    base = jnp.where(valid, starts[tile_cls] + first, 0)
    cnt = jnp.where(valid, jnp.clip(counts[tile_cls] - first, 0, tm), 0)
    i32 = lambda t: t.astype(jnp.int32)
    return i32(pos), order, i32(base), i32(cnt), i32(lo), i32(hi), i32(valid)


def _moe(u2t, route_t, h1, mod_l, ln2, w_rt, w_gu, w_dn, layer, seq_len, ctx_row, alpha):
    n = h1.shape[0]
    pos, order, base, cnt, lo, hi, valid = _sort_plan(route_t, n)
    xs, gates = _group_tokens(lo, hi, base, cnt, order, u2t, w_rt)
    fs = _experts(lo, hi, valid, xs, gates, w_gu, w_dn, layer)
    return _final_ln(h1, pos, fs, mod_l, ln2, seq_len, ctx_row, alpha)


def _rope_tables(seq_len, n_heads, scale):
    t = np.arange(seq_len)
    quarter = HEAD_DIM // 4
    inv = ROPE_BASE ** (-np.arange(quarter, dtype=np.float32) / quarter)
    d = np.arange(HEAD_DIM)
    pos = np.where((d // (HEAD_DIM // 2) == 0)[None, :], (t // GRID_W)[:, None], (t % GRID_W)[:, None])
    ang = jnp.asarray(pos.astype(np.float32)) * jnp.asarray(inv[d % quarter])[None, :]
    sign = np.where(d % (HEAD_DIM // 2) < quarter, -1.0, 1.0).astype(np.float32)
    cos = jnp.cos(ang) * scale
    sin = jnp.sin(ang) * (sign * scale)[None, :]
    return jnp.tile(cos, (1, n_heads)), jnp.tile(sin, (1, n_heads))


def _flat_tables(seq_len, n_heads, scale):
    return (jnp.full((seq_len, n_heads * HEAD_DIM), scale, F32), jnp.zeros((seq_len, n_heads * HEAD_DIM), F32))


def _prep_in_weights(w_in_l):
    splits = np.cumsum([2 * D_CONV, D_ATTN, D_KV, D_KV, D_MLSTM, D_MLSTM, D_MLSTM, D_MLSTM, N_GATE_COLS])
    a, q, k, v, mq, mk, mv, mo, mg, bg = jnp.split(w_in_l, splits, axis=1)
    w_main = jnp.concatenate([a, q, k, mq, mv, mo, bg], axis=1).astype(BF16)
    order = np.array([d * 8 + kind * 4 + h for h in range(N_MLSTM_HEADS) for d in range(2) for kind in range(2)])
    w_t = jnp.concatenate([mk.T, mg[:, order].T, v.T], axis=0).astype(BF16)
    return w_main, w_t


def kernel(x, c, ctx, c_ctx, ln_in_g, ln_in_b, w_router, b_router, w_mod, b_mod, w_in, conv_w, conv_b, conv_ln_g,
           conv_ln_b, w_a_out, attn_sink, w_b_out, mlstm_gate_b, mlstm_norm_g, w_c_out, w_out, ln1_g, ln1_b,
           moe_w_gu, moe_w_dn, ln2_g, ln2_b):
    batch, seq_len, d = x.shape
    ctx_len = ctx.shape[1]
    depth = w_in.shape[0]
    alpha = (2.0 * depth) ** 0.25
    ctx_row = batch
    assert d == 1024 and batch < MOD_ROWS
    assert seq_len % MERGE_TILE == 0 and seq_len % TOKEN_TILE == 0 and (batch * ctx_len) % MERGE_TILE == 0
    assert ctx_len % MLSTM_CHUNK == 0 and ctx_len % 8 == 0
    assert seq_len % FINAL_TILE == 0 and (batch * ctx_len) % FINAL_TILE == 0 and FINAL_TILE % EXPERT_TILE == 0

    cc = jnp.zeros((MOD_ROWS, d), F32).at[0:batch].set(c).at[batch].set(c_ctx)
    mod = _modulation(cc, w_mod, b_mod).reshape(depth, MOD_ROWS * N_MOD, 1, d)

    attn_scale = HEAD_DIM ** -0.5
    rope_lat = _rope_tables(seq_len, N_Q_HEADS, attn_scale) + _rope_tables(seq_len, N_KV_HEADS, 1.0)
    rope_ctx = _flat_tables(ctx_len, N_Q_HEADS, attn_scale) + _flat_tables(ctx_len, N_KV_HEADS, 1.0)

    vec = lambda t: t.reshape(1, -1)
    ln_in = (vec(ln_in_g), vec(ln_in_b))
    w_rt = w_router.T
    b_r = b_router.reshape(N_EXPERTS, 1)

    h = x.reshape(batch * seq_len, d)
    hc = ctx.reshape(batch * ctx_len, d)
    for l in range(depth):
        need_ctx = l < depth - 1
        pre_ln = l == 0
        mod_l = mod[l]
        w_main, w_t = _prep_in_weights(w_in[l])
        lat = _in_proj(h, ln_in[0], ln_in[1], mod_l, w_main, w_t, rope_lat, seq_len, None, pre_ln)
        cx = _in_proj(hc, ln_in[0], ln_in[1], mod_l, w_main, w_t, rope_ctx, ctx_len, ctx_row, pre_ln, full=need_ctx)
        a_l, q_l, k_l, v_l, mq_l, mv_l, mo_l, bg_l, kt_l, gt_l = lat
        a_c, q_c, k_c, v_c, mq_c, mv_c, mo_c, bg_c, kt_c, gt_c = cx

        conv_args = (conv_w[l], vec(conv_b[l]), vec(conv_ln_g[l]), vec(conv_ln_b[l]))
        gate_b = jnp.transpose(mlstm_gate_b[l], (2, 0, 1)).reshape(N_MLSTM_HEADS, 4, 1)
        ya = _conv_branch(a_l, *conv_args, seq_len)
        yb = _attn_latent(q_l, k_l, v_l, k_c, v_c, attn_sink[l], seq_len, ctx_len)
        yc, yc_c = _mlstm_branch((mq_c, kt_c, mv_c, mo_c, gt_c), (mq_l, kt_l, mv_l, mo_l, gt_l),
                                 gate_b, vec(mlstm_norm_g[l]), seq_len, ctx_len, need_ctx)

        ln1 = (vec(ln1_g[l]), vec(ln1_b[l]))
        ln2 = (vec(ln2_g[l]), vec(ln2_b[l]))
        w_a, w_b, w_c, w_o = (w.astype(BF16) for w in (w_a_out[l], w_b_out[l], w_c_out[l], w_out[l]))

        h1, pay, route_t = _merge(ya, yb, yc, bg_l, h, mod_l, ln_in, ln1, w_a, w_b, w_c, w_o, w_rt, b_r,
                                  seq_len, None, pre_ln, alpha)
        h = _moe(pay, route_t, h1, mod_l, ln2, w_rt, moe_w_gu, moe_w_dn, l, seq_len, None, alpha)
        if need_ctx:
            ya_c = _conv_branch(a_c, *conv_args, ctx_len)
            yb_c = _attn_context(q_c, k_c, v_c, attn_sink[l], ctx_len)
            h1c, pay_c, route_tc = _merge(ya_c, yb_c, yc_c, bg_c, hc, mod_l, ln_in, ln1, w_a, w_b, w_c, w_o, w_rt,
                                          b_r, ctx_len, ctx_row, pre_ln, alpha)
            hc = _moe(pay_c, route_tc, h1c, mod_l, ln2, w_rt, moe_w_gu, moe_w_dn, l, ctx_len, ctx_row, alpha)
    return h.reshape(batch, seq_len, d)
```

```python
import functools

import numpy as np
import jax
import jax.numpy as jnp
from jax import lax
from jax.experimental import pallas as pl
from jax.experimental.pallas import tpu as pltpu

GRID_W = 64
LN_EPS = 1e-5
D_CONV = 512
CONV_WIDTH = 31
N_Q_HEADS = 8
N_KV_HEADS = 2
HEAD_DIM = 64
WINDOW = 128
BLOCK = 128
ROPE_BASE = 10000.0
D_ATTN = N_Q_HEADS * HEAD_DIM
D_KV = N_KV_HEADS * HEAD_DIM
N_MLSTM_HEADS = 4
MLSTM_HEAD_DIM = 128
D_MLSTM = N_MLSTM_HEADS * MLSTM_HEAD_DIM
N_GATE_COLS = 2 * 2 * N_MLSTM_HEADS
N_BRANCHES = 3
N_EXPERTS = 16
N_GROUPS = 4
EXPERTS_PER_GROUP = N_EXPERTS // N_GROUPS
D_EXPERT = 512
N_MOD = 6

LANES = 128
V7X_VMEM_LIMIT_BYTES = 56 * 1024 * 1024

MOD_ROWS = 16
MOD_COL_BLOCK = 1536
TOKEN_TILE = 512
EXPERT_TILE = 256
GROUP_COPY_PARTS = 4
FINAL_TILE = 512
MLSTM_CHUNK = 128
MLSTM_HEADS_PER_STEP = 4
MERGE_TILE = 512
CONV_ROWS = 64
CONV_PAD = 16

_PAIRS = [(i, j) for i in range(EXPERTS_PER_GROUP) for j in range(i + 1, EXPERTS_PER_GROUP)]
N_CLASSES = N_GROUPS * len(_PAIRS)
N_CLASS_ROWS = 32
AUX_CLS, AUX_W_LO, AUX_W_HI, AUX_RANK = 0, 1, 2, 3
TOKEN_SUBROWS = 1024 // LANES

F32 = jnp.float32
BF16 = jnp.bfloat16
HIGHEST = lax.Precision.HIGHEST
NEG_INF = float("-inf")

_C_A = 0
_C_Q = _C_A + 2 * D_CONV
_C_K = _C_Q + D_ATTN
_C_MQ = _C_K + D_KV
_C_MV = _C_MQ + D_MLSTM
_C_MO = _C_MV + D_MLSTM
_C_BG = _C_MO + D_MLSTM
_C_END = _C_BG + N_BRANCHES * 1024
_R_KT = 0
_R_GT = _R_KT + D_MLSTM
_R_VT = _R_GT + N_GATE_COLS


def _cparams(*sem):
    return pltpu.CompilerParams(dimension_semantics=sem, vmem_limit_bytes=V7X_VMEM_LIMIT_BYTES)


def _ln(x, g, b):
    mu = jnp.mean(x, axis=-1, keepdims=True)
    xc = x - mu
    var = jnp.mean(xc * xc, axis=-1, keepdims=True)
    return xc * lax.rsqrt(var + LN_EPS) * g + b


def _sigmoid(x):
    return 0.5 * jnp.tanh(0.5 * x) + 0.5


def _log_sigmoid(x):
    return jnp.minimum(x, 0.0) - jnp.log(1.0 + jnp.exp(-jnp.abs(x)))


def _dot_nt(a, b, precision=None):
    return lax.dot_general(a, b, (((1,), (1,)), ((), ())), preferred_element_type=F32, precision=precision)


def _mod_kernel(c_ref, w_ref, b_ref, o_ref):
    c = c_ref[...]
    s = c * _sigmoid(c)
    o_ref[...] = jnp.dot(s, w_ref[...], preferred_element_type=F32, precision=HIGHEST) + b_ref[...]


def _modulation(cc, w_mod, b_mod):
    depth, d, n = w_mod.shape
    return pl.pallas_call(
        _mod_kernel,
        grid=(depth, n // MOD_COL_BLOCK),
        in_specs=[
            pl.BlockSpec((MOD_ROWS, d), lambda l, j: (0, 0)),
            pl.BlockSpec((None, d, MOD_COL_BLOCK), lambda l, j: (l, 0, j)),
            pl.BlockSpec((None, 1, MOD_COL_BLOCK), lambda l, j: (l, 0, j)),
        ],
        out_specs=pl.BlockSpec((None, MOD_ROWS, MOD_COL_BLOCK), lambda l, j: (l, 0, j)),
        out_shape=jax.ShapeDtypeStruct((depth, MOD_ROWS, n), F32),
        compiler_params=_cparams("parallel", "parallel"),
        name="modulation",
    )(cc, w_mod, b_mod.reshape(depth, 1, n))


def _mod_spec(which, tile, seq_len, ctx_row):
    tiles_per_seq = seq_len // tile
    if ctx_row is None:
        return pl.BlockSpec((None, 1, 1024), lambda i, *_: ((i // tiles_per_seq) * N_MOD + which, 0, 0))
    return pl.BlockSpec((None, 1, 1024), lambda i, *_: (ctx_row * N_MOD + which, 0, 0))


def _in_kernel(pre_ln, full, x_ref, lg_ref, lb_ref, sc_ref, sh_ref, w_ref, wt_ref, cq_ref, sq_ref, ck_ref, sk_ref,
               *out_refs):
    if full:
        a_ref, q_ref, k_ref, v_ref, mq_ref, mv_ref, mo_ref, bg_ref, kt_ref, gt_ref = out_refs
    else:
        k_ref, v_ref, mq_ref, mv_ref, mo_ref, kt_ref, gt_ref = out_refs
    x = x_ref[...]
    if pre_ln:
        x = _ln(x, lg_ref[...], lb_ref[...])
    u = (x * (1.0 + sc_ref[...]) + sh_ref[...]).astype(BF16)

    def seg(lo, hi):
        return jnp.dot(u, w_ref[:, lo:hi], preferred_element_type=F32)

    def rotary(t, cos, sin):
        n = t.shape[1]
        quarter = HEAD_DIM // 4
        lane = lax.broadcasted_iota(jnp.int32, t.shape, 1)
        up = pltpu.roll(t, shift=n - quarter, axis=1)
        down = pltpu.roll(t, shift=quarter, axis=1)
        partner = jnp.where(lane % (2 * quarter) < quarter, up, down)
        return (t * cos + partner * sin).astype(BF16)

    if full:
        a_ref[...] = seg(_C_A, _C_Q).astype(BF16)
        q_ref[...] = rotary(seg(_C_Q, _C_K), cq_ref[...], sq_ref[...])
        for j in range(N_BRANCHES):
            bg_ref[:, j * 1024:(j + 1) * 1024] = _sigmoid(seg(_C_BG + j * 1024, _C_BG + (j + 1) * 1024)).astype(BF16)
    k_ref[...] = rotary(seg(_C_K, _C_MQ), ck_ref[...], sk_ref[...])
    v_ref[...] = _dot_nt(wt_ref[_R_VT:_R_VT + D_KV, :], u).astype(BF16)
    mq_ref[...] = seg(_C_MQ, _C_MV).astype(BF16)
    mv_ref[...] = seg(_C_MV, _C_MO).astype(BF16)
    mo_ref[...] = seg(_C_MO, _C_BG).astype(BF16)
    n_chunks = u.shape[0] // MLSTM_CHUNK
    kt = _dot_nt(wt_ref[_R_KT:_R_KT + D_MLSTM, :], u)
    for h in range(N_MLSTM_HEADS):
        for c in range(n_chunks):
            kt_ref[h, c] = kt[h * MLSTM_HEAD_DIM:(h + 1) * MLSTM_HEAD_DIM,
                              c * MLSTM_CHUNK:(c + 1) * MLSTM_CHUNK].astype(BF16)
    gt = _dot_nt(wt_ref[_R_GT:_R_GT + N_GATE_COLS, :], u)
    for h in range(N_MLSTM_HEADS):
        for c in range(n_chunks):
            gt_ref[h, c] = gt[h * 4:(h + 1) * 4, c * MLSTM_CHUNK:(c + 1) * MLSTM_CHUNK]


def _in_proj(x, ln_g, ln_b, mod_l, w_main, w_t, rope, seq_len, ctx_row, pre_ln, full=True):
    n = x.shape[0]
    tm = min(TOKEN_TILE, seq_len)
    cq, sq, ck, sk = rope
    tps = seq_len // tm
    nch = n // MLSTM_CHUNK
    row = lambda i: (i, 0)
    pos = lambda i: (i % tps, 0)
    const = lambda i: (0, 0)
    out_shape = [
        jax.ShapeDtypeStruct((n, 2 * D_CONV), BF16),
        jax.ShapeDtypeStruct((n, D_ATTN), BF16),
        jax.ShapeDtypeStruct((n, D_KV), BF16),
        jax.ShapeDtypeStruct((D_KV, n), BF16),
        jax.ShapeDtypeStruct((n, D_MLSTM), BF16),
        jax.ShapeDtypeStruct((n, D_MLSTM), BF16),
        jax.ShapeDtypeStruct((n, D_MLSTM), BF16),
        jax.ShapeDtypeStruct((n, N_BRANCHES * 1024), BF16),
        jax.ShapeDtypeStruct((N_MLSTM_HEADS, nch, MLSTM_HEAD_DIM, MLSTM_CHUNK), BF16),
        jax.ShapeDtypeStruct((N_MLSTM_HEADS, nch, 4, MLSTM_CHUNK), F32),
    ]
    cpt = tm // MLSTM_CHUNK
    out_specs = [
        pl.BlockSpec((tm, 2 * D_CONV), row),
        pl.BlockSpec((tm, D_ATTN), row),
        pl.BlockSpec((tm, D_KV), row),
        pl.BlockSpec((D_KV, tm), lambda i: (0, i)),
        pl.BlockSpec((tm, D_MLSTM), row),
        pl.BlockSpec((tm, D_MLSTM), row),
        pl.BlockSpec((tm, D_MLSTM), row),
        pl.BlockSpec((tm, N_BRANCHES * 1024), row),
        pl.BlockSpec((N_MLSTM_HEADS, cpt, MLSTM_HEAD_DIM, MLSTM_CHUNK), lambda i: (0, i, 0, 0)),
        pl.BlockSpec((N_MLSTM_HEADS, cpt, 4, MLSTM_CHUNK), lambda i: (0, i, 0, 0)),
    ]
    in_specs = [
        pl.BlockSpec((tm, 1024), row),
        pl.BlockSpec((1, 1024), const),
        pl.BlockSpec((1, 1024), const),
        _mod_spec(1, tm, seq_len, ctx_row),
        _mod_spec(0, tm, seq_len, ctx_row),
        pl.BlockSpec(w_main.shape, const, pipeline_mode=pl.Buffered(1)),
        pl.BlockSpec(w_t.shape, const, pipeline_mode=pl.Buffered(1)),
        pl.BlockSpec((tm, D_ATTN), pos),
        pl.BlockSpec((tm, D_ATTN), pos),
        pl.BlockSpec((tm, D_KV), pos),
        pl.BlockSpec((tm, D_KV), pos),
    ]
    skipped = () if full else (0, 1, 7)
    keep = [j for j in range(len(out_shape)) if j not in skipped]
    outs = pl.pallas_call(
        functools.partial(_in_kernel, pre_ln, full),
        grid=(n // tm,),
        in_specs=in_specs,
        out_specs=[out_specs[j] for j in keep],
        out_shape=[out_shape[j] for j in keep],
        compiler_params=_cparams("parallel"),
        name="in_proj",
    )(x, ln_g, ln_b, mod_l, mod_l, w_main, w_t, cq, sq, ck, sk)
    result = [None] * len(out_shape)
    for j, o in zip(keep, outs):
        result[j] = o
    return result


def _conv_kernel(a_ref, w_ref, cb_ref, g_ref, b_ref, o_ref, upad_ref):
    t = a_ref.shape[0]
    zeros = jnp.zeros((CONV_PAD, D_CONV), F32)
    upad_ref[0:CONV_PAD, :] = zeros
    upad_ref[CONV_PAD + t:2 * CONV_PAD + t, :] = zeros
    val = a_ref[:, 0:D_CONV].astype(F32)
    gate = a_ref[:, D_CONV:2 * D_CONV].astype(F32)
    upad_ref[CONV_PAD:CONV_PAD + t, :] = val * _sigmoid(gate)
    half = CONV_WIDTH // 2

    def body(c, carry):
        r0 = pl.multiple_of(c * CONV_ROWS, CONV_ROWS)
        n_win = CONV_ROWS + 2 * CONV_PAD
        win = upad_ref[pl.ds(r0, n_win), :]
        acc = jnp.zeros((CONV_ROWS, D_CONV), F32) + cb_ref[...]
        for res in range(8):
            rolled = win if res == 0 else pltpu.roll(win, shift=n_win - res, axis=0)
            for k in range(CONV_WIDTH):
                off = CONV_PAD - half + k
                if off % 8 == res:
                    acc = acc + rolled[off - res:off - res + CONV_ROWS, :] * w_ref[k:k + 1, :]
        y = _ln(acc, g_ref[...], b_ref[...])
        o_ref[pl.ds(r0, CONV_ROWS), :] = (y * _sigmoid(y)).astype(BF16)
        return carry

    lax.fori_loop(0, t // CONV_ROWS, body, 0)


def _conv_branch(a_in, conv_w, conv_b, ln_g, ln_b, seq_len):
    n = a_in.shape[0]
    const = lambda b: (0, 0)
    return pl.pallas_call(
        _conv_kernel,
        grid=(n // seq_len,),
        in_specs=[
            pl.BlockSpec((seq_len, 2 * D_CONV), lambda b: (b, 0)),
            pl.BlockSpec((CONV_WIDTH, D_CONV), const),
            pl.BlockSpec((1, D_CONV), const),
            pl.BlockSpec((1, D_CONV), const),
            pl.BlockSpec((1, D_CONV), const),
        ],
        out_specs=pl.BlockSpec((seq_len, D_CONV), lambda b: (b, 0)),
        out_shape=jax.ShapeDtypeStruct((n, D_CONV), BF16),
        scratch_shapes=[pltpu.VMEM((seq_len + 2 * CONV_PAD, D_CONV), F32)],
        compiler_params=_cparams("parallel"),
        name="conv_branch",
    )(a_in, conv_w, conv_b, ln_g, ln_b)


def _attn_heads(q, keys, vals_t, masks, sink_ref, o_ref):
    rows = q.shape[0]
    group = N_Q_HEADS // N_KV_HEADS
    for hk in range(N_KV_HEADS):
        lo = hk * HEAD_DIM
        qs = jnp.concatenate([q[:, (hk * group + g) * HEAD_DIM:(hk * group + g + 1) * HEAD_DIM]
                              for g in range(group)], axis=0)
        sink = jnp.concatenate([jnp.full((1, rows), sink_ref[hk * group + g], F32) for g in range(group)], axis=1)
        scores = []
        m = sink
        for kk, mask in zip(keys, masks):
            s = _dot_nt(kk[:, lo:lo + HEAD_DIM], qs)
            if mask is not None:
                s = jnp.where(mask, s, NEG_INF)
            scores.append(s)
            m = jnp.maximum(m, jnp.max(s, axis=0, keepdims=True))
        acc = jnp.zeros((2 * HEAD_DIM, rows * group), F32)
        for s, vt in zip(scores, vals_t):
            n_k = s.shape[0]
            p = jnp.exp(s - m).astype(BF16)
            ones_rows = jnp.where(lax.broadcasted_iota(jnp.int32, (HEAD_DIM, n_k), 0) == 0, 1.0, 0.0).astype(BF16)
            v_aug = jnp.concatenate([vt[lo:lo + HEAD_DIM, :], ones_rows], axis=0)
            acc = acc + jnp.dot(v_aug, p, preferred_element_type=F32)
        denom = acc[HEAD_DIM:HEAD_DIM + 1, :] + jnp.exp(sink - m)
        o_t = acc * (1.0 / denom)
        for g in range(group):
            col = (hk * group + g) * HEAD_DIM
            o_ref[:, col:col + HEAD_DIM] = o_t[:, g * rows:(g + 1) * rows].T[:, 0:HEAD_DIM].astype(BF16)


def _attn_lat_kernel(sink_ref, q_ref, kp_ref, k0_ref, kn_ref, vp_ref, v0_ref, vn_ref, kc_ref, vc_ref, o_ref):
    n = pl.program_id(1)
    nb = pl.num_programs(1)
    stacked = (N_Q_HEADS // N_KV_HEADS) * BLOCK
    ki = lax.broadcasted_iota(jnp.int32, (BLOCK, stacked), 0)
    qi = lax.broadcasted_iota(jnp.int32, (BLOCK, stacked), 1) % BLOCK
    mask_prev = ki >= qi + jnp.where(n > 0, 0, BLOCK)
    mask_next = ki <= qi - jnp.where(n < nb - 1, 0, BLOCK)
    _attn_heads(q_ref[...],
                [kp_ref[...], k0_ref[...], kn_ref[...], kc_ref[...]],
                [vp_ref[...], v0_ref[...], vn_ref[...], vc_ref[...]],
                [mask_prev, None, mask_next, None], sink_ref, o_ref)


def _attn_ctx_kernel(sink_ref, q_ref, kc_ref, vc_ref, o_ref):
    _attn_heads(q_ref[...], [kc_ref[...]], [vc_ref[...]], [None], sink_ref, o_ref)


def _attn_latent(q, k, vt, kc, vct, sink, seq_len, ctx_len):
    n = q.shape[0]
    nb = seq_len // BLOCK
    batch = n // seq_len
    blk_prev = lambda b, j: b * nb + jnp.maximum(j - 1, 0)
    blk_next = lambda b, j: b * nb + jnp.minimum(j + 1, nb - 1)
    cur = lambda b, j: (b * nb + j, 0)
    kspec = lambda blk: pl.BlockSpec((BLOCK, D_KV), lambda b, j: (blk(b, j), 0))
    vspec = lambda blk: pl.BlockSpec((D_KV, BLOCK), lambda b, j: (0, blk(b, j)))
    blk_cur = lambda b, j: b * nb + j
    return pl.pallas_call(
        _attn_lat_kernel,
        grid=(batch, nb),
        in_specs=[
            pl.BlockSpec(memory_space=pltpu.SMEM),
            pl.BlockSpec((BLOCK, D_ATTN), cur),
            kspec(blk_prev), kspec(blk_cur), kspec(blk_next), vspec(blk_prev), vspec(blk_cur), vspec(blk_next),
            pl.BlockSpec((ctx_len, D_KV), lambda b, j: (b, 0)),
            pl.BlockSpec((D_KV, ctx_len), lambda b, j: (0, b)),
        ],
        out_specs=pl.BlockSpec((BLOCK, D_ATTN), cur),
        out_shape=jax.ShapeDtypeStruct((n, D_ATTN), BF16),
        compiler_params=_cparams("parallel", "parallel"),
        name="attn_latent",
    )(sink, q, k, k, k, vt, vt, vt, kc, vct)


def _attn_context(qc, kc, vct, sink, ctx_len):
    n = qc.shape[0]
    blk = lambda b: (b, 0)
    return pl.pallas_call(
        _attn_ctx_kernel,
        grid=(n // ctx_len,),
        in_specs=[
            pl.BlockSpec(memory_space=pltpu.SMEM),
            pl.BlockSpec((ctx_len, D_ATTN), blk),
            pl.BlockSpec((ctx_len, D_KV), blk),
            pl.BlockSpec((D_KV, ctx_len), lambda b: (0, b)),
        ],
        out_specs=pl.BlockSpec((ctx_len, D_ATTN), blk),
        out_shape=jax.ShapeDtypeStruct((n, D_ATTN), BF16),
        compiler_params=_cparams("parallel"),
        name="attn_context",
    )(sink, qc, kc, vct)


def _mlstm_kernel(ctx_out, nch_c, nch_l, hps,
                  qc_ref, ktc_ref, vc_ref, moc_ref, gc_ref,
                  ql_ref, ktl_ref, vl_ref, mol_ref, gl_ref,
                  gb_ref, ng_ref, *rest):
    if ctx_out:
        yl_ref, yc_ref, af_ref, lff_ref, ab_ref, lfb_ref, cf_ref, cb_ref, hf_ref, hb_ref = rest
    else:
        yl_ref, af_ref, lff_ref, ab_ref, lfb_ref, cf_ref, cb_ref, hf_ref, hb_ref = rest
        yc_ref = None
    lc = MLSTM_CHUNK
    dh = MLSTM_HEAD_DIM
    k_scale = MLSTM_HEAD_DIM ** -0.5
    ti = lax.broadcasted_iota(jnp.int32, (lc, lc), 0)
    si = lax.broadcasted_iota(jnp.int32, (lc, lc), 1)
    lower = si <= ti
    upper = si >= ti
    pre_mat = jnp.where(upper, 1.0, 0.0)
    suf_mat = jnp.where(lower, 1.0, 0.0)
    n_rows = nch_c + nch_l
    pad_rows = -n_rows % 8

    for hh in range(hps):
        gb = gb_ref[hh]

        def gate_rows(kind):
            rows = ([gc_ref[hh, c, kind:kind + 1, :] for c in range(nch_c)]
                    + [gl_ref[hh, c, kind:kind + 1, :] for c in range(nch_l)])
            rows = jnp.concatenate(rows, axis=0) + gb[kind:kind + 1, :]
            if pad_rows:
                rows = jnp.concatenate([rows, jnp.zeros((pad_rows, lc), F32)], axis=0)
            return rows

        lf_f = _log_sigmoid(gate_rows(1))
        lf_b = _log_sigmoid(gate_rows(3))
        a_f = gate_rows(0) - jnp.dot(lf_f, pre_mat, preferred_element_type=F32, precision=HIGHEST)
        a_b = gate_rows(2) - jnp.dot(lf_b, suf_mat, preferred_element_type=F32, precision=HIGHEST)
        for c in range(n_rows):
            af_ref[hh * n_rows + c] = a_f[c:c + 1, :]
            lff_ref[hh * n_rows + c] = lf_f[c:c + 1, :]
            ab_ref[hh * n_rows + c] = a_b[c:c + 1, :]
            lfb_ref[hh * n_rows + c] = lf_b[c:c + 1, :]

    cf_ref[...] = jnp.zeros_like(cf_ref)
    cb_ref[...] = jnp.zeros_like(cb_ref)
    ones_col = jnp.where(lax.broadcasted_iota(jnp.int32, (lc, dh), 1) == 0, 1.0, 0.0).astype(BF16)

    def chunk(q, kt, v, a_row, lf_row, c_ref, hh, m, mask):
        a_mat = jnp.where(mask, jnp.broadcast_to(a_row, (lc, lc)), NEG_INF)
        cm = jnp.max(a_mat, axis=1, keepdims=True)
        b_col = jnp.sum(jnp.where(mask, jnp.broadcast_to(lf_row, (lc, lc)), 0.0), axis=1, keepdims=True)
        mx = jnp.maximum(m, cm)
        mx_last = jnp.maximum(m, jnp.max(a_row, axis=1, keepdims=True))
        w = jnp.exp(a_mat - mx)
        s = jnp.dot(q, kt, preferred_element_type=F32) * k_scale
        p = (s * w).astype(BF16)
        w_s = jnp.exp(a_row - mx_last) * k_scale
        ktw = (kt.astype(F32) * w_s).astype(BF16)
        v_aug = jnp.concatenate([v, ones_col], axis=1)
        both = jnp.dot(jnp.concatenate([p, ktw], axis=0), v_aug, preferred_element_type=F32)
        c_old = c_ref[hh]
        inter = jnp.dot(q, c_old.astype(BF16), preferred_element_type=F32)
        tot = both[0:lc, :] + jnp.exp(m - mx) * inter
        den = tot[:, dh:dh + 1]
        h = tot[:, 0:dh] / jnp.maximum(jnp.abs(den), jnp.exp(-(b_col + mx)))
        c_ref[hh] = jnp.exp(m - mx_last) * c_old + both[lc:lc + dh, :]
        m_new = jnp.sum(lf_row, axis=1, keepdims=True) + mx_last
        return h, m_new

    def step(q_ref, kt_ref, v_ref, row0, c_f, c_b, rf, rb, ms):
        out = []
        for hh in range(hps):
            cols = slice(hh * dh, (hh + 1) * dh)
            base = hh * n_rows + row0
            h_f, m_f = chunk(q_ref[pl.ds(rf, lc), cols], kt_ref[hh, c_f], v_ref[pl.ds(rf, lc), cols],
                             af_ref[base + c_f], lff_ref[base + c_f], cf_ref, hh, ms[2 * hh], lower)
            h_b, m_b = chunk(q_ref[pl.ds(rb, lc), cols], kt_ref[hh, c_b], v_ref[pl.ds(rb, lc), cols],
                             ab_ref[base + c_b], lfb_ref[base + c_b], cb_ref, hh, ms[2 * hh + 1], upper)
            hf_ref[pl.ds(rf, lc), cols] = h_f
            hb_ref[pl.ds(rb, lc), cols] = h_b
            out += [m_f, m_b]
        return tuple(out)

    def finish(n_tok, mo_ref, o_ref):
        for hh in range(hps):
            cols = slice(hh * dh, (hh + 1) * dh)
            h = hf_ref[0:n_tok, cols] + hb_ref[0:n_tok, cols]
            mu = jnp.mean(h, axis=-1, keepdims=True)
            hc = h - mu
            var = jnp.mean(hc * hc, axis=-1, keepdims=True)
            y = hc * lax.rsqrt(var + LN_EPS) * ng_ref[:, cols]
            o_ref[:, cols] = (_sigmoid(mo_ref[:, cols].astype(F32)) * y).astype(BF16)

    ms = tuple(jnp.zeros((1, 1), F32) for _ in range(2 * hps))
    for c in range(nch_c):
        c_b = nch_c - 1 - c
        ms = step(qc_ref, ktc_ref, vc_ref, 0, c, c_b, c * lc, c_b * lc, ms)
    if ctx_out:
        finish(nch_c * lc, moc_ref, yc_ref)

    def body(c, ms):
        c_b = nch_l - 1 - c
        return step(ql_ref, ktl_ref, vl_ref, nch_c, c, c_b,
                    pl.multiple_of(c * lc, lc), pl.multiple_of(c_b * lc, lc), ms)

    lax.fori_loop(0, nch_l, body, ms, unroll=2)
    finish(nch_l * lc, mol_ref, yl_ref)


def _mlstm_branch(ctx_p, lat_p, gate_b, norm_g, seq_len, ctx_len, ctx_out):
    mq_c, kt_c, mv_c, mo_c, g_c = ctx_p
    mq_l, kt_l, mv_l, mo_l, g_l = lat_p
    n_l, n_c = mq_l.shape[0], mq_c.shape[0]
    batch = n_l // seq_len
    lc, dh, hps = MLSTM_CHUNK, MLSTM_HEAD_DIM, MLSTM_HEADS_PER_STEP
    nch_c, nch_l = ctx_len // lc, seq_len // lc

    def stream(t, nch):
        tok = pl.BlockSpec((t, hps * dh), lambda b, h: (b, h))
        return [tok,
                pl.BlockSpec((hps, nch, dh, lc), lambda b, h: (h, b, 0, 0)),
                tok, tok,
                pl.BlockSpec((hps, nch, 4, lc), lambda b, h: (h, b, 0, 0))]

    in_specs = stream(ctx_len, nch_c) + stream(seq_len, nch_l) + [
        pl.BlockSpec((hps, 4, 1), lambda b, h: (h, 0, 0)),
        pl.BlockSpec((1, hps * dh), lambda b, h: (0, h)),
    ]
    out_specs = [pl.BlockSpec((seq_len, hps * dh), lambda b, h: (b, h))]
    out_shape = [jax.ShapeDtypeStruct((n_l, D_MLSTM), BF16)]
    if ctx_out:
        out_specs.append(pl.BlockSpec((ctx_len, hps * dh), lambda b, h: (b, h)))
        out_shape.append(jax.ShapeDtypeStruct((n_c, D_MLSTM), BF16))
    row_scratch = pltpu.VMEM((hps * (nch_c + nch_l), 1, lc), F32)
    outs = pl.pallas_call(
        functools.partial(_mlstm_kernel, ctx_out, nch_c, nch_l, hps),
        grid=(batch, N_MLSTM_HEADS // hps),
        in_specs=in_specs,
        out_specs=out_specs,
        out_shape=out_shape,
        scratch_shapes=[row_scratch, row_scratch, row_scratch, row_scratch,
                        pltpu.VMEM((hps, dh, 2 * dh), F32), pltpu.VMEM((hps, dh, 2 * dh), F32),
                        pltpu.VMEM((seq_len, hps * dh), F32), pltpu.VMEM((seq_len, hps * dh), F32)],
        compiler_params=_cparams("parallel", "parallel"),
        name="mlstm_branch",
    )(mq_c, kt_c, mv_c, mo_c, g_c, mq_l, kt_l, mv_l, mo_l, g_l, gate_b, norm_g)
    return outs if ctx_out else (outs[0], None)


def _route(logits_t, br):
    sc = [_sigmoid(logits_t[e:e + 1, :]) for e in range(N_EXPERTS)]
    sel = [sc[e] + br[e:e + 1, :] for e in range(N_EXPERTS)]
    epg = EXPERTS_PER_GROUP
    group_score = []
    for g in range(N_GROUPS):
        v = sel[g * epg:(g + 1) * epg]
        best = None
        for i in range(epg):
            for j in range(i + 1, epg):
                pair = v[i] + v[j]
                best = pair if best is None else jnp.maximum(best, pair)
        group_score.append(best)
    g_idx = jnp.zeros_like(group_score[0], dtype=jnp.int32)
    best = group_score[0]
    for g in range(1, N_GROUPS):
        better = group_score[g] > best
        g_idx = jnp.where(better, g, g_idx)
        best = jnp.maximum(best, group_score[g])
    chosen = []
    for g in range(N_GROUPS):
        v = sel[g * epg:(g + 1) * epg]
        in_g = g_idx == g
        for i in range(epg):
            rank = jnp.zeros_like(g_idx)
            for j in range(epg):
                if j == i:
                    continue
                ahead = (v[j] >= v[i]) if j < i else (v[j] > v[i])
                rank = rank + jnp.where(ahead, 1, 0)
            chosen.append(in_g & (rank < 2))
    cls = jnp.zeros_like(sc[0])
    w_lo = jnp.zeros_like(sc[0])
    w_hi = jnp.zeros_like(sc[0])
    for g in range(N_GROUPS):
        for pid, (i, j) in enumerate(_PAIRS):
            lo, hi = g * epg + i, g * epg + j
            is_pair = chosen[lo] & chosen[hi]
            cls = jnp.where(is_pair, float(g * len(_PAIRS) + pid), cls)
            w_lo = jnp.where(is_pair, sc[lo], w_lo)
            w_hi = jnp.where(is_pair, sc[hi], w_hi)
    total = w_lo + w_hi
    return cls, w_lo / total, w_hi / total


def _merge_kernel(pre_ln, alpha, ya_ref, yb_ref, yc_ref, bg_ref, h_ref, g1_ref, sc2_ref, sh2_ref,
                  lig_ref, lib_ref, l1g_ref, l1b_ref, wa_ref, wb_ref, wc_ref, wo_ref, wr_ref, br_ref,
                  h1_ref, u2t_ref, route_ref, cnt_ref, tri_ref):
    tm = h_ref.shape[0]

    def branch(y_ref, w_ref, j):
        gate = bg_ref[:, j * 1024:(j + 1) * 1024].astype(F32)
        return gate * jnp.dot(y_ref[...], w_ref[...], preferred_element_type=F32)

    mix = branch(ya_ref, wa_ref, 0) + branch(yb_ref, wb_ref, 1) + branch(yc_ref, wc_ref, 2)
    y = jnp.dot(mix.astype(BF16), wo_ref[...], preferred_element_type=F32)
    h = h_ref[...]
    if pre_ln:
        h = _ln(h, lig_ref[...], lib_ref[...])
    h1 = _ln(alpha * h + g1_ref[...] * y, l1g_ref[...], l1b_ref[...])
    h1_ref[...] = h1
    u2 = h1 * (1.0 + sc2_ref[...]) + sh2_ref[...]
    logits_t = _dot_nt(wr_ref[...], u2, precision=HIGHEST)
    cls, w_lo, w_hi = _route(logits_t, br_ref[...])

    @pl.when(pl.program_id(0) == 0)
    def _():
        cnt_ref[...] = jnp.zeros_like(cnt_ref)
        earlier = lax.broadcasted_iota(jnp.int32, (tm, tm), 0) <= lax.broadcasted_iota(jnp.int32, (tm, tm), 1)
        tri_ref[...] = jnp.where(earlier, 1.0, 0.0).astype(BF16)

    crow = lax.broadcasted_iota(jnp.int32, (N_CLASS_ROWS, tm), 0).astype(F32)
    onehot = jnp.where(crow == cls, 1.0, 0.0)
    incl = jnp.dot(onehot.astype(BF16), tri_ref[...], preferred_element_type=F32)
    base = cnt_ref[...]
    rank = jnp.sum(onehot * (incl - 1.0 + base), axis=0, keepdims=True)
    cnt_ref[...] = base + incl[:, tm - 1:tm]

    route_ref[...] = jnp.concatenate([cls, w_lo, w_hi, rank, jnp.zeros((4, tm), F32)], axis=0)
    for s in range(TOKEN_SUBROWS):
        u2t_ref[pl.ds(s, tm, stride=TOKEN_SUBROWS), :] = u2[:, s * LANES:(s + 1) * LANES]


def _merge(ya, yb, yc, bg, h, mod_l, ln_in, ln1, w_a, w_b, w_c, w_o, w_rt, b_r, seq_len, ctx_row, pre_ln, alpha):
    n = ya.shape[0]
    tm = MERGE_TILE
    row = lambda i: (i, 0)
    const = lambda i: (0, 0)
    vec = pl.BlockSpec((1, 1024), const)
    wspec = lambda w: pl.BlockSpec(w.shape, const)
    return pl.pallas_call(
        functools.partial(_merge_kernel, pre_ln, alpha),
        grid=(n // tm,),
        in_specs=[
            pl.BlockSpec((tm, D_CONV), row),
            pl.BlockSpec((tm, D_ATTN), row),
            pl.BlockSpec((tm, D_MLSTM), row),
            pl.BlockSpec((tm, N_BRANCHES * 1024), row),
            pl.BlockSpec((tm, 1024), row),
            _mod_spec(2, tm, seq_len, ctx_row),
            _mod_spec(4, tm, seq_len, ctx_row),
            _mod_spec(3, tm, seq_len, ctx_row),
            vec, vec, vec, vec,
            wspec(w_a), wspec(w_b), wspec(w_c), wspec(w_o), wspec(w_rt), wspec(b_r),
        ],
        out_specs=[
            pl.BlockSpec((tm, 1024), row),
            pl.BlockSpec((tm * TOKEN_SUBROWS, LANES), row),
            pl.BlockSpec((8, tm), lambda i: (0, i)),
        ],
        out_shape=[
            jax.ShapeDtypeStruct((n, 1024), F32),
            jax.ShapeDtypeStruct((n * TOKEN_SUBROWS, LANES), F32),
            jax.ShapeDtypeStruct((8, n), F32),
        ],
        scratch_shapes=[pltpu.VMEM((N_CLASS_ROWS, 1), F32), pltpu.VMEM((tm, tm), BF16)],
        compiler_params=_cparams("arbitrary"),
        name="merge",
    )(ya, yb, yc, bg, h, mod_l, mod_l, mod_l, ln_in[0], ln_in[1], ln1[0], ln1[1], w_a, w_b, w_c, w_o, w_rt, b_r)


class _TokenGather:
    def __init__(self, idx_ref, base, src_hbm, buf, sem, slot):
        self.idx_ref, self.base, self.src_hbm, self.buf, self.sem, self.slot = idx_ref, base, src_hbm, buf, sem, slot
        self.tokens = buf.shape[1] // TOKEN_SUBROWS
        self.last = idx_ref.shape[0] - 1

    def _copy(self, k):
        sub = TOKEN_SUBROWS
        p = pl.multiple_of(self.idx_ref[jnp.minimum(self.base + k, self.last)] * sub, sub)
        return pltpu.make_async_copy(self.src_hbm.at[pl.ds(p, sub)], self.buf.at[self.slot, pl.ds(k * sub, sub)],
                                     self.sem.at[self.slot])

    def _share(self, part, parts):
        share = self.tokens // parts
        return range(part * share, (part + 1) * share)

    def start(self, part=0, parts=1):
        for k in self._share(part, parts):
            self._copy(k).start(priority=k % 2)

    def wait(self, part=0, parts=1):
        for k in self._share(part, parts):
            self._copy(k).wait()


def _untile_tokens(ref):
    tokens = ref.shape[0] // TOKEN_SUBROWS
    return jnp.concatenate([ref[pl.ds(s, tokens, stride=TOKEN_SUBROWS), :] for s in range(TOKEN_SUBROWS)], axis=1)


def _group_kernel(lo_ref, hi_ref, base_ref, cnt_ref, order_ref, x_hbm, wr_ref, xs_ref, gate_ref, buf, sem):
    i = pl.program_id(0)
    last = pl.num_programs(0) - 1
    slot = i % 2
    nxt = jnp.minimum(i + 1, last)
    gather = functools.partial(_TokenGather, order_ref, src_hbm=x_hbm, buf=buf, sem=sem)
    parts = GROUP_COPY_PARTS
    share = (buf.shape[1] // TOKEN_SUBROWS) // parts

    @pl.when(i == 0)
    def _():
        buf[...] = jnp.zeros_like(buf)

    for p in range(parts):
        @pl.when(jnp.logical_and(i == 0, cnt_ref[0] > p * share))
        def _():
            gather(base=base_ref[0], slot=0).start(p, parts)

    for p in range(parts):
        @pl.when(cnt_ref[i] > p * share)
        def _():
            gather(base=base_ref[i], slot=slot).wait(p, parts)

    for p in range(parts):
        @pl.when(cnt_ref[nxt] > p * share)
        def _():
            gather(base=base_ref[nxt], slot=1 - slot).start(p, parts)

    x32 = _untile_tokens(buf.at[slot])

    def affinity(e):
        logit = jnp.sum(x32 * wr_ref[pl.ds(e, 1), :], axis=1, keepdims=True)
        return _sigmoid(logit)

    s_lo, s_hi = affinity(lo_ref[i]), affinity(hi_ref[i])
    total = s_lo + s_hi
    xs_ref[...] = x32.astype(BF16)
    gate_ref[...] = jnp.concatenate([s_lo / total, s_hi / total], axis=1)

    for p in range(parts):
        @pl.when(jnp.logical_and(i == last, cnt_ref[nxt] > p * share))
        def _():
            gather(base=base_ref[nxt], slot=1 - slot).wait(p, parts)


def _group_tokens(lo, hi, base, cnt, order, u2t, w_rt):
    tm = EXPERT_TILE
    n_tiles = base.shape[0]
    row = lambda i, *_: (i, 0)
    grid_spec = pltpu.PrefetchScalarGridSpec(
        num_scalar_prefetch=5,
        grid=(n_tiles,),
        in_specs=[
            pl.BlockSpec(memory_space=pl.ANY),
            pl.BlockSpec(w_rt.shape, lambda i, *_: (0, 0)),
        ],
        out_specs=[pl.BlockSpec((tm, 1024), row), pl.BlockSpec((tm, 2), row)],
        scratch_shapes=[pltpu.VMEM((2, tm * TOKEN_SUBROWS, LANES), F32), pltpu.SemaphoreType.DMA((2,))],
    )
    return pl.pallas_call(
        _group_kernel,
        grid_spec=grid_spec,
        out_shape=[jax.ShapeDtypeStruct((n_tiles * tm, 1024), BF16), jax.ShapeDtypeStruct((n_tiles * tm, 2), F32)],
        compiler_params=_cparams("arbitrary"),
        name="group_tokens",
    )(lo, hi, base, cnt, order, u2t, w_rt)


def _experts_kernel(lo_ref, hi_ref, valid_ref, x_ref, gate_ref, wgl_ref, wgh_ref, wdl_ref, wdh_ref, o_ref):
    i = pl.program_id(0)
    tm = x_ref.shape[0]

    @pl.when(valid_ref[i] != 0)
    def _():
        x = x_ref[...]

        def expert(wg_ref, wd_ref):
            gu = jnp.dot(x, wg_ref[...].astype(BF16), preferred_element_type=F32)
            g_ = gu[:, 0:D_EXPERT]
            act = (g_ * _sigmoid(g_) * gu[:, D_EXPERT:2 * D_EXPERT]).astype(BF16)
            return jnp.dot(act, wd_ref[...].astype(BF16), preferred_element_type=F32)

        out = gate_ref[:, 0:1] * expert(wgl_ref, wdl_ref) + gate_ref[:, 1:2] * expert(wgh_ref, wdh_ref)
        for s in range(TOKEN_SUBROWS):
            o_ref[pl.ds(s, tm, stride=TOKEN_SUBROWS), :] = out[:, s * LANES:(s + 1) * LANES]

    @pl.when(valid_ref[i] == 0)
    def _():
        o_ref[...] = jnp.zeros_like(o_ref)


def _experts(lo, hi, valid, xs, gates, w_gu, w_dn, layer):
    tm = EXPERT_TILE
    n_tiles = xs.shape[0] // tm
    row = lambda i, *_: (i, 0)
    grid_spec = pltpu.PrefetchScalarGridSpec(
        num_scalar_prefetch=3,
        grid=(n_tiles,),
        in_specs=[
            pl.BlockSpec((tm, 1024), row),
            pl.BlockSpec((tm, 2), row),
            pl.BlockSpec((None, None, 1024, 2 * D_EXPERT), lambda i, lo, hi, *_: (layer, lo[i], 0, 0)),
            pl.BlockSpec((None, None, 1024, 2 * D_EXPERT), lambda i, lo, hi, *_: (layer, hi[i], 0, 0)),
            pl.BlockSpec((None, None, D_EXPERT, 1024), lambda i, lo, hi, *_: (layer, lo[i], 0, 0)),
            pl.BlockSpec((None, None, D_EXPERT, 1024), lambda i, lo, hi, *_: (layer, hi[i], 0, 0)),
        ],
        out_specs=pl.BlockSpec((tm * TOKEN_SUBROWS, LANES), row),
    )
    return pl.pallas_call(
        _experts_kernel,
        grid_spec=grid_spec,
        out_shape=jax.ShapeDtypeStruct((n_tiles * tm * TOKEN_SUBROWS, LANES), F32),
        compiler_params=_cparams("arbitrary"),
        name="moe_experts",
    )(lo, hi, valid, xs, gates, w_gu, w_gu, w_dn, w_dn)


def _final_kernel(alpha, pos_ref, f_hbm, h1_ref, g2_ref, lg_ref, lb_ref, o_ref, buf, sem):
    i = pl.program_id(0)
    last = pl.num_programs(0) - 1
    slot = i % 2
    nxt = jnp.minimum(i + 1, last)
    gather = functools.partial(_TokenGather, pos_ref, src_hbm=f_hbm, buf=buf, sem=sem)

    tm = o_ref.shape[0]

    @pl.when(i == 0)
    def _():
        gather(base=0, slot=0).start()

    gather(base=i * tm, slot=slot).wait()
    gather(base=nxt * tm, slot=1 - slot).start()
    f = _untile_tokens(buf.at[slot])
    o_ref[...] = _ln(alpha * h1_ref[...] + g2_ref[...] * f, lg_ref[...], lb_ref[...])

    @pl.when(i == last)
    def _():
        gather(base=nxt * tm, slot=1 - slot).wait()


def _final_ln(h1, pos, fs, mod_l, ln2, seq_len, ctx_row, alpha):
    n = h1.shape[0]
    tm = FINAL_TILE
    row = lambda i, *_: (i, 0)
    const = lambda i, *_: (0, 0)
    grid_spec = pltpu.PrefetchScalarGridSpec(
        num_scalar_prefetch=1,
        grid=(n // tm,),
        in_specs=[
            pl.BlockSpec(memory_space=pl.ANY),
            pl.BlockSpec((tm, 1024), row),
            _mod_spec(5, tm, seq_len, ctx_row),
            pl.BlockSpec((1, 1024), const),
            pl.BlockSpec((1, 1024), const),
        ],
        out_specs=pl.BlockSpec((tm, 1024), row),
        scratch_shapes=[pltpu.VMEM((2, tm * TOKEN_SUBROWS, LANES), F32), pltpu.SemaphoreType.DMA((2,))],
    )
    return pl.pallas_call(
        functools.partial(_final_kernel, alpha),
        grid_spec=grid_spec,
        out_shape=jax.ShapeDtypeStruct((n, 1024), F32),
        compiler_params=_cparams("arbitrary"),
        name="final_ln",
    )(pos, fs, h1, mod_l, ln2[0], ln2[1])


def _sort_plan(route_t, n):
    tm = EXPERT_TILE
    n_tiles = n // tm + N_CLASSES
    cls = route_t[AUX_CLS].astype(jnp.int32)
    rank = route_t[AUX_RANK].astype(jnp.int32)
    onehot = cls[:, None] == jnp.arange(N_CLASSES, dtype=jnp.int32)[None, :]
    counts = jnp.sum(onehot, axis=0, dtype=jnp.int32)
    padded = (counts + tm - 1) // tm * tm
    ends = jnp.cumsum(padded)
    offs = ends - padded
    pos = jnp.sum(jnp.where(onehot, offs[None, :], 0), axis=1) + rank
    tile_ends = ends // tm
    j = jnp.arange(n_tiles, dtype=jnp.int32)
    n_used = tile_ends[-1]
    valid = j < n_used
    tile_cls = jnp.sum(j[:, None] >= tile_ends[None, :], axis=1)
    last_cls = jnp.sum((n_used - 1) >= tile_ends)
    tile_cls = jnp.where(valid, tile_cls, last_cls)
    group, pid = tile_cls // len(_PAIRS), tile_cls % len(_PAIRS)
    pair = jnp.asarray(np.array(_PAIRS, dtype=np.int32))
    lo = group * EXPERTS_PER_GROUP + pair[pid, 0]
    hi = group * EXPERTS_PER_GROUP + pair[pid, 1]
    order = jnp.argsort(cls, stable=True).astype(jnp.int32)
    starts = jnp.cumsum(counts) - counts
    first = j * tm - offs[tile_cls]
    base = jnp.where(valid, starts[tile_cls] + first, 0)
    cnt = jnp.where(valid, jnp.clip(counts[tile_cls] - first, 0, tm), 0)
    i32 = lambda t: t.astype(jnp.int32)
    return i32(pos), order, i32(base), i32(cnt), i32(lo), i32(hi), i32(valid)


def _moe(u2t, route_t, h1, mod_l, ln2, w_rt, w_gu, w_dn, layer, seq_len, ctx_row, alpha):
    n = h1.shape[0]
    pos, order, base, cnt, lo, hi, valid = _sort_plan(route_t, n)
    xs, gates = _group_tokens(lo, hi, base, cnt, order, u2t, w_rt)
    fs = _experts(lo, hi, valid, xs, gates, w_gu, w_dn, layer)
    return _final_ln(h1, pos, fs, mod_l, ln2, seq_len, ctx_row, alpha)


def _rope_tables(seq_len, n_heads, scale):
    t = np.arange(seq_len)
    quarter = HEAD_DIM // 4
    inv = ROPE_BASE ** (-np.arange(quarter, dtype=np.float32) / quarter)
    d = np.arange(HEAD_DIM)
    pos = np.where((d // (HEAD_DIM // 2) == 0)[None, :], (t // GRID_W)[:, None], (t % GRID_W)[:, None])
    ang = jnp.asarray(pos.astype(np.float32)) * jnp.asarray(inv[d % quarter])[None, :]
    sign = np.where(d % (HEAD_DIM // 2) < quarter, -1.0, 1.0).astype(np.float32)
    cos = jnp.cos(ang) * scale
    sin = jnp.sin(ang) * (sign * scale)[None, :]
    return jnp.tile(cos, (1, n_heads)), jnp.tile(sin, (1, n_heads))


def _flat_tables(seq_len, n_heads, scale):
    return (jnp.full((seq_len, n_heads * HEAD_DIM), scale, F32), jnp.zeros((seq_len, n_heads * HEAD_DIM), F32))


def _prep_in_weights(w_in_l):
    splits = np.cumsum([2 * D_CONV, D_ATTN, D_KV, D_KV, D_MLSTM, D_MLSTM, D_MLSTM, D_MLSTM, N_GATE_COLS])
    a, q, k, v, mq, mk, mv, mo, mg, bg = jnp.split(w_in_l, splits, axis=1)
    w_main = jnp.concatenate([a, q, k, mq, mv, mo, bg], axis=1).astype(BF16)
    order = np.array([d * 8 + kind * 4 + h for h in range(N_MLSTM_HEADS) for d in range(2) for kind in range(2)])
    w_t = jnp.concatenate([mk.T, mg[:, order].T, v.T], axis=0).astype(BF16)
    return w_main, w_t


def kernel(x, c, ctx, c_ctx, ln_in_g, ln_in_b, w_router, b_router, w_mod, b_mod, w_in, conv_w, conv_b, conv_ln_g,
           conv_ln_b, w_a_out, attn_sink, w_b_out, mlstm_gate_b, mlstm_norm_g, w_c_out, w_out, ln1_g, ln1_b,
           moe_w_gu, moe_w_dn, ln2_g, ln2_b):
    batch, seq_len, d = x.shape
    ctx_len = ctx.shape[1]
    depth = w_in.shape[0]
    alpha = (2.0 * depth) ** 0.25
    ctx_row = batch
    assert d == 1024 and batch < MOD_ROWS
    assert seq_len % MERGE_TILE == 0 and seq_len % TOKEN_TILE == 0 and (batch * ctx_len) % MERGE_TILE == 0
    assert ctx_len % MLSTM_CHUNK == 0 and ctx_len % 8 == 0
    assert seq_len % FINAL_TILE == 0 and (batch * ctx_len) % FINAL_TILE == 0 and FINAL_TILE % EXPERT_TILE == 0

    cc = jnp.zeros((MOD_ROWS, d), F32).at[0:batch].set(c).at[batch].set(c_ctx)
    mod = _modulation(cc, w_mod, b_mod).reshape(depth, MOD_ROWS * N_MOD, 1, d)

    attn_scale = HEAD_DIM ** -0.5
    rope_lat = _rope_tables(seq_len, N_Q_HEADS, attn_scale) + _rope_tables(seq_len, N_KV_HEADS, 1.0)
    rope_ctx = _flat_tables(ctx_len, N_Q_HEADS, attn_scale) + _flat_tables(ctx_len, N_KV_HEADS, 1.0)

    vec = lambda t: t.reshape(1, -1)
    ln_in = (vec(ln_in_g), vec(ln_in_b))
    w_rt = w_router.T
    b_r = b_router.reshape(N_EXPERTS, 1)

    h = x.reshape(batch * seq_len, d)
    hc = ctx.reshape(batch * ctx_len, d)
    for l in range(depth):
        need_ctx = l < depth - 1
        pre_ln = l == 0
        mod_l = mod[l]
        w_main, w_t = _prep_in_weights(w_in[l])
        lat = _in_proj(h, ln_in[0], ln_in[1], mod_l, w_main, w_t, rope_lat, seq_len, None, pre_ln)
        cx = _in_proj(hc, ln_in[0], ln_in[1], mod_l, w_main, w_t, rope_ctx, ctx_len, ctx_row, pre_ln, full=need_ctx)
        a_l, q_l, k_l, v_l, mq_l, mv_l, mo_l, bg_l, kt_l, gt_l = lat
        a_c, q_c, k_c, v_c, mq_c, mv_c, mo_c, bg_c, kt_c, gt_c = cx

        conv_args = (conv_w[l], vec(conv_b[l]), vec(conv_ln_g[l]), vec(conv_ln_b[l]))
        gate_b = jnp.transpose(mlstm_gate_b[l], (2, 0, 1)).reshape(N_MLSTM_HEADS, 4, 1)
        ya = _conv_branch(a_l, *conv_args, seq_len)
        yb = _attn_latent(q_l, k_l, v_l, k_c, v_c, attn_sink[l], seq_len, ctx_len)
        yc, yc_c = _mlstm_branch((mq_c, kt_c, mv_c, mo_c, gt_c), (mq_l, kt_l, mv_l, mo_l, gt_l),
                                 gate_b, vec(mlstm_norm_g[l]), seq_len, ctx_len, need_ctx)

        ln1 = (vec(ln1_g[l]), vec(ln1_b[l]))
        ln2 = (vec(ln2_g[l]), vec(ln2_b[l]))
        w_a, w_b, w_c, w_o = (w.astype(BF16) for w in (w_a_out[l], w_b_out[l], w_c_out[l], w_out[l]))

        h1, pay, route_t = _merge(ya, yb, yc, bg_l, h, mod_l, ln_in, ln1, w_a, w_b, w_c, w_o, w_rt, b_r,
                                  seq_len, None, pre_ln, alpha)
        h = _moe(pay, route_t, h1, mod_l, ln2, w_rt, moe_w_gu, moe_w_dn, l, seq_len, None, alpha)
        if need_ctx:
            ya_c = _conv_branch(a_c, *conv_args, ctx_len)
            yb_c = _attn_context(q_c, k_c, v_c, attn_sink[l], ctx_len)
            h1c, pay_c, route_tc = _merge(ya_c, yb_c, yc_c, bg_c, hc, mod_l, ln_in, ln1, w_a, w_b, w_c, w_o, w_rt,
                                          b_r, ctx_len, ctx_row, pre_ln, alpha)
            hc = _moe(pay_c, route_tc, h1c, mod_l, ln2, w_rt, moe_w_gu, moe_w_dn, l, ctx_len, ctx_row, alpha)
    return h.reshape(batch, seq_len, d)
```
